```python
import jax
import jax.numpy as jnp
from jax import lax
import numpy as np

D_MODEL = 1024
BATCH = 8
SEQ = 2048
DEPTH = 2
DEC_BATCH = 32
DEC_SEQ = 1
PAST_LEN = 8192
PAGE_SIZE = 128

N_AB = (DEPTH + 1) // 2
N_C = DEPTH // 2
H_A = 8
DH_A = 64
MOBA_BLOCK = 256
MOBA_TOPK = 3
MOBA_Q_BLOCK = 16
H_B = 8
DK_B = 64
DV_B = 64
RET_CHUNK = 128
H_C = 8
DH_C = D_MODEL // H_C
MLSTM_CHUNK = 128
D_FF = 11 * D_MODEL // 4
CONV_W = 3
ROPE_THETA = 10000.0
EPS = 1e-6
W_A = H_A * DH_A
W_B = H_B * DV_B
W_C = H_C * DH_C
IN_AB = 3 * W_A + 2 * H_B * DK_B + 2 * W_B
IN_C = 4 * W_C + 2 * H_C
AB_SPLITS = (W_A, 2 * W_A, 3 * W_A, 3 * W_A + H_B * DK_B, 3 * W_A + 2 * H_B * DK_B, 3 * W_A + 2 * H_B * DK_B + W_B)
C_SPLITS = (W_C, 2 * W_C, 3 * W_C, 4 * W_C, 4 * W_C + H_C)
F32 = jnp.float32

kernel_name = "moba_retnet_mlstm_convffn_step"


def rmsnorm(x, g):
    xf = x.astype(F32)
    y = xf * lax.rsqrt(jnp.mean(xf * xf, axis=-1, keepdims=True) + EPS)
    return (y * g.astype(F32)).astype(x.dtype)


def head_rmsnorm(x, g):
    H, d = x.shape[-2], x.shape[-1]
    y = x * lax.rsqrt(jnp.mean(x * x, axis=-1, keepdims=True) + EPS)
    return y * g.astype(F32).reshape(H, d)


def rope(x, pos):
    half = x.shape[-1] // 2
    inv = ROPE_THETA ** (-jnp.arange(half, dtype=F32) / half)
    ang = pos[:, None] * inv[None, :]
    cos = jnp.cos(ang)[None, :, None, :]
    sin = jnp.sin(ang)[None, :, None, :]
    xf = x.astype(F32)
    x1, x2 = xf[..., :half], xf[..., half:]
    return jnp.concatenate([x1 * cos - x2 * sin, x2 * cos + x1 * sin], axis=-1).astype(x.dtype)


def to_chunks(a, c):
    B, T = a.shape[0], a.shape[1]
    return jnp.moveaxis(a.reshape((B, T // c, c) + a.shape[2:]), 1, 0)


def from_chunks(a):
    n, B, c = a.shape[0], a.shape[1], a.shape[2]
    return jnp.moveaxis(a, 0, 1).reshape((B, n * c) + a.shape[3:])


def moba_attend(q, q_pos, k_all, v_all):
    B, Q, H, dh = q.shape
    L = k_all.shape[1]
    nb = -(-L // MOBA_BLOCK)
    pad = ((0, 0), (0, nb * MOBA_BLOCK - L), (0, 0), (0, 0))
    kb = jnp.pad(k_all, pad).reshape(B, nb, MOBA_BLOCK, H, dh)
    vb = jnp.pad(v_all, pad).reshape(B, nb, MOBA_BLOCK, H, dh)
    kmean = jnp.mean(kb.astype(F32), axis=2)
    topk = min(MOBA_TOPK, nb)
    n_slots = topk + 1
    bi = jnp.arange(B)[:, None, None, None]
    hi = jnp.arange(H)[None, None, :, None]
    is_own = (jnp.arange(n_slots) == topk)[:, None]
    offs = jnp.arange(MOBA_BLOCK)
    scale = dh ** -0.5

    def attend(args):
        qc, pc = args
        own = pc // MOBA_BLOCK
        gate = jnp.einsum('bqhd,bnhd->bqhn', qc.astype(F32), kmean)
        past = jnp.arange(nb)[None, :] < own[:, None]
        gate = jnp.where(past[None, :, None, :], gate, -jnp.inf)
        _, sel = lax.top_k(gate, topk)
        own_idx = jnp.broadcast_to(own[None, :, None, None], sel.shape[:3] + (1,)).astype(sel.dtype)
        idx = jnp.concatenate([sel, own_idx], axis=-1)
        kg = kb[bi, idx, :, hi, :]
        vg = vb[bi, idx, :, hi, :]
        s = jnp.einsum('bqhd,bqhjkd->bqhjk', qc.astype(F32), kg.astype(F32)) * scale
        kpos = idx[..., None] * MOBA_BLOCK + offs
        valid = jnp.where(is_own, kpos <= pc[None, :, None, None, None],
                          (idx < own[None, :, None, None])[..., None])
        s = jnp.where(valid, s, -jnp.inf)
        p = jax.nn.softmax(s.reshape(s.shape[:3] + (-1,)), axis=-1).reshape(s.shape)
        return jnp.einsum('bqhjk,bqhjkd->bqhd', p, vg.astype(F32)).astype(q.dtype)

    qblk = MOBA_Q_BLOCK if Q % MOBA_Q_BLOCK == 0 else Q
    out = lax.map(attend, (to_chunks(q, qblk), q_pos.reshape(Q // qblk, qblk)))
    return from_chunks(out)


def retention(q, k, v, s0, chunk):
    log_g = jnp.log1p(-jnp.exp2(-5.0 - jnp.arange(H_B, dtype=F32)))
    i = jnp.arange(chunk, dtype=F32)
    d_in = jnp.exp((i[:, None] + 1.0) * log_g)[None, :, :, None]
    d_out = jnp.exp((chunk - 1.0 - i)[:, None] * log_g)[None, :, :, None]
    diff = i[:, None] - i[None, :]
    d_mask = jnp.where(diff >= 0, jnp.exp(jnp.maximum(diff, 0.0)[None] * log_g[:, None, None]), 0.0)
    g_chunk = jnp.exp(chunk * log_g)[None, :, None, None]

    def step(S, inp):
        qj, kj, vj = (t.astype(F32) for t in inp)
        att = jnp.einsum('bihd,bjhd->bhij', qj, kj) * d_mask
        o = jnp.einsum('bhij,bjhe->bihe', att, vj) + jnp.einsum('bihd,bhde->bihe', qj * d_in, S)
        S = S * g_chunk + jnp.einsum('bjhd,bjhe->bhde', kj * d_out, vj)
        return S, o

    S, o = lax.scan(step, s0.astype(F32), (to_chunks(q, chunk), to_chunks(k, chunk), to_chunks(v, chunk)))
    return from_chunks(o), S


def mlstm(q, k, v, i_pre, f_pre, C0, n0, m0, chunk):
    causal = jnp.tril(jnp.ones((chunk, chunk), dtype=bool))

    def step(carry, inp):
        C, n, m = carry
        qj, kj, vj, ij, fj = (t.astype(F32) for t in inp)
        b = jnp.cumsum(jax.nn.log_sigmoid(fj), axis=1).transpose(0, 2, 1)
        ig = ij.transpose(0, 2, 1)
        log_d = jnp.where(causal, b[..., :, None] - b[..., None, :] + ig[..., None, :], -jnp.inf)
        log_inter = b + m[..., None]
        m_t = jnp.maximum(log_inter, jnp.max(log_d, axis=-1))
        w_intra = jnp.exp(log_d - m_t[..., None])
        w_inter = jnp.exp(log_inter - m_t)
        s = jnp.einsum('bthd,bshd->bhts', qj, kj) * w_intra
        num = (jnp.einsum('bhts,bshe->bthe', s, vj)
               + jnp.einsum('bthd,bhed->bthe', qj, C) * w_inter.transpose(0, 2, 1)[..., None])
        den = jnp.sum(s, axis=-1) + w_inter * jnp.einsum('bthd,bhd->bht', qj, n)
        h = num / jnp.maximum(jnp.abs(den), jnp.exp(-m_t)).transpose(0, 2, 1)[..., None]
        b_last = b[..., -1]
        log_w = b_last[..., None] - b + ig
        m_new = jnp.maximum(b_last + m, jnp.max(log_w, axis=-1))
        w_k = jnp.exp(log_w - m_new[..., None]).transpose(0, 2, 1)[..., None]
        decay = jnp.exp(b_last + m - m_new)
        C = decay[..., None, None] * C + jnp.einsum('bshe,bshd->bhed', vj * w_k, kj)
        n = decay[..., None] * n + jnp.sum(kj * w_k, axis=1)
        return (C, n, m_new), h

    xs = (to_chunks(q, chunk), to_chunks(k, chunk), to_chunks(v, chunk), to_chunks(i_pre, chunk), to_chunks(f_pre, chunk))
    (C, n, m), h = lax.scan(step, (C0.astype(F32), n0.astype(F32), m0.astype(F32)), xs)
    return from_chunks(h), C, n, m


def ab_layer(x, pos, k_prev, v_prev, s0, chunk, norm_g, w_in, ret_g, w_out):
    B, T, _ = x.shape
    z = rmsnorm(x, norm_g) @ w_in
    qa, ka, va, qb, kb, vb, gb = jnp.split(z, AB_SPLITS, axis=-1)
    posf = pos.astype(F32)
    qa = rope(qa.reshape(B, T, H_A, DH_A), posf)
    ka = rope(ka.reshape(B, T, H_A, DH_A), posf)
    va = va.reshape(B, T, H_A, DH_A)
    if k_prev is None:
        k_all, v_all = ka, va
    else:
        k_all = jnp.concatenate([k_prev.astype(ka.dtype), ka], axis=1)
        v_all = jnp.concatenate([v_prev.astype(va.dtype), va], axis=1)
    oa = moba_attend(qa, pos, k_all, v_all)
    qb = rope(qb.reshape(B, T, H_B, DK_B), posf)
    kb = rope(kb.reshape(B, T, H_B, DK_B), posf) * DK_B ** -0.5
    ob, s_new = retention(qb, kb, vb.reshape(B, T, H_B, DV_B), s0, chunk)
    ob = head_rmsnorm(ob, ret_g) * jax.nn.silu(gb.astype(F32)).reshape(B, T, H_B, DV_B)
    mix = jnp.concatenate([oa.reshape(B, T, W_A), ob.reshape(B, T, W_B).astype(x.dtype)], axis=-1)
    return mix @ w_out, ka, va, s_new


def mlstm_layer(x, C0, n0, m0, chunk, norm_g, w_in, gate_bias, out_g, w_out):
    B, T, _ = x.shape
    z = rmsnorm(x, norm_g) @ w_in
    q, k, v, o, ig, fg = jnp.split(z, C_SPLITS, axis=-1)
    q = q.reshape(B, T, H_C, DH_C)
    k = k.reshape(B, T, H_C, DH_C) * DH_C ** -0.5
    v = v.reshape(B, T, H_C, DH_C)
    ig = ig.astype(F32) + gate_bias[0].astype(F32)
    fg = fg.astype(F32) + gate_bias[1].astype(F32)
    h, C, n, m = mlstm(q, k, v, ig, fg, C0, n0, m0, chunk)
    h = head_rmsnorm(h, out_g) * jax.nn.sigmoid(o.astype(F32)).reshape(B, T, H_C, DH_C)
    return h.reshape(B, T, W_C).astype(x.dtype) @ w_out, C, n, m


def conv_ffn(x, buf, norm_g, w1, w3, conv_w, conv_b, w2):
    h = rmsnorm(x, norm_g)
    a = h @ w1
    g = h @ w3
    T = a.shape[1]
    ext = jnp.concatenate([buf.astype(a.dtype), a], axis=1)
    ac = conv_b
    for j in range(CONV_W):
        ac = ac + ext[:, j:j + T] * conv_w[j]
    y = (jax.nn.gelu(ac) * g) @ w2
    return y, ext[:, -(CONV_W - 1):]


def setup_inputs(seed: int = 0) -> dict:
    key = jax.random.key(seed)
    ks = jax.random.split(key, 32)

    def nrm(i, shape, scale):
        return jax.random.normal(ks[i], shape, F32) * scale

    n_pages = PAST_LEN // PAGE_SIZE
    n_pool = (5 * DEC_BATCH * n_pages) // 4
    page_table = jax.random.permutation(ks[4], n_pool)[: DEC_BATCH * n_pages].reshape(DEC_BATCH, n_pages).astype(jnp.int32)
    gate_base = jnp.stack([jnp.zeros((H_C,), F32), jnp.linspace(3.0, 6.0, H_C, dtype=F32)])
    out_scale = 0.5
    return {
        'x_prompt': nrm(0, (BATCH, SEQ, D_MODEL), 1.0),
        'x_sample': nrm(1, (DEC_BATCH, DEC_SEQ, D_MODEL), 1.0),
        'cache_k': nrm(2, (N_AB, n_pool, PAGE_SIZE, H_A, DH_A), 1.0),
        'cache_v': nrm(3, (N_AB, n_pool, PAGE_SIZE, H_A, DH_A), 1.0),
        'page_table': page_table,
        'state_ret': nrm(5, (N_AB, DEC_BATCH, H_B, DK_B, DV_B), 0.3),
        'state_mlstm_C': nrm(6, (N_C, DEC_BATCH, H_C, DH_C, DH_C), 0.3),
        'state_mlstm_n': nrm(7, (N_C, DEC_BATCH, H_C, DH_C), 1.0),
        'state_mlstm_m': nrm(8, (N_C, DEC_BATCH, H_C), 0.5),
        'state_ffn_conv': nrm(9, (DEPTH, DEC_BATCH, CONV_W - 1, D_FF), 1.0),
        'ab_norm_g': 1.0 + nrm(10, (N_AB, D_MODEL), 0.05),
        'ab_w_in': nrm(11, (N_AB, D_MODEL, IN_AB), D_MODEL ** -0.5),
        'ab_ret_norm_g': 1.0 + nrm(12, (N_AB, W_B), 0.05),
        'ab_w_out': nrm(13, (N_AB, W_A + W_B, D_MODEL), out_scale * (W_A + W_B) ** -0.5),
        'c_norm_g': 1.0 + nrm(14, (N_C, D_MODEL), 0.05),
        'c_w_in': nrm(15, (N_C, D_MODEL, IN_C), D_MODEL ** -0.5),
        'c_gate_bias': gate_base[None] + nrm(16, (N_C, 2, H_C), 0.1),
        'c_out_norm_g': 1.0 + nrm(17, (N_C, W_C), 0.05),
        'c_w_out': nrm(18, (N_C, W_C, D_MODEL), out_scale * W_C ** -0.5),
        'ffn_norm_g': 1.0 + nrm(19, (DEPTH, D_MODEL), 0.05),
        'ffn_w1': nrm(20, (DEPTH, D_MODEL, D_FF), D_MODEL ** -0.5),
        'ffn_w3': nrm(21, (DEPTH, D_MODEL, D_FF), D_MODEL ** -0.5),
        'ffn_conv_w': nrm(22, (DEPTH, CONV_W, D_FF), CONV_W ** -0.5),
        'ffn_conv_b': nrm(23, (DEPTH, D_FF), 0.02),
        'ffn_w2': nrm(24, (DEPTH, D_FF, D_MODEL), out_scale * D_FF ** -0.5),
        'final_norm_g': 1.0 + nrm(25, (D_MODEL,), 0.05),
    }


def reference(x_prompt, x_sample, cache_k, cache_v, page_table, state_ret, state_mlstm_C, state_mlstm_n,
              state_mlstm_m, state_ffn_conv, ab_norm_g, ab_w_in, ab_ret_norm_g, ab_w_out, c_norm_g, c_w_in,
              c_gate_bias, c_out_norm_g, c_w_out, ffn_norm_g, ffn_w1, ffn_w3, ffn_conv_w, ffn_conv_b, ffn_w2,
              final_norm_g):
    Bp, Tp, _ = x_prompt.shape
    Bs, Ts, _ = x_sample.shape
    pos_p = jnp.arange(Tp, dtype=jnp.int32)
    pos_s = PAST_LEN + jnp.arange(Ts, dtype=jnp.int32)
    ret_chunk = RET_CHUNK if Tp % RET_CHUNK == 0 else Tp
    ml_chunk = MLSTM_CHUNK if Tp % MLSTM_CHUNK == 0 else Tp
    xp, xs = x_prompt, x_sample
    kp, vp, ksm, vsm, rp, rs = [], [], [], [], [], []
    cp, cs, nps, nss, mps, mss = [], [], [], [], [], []
    fp, fs = [], []
    for l in range(DEPTH):
        j = l // 2
        if l % 2 == 0:
            y, k_new, v_new, s_new = ab_layer(xp, pos_p, None, None, jnp.zeros((Bp, H_B, DK_B, DV_B), F32), ret_chunk,
                                              ab_norm_g[j], ab_w_in[j], ab_ret_norm_g[j], ab_w_out[j])
            xp = xp + y
            kp.append(k_new)
            vp.append(v_new)
            rp.append(s_new)
            k_prev = cache_k[j][page_table].reshape(Bs, PAST_LEN, H_A, DH_A)
            v_prev = cache_v[j][page_table].reshape(Bs, PAST_LEN, H_A, DH_A)
            y, k_new, v_new, s_new = ab_layer(xs, pos_s, k_prev, v_prev, state_ret[j], Ts,
                                              ab_norm_g[j], ab_w_in[j], ab_ret_norm_g[j], ab_w_out[j])
            xs = xs + y
            ksm.append(k_new)
            vsm.append(v_new)
            rs.append(s_new)
        else:
            y, C, n, m = mlstm_layer(xp, jnp.zeros((Bp, H_C, DH_C, DH_C), F32), jnp.zeros((Bp, H_C, DH_C), F32),
                                     jnp.zeros((Bp, H_C), F32), ml_chunk,
                                     c_norm_g[j], c_w_in[j], c_gate_bias[j], c_out_norm_g[j], c_w_out[j])
            xp = xp + y
            cp.append(C)
            nps.append(n)
            mps.append(m)
            y, C, n, m = mlstm_layer(xs, state_mlstm_C[j], state_mlstm_n[j], state_mlstm_m[j], Ts,
                                     c_norm_g[j], c_w_in[j], c_gate_bias[j], c_out_norm_g[j], c_w_out[j])
            xs = xs + y
            cs.append(C)
            nss.append(n)
            mss.append(m)
        y, buf = conv_ffn(xp, jnp.zeros((Bp, CONV_W - 1, D_FF), xp.dtype), ffn_norm_g[l], ffn_w1[l], ffn_w3[l],
                          ffn_conv_w[l], ffn_conv_b[l], ffn_w2[l])
        xp = xp + y
        fp.append(buf)
        y, buf = conv_ffn(xs, state_ffn_conv[l], ffn_norm_g[l], ffn_w1[l], ffn_w3[l],
                          ffn_conv_w[l], ffn_conv_b[l], ffn_w2[l])
        xs = xs + y
        fs.append(buf)
    y_prompt = rmsnorm(xp, final_norm_g)
    y_sample = rmsnorm(xs, final_norm_g)
    return (y_prompt, y_sample, jnp.stack(kp), jnp.stack(vp), jnp.stack(ksm), jnp.stack(vsm),
            jnp.stack(rp), jnp.stack(rs), jnp.stack(cp), jnp.stack(cs), jnp.stack(nps), jnp.stack(nss),
            jnp.stack(mps), jnp.stack(mss), jnp.stack(fp), jnp.stack(fs))
```

```python
import functools

import jax
import jax.numpy as jnp
from jax import lax
from jax.experimental import pallas as pl
from jax.experimental.pallas import tpu as pltpu

F32 = jnp.float32
BF16 = jnp.bfloat16
HIGHEST = lax.Precision.HIGHEST

LANES = 128
D_MODEL = 1024
H_A = 8
DH_A = 64
MOBA_BLOCK = 256
MOBA_TOPK = 3
H_B = 8
DK_B = 64
DV_B = 64
RET_CHUNK = 128
H_C = 8
DH_C = D_MODEL // H_C
MLSTM_CHUNK = 128
D_FF = 11 * D_MODEL // 4
CONV_W = 3
ROPE_THETA = 10000.0
EPS = 1e-6
W_A = H_A * DH_A
W_B = H_B * DV_B
W_C = H_C * DH_C
IN_AB = 3 * W_A + 2 * H_B * DK_B + 2 * W_B
AB_TILE = 512
AB_ROPE_TILES = (0, 1, 3, 4)
AB_KB_TILE = 4
VMEM_LIMIT = 56 * 1024 * 1024

NEG_INF = float("-inf")


def _cparams(sem):
    return pltpu.CompilerParams(dimension_semantics=sem, vmem_limit_bytes=VMEM_LIMIT)


def _nt(a, b, **kw):
    return lax.dot_general(a, b, (((1,), (1,)), ((), ())), preferred_element_type=F32, **kw)


def _tn(a, b, **kw):
    return lax.dot_general(a, b, (((0,), (0,)), ((), ())), preferred_element_type=F32, **kw)


def _rms_rows(x, g):
    ms = jnp.mean(x * x, axis=-1, keepdims=True)
    return x * lax.rsqrt(ms + EPS) * g


def _eye(n):
    return lax.broadcasted_iota(jnp.int32, (n, n), 0) == lax.broadcasted_iota(jnp.int32, (n, n), 1)


def _row_to_col(row, eye):
    return jnp.sum(jnp.where(eye, row, 0.0), axis=1, keepdims=True)


def _col_to_row(col, eye):
    return jnp.sum(jnp.where(eye, col, 0.0), axis=0, keepdims=True)


def _log_sigmoid(x):
    return jnp.minimum(x, 0.0) - jnp.log1p(jnp.exp(-jnp.abs(x)))


def _sigmoid(x):
    return 1.0 / (1.0 + jnp.exp(-x))


def _gelu_tanh(x):
    c = 0.7978845608028654
    return 0.5 * x * (1.0 + jnp.tanh(c * (x + 0.044715 * (x * x * x))))


def _norm_matmul_kernel(x_ref, g_ref, w_ref, cos_ref, sin_ref, o_ref, xn_ref, *, rope_tiles, scale_tile, scale):
    j = pl.program_id(1)

    @pl.when(j == 0)
    def _():
        xn_ref[...] = _rms_rows(x_ref[...], g_ref[...]).astype(BF16)

    z = jnp.dot(xn_ref[...], w_ref[...], preferred_element_type=F32)
    if not rope_tiles:
        o_ref[...] = z
        return

    is_rope = functools.reduce(jnp.logical_or, [j == t for t in rope_tiles])

    @pl.when(is_rope)
    def _():
        sc = jnp.where(j == scale_tile, scale, 1.0).astype(F32)
        lane = lax.broadcasted_iota(jnp.int32, (1, LANES), 1)
        first_half = (lane % DH_A) < (DH_A // 2)
        cos = cos_ref[...]
        sin = sin_ref[...]
        for c in range(z.shape[1] // LANES):
            zc = z[:, c * LANES:(c + 1) * LANES]
            partner = jnp.where(first_half, pltpu.roll(zc, LANES - DH_A // 2, 1), pltpu.roll(zc, DH_A // 2, 1))
            o_ref[:, c * LANES:(c + 1) * LANES] = (zc * cos + partner * sin) * sc

    @pl.when(jnp.logical_not(is_rope))
    def _():
        o_ref[...] = z


def _norm_matmul(x, g, w, tm, tn, rope=None):
    M, D = x.shape
    N = w.shape[1]
    assert M % tm == 0 and N % tn == 0
    if rope is None:
        cos = sin = jnp.zeros((8, LANES), F32)
        tab_spec = pl.BlockSpec((8, LANES), lambda i, j: (0, 0))
        kern = functools.partial(_norm_matmul_kernel, rope_tiles=(), scale_tile=-1, scale=1.0)
    else:
        cos, sin = rope
        nt = cos.shape[0] // tm
        tab_spec = pl.BlockSpec((tm, LANES), lambda i, j: (i % nt, 0))
        kern = functools.partial(_norm_matmul_kernel, rope_tiles=AB_ROPE_TILES, scale_tile=AB_KB_TILE,
                                 scale=DK_B ** -0.5)
    return pl.pallas_call(
        kern,
        grid=(M // tm, N // tn),
        in_specs=[pl.BlockSpec((tm, D), lambda i, j: (i, 0)),
                  pl.BlockSpec((1, D), lambda i, j: (0, 0)),
                  pl.BlockSpec((D, tn), lambda i, j: (0, j)),
                  tab_spec, tab_spec],
        out_specs=pl.BlockSpec((tm, tn), lambda i, j: (i, j)),
        out_shape=jax.ShapeDtypeStruct((M, N), F32),
        scratch_shapes=[pltpu.VMEM((tm, D), BF16)],
        compiler_params=_cparams(("parallel", "arbitrary")),
        name="norm_matmul",
    )(x, g.reshape(1, D), w, cos, sin)


def _proj_residual_kernel(*refs, n_in):
    a_refs = refs[:n_in]
    w_refs = refs[n_in:2 * n_in]
    res_ref = refs[2 * n_in]
    o_ref = refs[2 * n_in + 1]
    y = res_ref[...]
    acc = None
    for a_ref, w_ref in zip(a_refs, w_refs):
        d = jnp.dot(a_ref[...], w_ref[...], preferred_element_type=F32)
        acc = d if acc is None else acc + d
    o_ref[...] = y + acc


def _proj_residual(acts, ws, res, tm):
    M, D = res.shape
    n_in = len(acts)
    in_specs = ([pl.BlockSpec((tm, a.shape[1]), lambda i: (i, 0)) for a in acts]
                + [pl.BlockSpec(w.shape, lambda i: (0, 0)) for w in ws]
                + [pl.BlockSpec((tm, D), lambda i: (i, 0))])
    return pl.pallas_call(
        functools.partial(_proj_residual_kernel, n_in=n_in),
        grid=(M // tm,),
        in_specs=in_specs,
        out_specs=pl.BlockSpec((tm, D), lambda i: (i, 0)),
        out_shape=jax.ShapeDtypeStruct((M, D), F32),
        compiler_params=_cparams(("parallel",)),
        name="proj_residual",
    )(*acts, *ws, res)


PREV_ROWS = 16


def _ffn_kernel(*refs, seq_mode, tiles_per_seq, final_norm):
    if seq_mode:
        (x_ref, xp_ref, g_ref, w1_ref, w3_ref, cw_ref, cb_ref, w2_ref, fg_ref, o_ref, xn_ref, xpn_ref) = refs
    else:
        (x_ref, s0_ref, s1_ref, g_ref, w1_ref, w3_ref, cw_ref, cb_ref, w2_ref, fg_ref, o_ref, xn_ref) = refs
    i = pl.program_id(0)
    j = pl.program_id(1)
    nj = pl.num_programs(1)

    @pl.when(j == 0)
    def _():
        xn_ref[...] = _rms_rows(x_ref[...], g_ref[...]).astype(BF16)
        if seq_mode:
            xpn_ref[...] = _rms_rows(xp_ref[...], g_ref[...]).astype(BF16)
        o_ref[...] = jnp.zeros_like(o_ref)

    xn = xn_ref[...]
    a = jnp.dot(xn, w1_ref[...], preferred_element_type=F32)
    gate = jnp.dot(xn, w3_ref[...], preferred_element_type=F32)
    if seq_mode:
        tm = a.shape[0]
        ap = jnp.dot(xpn_ref[...], w1_ref[...], preferred_element_type=F32)
        has_prev = ((i % tiles_per_seq) != 0).astype(F32)
        p1 = ap[PREV_ROWS - 1:PREV_ROWS, :] * has_prev
        p2 = ap[PREV_ROWS - 2:PREV_ROWS - 1, :] * has_prev
        row = lax.broadcasted_iota(jnp.int32, (tm, 1), 0)
        a1 = jnp.where(row == 0, p1, pltpu.roll(a, 1, 0))
        a2 = jnp.where(row == 0, p2, jnp.where(row == 1, p1, pltpu.roll(a, 2, 0)))
    else:
        a1 = s1_ref[...]
        a2 = s0_ref[...]
    cw = cw_ref[...]
    ac = cb_ref[...] + a2 * cw[0:1, :]
    ac = ac + a1 * cw[1:2, :]
    ac = ac + a * cw[2:3, :]
    y = (_gelu_tanh(ac) * gate).astype(BF16)
    o_ref[...] += jnp.dot(y, w2_ref[...], preferred_element_type=F32)

    @pl.when(j == nj - 1)
    def _():
        r = x_ref[...] + o_ref[...]
        if final_norm:
            r = _rms_rows(r, fg_ref[...])
        o_ref[...] = r


def _ffn(x, norm_g, w1, w3, conv_w, conv_b, w2, final_g, tm, tf, seq_len=None, state=None):
    M, D = x.shape
    F = w1.shape[1]
    seq_mode = state is None
    final_norm = final_g is not None
    fg = (final_g if final_norm else jnp.ones((D,), F32)).reshape(1, D)
    common = [pl.BlockSpec((1, D), lambda i, j: (0, 0)),
              pl.BlockSpec((D, tf), lambda i, j: (0, j)),
              pl.BlockSpec((D, tf), lambda i, j: (0, j)),
              pl.BlockSpec((CONV_W, tf), lambda i, j: (0, j)),
              pl.BlockSpec((1, tf), lambda i, j: (0, j)),
              pl.BlockSpec((tf, D), lambda i, j: (j, 0)),
              pl.BlockSpec((1, D), lambda i, j: (0, 0))]
    common_args = (norm_g.reshape(1, D), w1, w3, conv_w, conv_b.reshape(1, F), w2, fg)
    x_spec = pl.BlockSpec((tm, D), lambda i, j: (i, 0))
    scratch = [pltpu.VMEM((tm, D), BF16)]
    if seq_mode:
        assert seq_len % tm == 0 and tm % PREV_ROWS == 0
        r = tm // PREV_ROWS
        in_specs = [x_spec, pl.BlockSpec((PREV_ROWS, D), lambda i, j: (jnp.maximum(i * r - 1, 0), 0))] + common
        args = (x, x) + common_args
        scratch.append(pltpu.VMEM((PREV_ROWS, D), BF16))
        tiles_per_seq = seq_len // tm
    else:
        s_spec = pl.BlockSpec((tm, tf), lambda i, j: (i, j))
        in_specs = [x_spec, s_spec, s_spec] + common
        args = (x, state[0], state[1]) + common_args
        tiles_per_seq = 1
    return pl.pallas_call(
        functools.partial(_ffn_kernel, seq_mode=seq_mode, tiles_per_seq=tiles_per_seq, final_norm=final_norm),
        grid=(M // tm, F // tf),
        in_specs=in_specs,
        out_specs=pl.BlockSpec((tm, D), lambda i, j: (i, 0)),
        out_shape=jax.ShapeDtypeStruct((M, D), F32),
        scratch_shapes=scratch,
        compiler_params=_cparams(("parallel", "arbitrary")),
        name="conv_ffn",
    )(*args)


def _moba_prefill_kernel(q_ref, k_ref, v_ref, o_ref, kb_ref, vb_ref, kmean_ref, bias_ref, *, nb):
    qi = pl.program_id(2)
    blk = MOBA_BLOCK
    heads = LANES // DH_A

    @pl.when(qi == 0)
    def _():
        kmean_ref[...] = jnp.zeros_like(kmean_ref)
        for n in range(nb):
            kn = k_ref[n * blk:(n + 1) * blk, :]
            kmean_ref[n:n + 1, :] = jnp.sum(kn, axis=0, keepdims=True) * (1.0 / blk)
        kb_ref[...] = k_ref[...].astype(BF16)
        vb_ref[...] = v_ref[...].astype(BF16)

    q2 = q_ref[...]
    lane = lax.broadcasted_iota(jnp.int32, (1, LANES), 1)
    nrow = kmean_ref.shape[0]
    blk_id = lax.broadcasted_iota(jnp.int32, (nrow, 1), 0)
    r_id = lax.broadcasted_iota(jnp.int32, (blk, blk), 0)
    c_id = lax.broadcasted_iota(jnp.int32, (blk, blk), 1)
    own_start = pl.multiple_of(qi * blk, blk)
    out = jnp.zeros((blk, LANES), F32)
    for h in range(heads):
        hm = (lane // DH_A) == h
        qh = jnp.where(hm, q2, 0.0)
        gt = _nt(kmean_ref[...], qh, precision=HIGHEST)
        past = blk_id < qi
        sel_t = jnp.zeros_like(gt)
        for n in range(nb):
            gn = gt[n:n + 1, :]
            beats = jnp.logical_and(past, jnp.logical_or(gt > gn, jnp.logical_and(gt == gn, blk_id < n)))
            rank = jnp.sum(beats.astype(F32), axis=0, keepdims=True)
            sel_n = jnp.where(jnp.logical_and(rank < MOBA_TOPK, n < qi), 1.0, 0.0)
            sel_t = jnp.where(blk_id == n, sel_n, sel_t)
        sel = sel_t.T
        for n in range(nb - 1):
            bias_ref[n] = jnp.broadcast_to(jnp.where(sel[:, n:n + 1] > 0.5, 0.0, NEG_INF), (blk, LANES))

        qs = (qh * (DH_A ** -0.5)).astype(BF16)
        s = _nt(qs, kb_ref[pl.ds(own_start, blk), :])
        s = jnp.where(c_id <= r_id, s, NEG_INF)
        m0 = jnp.max(s, axis=-1, keepdims=True)
        p = jnp.exp(s - m0)
        l0 = jnp.sum(p, axis=-1, keepdims=True)
        acc0 = jnp.dot(p.astype(BF16), vb_ref[pl.ds(own_start, blk), :], preferred_element_type=F32)

        def body(n, carry):
            m, l, acc = carry
            start = pl.multiple_of(n * blk, blk)
            b = bias_ref[n]
            s = _nt(qs, kb_ref[pl.ds(start, blk), :]) + jnp.concatenate([b] * (blk // LANES), axis=1)
            m_new = jnp.maximum(m, jnp.max(s, axis=-1, keepdims=True))
            alpha = jnp.exp(m - m_new)
            p = jnp.exp(s - m_new)
            l = alpha * l + jnp.sum(p, axis=-1, keepdims=True)
            acc = alpha * acc + jnp.dot(p.astype(BF16), vb_ref[pl.ds(start, blk), :], preferred_element_type=F32)
            return m_new, l, acc

        _, l, acc = lax.fori_loop(0, qi, body, (m0, l0, acc0))
        out = jnp.where(hm, acc / l, out)
    o_ref[...] = out.astype(o_ref.dtype)


def _moba_prefill(z, B, T):
    blk = MOBA_BLOCK
    assert T % blk == 0
    nb = T // blk
    assert nb <= 8
    nq = T // blk
    cpt = AB_TILE // LANES
    return pl.pallas_call(
        functools.partial(_moba_prefill_kernel, nb=nb),
        grid=(B, W_A // LANES, nq),
        in_specs=[pl.BlockSpec((blk, LANES), lambda b, p, qi: (b * nq + qi, p)),
                  pl.BlockSpec((T, LANES), lambda b, p, qi: (b, cpt + p)),
                  pl.BlockSpec((T, LANES), lambda b, p, qi: (b, 2 * cpt + p))],
        out_specs=pl.BlockSpec((blk, LANES), lambda b, p, qi: (b * nq + qi, p)),
        out_shape=jax.ShapeDtypeStruct((B * T, W_A), BF16),
        scratch_shapes=[pltpu.VMEM((T, LANES), BF16), pltpu.VMEM((T, LANES), BF16),
                        pltpu.VMEM((8, LANES), F32), pltpu.VMEM((max(nb - 1, 1), blk, LANES), F32)],
        compiler_params=_cparams(("parallel", "parallel", "arbitrary")),
        name="moba_prefill",
    )(z, z, z)


def _ret_prefill_kernel(q_ref, k_ref, v_ref, gb_ref, rg_ref, dmask_ref, din_ref, dout_ref, gch_ref,
                        o_ref, s_ref, *, n_chunks):
    C = RET_CHUNK
    heads = LANES // DK_B
    lane = lax.broadcasted_iota(jnp.int32, (1, LANES), 1)
    row_h = lax.broadcasted_iota(jnp.int32, (LANES, LANES), 0) // DK_B
    col_h = lax.broadcasted_iota(jnp.int32, (LANES, LANES), 1) // DV_B
    same_head = row_h == col_h
    seg_ones = jnp.where(same_head, 1.0, 0.0).astype(F32)
    din = din_ref[...]
    dout = dout_ref[...]
    gch = gch_ref[...]
    rg = rg_ref[...]

    def chunk(j, S):
        start = pl.multiple_of(j * C, C)
        q = q_ref[pl.ds(start, C), :]
        k = k_ref[pl.ds(start, C), :]
        v = v_ref[pl.ds(start, C), :]
        kb = k.astype(BF16)
        vb = v.astype(BF16)
        o = jnp.dot((q * din).astype(BF16), S.astype(BF16), preferred_element_type=F32)
        for h in range(heads):
            hm = (lane // DK_B) == h
            att = _nt(jnp.where(hm, q, 0.0).astype(BF16), kb) * dmask_ref[h]
            oh = jnp.dot(att.astype(BF16), vb, preferred_element_type=F32)
            o = o + jnp.where(hm, oh, 0.0)
        S = S * gch + jnp.where(same_head, _tn((k * dout).astype(BF16), vb), 0.0)
        ms = jnp.dot(o * o, seg_ones, preferred_element_type=F32, precision=HIGHEST) * (1.0 / DV_B)
        g = gb_ref[pl.ds(start, C), :]
        y = o * lax.rsqrt(ms + EPS) * rg * (g * _sigmoid(g))
        o_ref[pl.ds(start, C), :] = y.astype(o_ref.dtype)
        return S

    s_ref[...] = lax.fori_loop(0, n_chunks, chunk, jnp.zeros((LANES, LANES), F32))


def _ret_tables(chunk):
    log_g = jnp.log1p(-jnp.exp2(-5.0 - jnp.arange(H_B, dtype=F32)))
    i = jnp.arange(chunk, dtype=F32)
    d_in = jnp.exp((i[:, None] + 1.0) * log_g)
    d_out = jnp.exp((chunk - 1.0 - i)[:, None] * log_g)
    diff = i[:, None] - i[None, :]
    d_mask = jnp.where(diff >= 0, jnp.exp(jnp.maximum(diff, 0.0)[None] * log_g[:, None, None]), 0.0)
    g_chunk = jnp.exp(chunk * log_g)
    return d_in, d_out, d_mask, g_chunk


def _ret_prefill(z, ret_g, B, T):
    C = RET_CHUNK
    assert T % C == 0
    d_in, d_out, d_mask, g_chunk = _ret_tables(C)
    npair = W_B // LANES
    lanes = lambda t: jnp.repeat(t, DK_B, axis=-1)
    din_l = lanes(d_in).reshape(C, npair, LANES).transpose(1, 0, 2)
    dout_l = lanes(d_out).reshape(C, npair, LANES).transpose(1, 0, 2)
    gch_l = lanes(g_chunk).reshape(npair, 1, LANES)
    cpt = AB_TILE // LANES
    col = lambda t: (lambda b, p: (b, t * cpt + p))
    return pl.pallas_call(
        functools.partial(_ret_prefill_kernel, n_chunks=T // C),
        grid=(B, npair),
        in_specs=[pl.BlockSpec((T, LANES), col(3)), pl.BlockSpec((T, LANES), col(4)),
                  pl.BlockSpec((T, LANES), col(5)), pl.BlockSpec((T, LANES), col(6)),
                  pl.BlockSpec((1, LANES), lambda b, p: (0, p)),
                  pl.BlockSpec((LANES // DK_B, C, C), lambda b, p: (p, 0, 0)),
                  pl.BlockSpec((None, C, LANES), lambda b, p: (p, 0, 0)),
                  pl.BlockSpec((None, C, LANES), lambda b, p: (p, 0, 0)),
                  pl.BlockSpec((None, 1, LANES), lambda b, p: (p, 0, 0))],
        out_specs=[pl.BlockSpec((T, LANES), lambda b, p: (b, p)),
                   pl.BlockSpec((None, None, LANES, LANES), lambda b, p: (b, p, 0, 0))],
        out_shape=[jax.ShapeDtypeStruct((B * T, W_B), BF16),
                   jax.ShapeDtypeStruct((B, npair, LANES, LANES), F32)],
        compiler_params=_cparams(("parallel", "parallel")),
        name="retention_prefill",
    )(z, z, z, z, ret_g.reshape(1, W_B), d_mask, din_l, dout_l, gch_l)


def _mlstm_prefill_kernel(bias_ref, q_ref, k_ref, v_ref, og_ref, ig_ref, fg_ref, ng_ref,
                          h_ref, c_ref, n_ref, m_ref, b_scr, i_scr, *, n_chunks):
    L = MLSTM_CHUNK
    hd = pl.program_id(1)
    eye = _eye(L)
    r_id = lax.broadcasted_iota(jnp.int32, (L, L), 0)
    c_id = lax.broadcasted_iota(jnp.int32, (L, L), 1)
    causal = c_id <= r_id
    upper = jnp.where(r_id <= c_id, 1.0, 0.0).astype(F32)
    i_scr[...] = ig_ref[...] + bias_ref[0, hd]
    lf = _log_sigmoid(fg_ref[...] + bias_ref[1, hd])
    b_scr[...] = jnp.dot(lf, upper, preferred_element_type=F32, precision=HIGHEST)
    ng = ng_ref[...]

    def chunk(j, carry):
        Cm, n, m = carry
        start = pl.multiple_of(j * L, L)
        q = q_ref[pl.ds(start, L), :]
        k = k_ref[pl.ds(start, L), :] * (DH_C ** -0.5)
        v = v_ref[pl.ds(start, L), :]
        b_row = b_scr[pl.ds(j, 1), :]
        i_row = i_scr[pl.ds(j, 1), :]
        b_col = _row_to_col(b_row, eye)
        log_d = jnp.where(causal, (b_col - b_row) + i_row, NEG_INF)
        log_inter = b_col + m
        m_t = jnp.maximum(log_inter, jnp.max(log_d, axis=-1, keepdims=True))
        w_intra = jnp.exp(log_d - m_t)
        w_inter = jnp.exp(log_inter - m_t)
        qb = q.astype(BF16)
        kb = k.astype(BF16)
        s = _nt(qb, kb) * w_intra
        num = (jnp.dot(s.astype(BF16), v.astype(BF16), preferred_element_type=F32)
               + _nt(qb, Cm.astype(BF16)) * w_inter)
        den = jnp.sum(s, axis=-1, keepdims=True) + w_inter * jnp.sum(q * n, axis=-1, keepdims=True)
        h = num / jnp.maximum(jnp.abs(den), jnp.exp(-m_t))
        b_last = b_row[:, L - 1:L]
        log_w = (b_last - b_row) + i_row
        m_new = jnp.maximum(b_last + m, jnp.max(log_w, axis=-1, keepdims=True))
        w_k = _row_to_col(jnp.exp(log_w - m_new), eye)
        decay = jnp.exp(b_last + m - m_new)
        Cm = decay * Cm + _tn((v * w_k).astype(BF16), kb)
        n = decay * n + jnp.sum(k * w_k, axis=0, keepdims=True)
        og = og_ref[pl.ds(start, L), :]
        hn = h * lax.rsqrt(jnp.mean(h * h, axis=-1, keepdims=True) + EPS) * ng
        h_ref[pl.ds(start, L), :] = (hn * _sigmoid(og)).astype(h_ref.dtype)
        return Cm, n, m_new

    init = (jnp.zeros((L, L), F32), jnp.zeros((1, L), F32), jnp.zeros((1, 1), F32))
    Cm, n, m = lax.fori_loop(0, n_chunks, chunk, init)
    c_ref[...] = Cm
    n_ref[...] = n
    m_ref[...] = jnp.broadcast_to(m, m_ref.shape)


def _mlstm_prefill(z, ig, fg, gate_bias, out_g, B, T):
    L = MLSTM_CHUNK
    assert T % L == 0 and DH_C == LANES
    nc = T // L
    col = lambda t: (lambda b, h, bias: (b, t * H_C + h))
    gspec = pl.BlockSpec((None, None, nc, L), lambda b, h, bias: (b, h, 0, 0))
    grid_spec = pltpu.PrefetchScalarGridSpec(
        num_scalar_prefetch=1,
        grid=(B, H_C),
        in_specs=[pl.BlockSpec((T, LANES), col(0)), pl.BlockSpec((T, LANES), col(1)),
                  pl.BlockSpec((T, LANES), col(2)), pl.BlockSpec((T, LANES), col(3)),
                  gspec, gspec,
                  pl.BlockSpec((1, LANES), lambda b, h, bias: (0, h))],
        out_specs=[pl.BlockSpec((T, LANES), lambda b, h, bias: (b, h)),
                   pl.BlockSpec((None, None, L, L), lambda b, h, bias: (b, h, 0, 0)),
                   pl.BlockSpec((None, None, 1, L), lambda b, h, bias: (b, h, 0, 0)),
                   pl.BlockSpec((None, None, 1, LANES), lambda b, h, bias: (b, h, 0, 0))],
        scratch_shapes=[pltpu.VMEM((nc, L), F32), pltpu.VMEM((nc, L), F32)],
    )
    return pl.pallas_call(
        functools.partial(_mlstm_prefill_kernel, n_chunks=nc),
        grid_spec=grid_spec,
        out_shape=[jax.ShapeDtypeStruct((B * T, W_C), BF16),
                   jax.ShapeDtypeStruct((B, H_C, L, L), F32),
                   jax.ShapeDtypeStruct((B, H_C, 1, L), F32),
                   jax.ShapeDtypeStruct((B, H_C, 1, LANES), F32)],
        compiler_params=_cparams(("parallel", "parallel")),
        name="mlstm_prefill",
    )(gate_bias, z, z, z, z, ig, fg, out_g.reshape(1, W_C))


def _moba_select_kernel(pt_ref, k_ref, q_ref, sel_ref, ksum_ref, *, pages_per_block):
    p = pl.program_id(1)
    n_pages = pl.num_programs(1)
    n = p // pages_per_block
    page_sum = jnp.sum(k_ref[...], axis=0)

    @pl.when(p % pages_per_block == 0)
    def _():
        ksum_ref[n] = page_sum

    @pl.when(p % pages_per_block != 0)
    def _():
        ksum_ref[n] = ksum_ref[n] + page_sum

    @pl.when(p == n_pages - 1)
    def _():
        nb = ksum_ref.shape[0]
        kmean = ksum_ref[...] * (1.0 / MOBA_BLOCK)
        gate = jnp.sum(kmean * q_ref[...][None], axis=-1)
        blk_id = lax.broadcasted_iota(jnp.int32, (nb, 1), 0)
        rank = jnp.zeros(gate.shape, jnp.int32)
        for m in range(nb):
            gm = gate[m:m + 1, :]
            beats = jnp.logical_or(gm > gate, jnp.logical_and(gm == gate, m < blk_id))
            rank = rank + beats.astype(jnp.int32)
        out_row = lax.broadcasted_iota(jnp.int32, sel_ref.shape, 0)
        out = jnp.zeros(sel_ref.shape, jnp.int32)
        for r in range(MOBA_TOPK):
            idx = jnp.sum(jnp.where(rank == r, blk_id, 0), axis=0, keepdims=True)
            out = jnp.where(out_row == r, idx, out)
        sel_ref[...] = out


def _moba_select(cache_k, page_table, q):
    Bs, n_pages = page_table.shape
    _, page, H, dh = cache_k.shape
    ppb = MOBA_BLOCK // page
    nb = n_pages // ppb
    assert nb >= MOBA_TOPK
    grid_spec = pltpu.PrefetchScalarGridSpec(
        num_scalar_prefetch=1,
        grid=(Bs, n_pages),
        in_specs=[pl.BlockSpec((None, page, H, dh), lambda b, p, pt: (pt[b, p], 0, 0, 0)),
                  pl.BlockSpec((None, H, dh), lambda b, p, pt: (b, 0, 0))],
        out_specs=pl.BlockSpec((None, 8, H), lambda b, p, pt: (b, 0, 0)),
        scratch_shapes=[pltpu.VMEM((nb, H, dh), F32)],
    )
    return pl.pallas_call(
        functools.partial(_moba_select_kernel, pages_per_block=ppb),
        grid_spec=grid_spec,
        out_shape=jax.ShapeDtypeStruct((Bs, 8, H), jnp.int32),
        compiler_params=_cparams(("parallel", "arbitrary")),
        name="moba_select",
    )(page_table, cache_k, q)


def _moba_decode_kernel(pt_ref, sel_ref, q_ref, kn_ref, vn_ref, ck_ref, cv_ref, o_ref, kbuf, vbuf, sem,
                        *, pages_per_block, page):
    b = pl.program_id(0)
    nb_ = pl.num_programs(0)
    H = q_ref.shape[0]
    ppb = pages_per_block

    def copies(bb, slot):
        out = []
        for h in range(H):
            for r in range(MOBA_TOPK):
                blk = sel_ref[bb, r, h]
                for g in range(ppb):
                    pg = pt_ref[bb, blk * ppb + g]
                    dst = pl.ds((r * ppb + g) * page, page)
                    out.append(pltpu.make_async_copy(ck_ref.at[pg, :, h, :], kbuf.at[slot, h, dst, :], sem.at[slot, 0]))
                    out.append(pltpu.make_async_copy(cv_ref.at[pg, :, h, :], vbuf.at[slot, h, dst, :], sem.at[slot, 1]))
        return out

    slot = b % 2

    @pl.when(b == 0)
    def _():
        for c in copies(b, slot):
            c.start()

    @pl.when(b + 1 < nb_)
    def _():
        for c in copies(b + 1, 1 - slot):
            c.start()

    for c in copies(b, slot):
        c.wait()

    row = lax.broadcasted_iota(jnp.int32, (H, 1), 0)
    out = jnp.zeros(o_ref.shape, F32)
    q = q_ref[...]
    kn = kn_ref[...]
    vn = vn_ref[...]
    for h in range(H):
        qh = q[h:h + 1, :] * (DH_A ** -0.5)
        q8 = jnp.broadcast_to(qh, (8, qh.shape[1])).astype(BF16)
        kh = kbuf[slot, h].astype(BF16)
        vh = vbuf[slot, h].astype(BF16)
        s = _nt(q8, kh)[0:1, :]
        s_self = jnp.sum(qh * kn[h:h + 1, :], axis=-1, keepdims=True)
        m = jnp.maximum(jnp.max(s, axis=-1, keepdims=True), s_self)
        p = jnp.exp(s - m)
        p_self = jnp.exp(s_self - m)
        l = jnp.sum(p, axis=-1, keepdims=True) + p_self
        p8 = jnp.broadcast_to(p, (8, p.shape[1])).astype(BF16)
        pv = jnp.dot(p8, vh, preferred_element_type=F32)[0:1, :]
        oh = (pv + p_self * vn[h:h + 1, :]) / l
        out = jnp.where(row == h, oh, out)
    o_ref[...] = out


def _moba_decode(cache_k, cache_v, page_table, sel, q, k_new, v_new):
    Bs, n_pages = page_table.shape
    _, page, H, dh = cache_k.shape
    ppb = MOBA_BLOCK // page
    rows = MOBA_TOPK * MOBA_BLOCK
    vec = pl.BlockSpec((None, H, dh), lambda b, pt, sl: (b, 0, 0))
    grid_spec = pltpu.PrefetchScalarGridSpec(
        num_scalar_prefetch=2,
        grid=(Bs,),
        in_specs=[vec, vec, vec, pl.BlockSpec(memory_space=pl.ANY), pl.BlockSpec(memory_space=pl.ANY)],
        out_specs=vec,
        scratch_shapes=[pltpu.VMEM((2, H, rows, dh), F32), pltpu.VMEM((2, H, rows, dh), F32),
                        pltpu.SemaphoreType.DMA((2, 2))],
    )
    return pl.pallas_call(
        functools.partial(_moba_decode_kernel, pages_per_block=ppb, page=page),
        grid_spec=grid_spec,
        out_shape=jax.ShapeDtypeStruct((Bs, H, dh), F32),
        compiler_params=_cparams(("arbitrary",)),
        name="moba_decode",
    )(page_table, sel, q, k_new, v_new, cache_k, cache_v)


def _ret_decode_kernel(q_ref, k_ref, v_ref, gb_ref, rg_ref, g_ref, s_ref, o_ref, so_ref):
    H, dk = q_ref.shape
    eye = _eye(dk)
    row = lax.broadcasted_iota(jnp.int32, (H, 1), 0)
    q = q_ref[...]
    k = k_ref[...]
    v = v_ref[...]
    g_all = g_ref[...]
    out = jnp.zeros(o_ref.shape, F32)
    for h in range(H):
        qh, kh, vh = q[h:h + 1, :], k[h:h + 1, :], v[h:h + 1, :]
        g = g_all[h:h + 1, 0:1]
        S = s_ref[h]
        att = jnp.sum(qh * kh, axis=-1, keepdims=True)
        cross = jnp.sum(_row_to_col(qh * g, eye) * S, axis=0, keepdims=True)
        out = jnp.where(row == h, att * vh + cross, out)
        so_ref[h] = S * g + _row_to_col(kh, eye) * vh
    gb = gb_ref[...]
    y = out * lax.rsqrt(jnp.mean(out * out, axis=-1, keepdims=True) + EPS) * rg_ref[...]
    o_ref[...] = y * (gb * _sigmoid(gb))


def _ret_decode(q, k, v, gb, ret_g, state):
    Bs, H, dk = q.shape
    log_g = jnp.log1p(-jnp.exp2(-5.0 - jnp.arange(H_B, dtype=F32)))
    g = jnp.broadcast_to(jnp.exp(1.0 * log_g)[:, None], (H, LANES))
    vec = pl.BlockSpec((None, H, dk), lambda b: (b, 0, 0))
    st = pl.BlockSpec((None, H, dk, DV_B), lambda b: (b, 0, 0, 0))
    return pl.pallas_call(
        _ret_decode_kernel,
        grid=(Bs,),
        in_specs=[vec, vec, vec, vec, pl.BlockSpec((H, dk), lambda b: (0, 0)),
                  pl.BlockSpec((H, LANES), lambda b: (0, 0)), st],
        out_specs=[vec, st],
        out_shape=[jax.ShapeDtypeStruct((Bs, H, DV_B), F32), jax.ShapeDtypeStruct(state.shape, F32)],
        compiler_params=_cparams(("parallel",)),
        name="retention_decode",
    )(q, k, v, gb, ret_g.reshape(H, DV_B), g, state)


def _mlstm_decode_kernel(q_ref, k_ref, v_ref, og_ref, ig_ref, fg_ref, bi_ref, bf_ref, ng_ref, c_ref, n_ref,
                         m_ref, h_ref, co_ref, no_ref, mo_ref):
    H, d = q_ref.shape
    eye = _eye(d)
    row = lax.broadcasted_iota(jnp.int32, (H, 1), 0)
    q = q_ref[...]
    k = k_ref[...] * (DH_C ** -0.5)
    v = v_ref[...]
    n0 = n_ref[...]
    ig = ig_ref[...] + bi_ref[...]
    b = _log_sigmoid(fg_ref[...] + bf_ref[...])
    m0 = m_ref[...]
    log_inter = b + m0
    m_t = jnp.maximum(log_inter, ig)
    w_intra = jnp.exp(ig - m_t)
    w_inter = jnp.exp(log_inter - m_t)
    s = jnp.sum(q * k, axis=-1, keepdims=True) * w_intra
    den = s + w_inter * jnp.sum(q * n0, axis=-1, keepdims=True)
    scale = 1.0 / jnp.maximum(jnp.abs(den), jnp.exp(-m_t))
    w_k = w_intra
    decay = w_inter
    hs = jnp.zeros(h_ref.shape, F32)
    for h in range(H):
        qh, kh, vh = q[h:h + 1, :], k[h:h + 1, :], v[h:h + 1, :]
        Cm = c_ref[h]
        cq = _col_to_row(jnp.sum(Cm * qh, axis=1, keepdims=True), eye)
        num = s[h:h + 1, :] * vh + cq * w_inter[h:h + 1, :]
        hs = jnp.where(row == h, num * scale[h:h + 1, :], hs)
        co_ref[h] = decay[h:h + 1, :] * Cm + _row_to_col(vh * w_k[h:h + 1, :], eye) * kh
    no_ref[...] = decay * n0 + k * w_k
    mo_ref[...] = m_t
    hn = hs * lax.rsqrt(jnp.mean(hs * hs, axis=-1, keepdims=True) + EPS) * ng_ref[...]
    h_ref[...] = hn * _sigmoid(og_ref[...])


def _mlstm_decode(q, k, v, og, ig, fg, gate_bias, out_g, C0, n0, m0):
    Bs, H, d = q.shape
    vec = pl.BlockSpec((None, H, d), lambda b: (b, 0, 0))
    sc = pl.BlockSpec((None, H, 1), lambda b: (b, 0, 0))
    st = pl.BlockSpec((None, H, d, d), lambda b: (b, 0, 0, 0))
    bias = pl.BlockSpec((H, 1), lambda b: (0, 0))
    return pl.pallas_call(
        _mlstm_decode_kernel,
        grid=(Bs,),
        in_specs=[vec, vec, vec, vec, sc, sc, bias, bias, pl.BlockSpec((H, d), lambda b: (0, 0)), st, vec, sc],
        out_specs=[vec, st, vec, sc],
        out_shape=[jax.ShapeDtypeStruct((Bs, H, d), F32), jax.ShapeDtypeStruct(C0.shape, F32),
                   jax.ShapeDtypeStruct((Bs, H, d), F32), jax.ShapeDtypeStruct((Bs, H, 1), F32)],
        compiler_params=_cparams(("parallel",)),
        name="mlstm_decode",
    )(q, k, v, og, ig, fg, gate_bias[0].reshape(H, 1), gate_bias[1].reshape(H, 1), out_g.reshape(H, d), C0, n0, m0)


def _rope_tables(pos):
    half = DH_A // 2
    inv = ROPE_THETA ** (-jnp.arange(half, dtype=F32) / half)
    ang = pos[:, None] * inv[None, :]
    cos = jnp.cos(ang)
    sin = jnp.sin(ang)
    reps = LANES // DH_A
    cos_t = jnp.tile(jnp.concatenate([cos, cos], axis=-1), (1, reps))
    sin_t = jnp.tile(jnp.concatenate([-sin, sin], axis=-1), (1, reps))
    return cos_t, sin_t


def _row_tile(m, pref):
    return pref if m % pref == 0 else m


def kernel(x_prompt, x_sample, cache_k, cache_v, page_table, state_ret, state_mlstm_C, state_mlstm_n,
           state_mlstm_m, state_ffn_conv, ab_norm_g, ab_w_in, ab_ret_norm_g, ab_w_out, c_norm_g, c_w_in,
           c_gate_bias, c_out_norm_g, c_w_out, ffn_norm_g, ffn_w1, ffn_w3, ffn_conv_w, ffn_conv_b, ffn_w2,
           final_norm_g):
    Bp, Tp, D = x_prompt.shape
    Bs, Ts, _ = x_sample.shape
    assert Ts == 1
    past_len = page_table.shape[1] * cache_k.shape[2]
    Mp = Bp * Tp
    xp = x_prompt.reshape(Mp, D)
    xs = x_sample.reshape(Bs, D)
    tm_p = _row_tile(Tp, 1024)
    tm_s = Bs
    tf = 256
    rope_p = _rope_tables(jnp.arange(Tp, dtype=jnp.int32).astype(F32))
    rope_s = _rope_tables(jnp.full((Bs,), past_len, jnp.int32).astype(F32))
    gw = 2 * H_C

    outs = {}

    w_in = ab_w_in[0].astype(BF16)
    w_out = ab_w_out[0].astype(BF16)
    w_out_a, w_out_b = w_out[:W_A], w_out[W_A:]

    z = _norm_matmul(xp, ab_norm_g[0], w_in, tm_p, AB_TILE, rope=rope_p)
    oa = _moba_prefill(z, Bp, Tp)
    ob, s_pair = _ret_prefill(z, ab_ret_norm_g[0], Bp, Tp)
    xp = _proj_residual([oa, ob], [w_out_a, w_out_b], xp, tm_p)
    outs["k_prompt"] = z[:, W_A:2 * W_A].reshape(1, Bp, Tp, H_A, DH_A)
    outs["v_prompt"] = z[:, 2 * W_A:3 * W_A].reshape(1, Bp, Tp, H_A, DH_A)
    hb = LANES // DK_B
    s_heads = jnp.stack([s_pair[:, :, i * DK_B:(i + 1) * DK_B, i * DV_B:(i + 1) * DV_B] for i in range(hb)], axis=2)
    outs["ret_prompt"] = s_heads.reshape(1, Bp, H_B, DK_B, DV_B)

    zs = _norm_matmul(xs, ab_norm_g[0], w_in, tm_s, AB_TILE, rope=rope_s)
    seg = lambda t: zs[:, t * AB_TILE:(t + 1) * AB_TILE].reshape(Bs, H_A, DH_A)
    qa_s, ka_s, va_s, qb_s, kb_s, vb_s, gb_s = (seg(t) for t in range(7))
    sel = _moba_select(cache_k[0], page_table, qa_s)
    oa_s = _moba_decode(cache_k[0], cache_v[0], page_table, sel, qa_s, ka_s, va_s)
    ob_s, s_new = _ret_decode(qb_s, kb_s, vb_s, gb_s, ab_ret_norm_g[0], state_ret[0])
    xs = _proj_residual([oa_s.reshape(Bs, W_A).astype(BF16), ob_s.reshape(Bs, W_B).astype(BF16)],
                        [w_out_a, w_out_b], xs, tm_s)
    outs["k_sample"] = ka_s.reshape(1, Bs, 1, H_A, DH_A)
    outs["v_sample"] = va_s.reshape(1, Bs, 1, H_A, DH_A)
    outs["ret_sample"] = s_new[None]

    conv_p, conv_s = [], []

    def ffn_both(l, xp, xs, final_g):
        w1 = ffn_w1[l].astype(BF16)
        w3 = ffn_w3[l].astype(BF16)
        w2 = ffn_w2[l].astype(BF16)
        tail = xp.reshape(Bp, Tp, D)[:, Tp - 8:, :].reshape(Bp * 8, D)
        a_tail = _norm_matmul(tail, ffn_norm_g[l], w1, Bp * 8, tf)
        conv_p.append(a_tail.reshape(Bp, 8, D_FF)[:, 8 - (CONV_W - 1):, :])
        a_s = _norm_matmul(xs, ffn_norm_g[l], w1, tm_s, tf)
        st = state_ffn_conv[l]
        conv_s.append(jnp.stack([st[:, 1, :], a_s], axis=1))
        xp = _ffn(xp, ffn_norm_g[l], w1, w3, ffn_conv_w[l], ffn_conv_b[l], w2, final_g, tm_p, tf, seq_len=Tp)
        xs = _ffn(xs, ffn_norm_g[l], w1, w3, ffn_conv_w[l], ffn_conv_b[l], w2, final_g, tm_s, tf,
                  state=(st[:, 0, :], st[:, 1, :]))
        return xp, xs

    xp, xs = ffn_both(0, xp, xs, None)

    w_in = c_w_in[0]
    w_main = w_in[:, :4 * W_C].astype(BF16)
    w_gate = jnp.pad(w_in[:, 4 * W_C:], ((0, 0), (0, LANES - gw))).astype(BF16)
    w_out = c_w_out[0].astype(BF16)
    L = MLSTM_CHUNK

    z = _norm_matmul(xp, c_norm_g[0], w_main, tm_p, 512)
    zg = _norm_matmul(xp, c_norm_g[0], w_gate, tm_p, LANES)
    gates = zg[:, :gw].reshape(Bp, Tp // L, L, 2, H_C).transpose(3, 0, 4, 1, 2)
    h, C_p, n_p, m_p = _mlstm_prefill(z, gates[0], gates[1], c_gate_bias[0], c_out_norm_g[0], Bp, Tp)
    xp = _proj_residual([h], [w_out], xp, tm_p)
    outs["C_prompt"] = C_p[None]
    outs["n_prompt"] = n_p.reshape(1, Bp, H_C, DH_C)
    outs["m_prompt"] = m_p[:, :, 0, 0][None]

    zs = _norm_matmul(xs, c_norm_g[0], w_main, tm_s, 512)
    zgs = _norm_matmul(xs, c_norm_g[0], w_gate, tm_s, LANES)
    segc = lambda t: zs[:, t * W_C:(t + 1) * W_C].reshape(Bs, H_C, DH_C)
    ig_s = zgs[:, :H_C][:, :, None]
    fg_s = zgs[:, H_C:gw][:, :, None]
    h_s, C_s, n_s, m_s = _mlstm_decode(segc(0), segc(1), segc(2), segc(3), ig_s, fg_s, c_gate_bias[0], c_out_norm_g[0],
                                       state_mlstm_C[0], state_mlstm_n[0], state_mlstm_m[0][:, :, None])
    xs = _proj_residual([h_s.reshape(Bs, W_C).astype(BF16)], [w_out], xs, tm_s)
    outs["C_sample"] = C_s[None]
    outs["n_sample"] = n_s[None]
    outs["m_sample"] = m_s[:, :, 0][None]

    xp, xs = ffn_both(1, xp, xs, final_norm_g)

    return (xp.reshape(Bp, Tp, D), xs.reshape(Bs, 1, D),
            outs["k_prompt"], outs["v_prompt"], outs["k_sample"], outs["v_sample"],
            outs["ret_prompt"], outs["ret_sample"], outs["C_prompt"], outs["C_sample"],
            outs["n_prompt"], outs["n_sample"], outs["m_prompt"], outs["m_sample"],
            jnp.stack(conv_p), jnp.stack(conv_s))
```

```python
import functools

import jax
import jax.numpy as jnp
from jax import lax
from jax.experimental import pallas as pl
from jax.experimental.pallas import tpu as pltpu

F32 = jnp.float32
BF16 = jnp.bfloat16
HIGHEST = lax.Precision.HIGHEST

LANES = 128
D_MODEL = 1024
H_A = 8
DH_A = 64
MOBA_BLOCK = 256
MOBA_TOPK = 3
H_B = 8
DK_B = 64
DV_B = 64
RET_CHUNK = 128
H_C = 8
DH_C = D_MODEL // H_C
MLSTM_CHUNK = 128
D_FF = 11 * D_MODEL // 4
CONV_W = 3
ROPE_THETA = 10000.0
EPS = 1e-6
W_A = H_A * DH_A
W_B = H_B * DV_B
W_C = H_C * DH_C
IN_AB = 3 * W_A + 2 * H_B * DK_B + 2 * W_B
AB_TILE = 512
AB_ROPE_TILES = (0, 1, 3, 4)
AB_KB_TILE = 4
VMEM_LIMIT = 56 * 1024 * 1024

NEG_INF = float("-inf")


def _cparams(sem):
    return pltpu.CompilerParams(dimension_semantics=sem, vmem_limit_bytes=VMEM_LIMIT)


def _nt(a, b, **kw):
    return lax.dot_general(a, b, (((1,), (1,)), ((), ())), preferred_element_type=F32, **kw)


def _tn(a, b, **kw):
    return lax.dot_general(a, b, (((0,), (0,)), ((), ())), preferred_element_type=F32, **kw)


def _rms_rows(x, g):
    ms = jnp.mean(x * x, axis=-1, keepdims=True)
    return x * lax.rsqrt(ms + EPS) * g


def _eye(n):
    return lax.broadcasted_iota(jnp.int32, (n, n), 0) == lax.broadcasted_iota(jnp.int32, (n, n), 1)


def _row_to_col(row, eye):
    return jnp.sum(jnp.where(eye, row, 0.0), axis=1, keepdims=True)


def _col_to_row(col, eye):
    return jnp.sum(jnp.where(eye, col, 0.0), axis=0, keepdims=True)


def _log_sigmoid(x):
    return jnp.minimum(x, 0.0) - jnp.log1p(jnp.exp(-jnp.abs(x)))


def _sigmoid(x):
    return 1.0 / (1.0 + jnp.exp(-x))


def _gelu_tanh(x):
    c = 0.7978845608028654
    return 0.5 * x * (1.0 + jnp.tanh(c * (x + 0.044715 * (x * x * x))))


def _norm_matmul_kernel(x_ref, g_ref, w_ref, cos_ref, sin_ref, o_ref, xn_ref, *, rope_tiles, scale_tile, scale):
    j = pl.program_id(1)

    @pl.when(j == 0)
    def _():
        xn_ref[...] = _rms_rows(x_ref[...], g_ref[...]).astype(BF16)

    z = jnp.dot(xn_ref[...], w_ref[...], preferred_element_type=F32)
    if not rope_tiles:
        o_ref[...] = z
        return

    is_rope = functools.reduce(jnp.logical_or, [j == t for t in rope_tiles])

    @pl.when(is_rope)
    def _():
        sc = jnp.where(j == scale_tile, scale, 1.0).astype(F32)
        lane = lax.broadcasted_iota(jnp.int32, (1, LANES), 1)
        first_half = (lane % DH_A) < (DH_A // 2)
        cos = cos_ref[...]
        sin = sin_ref[...]
        for c in range(z.shape[1] // LANES):
            zc = z[:, c * LANES:(c + 1) * LANES]
            partner = jnp.where(first_half, pltpu.roll(zc, LANES - DH_A // 2, 1), pltpu.roll(zc, DH_A // 2, 1))
            o_ref[:, c * LANES:(c + 1) * LANES] = (zc * cos + partner * sin) * sc

    @pl.when(jnp.logical_not(is_rope))
    def _():
        o_ref[...] = z


def _norm_matmul(x, g, w, tm, tn, rope=None):
    M, D = x.shape
    N = w.shape[1]
    assert M % tm == 0 and N % tn == 0
    if rope is None:
        cos = sin = jnp.zeros((8, LANES), F32)
        tab_spec = pl.BlockSpec((8, LANES), lambda i, j: (0, 0))
        kern = functools.partial(_norm_matmul_kernel, rope_tiles=(), scale_tile=-1, scale=1.0)
    else:
        cos, sin = rope
        nt = cos.shape[0] // tm
        tab_spec = pl.BlockSpec((tm, LANES), lambda i, j: (i % nt, 0))
        kern = functools.partial(_norm_matmul_kernel, rope_tiles=AB_ROPE_TILES, scale_tile=AB_KB_TILE,
                                 scale=DK_B ** -0.5)
    return pl.pallas_call(
        kern,
        grid=(M // tm, N // tn),
        in_specs=[pl.BlockSpec((tm, D), lambda i, j: (i, 0)),
                  pl.BlockSpec((1, D), lambda i, j: (0, 0)),
                  pl.BlockSpec((D, tn), lambda i, j: (0, j)),
                  tab_spec, tab_spec],
        out_specs=pl.BlockSpec((tm, tn), lambda i, j: (i, j)),
        out_shape=jax.ShapeDtypeStruct((M, N), F32),
        scratch_shapes=[pltpu.VMEM((tm, D), BF16)],
        compiler_params=_cparams(("parallel", "arbitrary")),
        name="norm_matmul",
    )(x, g.reshape(1, D), w, cos, sin)


def _proj_residual_kernel(*refs, n_in):
    a_refs = refs[:n_in]
    w_refs = refs[n_in:2 * n_in]
    res_ref = refs[2 * n_in]
    o_ref = refs[2 * n_in + 1]
    y = res_ref[...]
    acc = None
    for a_ref, w_ref in zip(a_refs, w_refs):
        d = jnp.dot(a_ref[...], w_ref[...], preferred_element_type=F32)
        acc = d if acc is None else acc + d
    o_ref[...] = y + acc


def _proj_residual(acts, ws, res, tm):
    M, D = res.shape
    n_in = len(acts)
    in_specs = ([pl.BlockSpec((tm, a.shape[1]), lambda i: (i, 0)) for a in acts]
                + [pl.BlockSpec(w.shape, lambda i: (0, 0)) for w in ws]
                + [pl.BlockSpec((tm, D), lambda i: (i, 0))])
    return pl.pallas_call(
        functools.partial(_proj_residual_kernel, n_in=n_in),
        grid=(M // tm,),
        in_specs=in_specs,
        out_specs=pl.BlockSpec((tm, D), lambda i: (i, 0)),
        out_shape=jax.ShapeDtypeStruct((M, D), F32),
        compiler_params=_cparams(("parallel",)),
        name="proj_residual",
    )(*acts, *ws, res)


PREV_ROWS = 16


def _ffn_kernel(*refs, seq_mode, tiles_per_seq, final_norm):
    if seq_mode:
        (x_ref, xp_ref, g_ref, w1_ref, w3_ref, cw_ref, cb_ref, w2_ref, fg_ref, o_ref, xn_ref, xpn_ref) = refs
    else:
        (x_ref, s0_ref, s1_ref, g_ref, w1_ref, w3_ref, cw_ref, cb_ref, w2_ref, fg_ref, o_ref, xn_ref) = refs
    i = pl.program_id(0)
    j = pl.program_id(1)
    nj = pl.num_programs(1)

    @pl.when(j == 0)
    def _():
        xn_ref[...] = _rms_rows(x_ref[...], g_ref[...]).astype(BF16)
        if seq_mode:
            xpn_ref[...] = _rms_rows(xp_ref[...], g_ref[...]).astype(BF16)
        o_ref[...] = jnp.zeros_like(o_ref)

    xn = xn_ref[...]
    a = jnp.dot(xn, w1_ref[...], preferred_element_type=F32)
    gate = jnp.dot(xn, w3_ref[...], preferred_element_type=F32)
    if seq_mode:
        tm = a.shape[0]
        ap = jnp.dot(xpn_ref[...], w1_ref[...], preferred_element_type=F32)
        has_prev = ((i % tiles_per_seq) != 0).astype(F32)
        p1 = ap[PREV_ROWS - 1:PREV_ROWS, :] * has_prev
        p2 = ap[PREV_ROWS - 2:PREV_ROWS - 1, :] * has_prev
        row = lax.broadcasted_iota(jnp.int32, (tm, 1), 0)
        a1 = jnp.where(row == 0, p1, pltpu.roll(a, 1, 0))
        a2 = jnp.where(row == 0, p2, jnp.where(row == 1, p1, pltpu.roll(a, 2, 0)))
    else:
        a1 = s1_ref[...]
        a2 = s0_ref[...]
    cw = cw_ref[...]
    ac = cb_ref[...] + a2 * cw[0:1, :]
    ac = ac + a1 * cw[1:2, :]
    ac = ac + a * cw[2:3, :]
    y = (_gelu_tanh(ac) * gate).astype(BF16)
    o_ref[...] += jnp.dot(y, w2_ref[...], preferred_element_type=F32)

    @pl.when(j == nj - 1)
    def _():
        r = x_ref[...] + o_ref[...]
        if final_norm:
            r = _rms_rows(r, fg_ref[...])
        o_ref[...] = r


def _ffn(x, norm_g, w1, w3, conv_w, conv_b, w2, final_g, tm, tf, seq_len=None, state=None):
    M, D = x.shape
    F = w1.shape[1]
    seq_mode = state is None
    final_norm = final_g is not None
    fg = (final_g if final_norm else jnp.ones((D,), F32)).reshape(1, D)
    common = [pl.BlockSpec((1, D), lambda i, j: (0, 0)),
              pl.BlockSpec((D, tf), lambda i, j: (0, j)),
              pl.BlockSpec((D, tf), lambda i, j: (0, j)),
              pl.BlockSpec((CONV_W, tf), lambda i, j: (0, j)),
              pl.BlockSpec((1, tf), lambda i, j: (0, j)),
              pl.BlockSpec((tf, D), lambda i, j: (j, 0)),
              pl.BlockSpec((1, D), lambda i, j: (0, 0))]
    common_args = (norm_g.reshape(1, D), w1, w3, conv_w, conv_b.reshape(1, F), w2, fg)
    x_spec = pl.BlockSpec((tm, D), lambda i, j: (i, 0))
    scratch = [pltpu.VMEM((tm, D), BF16)]
    if seq_mode:
        assert seq_len % tm == 0 and tm % PREV_ROWS == 0
        r = tm // PREV_ROWS
        in_specs = [x_spec, pl.BlockSpec((PREV_ROWS, D), lambda i, j: (jnp.maximum(i * r - 1, 0), 0))] + common
        args = (x, x) + common_args
        scratch.append(pltpu.VMEM((PREV_ROWS, D), BF16))
        tiles_per_seq = seq_len // tm
    else:
        s_spec = pl.BlockSpec((tm, tf), lambda i, j: (i, j))
        in_specs = [x_spec, s_spec, s_spec] + common
        args = (x, state[0], state[1]) + common_args
        tiles_per_seq = 1
    return pl.pallas_call(
        functools.partial(_ffn_kernel, seq_mode=seq_mode, tiles_per_seq=tiles_per_seq, final_norm=final_norm),
        grid=(M // tm, F // tf),
        in_specs=in_specs,
        out_specs=pl.BlockSpec((tm, D), lambda i, j: (i, 0)),
        out_shape=jax.ShapeDtypeStruct((M, D), F32),
        scratch_shapes=scratch,
        compiler_params=_cparams(("parallel", "arbitrary")),
        name="conv_ffn",
    )(*args)


def _moba_prefill_kernel(q_ref, k_ref, v_ref, o_ref, kt_ref, vt_ref, kb_ref, vb_ref, kmean_ref, bias_ref, *, nb):
    qi = pl.program_id(2)
    blk = MOBA_BLOCK
    heads = LANES // DH_A

    @pl.when(qi == 0)
    def _():
        kt_ref[...] = k_ref[...].T
        vt_ref[...] = v_ref[...].T
        kmean_ref[...] = jnp.zeros_like(kmean_ref)
        for n in range(nb):
            kn = k_ref[n * blk:(n + 1) * blk, :]
            kmean_ref[n:n + 1, :] = jnp.sum(kn, axis=0, keepdims=True) * (1.0 / blk)
        kb_ref[...] = k_ref[...].astype(BF16)
        vb_ref[...] = v_ref[...].astype(BF16)

    q2 = q_ref[...]
    lane = lax.broadcasted_iota(jnp.int32, (1, LANES), 1)
    nrow = kmean_ref.shape[0]
    blk_id = lax.broadcasted_iota(jnp.int32, (nrow, 1), 0)
    r_id = lax.broadcasted_iota(jnp.int32, (blk, blk), 0)
    c_id = lax.broadcasted_iota(jnp.int32, (blk, blk), 1)
    own_start = pl.multiple_of(qi * blk, blk)
    out = jnp.zeros((blk, LANES), F32)
    for h in range(heads):
        hm = (lane // DH_A) == h
        qh = jnp.where(hm, q2, 0.0)
        gt = _nt(kmean_ref[...], qh, precision=HIGHEST)
        past = blk_id < qi
        sel_t = jnp.zeros_like(gt)
        for n in range(nb):
            gn = gt[n:n + 1, :]
            beats = jnp.logical_and(past, jnp.logical_or(gt > gn, jnp.logical_and(gt == gn, blk_id < n)))
            rank = jnp.sum(beats.astype(F32), axis=0, keepdims=True)
            sel_n = jnp.where(jnp.logical_and(rank < MOBA_TOPK, n < qi), 1.0, 0.0)
            sel_t = jnp.where(blk_id == n, sel_n, sel_t)
        sel = sel_t.T
        for n in range(nb - 1):
            bias_ref[n] = jnp.broadcast_to(jnp.where(sel[:, n:n + 1] > 0.5, 0.0, NEG_INF), (blk, LANES))

        qs = (qh * (DH_A ** -0.5)).astype(BF16)
        s = _nt(qs, kb_ref[pl.ds(own_start, blk), :])
        s = jnp.where(c_id <= r_id, s, NEG_INF)
        m0 = jnp.max(s, axis=-1, keepdims=True)
        p = jnp.exp(s - m0)
        l0 = jnp.sum(p, axis=-1, keepdims=True)
        acc0 = jnp.dot(p.astype(BF16), vb_ref[pl.ds(own_start, blk), :], preferred_element_type=F32)

        def body(n, carry):
            m, l, acc = carry
            start = pl.multiple_of(n * blk, blk)
            b = bias_ref[n]
            s = _nt(qs, kb_ref[pl.ds(start, blk), :]) + jnp.concatenate([b] * (blk // LANES), axis=1)
            m_new = jnp.maximum(m, jnp.max(s, axis=-1, keepdims=True))
            alpha = jnp.exp(m - m_new)
            p = jnp.exp(s - m_new)
            l = alpha * l + jnp.sum(p, axis=-1, keepdims=True)
            acc = alpha * acc + jnp.dot(p.astype(BF16), vb_ref[pl.ds(start, blk), :], preferred_element_type=F32)
            return m_new, l, acc

        _, l, acc = lax.fori_loop(0, qi, body, (m0, l0, acc0))
        out = jnp.where(hm, acc / l, out)
    o_ref[...] = out.astype(o_ref.dtype)


def _moba_prefill(z, B, T):
    blk = MOBA_BLOCK
    assert T % blk == 0
    nb = T // blk
    assert nb <= 8
    nq = T // blk
    cpt = AB_TILE // LANES
    return pl.pallas_call(
        functools.partial(_moba_prefill_kernel, nb=nb),
        grid=(B, W_A // LANES, nq),
        in_specs=[pl.BlockSpec((blk, LANES), lambda b, p, qi: (b * nq + qi, p)),
                  pl.BlockSpec((T, LANES), lambda b, p, qi: (b, cpt + p)),
                  pl.BlockSpec((T, LANES), lambda b, p, qi: (b, 2 * cpt + p))],
        out_specs=[pl.BlockSpec((blk, LANES), lambda b, p, qi: (b * nq + qi, p)),
                   pl.BlockSpec((None, LANES, T), lambda b, p, qi: (b, p, 0)),
                   pl.BlockSpec((None, LANES, T), lambda b, p, qi: (b, p, 0))],
        out_shape=[jax.ShapeDtypeStruct((B * T, W_A), BF16),
                   jax.ShapeDtypeStruct((B, W_A, T), F32),
                   jax.ShapeDtypeStruct((B, W_A, T), F32)],
        scratch_shapes=[pltpu.VMEM((T, LANES), BF16), pltpu.VMEM((T, LANES), BF16),
                        pltpu.VMEM((8, LANES), F32), pltpu.VMEM((max(nb - 1, 1), blk, LANES), F32)],
        compiler_params=_cparams(("parallel", "parallel", "arbitrary")),
        name="moba_prefill",
    )(z, z, z)


def _ret_prefill_kernel(q_ref, k_ref, v_ref, gb_ref, rg_ref, dmask_ref, din_ref, dout_ref, gch_ref,
                        o_ref, s_ref, *, n_chunks):
    C = RET_CHUNK
    heads = LANES // DK_B
    lane = lax.broadcasted_iota(jnp.int32, (1, LANES), 1)
    row_h = lax.broadcasted_iota(jnp.int32, (LANES, LANES), 0) // DK_B
    col_h = lax.broadcasted_iota(jnp.int32, (LANES, LANES), 1) // DV_B
    same_head = row_h == col_h
    seg_ones = jnp.where(same_head, 1.0, 0.0).astype(F32)
    din = din_ref[...]
    dout = dout_ref[...]
    gch = gch_ref[...]
    rg = rg_ref[...]

    def chunk(j, S):
        start = pl.multiple_of(j * C, C)
        q = q_ref[pl.ds(start, C), :]
        k = k_ref[pl.ds(start, C), :]
        v = v_ref[pl.ds(start, C), :]
        kb = k.astype(BF16)
        vb = v.astype(BF16)
        o = jnp.dot((q * din).astype(BF16), S.astype(BF16), preferred_element_type=F32)
        for h in range(heads):
            hm = (lane // DK_B) == h
            att = _nt(jnp.where(hm, q, 0.0).astype(BF16), kb) * dmask_ref[h]
            oh = jnp.dot(att.astype(BF16), vb, preferred_element_type=F32)
            o = o + jnp.where(hm, oh, 0.0)
        S = S * gch + jnp.where(same_head, _tn((k * dout).astype(BF16), vb), 0.0)
        ms = jnp.dot(o * o, seg_ones, preferred_element_type=F32, precision=HIGHEST) * (1.0 / DV_B)
        g = gb_ref[pl.ds(start, C), :]
        y = o * lax.rsqrt(ms + EPS) * rg * (g * _sigmoid(g))
        o_ref[pl.ds(start, C), :] = y.astype(o_ref.dtype)
        return S

    s_ref[...] = lax.fori_loop(0, n_chunks, chunk, jnp.zeros((LANES, LANES), F32))


def _ret_tables(chunk):
    log_g = jnp.log1p(-jnp.exp2(-5.0 - jnp.arange(H_B, dtype=F32)))
    i = jnp.arange(chunk, dtype=F32)
    d_in = jnp.exp((i[:, None] + 1.0) * log_g)
    d_out = jnp.exp((chunk - 1.0 - i)[:, None] * log_g)
    diff = i[:, None] - i[None, :]
    d_mask = jnp.where(diff >= 0, jnp.exp(jnp.maximum(diff, 0.0)[None] * log_g[:, None, None]), 0.0)
    g_chunk = jnp.exp(chunk * log_g)
    return d_in, d_out, d_mask, g_chunk


def _ret_prefill(z, ret_g, B, T):
    C = RET_CHUNK
    assert T % C == 0
    d_in, d_out, d_mask, g_chunk = _ret_tables(C)
    npair = W_B // LANES
    lanes = lambda t: jnp.repeat(t, DK_B, axis=-1)
    din_l = lanes(d_in).reshape(C, npair, LANES).transpose(1, 0, 2)
    dout_l = lanes(d_out).reshape(C, npair, LANES).transpose(1, 0, 2)
    gch_l = lanes(g_chunk).reshape(npair, 1, LANES)
    cpt = AB_TILE // LANES
    col = lambda t: (lambda b, p: (b, t * cpt + p))
    return pl.pallas_call(
        functools.partial(_ret_prefill_kernel, n_chunks=T // C),
        grid=(B, npair),
        in_specs=[pl.BlockSpec((T, LANES), col(3)), pl.BlockSpec((T, LANES), col(4)),
                  pl.BlockSpec((T, LANES), col(5)), pl.BlockSpec((T, LANES), col(6)),
                  pl.BlockSpec((1, LANES), lambda b, p: (0, p)),
                  pl.BlockSpec((LANES // DK_B, C, C), lambda b, p: (p, 0, 0)),
                  pl.BlockSpec((None, C, LANES), lambda b, p: (p, 0, 0)),
                  pl.BlockSpec((None, C, LANES), lambda b, p: (p, 0, 0)),
                  pl.BlockSpec((None, 1, LANES), lambda b, p: (p, 0, 0))],
        out_specs=[pl.BlockSpec((T, LANES), lambda b, p: (b, p)),
                   pl.BlockSpec((None, None, LANES, LANES), lambda b, p: (b, p, 0, 0))],
        out_shape=[jax.ShapeDtypeStruct((B * T, W_B), BF16),
                   jax.ShapeDtypeStruct((B, npair, LANES, LANES), F32)],
        compiler_params=_cparams(("parallel", "parallel")),
        name="retention_prefill",
    )(z, z, z, z, ret_g.reshape(1, W_B), d_mask, din_l, dout_l, gch_l)


def _mlstm_prefill_kernel(bias_ref, q_ref, k_ref, v_ref, og_ref, ig_ref, fg_ref, ng_ref,
                          h_ref, c_ref, n_ref, m_ref, b_scr, i_scr, *, n_chunks):
    L = MLSTM_CHUNK
    hd = pl.program_id(1)
    eye = _eye(L)
    r_id = lax.broadcasted_iota(jnp.int32, (L, L), 0)
    c_id = lax.broadcasted_iota(jnp.int32, (L, L), 1)
    causal = c_id <= r_id
    upper = jnp.where(r_id <= c_id, 1.0, 0.0).astype(F32)
    i_scr[...] = ig_ref[...] + bias_ref[0, hd]
    lf = _log_sigmoid(fg_ref[...] + bias_ref[1, hd])
    b_scr[...] = jnp.dot(lf, upper, preferred_element_type=F32, precision=HIGHEST)
    ng = ng_ref[...]

    def chunk(j, carry):
        Cm, n, m = carry
        start = pl.multiple_of(j * L, L)
        q = q_ref[pl.ds(start, L), :]
        k = k_ref[pl.ds(start, L), :] * (DH_C ** -0.5)
        v = v_ref[pl.ds(start, L), :]
        b_row = b_scr[pl.ds(j, 1), :]
        i_row = i_scr[pl.ds(j, 1), :]
        b_col = _row_to_col(b_row, eye)
        log_d = jnp.where(causal, (b_col - b_row) + i_row, NEG_INF)
        log_inter = b_col + m
        m_t = jnp.maximum(log_inter, jnp.max(log_d, axis=-1, keepdims=True))
        w_intra = jnp.exp(log_d - m_t)
        w_inter = jnp.exp(log_inter - m_t)
        qb = q.astype(BF16)
        kb = k.astype(BF16)
        s = _nt(qb, kb) * w_intra
        num = (jnp.dot(s.astype(BF16), v.astype(BF16), preferred_element_type=F32)
               + _nt(qb, Cm.astype(BF16)) * w_inter)
        den = jnp.sum(s, axis=-1, keepdims=True) + w_inter * jnp.sum(q * n, axis=-1, keepdims=True)
        h = num / jnp.maximum(jnp.abs(den), jnp.exp(-m_t))
        b_last = b_row[:, L - 1:L]
        log_w = (b_last - b_row) + i_row
        m_new = jnp.maximum(b_last + m, jnp.max(log_w, axis=-1, keepdims=True))
        w_k = _row_to_col(jnp.exp(log_w - m_new), eye)
        decay = jnp.exp(b_last + m - m_new)
        Cm = decay * Cm + _tn((v * w_k).astype(BF16), kb)
        n = decay * n + jnp.sum(k * w_k, axis=0, keepdims=True)
        og = og_ref[pl.ds(start, L), :]
        hn = h * lax.rsqrt(jnp.mean(h * h, axis=-1, keepdims=True) + EPS) * ng
        h_ref[pl.ds(start, L), :] = (hn * _sigmoid(og)).astype(h_ref.dtype)
        return Cm, n, m_new

    init = (jnp.zeros((L, L), F32), jnp.zeros((1, L), F32), jnp.zeros((1, 1), F32))
    Cm, n, m = lax.fori_loop(0, n_chunks, chunk, init)
    c_ref[...] = Cm
    n_ref[...] = n
    m_ref[...] = jnp.broadcast_to(m, m_ref.shape)


def _mlstm_prefill(z, ig, fg, gate_bias, out_g, B, T):
    L = MLSTM_CHUNK
    assert T % L == 0 and DH_C == LANES
    nc = T // L
    col = lambda t: (lambda b, h, bias: (b, t * H_C + h))
    gspec = pl.BlockSpec((None, None, nc, L), lambda b, h, bias: (b, h, 0, 0))
    grid_spec = pltpu.PrefetchScalarGridSpec(
        num_scalar_prefetch=1,
        grid=(B, H_C),
        in_specs=[pl.BlockSpec((T, LANES), col(0)), pl.BlockSpec((T, LANES), col(1)),
                  pl.BlockSpec((T, LANES), col(2)), pl.BlockSpec((T, LANES), col(3)),
                  gspec, gspec,
                  pl.BlockSpec((1, LANES), lambda b, h, bias: (0, h))],
        out_specs=[pl.BlockSpec((T, LANES), lambda b, h, bias: (b, h)),
                   pl.BlockSpec((None, None, L, L), lambda b, h, bias: (b, h, 0, 0)),
                   pl.BlockSpec((None, None, 1, L), lambda b, h, bias: (b, h, 0, 0)),
                   pl.BlockSpec((None, None, 1, LANES), lambda b, h, bias: (b, h, 0, 0))],
        scratch_shapes=[pltpu.VMEM((nc, L), F32), pltpu.VMEM((nc, L), F32)],
    )
    return pl.pallas_call(
        functools.partial(_mlstm_prefill_kernel, n_chunks=nc),
        grid_spec=grid_spec,
        out_shape=[jax.ShapeDtypeStruct((B * T, W_C), BF16),
                   jax.ShapeDtypeStruct((B, H_C, L, L), F32),
                   jax.ShapeDtypeStruct((B, H_C, 1, L), F32),
                   jax.ShapeDtypeStruct((B, H_C, 1, LANES), F32)],
        compiler_params=_cparams(("parallel", "parallel")),
        name="mlstm_prefill",
    )(gate_bias, z, z, z, z, ig, fg, out_g.reshape(1, W_C))


SELECT_PAGES = 16


def _moba_select_kernel(pt_ref, *refs, pages_per_block):
    k_refs = refs[:SELECT_PAGES]
    q_ref, sel_ref, gate_ref = refs[SELECT_PAGES:]
    s = pl.program_id(1)
    ppb = pages_per_block
    blocks_per_step = SELECT_PAGES // ppb
    qb = jnp.broadcast_to(q_ref[...], k_refs[0].shape)
    for i in range(blocks_per_step):
        acc = k_refs[i * ppb][...]
        for g in range(1, ppb):
            acc = acc + k_refs[i * ppb + g][...]
        gate_ref[s * blocks_per_step + i] = jnp.sum(acc * qb, axis=1)

    @pl.when(s == pl.num_programs(1) - 1)
    def _():
        nb = gate_ref.shape[0]
        gate = jnp.sum(gate_ref[...], axis=-1) * (1.0 / MOBA_BLOCK)
        blk_id = lax.broadcasted_iota(jnp.int32, (nb, 1), 0)
        rank = jnp.zeros(gate.shape, jnp.int32)
        for m in range(nb):
            gm = gate[m:m + 1, :]
            beats = jnp.logical_or(gm > gate, jnp.logical_and(gm == gate, m < blk_id))
            rank = rank + beats.astype(jnp.int32)
        out_row = lax.broadcasted_iota(jnp.int32, sel_ref.shape, 0)
        out = jnp.zeros(sel_ref.shape, jnp.int32)
        for r in range(MOBA_TOPK):
            idx = jnp.sum(jnp.where(rank == r, blk_id, 0), axis=0, keepdims=True)
            out = jnp.where(out_row == r, idx, out)
        sel_ref[...] = out


def _moba_select(cache_kt, page_table, q_col):
    Bs, n_pages = page_table.shape
    _, H, dh, page = cache_kt.shape
    ppb = MOBA_BLOCK // page
    nb = n_pages // ppb
    assert nb >= MOBA_TOPK and n_pages % SELECT_PAGES == 0 and SELECT_PAGES % ppb == 0

    def page_spec(i):
        return pl.BlockSpec((None, H, dh, page), lambda b, s, pt: (pt[b, s * SELECT_PAGES + i], 0, 0, 0))

    grid_spec = pltpu.PrefetchScalarGridSpec(
        num_scalar_prefetch=1,
        grid=(Bs, n_pages // SELECT_PAGES),
        in_specs=[page_spec(i) for i in range(SELECT_PAGES)]
        + [pl.BlockSpec((None, H, dh, 1), lambda b, s, pt: (b, 0, 0, 0))],
        out_specs=pl.BlockSpec((None, 8, H), lambda b, s, pt: (b, 0, 0)),
        scratch_shapes=[pltpu.VMEM((nb, H, page), F32)],
    )
    return pl.pallas_call(
        functools.partial(_moba_select_kernel, pages_per_block=ppb),
        grid_spec=grid_spec,
        out_shape=jax.ShapeDtypeStruct((Bs, 8, H), jnp.int32),
        compiler_params=_cparams(("parallel", "arbitrary")),
        name="moba_select",
    )(page_table, *([cache_kt] * SELECT_PAGES), q_col)


def _moba_decode_kernel(pt_ref, sel_ref, q_ref, kn_ref, vn_ref, ck_ref, cv_ref, o_ref, kbuf, vbuf, sem,
                        *, pages_per_block, page):
    b = pl.program_id(0)
    nb_ = pl.num_programs(0)
    H = q_ref.shape[0]
    ppb = pages_per_block

    def copies(bb, slot):
        out = []
        for h in range(H):
            for r in range(MOBA_TOPK):
                blk = sel_ref[bb, r, h]
                for g in range(ppb):
                    pg = pt_ref[bb, blk * ppb + g]
                    dst = pl.ds((r * ppb + g) * page, page)
                    out.append(pltpu.make_async_copy(ck_ref.at[pg, h], kbuf.at[slot, h, :, dst], sem.at[slot, 0]))
                    out.append(pltpu.make_async_copy(cv_ref.at[pg, h], vbuf.at[slot, h, :, dst], sem.at[slot, 1]))
        return out

    slot = b % 2

    @pl.when(b == 0)
    def _():
        for c in copies(b, slot):
            c.start()

    @pl.when(b + 1 < nb_)
    def _():
        for c in copies(b + 1, 1 - slot):
            c.start()

    for c in copies(b, slot):
        c.wait()

    row = lax.broadcasted_iota(jnp.int32, (H, 1), 0)
    out = jnp.zeros(o_ref.shape, F32)
    q = q_ref[...]
    kn = kn_ref[...]
    vn = vn_ref[...]
    for h in range(H):
        qh = q[h:h + 1, :] * (DH_A ** -0.5)
        q8 = jnp.broadcast_to(qh, (8, qh.shape[1])).astype(BF16)
        kh = kbuf[slot, h].astype(BF16)
        vh = vbuf[slot, h].astype(BF16)
        s = jnp.dot(q8, kh, preferred_element_type=F32)[0:1, :]
        s_self = jnp.sum(qh * kn[h:h + 1, :], axis=-1, keepdims=True)
        m = jnp.maximum(jnp.max(s, axis=-1, keepdims=True), s_self)
        p = jnp.exp(s - m)
        p_self = jnp.exp(s_self - m)
        l = jnp.sum(p, axis=-1, keepdims=True) + p_self
        p8 = jnp.broadcast_to(p, (8, p.shape[1])).astype(BF16)
        pv = _nt(p8, vh)[0:1, :]
        oh = (pv + p_self * vn[h:h + 1, :]) / l
        out = jnp.where(row == h, oh, out)
    o_ref[...] = out


def _moba_decode(cache_kt, cache_vt, page_table, sel, q, k_new, v_new):
    Bs, n_pages = page_table.shape
    _, H, dh, page = cache_kt.shape
    ppb = MOBA_BLOCK // page
    rows = MOBA_TOPK * MOBA_BLOCK
    vec = pl.BlockSpec((None, H, dh), lambda b, pt, sl: (b, 0, 0))
    grid_spec = pltpu.PrefetchScalarGridSpec(
        num_scalar_prefetch=2,
        grid=(Bs,),
        in_specs=[vec, vec, vec, pl.BlockSpec(memory_space=pl.ANY), pl.BlockSpec(memory_space=pl.ANY)],
        out_specs=vec,
        scratch_shapes=[pltpu.VMEM((2, H, dh, rows), F32), pltpu.VMEM((2, H, dh, rows), F32),
                        pltpu.SemaphoreType.DMA((2, 2))],
    )
    return pl.pallas_call(
        functools.partial(_moba_decode_kernel, pages_per_block=ppb, page=page),
        grid_spec=grid_spec,
        out_shape=jax.ShapeDtypeStruct((Bs, H, dh), F32),
        compiler_params=_cparams(("arbitrary",)),
        name="moba_decode",
    )(page_table, sel, q, k_new, v_new, cache_kt, cache_vt)


def _ret_decode_kernel(q_ref, k_ref, v_ref, gb_ref, rg_ref, g_ref, s_ref, o_ref, so_ref):
    H, dk = q_ref.shape
    eye = _eye(dk)
    row = lax.broadcasted_iota(jnp.int32, (H, 1), 0)
    q = q_ref[...]
    k = k_ref[...]
    v = v_ref[...]
    g_all = g_ref[...]
    out = jnp.zeros(o_ref.shape, F32)
    for h in range(H):
        qh, kh, vh = q[h:h + 1, :], k[h:h + 1, :], v[h:h + 1, :]
        g = g_all[h:h + 1, 0:1]
        S = s_ref[h]
        att = jnp.sum(qh * kh, axis=-1, keepdims=True)
        cross = jnp.sum(_row_to_col(qh * g, eye) * S, axis=0, keepdims=True)
        out = jnp.where(row == h, att * vh + cross, out)
        so_ref[h] = S * g + _row_to_col(kh, eye) * vh
    gb = gb_ref[...]
    y = out * lax.rsqrt(jnp.mean(out * out, axis=-1, keepdims=True) + EPS) * rg_ref[...]
    o_ref[...] = y * (gb * _sigmoid(gb))


def _ret_decode(q, k, v, gb, ret_g, state):
    Bs, H, dk = q.shape
    log_g = jnp.log1p(-jnp.exp2(-5.0 - jnp.arange(H_B, dtype=F32)))
    g = jnp.broadcast_to(jnp.exp(1.0 * log_g)[:, None], (H, LANES))
    vec = pl.BlockSpec((None, H, dk), lambda b: (b, 0, 0))
    st = pl.BlockSpec((None, H, dk, DV_B), lambda b: (b, 0, 0, 0))
    return pl.pallas_call(
        _ret_decode_kernel,
        grid=(Bs,),
        in_specs=[vec, vec, vec, vec, pl.BlockSpec((H, dk), lambda b: (0, 0)),
                  pl.BlockSpec((H, LANES), lambda b: (0, 0)), st],
        out_specs=[vec, st],
        out_shape=[jax.ShapeDtypeStruct((Bs, H, DV_B), F32), jax.ShapeDtypeStruct(state.shape, F32)],
        compiler_params=_cparams(("parallel",)),
        name="retention_decode",
    )(q, k, v, gb, ret_g.reshape(H, DV_B), g, state)


def _mlstm_decode_kernel(q_ref, k_ref, v_ref, og_ref, ig_ref, fg_ref, bi_ref, bf_ref, ng_ref, c_ref, n_ref,
                         m_ref, h_ref, co_ref, no_ref, mo_ref):
    H, d = q_ref.shape
    eye = _eye(d)
    row = lax.broadcasted_iota(jnp.int32, (H, 1), 0)
    q = q_ref[...]
    k = k_ref[...] * (DH_C ** -0.5)
    v = v_ref[...]
    n0 = n_ref[...]
    ig = ig_ref[...] + bi_ref[...]
    b = _log_sigmoid(fg_ref[...] + bf_ref[...])
    m0 = m_ref[...]
    log_inter = b + m0
    m_t = jnp.maximum(log_inter, ig)
    w_intra = jnp.exp(ig - m_t)
    w_inter = jnp.exp(log_inter - m_t)
    s = jnp.sum(q * k, axis=-1, keepdims=True) * w_intra
    den = s + w_inter * jnp.sum(q * n0, axis=-1, keepdims=True)
    scale = 1.0 / jnp.maximum(jnp.abs(den), jnp.exp(-m_t))
    w_k = w_intra
    decay = w_inter
    hs = jnp.zeros(h_ref.shape, F32)
    for h in range(H):
        qh, kh, vh = q[h:h + 1, :], k[h:h + 1, :], v[h:h + 1, :]
        Cm = c_ref[h]
        cq = _col_to_row(jnp.sum(Cm * qh, axis=1, keepdims=True), eye)
        num = s[h:h + 1, :] * vh + cq * w_inter[h:h + 1, :]
        hs = jnp.where(row == h, num * scale[h:h + 1, :], hs)
        co_ref[h] = decay[h:h + 1, :] * Cm + _row_to_col(vh * w_k[h:h + 1, :], eye) * kh
    no_ref[...] = decay * n0 + k * w_k
    mo_ref[...] = m_t
    hn = hs * lax.rsqrt(jnp.mean(hs * hs, axis=-1, keepdims=True) + EPS) * ng_ref[...]
    h_ref[...] = hn * _sigmoid(og_ref[...])


def _mlstm_decode(q, k, v, og, ig, fg, gate_bias, out_g, C0, n0, m0):
    Bs, H, d = q.shape
    vec = pl.BlockSpec((None, H, d), lambda b: (b, 0, 0))
    sc = pl.BlockSpec((None, H, 1), lambda b: (b, 0, 0))
    st = pl.BlockSpec((None, H, d, d), lambda b: (b, 0, 0, 0))
    bias = pl.BlockSpec((H, 1), lambda b: (0, 0))
    return pl.pallas_call(
        _mlstm_decode_kernel,
        grid=(Bs,),
        in_specs=[vec, vec, vec, vec, sc, sc, bias, bias, pl.BlockSpec((H, d), lambda b: (0, 0)), st, vec, sc],
        out_specs=[vec, st, vec, sc],
        out_shape=[jax.ShapeDtypeStruct((Bs, H, d), F32), jax.ShapeDtypeStruct(C0.shape, F32),
                   jax.ShapeDtypeStruct((Bs, H, d), F32), jax.ShapeDtypeStruct((Bs, H, 1), F32)],
        compiler_params=_cparams(("parallel",)),
        name="mlstm_decode",
    )(q, k, v, og, ig, fg, gate_bias[0].reshape(H, 1), gate_bias[1].reshape(H, 1), out_g.reshape(H, d), C0, n0, m0)


def _rope_tables(pos):
    half = DH_A // 2
    inv = ROPE_THETA ** (-jnp.arange(half, dtype=F32) / half)
    ang = pos[:, None] * inv[None, :]
    cos = jnp.cos(ang)
    sin = jnp.sin(ang)
    reps = LANES // DH_A
    cos_t = jnp.tile(jnp.concatenate([cos, cos], axis=-1), (1, reps))
    sin_t = jnp.tile(jnp.concatenate([-sin, sin], axis=-1), (1, reps))
    return cos_t, sin_t


def _row_tile(m, pref):
    return pref if m % pref == 0 else m


def kernel(x_prompt, x_sample, cache_k, cache_v, page_table, state_ret, state_mlstm_C, state_mlstm_n,
           state_mlstm_m, state_ffn_conv, ab_norm_g, ab_w_in, ab_ret_norm_g, ab_w_out, c_norm_g, c_w_in,
           c_gate_bias, c_out_norm_g, c_w_out, ffn_norm_g, ffn_w1, ffn_w3, ffn_conv_w, ffn_conv_b, ffn_w2,
           final_norm_g):
    Bp, Tp, D = x_prompt.shape
    Bs, Ts, _ = x_sample.shape
    assert Ts == 1
    past_len = page_table.shape[1] * cache_k.shape[2]
    Mp = Bp * Tp
    xp = x_prompt.reshape(Mp, D)
    xs = x_sample.reshape(Bs, D)
    tm_p = _row_tile(Tp, 1024)
    tm_s = Bs
    tf = 256
    rope_p = _rope_tables(jnp.arange(Tp, dtype=jnp.int32).astype(F32))
    rope_s = _rope_tables(jnp.full((Bs,), past_len, jnp.int32).astype(F32))
    gw = 2 * H_C

    outs = {}

    w_in = ab_w_in[0].astype(BF16)
    w_out = ab_w_out[0].astype(BF16)
    w_out_a, w_out_b = w_out[:W_A], w_out[W_A:]

    z = _norm_matmul(xp, ab_norm_g[0], w_in, tm_p, AB_TILE, rope=rope_p)
    oa, kt, vt = _moba_prefill(z, Bp, Tp)
    ob, s_pair = _ret_prefill(z, ab_ret_norm_g[0], Bp, Tp)
    xp = _proj_residual([oa, ob], [w_out_a, w_out_b], xp, tm_p)
    outs["k_prompt"] = kt.reshape(Bp, H_A, DH_A, Tp).transpose(0, 3, 1, 2)[None]
    outs["v_prompt"] = vt.reshape(Bp, H_A, DH_A, Tp).transpose(0, 3, 1, 2)[None]
    hb = LANES // DK_B
    s_heads = jnp.stack([s_pair[:, :, i * DK_B:(i + 1) * DK_B, i * DV_B:(i + 1) * DV_B] for i in range(hb)], axis=2)
    outs["ret_prompt"] = s_heads.reshape(1, Bp, H_B, DK_B, DV_B)

    zs = _norm_matmul(xs, ab_norm_g[0], w_in, tm_s, AB_TILE, rope=rope_s)
    seg = lambda t: zs[:, t * AB_TILE:(t + 1) * AB_TILE].reshape(Bs, H_A, DH_A)
    qa_s, ka_s, va_s, qb_s, kb_s, vb_s, gb_s = (seg(t) for t in range(7))
    cache_kt = cache_k[0].transpose(0, 2, 3, 1)
    cache_vt = cache_v[0].transpose(0, 2, 3, 1)
    sel = _moba_select(cache_kt, page_table, qa_s[..., None])
    oa_s = _moba_decode(cache_kt, cache_vt, page_table, sel, qa_s, ka_s, va_s)
    ob_s, s_new = _ret_decode(qb_s, kb_s, vb_s, gb_s, ab_ret_norm_g[0], state_ret[0])
    xs = _proj_residual([oa_s.reshape(Bs, W_A).astype(BF16), ob_s.reshape(Bs, W_B).astype(BF16)],
                        [w_out_a, w_out_b], xs, tm_s)
    outs["k_sample"] = ka_s.reshape(1, Bs, 1, H_A, DH_A)
    outs["v_sample"] = va_s.reshape(1, Bs, 1, H_A, DH_A)
    outs["ret_sample"] = s_new[None]

    conv_p, conv_s = [], []

    def ffn_both(l, xp, xs, final_g):
        w1 = ffn_w1[l].astype(BF16)
        w3 = ffn_w3[l].astype(BF16)
        w2 = ffn_w2[l].astype(BF16)
        tail = xp.reshape(Bp, Tp, D)[:, Tp - 8:, :].reshape(Bp * 8, D)
        a_tail = _norm_matmul(tail, ffn_norm_g[l], w1, Bp * 8, tf)
        conv_p.append(a_tail.reshape(Bp, 8, D_FF)[:, 8 - (CONV_W - 1):, :])
        a_s = _norm_matmul(xs, ffn_norm_g[l], w1, tm_s, tf)
        st = state_ffn_conv[l]
        conv_s.append(jnp.stack([st[:, 1, :], a_s], axis=1))
        xp = _ffn(xp, ffn_norm_g[l], w1, w3, ffn_conv_w[l], ffn_conv_b[l], w2, final_g, tm_p, tf, seq_len=Tp)
        xs = _ffn(xs, ffn_norm_g[l], w1, w3, ffn_conv_w[l], ffn_conv_b[l], w2, final_g, tm_s, tf,
                  state=(st[:, 0, :], st[:, 1, :]))
        return xp, xs

    xp, xs = ffn_both(0, xp, xs, None)

    w_in = c_w_in[0]
    w_main = w_in[:, :4 * W_C].astype(BF16)
    w_gate = jnp.pad(w_in[:, 4 * W_C:], ((0, 0), (0, LANES - gw))).astype(BF16)
    w_out = c_w_out[0].astype(BF16)
    L = MLSTM_CHUNK

    z = _norm_matmul(xp, c_norm_g[0], w_main, tm_p, 512)
    zg = _norm_matmul(xp, c_norm_g[0], w_gate, tm_p, LANES)
    gates = zg[:, :gw].reshape(Bp, Tp // L, L, 2, H_C).transpose(3, 0, 4, 1, 2)
    h, C_p, n_p, m_p = _mlstm_prefill(z, gates[0], gates[1], c_gate_bias[0], c_out_norm_g[0], Bp, Tp)
    xp = _proj_residual([h], [w_out], xp, tm_p)
    outs["C_prompt"] = C_p[None]
    outs["n_prompt"] = n_p.reshape(1, Bp, H_C, DH_C)
    outs["m_prompt"] = m_p[:, :, 0, 0][None]

    zs = _norm_matmul(xs, c_norm_g[0], w_main, tm_s, 512)
    zgs = _norm_matmul(xs, c_norm_g[0], w_gate, tm_s, LANES)
    segc = lambda t: zs[:, t * W_C:(t + 1) * W_C].reshape(Bs, H_C, DH_C)
    ig_s = zgs[:, :H_C][:, :, None]
    fg_s = zgs[:, H_C:gw][:, :, None]
    h_s, C_s, n_s, m_s = _mlstm_decode(segc(0), segc(1), segc(2), segc(3), ig_s, fg_s, c_gate_bias[0], c_out_norm_g[0],
                                       state_mlstm_C[0], state_mlstm_n[0], state_mlstm_m[0][:, :, None])
    xs = _proj_residual([h_s.reshape(Bs, W_C).astype(BF16)], [w_out], xs, tm_s)
    outs["C_sample"] = C_s[None]
    outs["n_sample"] = n_s[None]
    outs["m_sample"] = m_s[:, :, 0][None]

    xp, xs = ffn_both(1, xp, xs, final_norm_g)

    return (xp.reshape(Bp, Tp, D), xs.reshape(Bs, 1, D),
            outs["k_prompt"], outs["v_prompt"], outs["k_sample"], outs["v_sample"],
            outs["ret_prompt"], outs["ret_sample"], outs["C_prompt"], outs["C_sample"],
            outs["n_prompt"], outs["n_sample"], outs["m_prompt"], outs["m_sample"],
            jnp.stack(conv_p), jnp.stack(conv_s))
```

```python
import functools

import jax
import jax.numpy as jnp
from jax import lax
from jax.experimental import pallas as pl
from jax.experimental.pallas import tpu as pltpu

F32 = jnp.float32
BF16 = jnp.bfloat16
HIGHEST = lax.Precision.HIGHEST

LANES = 128
D_MODEL = 1024
H_A = 8
DH_A = 64
MOBA_BLOCK = 256
MOBA_TOPK = 3
H_B = 8
DK_B = 64
DV_B = 64
RET_CHUNK = 128
H_C = 8
DH_C = D_MODEL // H_C
MLSTM_CHUNK = 128
D_FF = 11 * D_MODEL // 4
CONV_W = 3
ROPE_THETA = 10000.0
EPS = 1e-6
W_A = H_A * DH_A
W_B = H_B * DV_B
W_C = H_C * DH_C
IN_AB = 3 * W_A + 2 * H_B * DK_B + 2 * W_B
AB_TILE = 512
AB_ROPE_TILES = (0, 1, 3, 4)
AB_KB_TILE = 4
VMEM_LIMIT = 56 * 1024 * 1024

NEG_INF = float("-inf")


def _cparams(sem):
    return pltpu.CompilerParams(dimension_semantics=sem, vmem_limit_bytes=VMEM_LIMIT)


def _nt(a, b, **kw):
    return lax.dot_general(a, b, (((1,), (1,)), ((), ())), preferred_element_type=F32, **kw)


def _tn(a, b, **kw):
    return lax.dot_general(a, b, (((0,), (0,)), ((), ())), preferred_element_type=F32, **kw)


def _rms_rows(x, g):
    ms = jnp.mean(x * x, axis=-1, keepdims=True)
    return x * lax.rsqrt(ms + EPS) * g


def _eye(n):
    return lax.broadcasted_iota(jnp.int32, (n, n), 0) == lax.broadcasted_iota(jnp.int32, (n, n), 1)


def _row_to_col(row, eye):
    return jnp.sum(jnp.where(eye, row, 0.0), axis=1, keepdims=True)


def _col_to_row(col, eye):
    return jnp.sum(jnp.where(eye, col, 0.0), axis=0, keepdims=True)


def _rowsum_rep(x):
    ones = jnp.ones((x.shape[1], LANES), BF16)
    hi = x.astype(BF16)
    lo = (x - hi.astype(F32)).astype(BF16)
    return (jnp.dot(hi, ones, preferred_element_type=F32) + jnp.dot(lo, ones, preferred_element_type=F32))


def _log_sigmoid(x):
    return jnp.minimum(x, 0.0) - jnp.log1p(jnp.exp(-jnp.abs(x)))


def _sigmoid(x):
    return 1.0 / (1.0 + jnp.exp(-x))


def _gelu_tanh(x):
    c = 0.7978845608028654
    return 0.5 * x * (1.0 + jnp.tanh(c * (x + 0.044715 * (x * x * x))))


def _norm_matmul_kernel(x_ref, g_ref, w_ref, cos_ref, sin_ref, o_ref, xn_ref, *, rope_tiles, scale_tile, scale):
    j = pl.program_id(1)

    @pl.when(j == 0)
    def _():
        xn_ref[...] = _rms_rows(x_ref[...], g_ref[...]).astype(BF16)

    z = jnp.dot(xn_ref[...], w_ref[...], preferred_element_type=F32)
    if not rope_tiles:
        o_ref[...] = z
        return

    is_rope = functools.reduce(jnp.logical_or, [j == t for t in rope_tiles])

    @pl.when(is_rope)
    def _():
        sc = jnp.where(j == scale_tile, scale, 1.0).astype(F32)
        lane = lax.broadcasted_iota(jnp.int32, (1, LANES), 1)
        first_half = (lane % DH_A) < (DH_A // 2)
        cos = cos_ref[...]
        sin = sin_ref[...]
        for c in range(z.shape[1] // LANES):
            zc = z[:, c * LANES:(c + 1) * LANES]
            partner = jnp.where(first_half, pltpu.roll(zc, LANES - DH_A // 2, 1), pltpu.roll(zc, DH_A // 2, 1))
            o_ref[:, c * LANES:(c + 1) * LANES] = (zc * cos + partner * sin) * sc

    @pl.when(jnp.logical_not(is_rope))
    def _():
        o_ref[...] = z


def _norm_matmul(x, g, w, tm, tn, rope=None):
    M, D = x.shape
    N = w.shape[1]
    assert M % tm == 0 and N % tn == 0
    if rope is None:
        cos = sin = jnp.zeros((8, LANES), F32)
        tab_spec = pl.BlockSpec((8, LANES), lambda i, j: (0, 0))
        kern = functools.partial(_norm_matmul_kernel, rope_tiles=(), scale_tile=-1, scale=1.0)
    else:
        cos, sin = rope
        nt = cos.shape[0] // tm
        tab_spec = pl.BlockSpec((tm, LANES), lambda i, j: (i % nt, 0))
        kern = functools.partial(_norm_matmul_kernel, rope_tiles=AB_ROPE_TILES, scale_tile=AB_KB_TILE,
                                 scale=DK_B ** -0.5)
    return pl.pallas_call(
        kern,
        grid=(M // tm, N // tn),
        in_specs=[pl.BlockSpec((tm, D), lambda i, j: (i, 0)),
                  pl.BlockSpec((1, D), lambda i, j: (0, 0)),
                  pl.BlockSpec((D, tn), lambda i, j: (0, j)),
                  tab_spec, tab_spec],
        out_specs=pl.BlockSpec((tm, tn), lambda i, j: (i, j)),
        out_shape=jax.ShapeDtypeStruct((M, N), F32),
        scratch_shapes=[pltpu.VMEM((tm, D), BF16)],
        compiler_params=_cparams(("parallel", "arbitrary")),
        name="norm_matmul",
    )(x, g.reshape(1, D), w, cos, sin)


def _proj_residual_kernel(*refs, n_in):
    a_refs = refs[:n_in]
    w_refs = refs[n_in:2 * n_in]
    res_ref = refs[2 * n_in]
    o_ref = refs[2 * n_in + 1]
    y = res_ref[...]
    acc = None
    for a_ref, w_ref in zip(a_refs, w_refs):
        d = jnp.dot(a_ref[...], w_ref[...], preferred_element_type=F32)
        acc = d if acc is None else acc + d
    o_ref[...] = y + acc


def _proj_residual(acts, ws, res, tm):
    M, D = res.shape
    n_in = len(acts)
    in_specs = ([pl.BlockSpec((tm, a.shape[1]), lambda i: (i, 0)) for a in acts]
                + [pl.BlockSpec(w.shape, lambda i: (0, 0)) for w in ws]
                + [pl.BlockSpec((tm, D), lambda i: (i, 0))])
    return pl.pallas_call(
        functools.partial(_proj_residual_kernel, n_in=n_in),
        grid=(M // tm,),
        in_specs=in_specs,
        out_specs=pl.BlockSpec((tm, D), lambda i: (i, 0)),
        out_shape=jax.ShapeDtypeStruct((M, D), F32),
        compiler_params=_cparams(("parallel",)),
        name="proj_residual",
    )(*acts, *ws, res)


PREV_ROWS = 16
FFN_CHUNK = 256


def _ffn_kernel(*refs, seq_mode, tiles_per_seq, final_norm):
    if seq_mode:
        (x_ref, xp_ref, g_ref, w1_ref, w3_ref, cw_ref, cb_ref, w2_ref, fg_ref, o_ref) = refs
    else:
        (x_ref, s0_ref, s1_ref, g_ref, w1_ref, w3_ref, cw_ref, cb_ref, w2_ref, fg_ref, o_ref) = refs
    i = pl.program_id(0)
    x = x_ref[...]
    tm = x.shape[0]
    F = w1_ref.shape[1]
    xn = _rms_rows(x, g_ref[...]).astype(BF16)
    if seq_mode:
        xpn = _rms_rows(xp_ref[...], g_ref[...]).astype(BF16)
        has_prev = ((i % tiles_per_seq) != 0).astype(F32)
        row = lax.broadcasted_iota(jnp.int32, (tm, 1), 0)
    acc = x
    for c0 in range(0, F, FFN_CHUNK):
        cols = slice(c0, min(c0 + FFN_CHUNK, F))
        w1c = w1_ref[:, cols]
        a = jnp.dot(xn, w1c, preferred_element_type=F32)
        gate = jnp.dot(xn, w3_ref[:, cols], preferred_element_type=F32)
        if seq_mode:
            ap = jnp.dot(xpn, w1c, preferred_element_type=F32)
            p1 = ap[PREV_ROWS - 1:PREV_ROWS, :] * has_prev
            p2 = ap[PREV_ROWS - 2:PREV_ROWS - 1, :] * has_prev
            a1 = jnp.where(row == 0, p1, pltpu.roll(a, 1, 0))
            a2 = jnp.where(row == 0, p2, jnp.where(row == 1, p1, pltpu.roll(a, 2, 0)))
        else:
            a1 = s1_ref[:, cols]
            a2 = s0_ref[:, cols]
        ac = cb_ref[:, cols] + a2 * cw_ref[0:1, cols]
        ac = ac + a1 * cw_ref[1:2, cols]
        ac = ac + a * cw_ref[2:3, cols]
        y = (_gelu_tanh(ac) * gate).astype(BF16)
        acc = acc + jnp.dot(y, w2_ref[cols, :], preferred_element_type=F32)
    if final_norm:
        acc = _rms_rows(acc, fg_ref[...])
    o_ref[...] = acc


def _ffn(x, norm_g, w1, w3, conv_w, conv_b, w2, final_g, tm, seq_len=None, state=None):
    M, D = x.shape
    F = w1.shape[1]
    seq_mode = state is None
    final_norm = final_g is not None
    fg = (final_g if final_norm else jnp.ones((D,), F32)).reshape(1, D)
    whole = lambda shape: pl.BlockSpec(shape, lambda i: (0, 0), pipeline_mode=pl.Buffered(1))
    common = [whole((1, D)), whole((D, F)), whole((D, F)), whole((CONV_W, F)), whole((1, F)), whole((F, D)),
              whole((1, D))]
    common_args = (norm_g.reshape(1, D), w1, w3, conv_w, conv_b.reshape(1, F), w2, fg)
    x_spec = pl.BlockSpec((tm, D), lambda i: (i, 0))
    if seq_mode:
        assert seq_len % tm == 0 and tm % PREV_ROWS == 0
        r = tm // PREV_ROWS
        in_specs = [x_spec, pl.BlockSpec((PREV_ROWS, D), lambda i: (jnp.maximum(i * r - 1, 0), 0))] + common
        args = (x, x) + common_args
        tiles_per_seq = seq_len // tm
    else:
        s_spec = pl.BlockSpec((tm, F), lambda i: (i, 0))
        in_specs = [x_spec, s_spec, s_spec] + common
        args = (x, state[0], state[1]) + common_args
        tiles_per_seq = 1
    return pl.pallas_call(
        functools.partial(_ffn_kernel, seq_mode=seq_mode, tiles_per_seq=tiles_per_seq, final_norm=final_norm),
        grid=(M // tm,),
        in_specs=in_specs,
        out_specs=pl.BlockSpec((tm, D), lambda i: (i, 0)),
        out_shape=jax.ShapeDtypeStruct((M, D), F32),
        compiler_params=_cparams(("parallel",)),
        name="conv_ffn",
    )(*args)


def _moba_prefill_kernel(q_ref, k_ref, v_ref, o_ref, kt_ref, vt_ref, kb_ref, vb_ref, *, nb):
    blk = MOBA_BLOCK
    heads = LANES // DH_A
    T = q_ref.shape[0]
    k = k_ref[...]
    kt_ref[...] = k.T
    vt_ref[...] = v_ref[...].T
    kb_ref[...] = k.astype(BF16)
    vb_ref[...] = v_ref[...].astype(BF16)
    kmean = jnp.concatenate(
        [jnp.sum(k[n * blk:(n + 1) * blk, :], axis=0, keepdims=True) * (1.0 / blk) for n in range(nb)]
        + [jnp.zeros((8 - nb, LANES), F32)] * (nb < 8), axis=0)

    q2 = q_ref[...]
    lane = lax.broadcasted_iota(jnp.int32, (1, LANES), 1)
    blk_id = lax.broadcasted_iota(jnp.int32, (8, 1), 0)
    q_blk = lax.broadcasted_iota(jnp.int32, (1, T), 1) // blk
    past = blk_id < q_blk
    causal = (lax.broadcasted_iota(jnp.int32, (blk, blk), 1) <= lax.broadcasted_iota(jnp.int32, (blk, blk), 0))
    hms, qss, sels = [], [], []
    for h in range(heads):
        hm = (lane // DH_A) == h
        qh = jnp.where(hm, q2, 0.0)
        gt = _nt(kmean, qh, precision=HIGHEST)
        sel_t = jnp.zeros_like(gt)
        for n in range(nb - 1):
            gn = gt[n:n + 1, :]
            beats = jnp.logical_and(past, jnp.logical_or(gt > gn, jnp.logical_and(gt == gn, blk_id < n)))
            rank = jnp.sum(beats.astype(F32), axis=0, keepdims=True)
            sel_n = jnp.where(jnp.logical_and(rank < MOBA_TOPK, n < q_blk), 1.0, 0.0)
            sel_t = jnp.where(blk_id == n, sel_n, sel_t)
        hms.append(hm)
        qss.append((qh * (DH_A ** -0.5)).astype(BF16))
        sels.append(sel_t.T)

    for qi in range(nb):
        rows = slice(qi * blk, (qi + 1) * blk)
        outs = []
        for h in range(heads):
            qs = qss[h][rows, :]
            pieces = []
            for n in range(qi + 1):
                s = _nt(qs, kb_ref[n * blk:(n + 1) * blk, :])
                keep = causal if n == qi else (sels[h][rows, n:n + 1] > 0.5)
                pieces.append(jnp.where(keep, s, NEG_INF))
            m = functools.reduce(jnp.maximum, pieces)
            m = jnp.max(m, axis=-1, keepdims=True)
            lsum = None
            acc = None
            for n in range(qi + 1):
                p = jnp.exp(pieces[n] - m)
                lsum = p if lsum is None else lsum + p
                d = jnp.dot(p.astype(BF16), vb_ref[n * blk:(n + 1) * blk, :], preferred_element_type=F32)
                acc = d if acc is None else acc + d
            outs.append(acc / jnp.sum(lsum, axis=-1, keepdims=True))
        out = outs[0]
        for h in range(1, heads):
            out = jnp.where(hms[h], outs[h], out)
        o_ref[rows, :] = out.astype(o_ref.dtype)


def _moba_prefill(z, B, T):
    blk = MOBA_BLOCK
    assert T % blk == 0
    nb = T // blk
    assert nb <= 8
    cpt = AB_TILE // LANES
    return pl.pallas_call(
        functools.partial(_moba_prefill_kernel, nb=nb),
        grid=(B, W_A // LANES),
        in_specs=[pl.BlockSpec((T, LANES), lambda b, p: (b, p)),
                  pl.BlockSpec((T, LANES), lambda b, p: (b, cpt + p)),
                  pl.BlockSpec((T, LANES), lambda b, p: (b, 2 * cpt + p))],
        out_specs=[pl.BlockSpec((T, LANES), lambda b, p: (b, p)),
                   pl.BlockSpec((None, LANES, T), lambda b, p: (b, p, 0)),
                   pl.BlockSpec((None, LANES, T), lambda b, p: (b, p, 0))],
        out_shape=[jax.ShapeDtypeStruct((B * T, W_A), BF16),
                   jax.ShapeDtypeStruct((B, W_A, T), F32),
                   jax.ShapeDtypeStruct((B, W_A, T), F32)],
        scratch_shapes=[pltpu.VMEM((T, LANES), BF16), pltpu.VMEM((T, LANES), BF16)],
        compiler_params=_cparams(("parallel", "parallel")),
        name="moba_prefill",
    )(z, z, z)


def _ret_prefill_kernel(q_ref, k_ref, v_ref, gb_ref, rg_ref, dmask_ref, din_ref, dout_ref, gch_ref,
                        o_ref, s_ref, *, n_chunks):
    C = RET_CHUNK
    heads = LANES // DK_B
    lane = lax.broadcasted_iota(jnp.int32, (1, LANES), 1)
    row_h = lax.broadcasted_iota(jnp.int32, (LANES, LANES), 0) // DK_B
    col_h = lax.broadcasted_iota(jnp.int32, (LANES, LANES), 1) // DV_B
    same_head = row_h == col_h
    seg_ones = jnp.where(same_head, 1.0, 0.0).astype(F32)
    din = din_ref[...]
    dout = dout_ref[...]
    gch = gch_ref[...]
    rg = rg_ref[...]

    def chunk(j, S):
        start = pl.multiple_of(j * C, C)
        q = q_ref[pl.ds(start, C), :]
        k = k_ref[pl.ds(start, C), :]
        v = v_ref[pl.ds(start, C), :]
        kb = k.astype(BF16)
        vb = v.astype(BF16)
        o = jnp.dot((q * din).astype(BF16), S.astype(BF16), preferred_element_type=F32)
        for h in range(heads):
            hm = (lane // DK_B) == h
            att = _nt(jnp.where(hm, q, 0.0).astype(BF16), kb) * dmask_ref[h]
            oh = jnp.dot(att.astype(BF16), vb, preferred_element_type=F32)
            o = o + jnp.where(hm, oh, 0.0)
        S = S * gch + jnp.where(same_head, _tn((k * dout).astype(BF16), vb), 0.0)
        ms = jnp.dot(o * o, seg_ones, preferred_element_type=F32, precision=HIGHEST) * (1.0 / DV_B)
        g = gb_ref[pl.ds(start, C), :]
        y = o * lax.rsqrt(ms + EPS) * rg * (g * _sigmoid(g))
        o_ref[pl.ds(start, C), :] = y.astype(o_ref.dtype)
        return S

    s_ref[...] = lax.fori_loop(0, n_chunks, chunk, jnp.zeros((LANES, LANES), F32), unroll=True)


def _ret_tables(chunk):
    log_g = jnp.log1p(-jnp.exp2(-5.0 - jnp.arange(H_B, dtype=F32)))
    i = jnp.arange(chunk, dtype=F32)
    d_in = jnp.exp((i[:, None] + 1.0) * log_g)
    d_out = jnp.exp((chunk - 1.0 - i)[:, None] * log_g)
    diff = i[:, None] - i[None, :]
    d_mask = jnp.where(diff >= 0, jnp.exp(jnp.maximum(diff, 0.0)[None] * log_g[:, None, None]), 0.0)
    g_chunk = jnp.exp(chunk * log_g)
    return d_in, d_out, d_mask, g_chunk


def _ret_prefill(z, ret_g, B, T):
    C = RET_CHUNK
    assert T % C == 0
    d_in, d_out, d_mask, g_chunk = _ret_tables(C)
    npair = W_B // LANES
    lanes = lambda t: jnp.repeat(t, DK_B, axis=-1)
    din_l = lanes(d_in).reshape(C, npair, LANES).transpose(1, 0, 2)
    dout_l = lanes(d_out).reshape(C, npair, LANES).transpose(1, 0, 2)
    gch_l = lanes(g_chunk).reshape(npair, 1, LANES)
    cpt = AB_TILE // LANES
    col = lambda t: (lambda b, p: (b, t * cpt + p))
    return pl.pallas_call(
        functools.partial(_ret_prefill_kernel, n_chunks=T // C),
        grid=(B, npair),
        in_specs=[pl.BlockSpec((T, LANES), col(3)), pl.BlockSpec((T, LANES), col(4)),
                  pl.BlockSpec((T, LANES), col(5)), pl.BlockSpec((T, LANES), col(6)),
                  pl.BlockSpec((1, LANES), lambda b, p: (0, p)),
                  pl.BlockSpec((LANES // DK_B, C, C), lambda b, p: (p, 0, 0)),
                  pl.BlockSpec((None, C, LANES), lambda b, p: (p, 0, 0)),
                  pl.BlockSpec((None, C, LANES), lambda b, p: (p, 0, 0)),
                  pl.BlockSpec((None, 1, LANES), lambda b, p: (p, 0, 0))],
        out_specs=[pl.BlockSpec((T, LANES), lambda b, p: (b, p)),
                   pl.BlockSpec((None, None, LANES, LANES), lambda b, p: (b, p, 0, 0))],
        out_shape=[jax.ShapeDtypeStruct((B * T, W_B), BF16),
                   jax.ShapeDtypeStruct((B, npair, LANES, LANES), F32)],
        compiler_params=_cparams(("parallel", "parallel")),
        name="retention_prefill",
    )(z, z, z, z, ret_g.reshape(1, W_B), d_mask, din_l, dout_l, gch_l)


def _mlstm_prefill_kernel(bias_ref, q_ref, k_ref, v_ref, og_ref, ig_ref, fg_ref, igc_ref, fgc_ref, ng_ref,
                          h_ref, c_ref, n_ref, m_ref, *, n_chunks):
    L = MLSTM_CHUNK
    hd = pl.program_id(1)
    r_id = lax.broadcasted_iota(jnp.int32, (L, L), 0)
    c_id = lax.broadcasted_iota(jnp.int32, (L, L), 1)
    causal = c_id <= r_id
    upper = jnp.where(r_id <= c_id, 1.0, 0.0).astype(F32)
    lower = jnp.where(c_id <= r_id, 1.0, 0.0).astype(F32)
    i_rows = ig_ref[...] + bias_ref[0, hd]
    b_rows = jnp.dot(_log_sigmoid(fg_ref[...] + bias_ref[1, hd]), upper, preferred_element_type=F32, precision=HIGHEST)
    i_cols = igc_ref[...] + bias_ref[0, hd]
    b_cols = jnp.dot(lower, _log_sigmoid(fgc_ref[...] + bias_ref[1, hd]), preferred_element_type=F32, precision=HIGHEST)
    ng = ng_ref[...]

    Cm = jnp.zeros((L, L), F32)
    n = jnp.zeros((1, L), F32)
    m = jnp.zeros((1, 1), F32)
    for j in range(n_chunks):
        rows = slice(j * L, (j + 1) * L)
        q = q_ref[rows, :]
        k = k_ref[rows, :] * (DH_C ** -0.5)
        v = v_ref[rows, :]
        b_row, i_row = b_rows[j:j + 1, :], i_rows[j:j + 1, :]
        b_col, i_col = b_cols[:, j:j + 1], i_cols[:, j:j + 1]
        b_last = b_row[:, L - 1:L]
        qb = q.astype(BF16)
        kb = k.astype(BF16)
        b_rep = jnp.broadcast_to(b_col, (L, L))
        log_d = jnp.where(causal, (b_rep - b_row) + i_row, NEG_INF)
        m_row = jnp.broadcast_to(jnp.max(log_d, axis=-1, keepdims=True), (L, L))
        s = _nt(qb, kb) * jnp.exp(log_d - m_row)
        sv = jnp.dot(s.astype(BF16), v.astype(BF16), preferred_element_type=F32)
        s_sum = _rowsum_rep(s)
        log_w = (b_last - b_col) + i_col
        m_loc = jnp.max(log_w, axis=0, keepdims=True)
        e_k = jnp.broadcast_to(jnp.exp(log_w - m_loc), (L, L))
        U = _tn((v * e_k).astype(BF16), kb)
        nk = jnp.sum(k * e_k, axis=0, keepdims=True)
        log_inter = b_rep + m
        m_t = jnp.maximum(log_inter, m_row)
        f_intra = jnp.exp(m_row - m_t)
        w_inter = jnp.exp(log_inter - m_t)
        num = sv * f_intra + _nt(qb, Cm.astype(BF16)) * w_inter
        den = s_sum * f_intra + w_inter * _rowsum_rep(q * n)
        h = num / jnp.maximum(jnp.abs(den), jnp.exp(-m_t))
        m_new = jnp.maximum(b_last + m, m_loc)
        decay = jnp.exp(b_last + m - m_new)
        f_k = jnp.exp(m_loc - m_new)
        Cm = decay * Cm + f_k * U
        n = decay * n + f_k * nk
        m = m_new
        hn = h * lax.rsqrt(_rowsum_rep(h * h) * (1.0 / L) + EPS) * ng
        h_ref[rows, :] = (hn * _sigmoid(og_ref[rows, :])).astype(h_ref.dtype)
    c_ref[...] = Cm
    n_ref[...] = n
    m_ref[...] = jnp.broadcast_to(m, m_ref.shape)


def _mlstm_prefill(z, ig, fg, gate_bias, out_g, B, T):
    L = MLSTM_CHUNK
    assert T % L == 0 and DH_C == LANES
    nc = T // L
    col = lambda t: (lambda b, h, bias: (b, t * H_C + h))
    gspec = pl.BlockSpec((None, None, nc, L), lambda b, h, bias: (b, h, 0, 0))
    gcspec = pl.BlockSpec((None, None, L, nc), lambda b, h, bias: (b, h, 0, 0))
    grid_spec = pltpu.PrefetchScalarGridSpec(
        num_scalar_prefetch=1,
        grid=(B, H_C),
        in_specs=[pl.BlockSpec((T, LANES), col(0)), pl.BlockSpec((T, LANES), col(1)),
                  pl.BlockSpec((T, LANES), col(2)), pl.BlockSpec((T, LANES), col(3)),
                  gspec, gspec, gcspec, gcspec,
                  pl.BlockSpec((1, LANES), lambda b, h, bias: (0, h))],
        out_specs=[pl.BlockSpec((T, LANES), lambda b, h, bias: (b, h)),
                   pl.BlockSpec((None, None, L, L), lambda b, h, bias: (b, h, 0, 0)),
                   pl.BlockSpec((None, None, 1, L), lambda b, h, bias: (b, h, 0, 0)),
                   pl.BlockSpec((None, None, 1, LANES), lambda b, h, bias: (b, h, 0, 0))],
    )
    return pl.pallas_call(
        functools.partial(_mlstm_prefill_kernel, n_chunks=nc),
        grid_spec=grid_spec,
        out_shape=[jax.ShapeDtypeStruct((B * T, W_C), BF16),
                   jax.ShapeDtypeStruct((B, H_C, L, L), F32),
                   jax.ShapeDtypeStruct((B, H_C, 1, L), F32),
                   jax.ShapeDtypeStruct((B, H_C, 1, LANES), F32)],
        compiler_params=_cparams(("parallel", "parallel")),
        name="mlstm_prefill",
    )(gate_bias, z, z, z, z, ig, fg, ig.swapaxes(2, 3), fg.swapaxes(2, 3), out_g.reshape(1, W_C))


SELECT_PAGES = 16


def _moba_select_kernel(pt_ref, *refs, pages_per_block):
    k_refs = refs[:SELECT_PAGES]
    q_ref, sel_ref, gate_ref = refs[SELECT_PAGES:]
    s = pl.program_id(1)
    ppb = pages_per_block
    blocks_per_step = SELECT_PAGES // ppb
    qb = jnp.broadcast_to(q_ref[...], k_refs[0].shape)
    for i in range(blocks_per_step):
        acc = k_refs[i * ppb][...]
        for g in range(1, ppb):
            acc = acc + k_refs[i * ppb + g][...]
        gate_ref[s * blocks_per_step + i] = jnp.sum(acc * qb, axis=1)

    @pl.when(s == pl.num_programs(1) - 1)
    def _():
        nb = gate_ref.shape[0]
        gate = jnp.sum(gate_ref[...], axis=-1) * (1.0 / MOBA_BLOCK)
        blk_id = lax.broadcasted_iota(jnp.int32, (nb, 1), 0)
        rank = jnp.zeros(gate.shape, jnp.int32)
        for m in range(nb):
            gm = gate[m:m + 1, :]
            beats = jnp.logical_or(gm > gate, jnp.logical_and(gm == gate, m < blk_id))
            rank = rank + beats.astype(jnp.int32)
        out_row = lax.broadcasted_iota(jnp.int32, sel_ref.shape, 0)
        out = jnp.zeros(sel_ref.shape, jnp.int32)
        for r in range(MOBA_TOPK):
            idx = jnp.sum(jnp.where(rank == r, blk_id, 0), axis=0, keepdims=True)
            out = jnp.where(out_row == r, idx, out)
        sel_ref[...] = out


def _moba_select(cache_kt, page_table, q_col):
    Bs, n_pages = page_table.shape
    _, H, dh, page = cache_kt.shape
    ppb = MOBA_BLOCK // page
    nb = n_pages // ppb
    assert nb >= MOBA_TOPK and n_pages % SELECT_PAGES == 0 and SELECT_PAGES % ppb == 0

    def page_spec(i):
        return pl.BlockSpec((None, H, dh, page), lambda b, s, pt: (pt[b, s * SELECT_PAGES + i], 0, 0, 0))

    grid_spec = pltpu.PrefetchScalarGridSpec(
        num_scalar_prefetch=1,
        grid=(Bs, n_pages // SELECT_PAGES),
        in_specs=[page_spec(i) for i in range(SELECT_PAGES)]
        + [pl.BlockSpec((None, H, dh, 1), lambda b, s, pt: (b, 0, 0, 0))],
        out_specs=pl.BlockSpec((None, 8, H), lambda b, s, pt: (b, 0, 0)),
        scratch_shapes=[pltpu.VMEM((nb, H, page), F32)],
    )
    return pl.pallas_call(
        functools.partial(_moba_select_kernel, pages_per_block=ppb),
        grid_spec=grid_spec,
        out_shape=jax.ShapeDtypeStruct((Bs, 8, H), jnp.int32),
        compiler_params=_cparams(("parallel", "arbitrary")),
        name="moba_select",
    )(page_table, *([cache_kt] * SELECT_PAGES), q_col)


def _moba_decode_kernel(pt_ref, sel_ref, q_ref, kn_ref, vn_ref, ck_ref, cv_ref, o_ref, kbuf, vbuf, sem,
                        *, pages_per_block, page):
    b = pl.program_id(0)
    nb_ = pl.num_programs(0)
    H = q_ref.shape[0]
    ppb = pages_per_block

    def copies(bb, slot):
        out = []
        for h in range(H):
            for r in range(MOBA_TOPK):
                blk = sel_ref[bb, r, h]
                for g in range(ppb):
                    pg = pt_ref[bb, blk * ppb + g]
                    dst = pl.ds((r * ppb + g) * page, page)
                    out.append(pltpu.make_async_copy(ck_ref.at[pg, h], kbuf.at[slot, h, :, dst], sem.at[slot, 0]))
                    out.append(pltpu.make_async_copy(cv_ref.at[pg, h], vbuf.at[slot, h, :, dst], sem.at[slot, 1]))
        return out

    slot = b % 2

    @pl.when(b == 0)
    def _():
        for c in copies(b, slot):
            c.start()

    @pl.when(b + 1 < nb_)
    def _():
        for c in copies(b + 1, 1 - slot):
            c.start()

    for c in copies(b, slot):
        c.wait()

    row = lax.broadcasted_iota(jnp.int32, (H, 1), 0)
    out = jnp.zeros(o_ref.shape, F32)
    q = q_ref[...]
    kn = kn_ref[...]
    vn = vn_ref[...]
    for h in range(H):
        qh = q[h:h + 1, :] * (DH_A ** -0.5)
        q8 = jnp.broadcast_to(qh, (8, qh.shape[1])).astype(BF16)
        kh = kbuf[slot, h].astype(BF16)
        vh = vbuf[slot, h].astype(BF16)
        s = jnp.dot(q8, kh, preferred_element_type=F32)[0:1, :]
        s_self = jnp.sum(qh * kn[h:h + 1, :], axis=-1, keepdims=True)
        m = jnp.maximum(jnp.max(s, axis=-1, keepdims=True), s_self)
        p = jnp.exp(s - m)
        p_self = jnp.exp(s_self - m)
        l = jnp.sum(p, axis=-1, keepdims=True) + p_self
        p8 = jnp.broadcast_to(p, (8, p.shape[1])).astype(BF16)
        pv = _nt(p8, vh)[0:1, :]
        oh = (pv + p_self * vn[h:h + 1, :]) / l
        out = jnp.where(row == h, oh, out)
    o_ref[...] = out


def _moba_decode(cache_kt, cache_vt, page_table, sel, q, k_new, v_new):
    Bs, n_pages = page_table.shape
    _, H, dh, page = cache_kt.shape
    ppb = MOBA_BLOCK // page
    rows = MOBA_TOPK * MOBA_BLOCK
    vec = pl.BlockSpec((None, H, dh), lambda b, pt, sl: (b, 0, 0))
    grid_spec = pltpu.PrefetchScalarGridSpec(
        num_scalar_prefetch=2,
        grid=(Bs,),
        in_specs=[vec, vec, vec, pl.BlockSpec(memory_space=pl.ANY), pl.BlockSpec(memory_space=pl.ANY)],
        out_specs=vec,
        scratch_shapes=[pltpu.VMEM((2, H, dh, rows), F32), pltpu.VMEM((2, H, dh, rows), F32),
                        pltpu.SemaphoreType.DMA((2, 2))],
    )
    return pl.pallas_call(
        functools.partial(_moba_decode_kernel, pages_per_block=ppb, page=page),
        grid_spec=grid_spec,
        out_shape=jax.ShapeDtypeStruct((Bs, H, dh), F32),
        compiler_params=_cparams(("arbitrary",)),
        name="moba_decode",
    )(page_table, sel, q, k_new, v_new, cache_kt, cache_vt)


def _ret_decode_kernel(q_ref, k_ref, v_ref, gb_ref, rg_ref, g_ref, s_ref, o_ref, so_ref):
    H, dk = q_ref.shape
    eye = _eye(dk)
    row = lax.broadcasted_iota(jnp.int32, (H, 1), 0)
    q = q_ref[...]
    k = k_ref[...]
    v = v_ref[...]
    g_all = g_ref[...]
    out = jnp.zeros(o_ref.shape, F32)
    for h in range(H):
        qh, kh, vh = q[h:h + 1, :], k[h:h + 1, :], v[h:h + 1, :]
        g = g_all[h:h + 1, 0:1]
        S = s_ref[h]
        att = jnp.sum(qh * kh, axis=-1, keepdims=True)
        cross = jnp.sum(_row_to_col(qh * g, eye) * S, axis=0, keepdims=True)
        out = jnp.where(row == h, att * vh + cross, out)
        so_ref[h] = S * g + _row_to_col(kh, eye) * vh
    gb = gb_ref[...]
    y = out * lax.rsqrt(jnp.mean(out * out, axis=-1, keepdims=True) + EPS) * rg_ref[...]
    o_ref[...] = y * (gb * _sigmoid(gb))


def _ret_decode(q, k, v, gb, ret_g, state):
    Bs, H, dk = q.shape
    log_g = jnp.log1p(-jnp.exp2(-5.0 - jnp.arange(H_B, dtype=F32)))
    g = jnp.broadcast_to(jnp.exp(1.0 * log_g)[:, None], (H, LANES))
    vec = pl.BlockSpec((None, H, dk), lambda b: (b, 0, 0))
    st = pl.BlockSpec((None, H, dk, DV_B), lambda b: (b, 0, 0, 0))
    return pl.pallas_call(
        _ret_decode_kernel,
        grid=(Bs,),
        in_specs=[vec, vec, vec, vec, pl.BlockSpec((H, dk), lambda b: (0, 0)),
                  pl.BlockSpec((H, LANES), lambda b: (0, 0)), st],
        out_specs=[vec, st],
        out_shape=[jax.ShapeDtypeStruct((Bs, H, DV_B), F32), jax.ShapeDtypeStruct(state.shape, F32)],
        compiler_params=_cparams(("parallel",)),
        name="retention_decode",
    )(q, k, v, gb, ret_g.reshape(H, DV_B), g, state)


def _mlstm_decode_kernel(q_ref, k_ref, v_ref, og_ref, ig_ref, fg_ref, bi_ref, bf_ref, ng_ref, c_ref, n_ref,
                         m_ref, h_ref, co_ref, no_ref, mo_ref):
    H, d = q_ref.shape
    eye = _eye(d)
    row = lax.broadcasted_iota(jnp.int32, (H, 1), 0)
    q = q_ref[...]
    k = k_ref[...] * (DH_C ** -0.5)
    v = v_ref[...]
    n0 = n_ref[...]
    ig = ig_ref[...] + bi_ref[...]
    b = _log_sigmoid(fg_ref[...] + bf_ref[...])
    m0 = m_ref[...]
    log_inter = b + m0
    m_t = jnp.maximum(log_inter, ig)
    w_intra = jnp.exp(ig - m_t)
    w_inter = jnp.exp(log_inter - m_t)
    s = jnp.sum(q * k, axis=-1, keepdims=True) * w_intra
    den = s + w_inter * jnp.sum(q * n0, axis=-1, keepdims=True)
    scale = 1.0 / jnp.maximum(jnp.abs(den), jnp.exp(-m_t))
    w_k = w_intra
    decay = w_inter
    hs = jnp.zeros(h_ref.shape, F32)
    for h in range(H):
        qh, kh, vh = q[h:h + 1, :], k[h:h + 1, :], v[h:h + 1, :]
        Cm = c_ref[h]
        cq = _col_to_row(jnp.sum(Cm * qh, axis=1, keepdims=True), eye)
        num = s[h:h + 1, :] * vh + cq * w_inter[h:h + 1, :]
        hs = jnp.where(row == h, num * scale[h:h + 1, :], hs)
        co_ref[h] = decay[h:h + 1, :] * Cm + _row_to_col(vh * w_k[h:h + 1, :], eye) * kh
    no_ref[...] = decay * n0 + k * w_k
    mo_ref[...] = m_t
    hn = hs * lax.rsqrt(jnp.mean(hs * hs, axis=-1, keepdims=True) + EPS) * ng_ref[...]
    h_ref[...] = hn * _sigmoid(og_ref[...])


def _mlstm_decode(q, k, v, og, ig, fg, gate_bias, out_g, C0, n0, m0):
    Bs, H, d = q.shape
    vec = pl.BlockSpec((None, H, d), lambda b: (b, 0, 0))
    sc = pl.BlockSpec((None, H, 1), lambda b: (b, 0, 0))
    st = pl.BlockSpec((None, H, d, d), lambda b: (b, 0, 0, 0))
    bias = pl.BlockSpec((H, 1), lambda b: (0, 0))
    return pl.pallas_call(
        _mlstm_decode_kernel,
        grid=(Bs,),
        in_specs=[vec, vec, vec, vec, sc, sc, bias, bias, pl.BlockSpec((H, d), lambda b: (0, 0)), st, vec, sc],
        out_specs=[vec, st, vec, sc],
        out_shape=[jax.ShapeDtypeStruct((Bs, H, d), F32), jax.ShapeDtypeStruct(C0.shape, F32),
                   jax.ShapeDtypeStruct((Bs, H, d), F32), jax.ShapeDtypeStruct((Bs, H, 1), F32)],
        compiler_params=_cparams(("parallel",)),
        name="mlstm_decode",
    )(q, k, v, og, ig, fg, gate_bias[0].reshape(H, 1), gate_bias[1].reshape(H, 1), out_g.reshape(H, d), C0, n0, m0)


def _rope_tables(pos):
    half = DH_A // 2
    inv = ROPE_THETA ** (-jnp.arange(half, dtype=F32) / half)
    ang = pos[:, None] * inv[None, :]
    cos = jnp.cos(ang)
    sin = jnp.sin(ang)
    reps = LANES // DH_A
    cos_t = jnp.tile(jnp.concatenate([cos, cos], axis=-1), (1, reps))
    sin_t = jnp.tile(jnp.concatenate([-sin, sin], axis=-1), (1, reps))
    return cos_t, sin_t


def _row_tile(m, pref):
    return pref if m % pref == 0 else m


def kernel(x_prompt, x_sample, cache_k, cache_v, page_table, state_ret, state_mlstm_C, state_mlstm_n,
           state_mlstm_m, state_ffn_conv, ab_norm_g, ab_w_in, ab_ret_norm_g, ab_w_out, c_norm_g, c_w_in,
           c_gate_bias, c_out_norm_g, c_w_out, ffn_norm_g, ffn_w1, ffn_w3, ffn_conv_w, ffn_conv_b, ffn_w2,
           final_norm_g):
    Bp, Tp, D = x_prompt.shape
    Bs, Ts, _ = x_sample.shape
    assert Ts == 1
    past_len = page_table.shape[1] * cache_k.shape[2]
    Mp = Bp * Tp
    xp = x_prompt.reshape(Mp, D)
    xs = x_sample.reshape(Bs, D)
    tm_p = _row_tile(Tp, 1024)
    tm_ffn = _row_tile(Tp, 512)
    tm_s = Bs
    tf = 256
    rope_p = _rope_tables(jnp.arange(Tp, dtype=jnp.int32).astype(F32))
    rope_s = _rope_tables(jnp.full((Bs,), past_len, jnp.int32).astype(F32))
    gw = 2 * H_C

    outs = {}

    w_in = ab_w_in[0].astype(BF16)
    w_out = ab_w_out[0].astype(BF16)
    w_out_a, w_out_b = w_out[:W_A], w_out[W_A:]

    z = _norm_matmul(xp, ab_norm_g[0], w_in, tm_p, AB_TILE, rope=rope_p)
    oa, kt, vt = _moba_prefill(z, Bp, Tp)
    ob, s_pair = _ret_prefill(z, ab_ret_norm_g[0], Bp, Tp)
    xp = _proj_residual([oa, ob], [w_out_a, w_out_b], xp, tm_p)
    outs["k_prompt"] = kt.reshape(Bp, H_A, DH_A, Tp).transpose(0, 3, 1, 2)[None]
    outs["v_prompt"] = vt.reshape(Bp, H_A, DH_A, Tp).transpose(0, 3, 1, 2)[None]
    hb = LANES // DK_B
    s_heads = jnp.stack([s_pair[:, :, i * DK_B:(i + 1) * DK_B, i * DV_B:(i + 1) * DV_B] for i in range(hb)], axis=2)
    outs["ret_prompt"] = s_heads.reshape(1, Bp, H_B, DK_B, DV_B)

    zs = _norm_matmul(xs, ab_norm_g[0], w_in, tm_s, AB_TILE, rope=rope_s)
    seg = lambda t: zs[:, t * AB_TILE:(t + 1) * AB_TILE].reshape(Bs, H_A, DH_A)
    qa_s, ka_s, va_s, qb_s, kb_s, vb_s, gb_s = (seg(t) for t in range(7))
    cache_kt = cache_k[0].transpose(0, 2, 3, 1)
    cache_vt = cache_v[0].transpose(0, 2, 3, 1)
    sel = _moba_select(cache_kt, page_table, qa_s[..., None])
    oa_s = _moba_decode(cache_kt, cache_vt, page_table, sel, qa_s, ka_s, va_s)
    ob_s, s_new = _ret_decode(qb_s, kb_s, vb_s, gb_s, ab_ret_norm_g[0], state_ret[0])
    xs = _proj_residual([oa_s.reshape(Bs, W_A).astype(BF16), ob_s.reshape(Bs, W_B).astype(BF16)],
                        [w_out_a, w_out_b], xs, tm_s)
    outs["k_sample"] = ka_s.reshape(1, Bs, 1, H_A, DH_A)
    outs["v_sample"] = va_s.reshape(1, Bs, 1, H_A, DH_A)
    outs["ret_sample"] = s_new[None]

    conv_p, conv_s = [], []

    def ffn_both(l, xp, xs, final_g):
        w1 = ffn_w1[l].astype(BF16)
        w3 = ffn_w3[l].astype(BF16)
        w2 = ffn_w2[l].astype(BF16)
        tail = xp.reshape(Bp, Tp, D)[:, Tp - 8:, :].reshape(Bp * 8, D)
        a_tail = _norm_matmul(tail, ffn_norm_g[l], w1, Bp * 8, tf)
        conv_p.append(a_tail.reshape(Bp, 8, D_FF)[:, 8 - (CONV_W - 1):, :])
        a_s = _norm_matmul(xs, ffn_norm_g[l], w1, tm_s, tf)
        st = state_ffn_conv[l]
        conv_s.append(jnp.stack([st[:, 1, :], a_s], axis=1))
        xp = _ffn(xp, ffn_norm_g[l], w1, w3, ffn_conv_w[l], ffn_conv_b[l], w2, final_g, tm_ffn, seq_len=Tp)
        xs = _ffn(xs, ffn_norm_g[l], w1, w3, ffn_conv_w[l], ffn_conv_b[l], w2, final_g, tm_s,
                  state=(st[:, 0, :], st[:, 1, :]))
        return xp, xs

    xp, xs = ffn_both(0, xp, xs, None)

    w_in = c_w_in[0]
    w_main = w_in[:, :4 * W_C].astype(BF16)
    w_gate = jnp.pad(w_in[:, 4 * W_C:], ((0, 0), (0, LANES - gw))).astype(BF16)
    w_out = c_w_out[0].astype(BF16)
    L = MLSTM_CHUNK

    z = _norm_matmul(xp, c_norm_g[0], w_main, tm_p, 512)
    zg = _norm_matmul(xp, c_norm_g[0], w_gate, tm_p, LANES)
    gates = zg[:, :gw].reshape(Bp, Tp // L, L, 2, H_C).transpose(3, 0, 4, 1, 2)
    h, C_p, n_p, m_p = _mlstm_prefill(z, gates[0], gates[1], c_gate_bias[0], c_out_norm_g[0], Bp, Tp)
    xp = _proj_residual([h], [w_out], xp, tm_p)
    outs["C_prompt"] = C_p[None]
    outs["n_prompt"] = n_p.reshape(1, Bp, H_C, DH_C)
    outs["m_prompt"] = m_p[:, :, 0, 0][None]

    zs = _norm_matmul(xs, c_norm_g[0], w_main, tm_s, 512)
    zgs = _norm_matmul(xs, c_norm_g[0], w_gate, tm_s, LANES)
    segc = lambda t: zs[:, t * W_C:(t + 1) * W_C].reshape(Bs, H_C, DH_C)
    ig_s = zgs[:, :H_C][:, :, None]
    fg_s = zgs[:, H_C:gw][:, :, None]
    h_s, C_s, n_s, m_s = _mlstm_decode(segc(0), segc(1), segc(2), segc(3), ig_s, fg_s, c_gate_bias[0], c_out_norm_g[0],
                                       state_mlstm_C[0], state_mlstm_n[0], state_mlstm_m[0][:, :, None])
    xs = _proj_residual([h_s.reshape(Bs, W_C).astype(BF16)], [w_out], xs, tm_s)
    outs["C_sample"] = C_s[None]
    outs["n_sample"] = n_s[None]
    outs["m_sample"] = m_s[:, :, 0][None]

    xp, xs = ffn_both(1, xp, xs, final_norm_g)

    return (xp.reshape(Bp, Tp, D), xs.reshape(Bs, 1, D),
            outs["k_prompt"], outs["v_prompt"], outs["k_sample"], outs["v_sample"],
            outs["ret_prompt"], outs["ret_sample"], outs["C_prompt"], outs["C_sample"],
            outs["n_prompt"], outs["n_sample"], outs["m_prompt"], outs["m_sample"],
            jnp.stack(conv_p), jnp.stack(conv_s))
```

```python
import functools

import jax
import jax.numpy as jnp
from jax import lax
from jax.experimental import pallas as pl
from jax.experimental.pallas import tpu as pltpu

F32 = jnp.float32
BF16 = jnp.bfloat16
HIGHEST = lax.Precision.HIGHEST

LANES = 128
D_MODEL = 1024
H_A = 8
DH_A = 64
MOBA_BLOCK = 256
MOBA_TOPK = 3
H_B = 8
DK_B = 64
DV_B = 64
RET_CHUNK = 128
H_C = 8
DH_C = D_MODEL // H_C
MLSTM_CHUNK = 128
D_FF = 11 * D_MODEL // 4
CONV_W = 3
ROPE_THETA = 10000.0
EPS = 1e-6
W_A = H_A * DH_A
W_B = H_B * DV_B
W_C = H_C * DH_C
IN_AB = 3 * W_A + 2 * H_B * DK_B + 2 * W_B
AB_TILE = 512
AB_ROPE_TILES = (0, 1, 3, 4)
AB_KB_TILE = 4
VMEM_LIMIT = 56 * 1024 * 1024

NEG_INF = float("-inf")
LOG2_E = 1.4426950408889634


def _cparams(sem):
    return pltpu.CompilerParams(dimension_semantics=sem, vmem_limit_bytes=VMEM_LIMIT)


def _nt(a, b, **kw):
    return lax.dot_general(a, b, (((1,), (1,)), ((), ())), preferred_element_type=F32, **kw)


def _tn(a, b, **kw):
    return lax.dot_general(a, b, (((0,), (0,)), ((), ())), preferred_element_type=F32, **kw)


def _rms_rows(x, g):
    ms = jnp.mean(x * x, axis=-1, keepdims=True)
    return x * lax.rsqrt(ms + EPS) * g


def _eye(n):
    return lax.broadcasted_iota(jnp.int32, (n, n), 0) == lax.broadcasted_iota(jnp.int32, (n, n), 1)


def _row_to_col(row, eye):
    return jnp.sum(jnp.where(eye, row, 0.0), axis=1, keepdims=True)


def _col_to_row(col, eye):
    return jnp.sum(jnp.where(eye, col, 0.0), axis=0, keepdims=True)


def _rowsum_rep(x):
    ones = jnp.ones((x.shape[1], LANES), BF16)
    hi = x.astype(BF16)
    lo = (x - hi.astype(F32)).astype(BF16)
    return (jnp.dot(hi, ones, preferred_element_type=F32) + jnp.dot(lo, ones, preferred_element_type=F32))


def _log_sigmoid(x):
    return jnp.minimum(x, 0.0) - jnp.log1p(jnp.exp(-jnp.abs(x)))


def _sigmoid(x):
    return 1.0 / (1.0 + jnp.exp(-x))


def _gelu_tanh(x):
    c = 0.7978845608028654
    return (0.5 * x) * (1.0 + jnp.tanh(x * (c + (0.044715 * c) * (x * x))))


def _norm_matmul_kernel(x_ref, g_ref, w_ref, cos_ref, sin_ref, o_ref, xn_ref, *, rope_tiles, scale_tile, scale):
    j = pl.program_id(1)

    @pl.when(j == 0)
    def _():
        xn_ref[...] = _rms_rows(x_ref[...], g_ref[...]).astype(BF16)

    z = jnp.dot(xn_ref[...], w_ref[...], preferred_element_type=F32)
    if not rope_tiles:
        o_ref[...] = z
        return

    is_rope = functools.reduce(jnp.logical_or, [j == t for t in rope_tiles])

    @pl.when(is_rope)
    def _():
        sc = jnp.where(j == scale_tile, scale, 1.0).astype(F32)
        lane = lax.broadcasted_iota(jnp.int32, (1, LANES), 1)
        first_half = (lane % DH_A) < (DH_A // 2)
        cos = cos_ref[...]
        sin = sin_ref[...]
        for c in range(z.shape[1] // LANES):
            zc = z[:, c * LANES:(c + 1) * LANES]
            partner = jnp.where(first_half, pltpu.roll(zc, LANES - DH_A // 2, 1), pltpu.roll(zc, DH_A // 2, 1))
            o_ref[:, c * LANES:(c + 1) * LANES] = (zc * cos + partner * sin) * sc

    @pl.when(jnp.logical_not(is_rope))
    def _():
        o_ref[...] = z


def _norm_matmul(x, g, w, tm, tn, rope=None):
    M, D = x.shape
    N = w.shape[1]
    assert M % tm == 0 and N % tn == 0
    if rope is None:
        cos = sin = jnp.zeros((8, LANES), F32)
        tab_spec = pl.BlockSpec((8, LANES), lambda i, j: (0, 0))
        kern = functools.partial(_norm_matmul_kernel, rope_tiles=(), scale_tile=-1, scale=1.0)
    else:
        cos, sin = rope
        nt = cos.shape[0] // tm
        tab_spec = pl.BlockSpec((tm, LANES), lambda i, j: (i % nt, 0))
        kern = functools.partial(_norm_matmul_kernel, rope_tiles=AB_ROPE_TILES, scale_tile=AB_KB_TILE,
                                 scale=DK_B ** -0.5)
    return pl.pallas_call(
        kern,
        grid=(M // tm, N // tn),
        in_specs=[pl.BlockSpec((tm, D), lambda i, j: (i, 0)),
                  pl.BlockSpec((1, D), lambda i, j: (0, 0)),
                  pl.BlockSpec((D, tn), lambda i, j: (0, j)),
                  tab_spec, tab_spec],
        out_specs=pl.BlockSpec((tm, tn), lambda i, j: (i, j)),
        out_shape=jax.ShapeDtypeStruct((M, N), F32),
        scratch_shapes=[pltpu.VMEM((tm, D), BF16)],
        compiler_params=_cparams(("parallel", "arbitrary")),
        name="norm_matmul",
    )(x, g.reshape(1, D), w, cos, sin)


def _proj_residual_kernel(*refs, n_in):
    a_refs = refs[:n_in]
    w_refs = refs[n_in:2 * n_in]
    res_ref = refs[2 * n_in]
    o_ref = refs[2 * n_in + 1]
    y = res_ref[...]
    acc = None
    for a_ref, w_ref in zip(a_refs, w_refs):
        d = jnp.dot(a_ref[...], w_ref[...], preferred_element_type=F32)
        acc = d if acc is None else acc + d
    o_ref[...] = y + acc


def _proj_residual(acts, ws, res, tm):
    M, D = res.shape
    n_in = len(acts)
    in_specs = ([pl.BlockSpec((tm, a.shape[1]), lambda i: (i, 0)) for a in acts]
                + [pl.BlockSpec(w.shape, lambda i: (0, 0)) for w in ws]
                + [pl.BlockSpec((tm, D), lambda i: (i, 0))])
    return pl.pallas_call(
        functools.partial(_proj_residual_kernel, n_in=n_in),
        grid=(M // tm,),
        in_specs=in_specs,
        out_specs=pl.BlockSpec((tm, D), lambda i: (i, 0)),
        out_shape=jax.ShapeDtypeStruct((M, D), F32),
        compiler_params=_cparams(("parallel",)),
        name="proj_residual",
    )(*acts, *ws, res)


PREV_ROWS = 16
FFN_CHUNK = 256


def _ffn_kernel(*refs, seq_mode, tiles_per_seq, final_norm):
    if seq_mode:
        (x_ref, xp_ref, g_ref, w1_ref, w3_ref, cw_ref, cb_ref, w2_ref, fg_ref, o_ref, y_ref) = refs
    else:
        (x_ref, s0_ref, s1_ref, g_ref, w1_ref, w3_ref, cw_ref, cb_ref, w2_ref, fg_ref, o_ref, y_ref) = refs
    i = pl.program_id(0)
    x = x_ref[...]
    tm = x.shape[0]
    F = w1_ref.shape[1]
    xn = _rms_rows(x, g_ref[...]).astype(BF16)
    if seq_mode:
        xpn = _rms_rows(xp_ref[...], g_ref[...]).astype(BF16)
        has_prev = ((i % tiles_per_seq) != 0).astype(F32)
        row8 = lax.broadcasted_iota(jnp.int32, (8, 1), 0)
    for c0 in range(0, F, FFN_CHUNK):
        cols = slice(c0, min(c0 + FFN_CHUNK, F))
        w1c = w1_ref[:, cols]
        a = jnp.dot(xn, w1c, preferred_element_type=F32)
        gate = jnp.dot(xn, w3_ref[:, cols], preferred_element_type=F32)
        if seq_mode:
            ap = jnp.dot(xpn, w1c, preferred_element_type=F32)
            p1 = ap[PREV_ROWS - 1:PREV_ROWS, :] * has_prev
            p2 = ap[PREV_ROWS - 2:PREV_ROWS - 1, :] * has_prev
            a1 = pltpu.roll(a, 1, 0)
            a2 = pltpu.roll(a, 2, 0)
            top1 = jnp.where(row8 == 0, p1, a1[0:8, :])
            top2 = jnp.where(row8 == 0, p2, jnp.where(row8 == 1, p1, a2[0:8, :]))
            a1 = jnp.concatenate([top1, a1[8:, :]], axis=0)
            a2 = jnp.concatenate([top2, a2[8:, :]], axis=0)
        else:
            a1 = s1_ref[:, cols]
            a2 = s0_ref[:, cols]
        ac = cb_ref[:, cols] + a2 * cw_ref[0:1, cols]
        ac = ac + a1 * cw_ref[1:2, cols]
        ac = ac + a * cw_ref[2:3, cols]
        y_ref[:, cols] = (_gelu_tanh(ac) * gate).astype(BF16)
    acc = x + jnp.dot(y_ref[...], w2_ref[...], preferred_element_type=F32)
    if final_norm:
        acc = _rms_rows(acc, fg_ref[...])
    o_ref[...] = acc


def _ffn(x, norm_g, w1, w3, conv_w, conv_b, w2, final_g, tm, seq_len=None, state=None):
    M, D = x.shape
    F = w1.shape[1]
    seq_mode = state is None
    final_norm = final_g is not None
    fg = (final_g if final_norm else jnp.ones((D,), F32)).reshape(1, D)
    whole = lambda shape: pl.BlockSpec(shape, lambda i: (0, 0), pipeline_mode=pl.Buffered(1))
    common = [whole((1, D)), whole((D, F)), whole((D, F)), whole((CONV_W, F)), whole((1, F)), whole((F, D)),
              whole((1, D))]
    common_args = (norm_g.reshape(1, D), w1, w3, conv_w, conv_b.reshape(1, F), w2, fg)
    x_spec = pl.BlockSpec((tm, D), lambda i: (i, 0))
    if seq_mode:
        assert seq_len % tm == 0 and tm % PREV_ROWS == 0
        r = tm // PREV_ROWS
        in_specs = [x_spec, pl.BlockSpec((PREV_ROWS, D), lambda i: (jnp.maximum(i * r - 1, 0), 0))] + common
        args = (x, x) + common_args
        tiles_per_seq = seq_len // tm
    else:
        s_spec = pl.BlockSpec((tm, F), lambda i: (i, 0))
        in_specs = [x_spec, s_spec, s_spec] + common
        args = (x, state[0], state[1]) + common_args
        tiles_per_seq = 1
    return pl.pallas_call(
        functools.partial(_ffn_kernel, seq_mode=seq_mode, tiles_per_seq=tiles_per_seq, final_norm=final_norm),
        grid=(M // tm,),
        in_specs=in_specs,
        out_specs=pl.BlockSpec((tm, D), lambda i: (i, 0)),
        out_shape=jax.ShapeDtypeStruct((M, D), F32),
        scratch_shapes=[pltpu.VMEM((tm, F), BF16)],
        compiler_params=_cparams(("parallel",)),
        name="conv_ffn",
    )(*args)


def _moba_prefill_kernel(q_ref, k_ref, v_ref, o_ref, kt_ref, vt_ref, kb_ref, vb_ref, *, nb):
    blk = MOBA_BLOCK
    heads = LANES // DH_A
    T = q_ref.shape[0]
    k = k_ref[...]
    kt_ref[...] = k.T
    vt_ref[...] = v_ref[...].T
    kb_ref[...] = k.astype(BF16)
    vb_ref[...] = v_ref[...].astype(BF16)
    kmean = jnp.concatenate(
        [jnp.sum(k[n * blk:(n + 1) * blk, :], axis=0, keepdims=True) * (1.0 / blk) for n in range(nb)]
        + [jnp.zeros((8 - nb, LANES), F32)] * (nb < 8), axis=0)

    q2 = q_ref[...]
    lane = lax.broadcasted_iota(jnp.int32, (1, LANES), 1)
    blk_id = lax.broadcasted_iota(jnp.int32, (8, 1), 0)
    q_blk = lax.broadcasted_iota(jnp.int32, (1, T), 1) // blk
    past = blk_id < q_blk
    causal = (lax.broadcasted_iota(jnp.int32, (blk, blk), 1) <= lax.broadcasted_iota(jnp.int32, (blk, blk), 0))
    hms, qss, sels = [], [], []
    for h in range(heads):
        hm = (lane // DH_A) == h
        qh = jnp.where(hm, q2, 0.0)
        gt = _nt(kmean, qh, precision=HIGHEST)
        sel_t = jnp.zeros_like(gt)
        for n in range(nb - 1):
            gn = gt[n:n + 1, :]
            beats = jnp.logical_and(past, jnp.logical_or(gt > gn, jnp.logical_and(gt == gn, blk_id < n)))
            rank = jnp.sum(beats.astype(F32), axis=0, keepdims=True)
            sel_n = jnp.where(jnp.logical_and(rank < MOBA_TOPK, n < q_blk), 1.0, 0.0)
            sel_t = jnp.where(blk_id == n, sel_n, sel_t)
        hms.append(hm)
        qss.append((qh * (DH_A ** -0.5 * LOG2_E)).astype(BF16))
        sels.append(sel_t.T)

    for qi in range(nb):
        rows = slice(qi * blk, (qi + 1) * blk)
        outs = []
        for h in range(heads):
            qs = qss[h][rows, :]
            pieces = []
            for n in range(qi + 1):
                s = _nt(qs, kb_ref[n * blk:(n + 1) * blk, :])
                keep = causal if n == qi else (sels[h][rows, n:n + 1] > 0.5)
                pieces.append(jnp.where(keep, s, NEG_INF))
            m = functools.reduce(jnp.maximum, pieces)
            m = jnp.max(m, axis=-1, keepdims=True)
            lsum = None
            acc = None
            for n in range(qi + 1):
                p = jnp.exp2(pieces[n] - m)
                lsum = p if lsum is None else lsum + p
                d = jnp.dot(p.astype(BF16), vb_ref[n * blk:(n + 1) * blk, :], preferred_element_type=F32)
                acc = d if acc is None else acc + d
            outs.append(acc / jnp.sum(lsum, axis=-1, keepdims=True))
        out = outs[0]
        for h in range(1, heads):
            out = jnp.where(hms[h], outs[h], out)
        o_ref[rows, :] = out.astype(o_ref.dtype)


def _moba_prefill(z, B, T):
    blk = MOBA_BLOCK
    assert T % blk == 0
    nb = T // blk
    assert nb <= 8
    cpt = AB_TILE // LANES
    return pl.pallas_call(
        functools.partial(_moba_prefill_kernel, nb=nb),
        grid=(B, W_A // LANES),
        in_specs=[pl.BlockSpec((T, LANES), lambda b, p: (b, p)),
                  pl.BlockSpec((T, LANES), lambda b, p: (b, cpt + p)),
                  pl.BlockSpec((T, LANES), lambda b, p: (b, 2 * cpt + p))],
        out_specs=[pl.BlockSpec((T, LANES), lambda b, p: (b, p)),
                   pl.BlockSpec((None, LANES, T), lambda b, p: (b, p, 0)),
                   pl.BlockSpec((None, LANES, T), lambda b, p: (b, p, 0))],
        out_shape=[jax.ShapeDtypeStruct((B * T, W_A), BF16),
                   jax.ShapeDtypeStruct((B, W_A, T), F32),
                   jax.ShapeDtypeStruct((B, W_A, T), F32)],
        scratch_shapes=[pltpu.VMEM((T, LANES), BF16), pltpu.VMEM((T, LANES), BF16)],
        compiler_params=_cparams(("parallel", "parallel")),
        name="moba_prefill",
    )(z, z, z)


def _ret_prefill_kernel(q_ref, k_ref, v_ref, gb_ref, rg_ref, dmask_ref, din_ref, dout_ref, gch_ref,
                        o_ref, s_ref, *, n_chunks):
    C = RET_CHUNK
    heads = LANES // DK_B
    lane = lax.broadcasted_iota(jnp.int32, (1, LANES), 1)
    row_h = lax.broadcasted_iota(jnp.int32, (LANES, LANES), 0) // DK_B
    col_h = lax.broadcasted_iota(jnp.int32, (LANES, LANES), 1) // DV_B
    same_head = row_h == col_h
    seg_ones = jnp.where(same_head, 1.0, 0.0).astype(F32)
    din = din_ref[...]
    dout = dout_ref[...]
    gch = gch_ref[...]
    rg = rg_ref[...]

    def chunk(j, S):
        start = pl.multiple_of(j * C, C)
        q = q_ref[pl.ds(start, C), :]
        k = k_ref[pl.ds(start, C), :]
        v = v_ref[pl.ds(start, C), :]
        kb = k.astype(BF16)
        vb = v.astype(BF16)
        o = jnp.dot((q * din).astype(BF16), S.astype(BF16), preferred_element_type=F32)
        for h in range(heads):
            hm = (lane // DK_B) == h
            att = _nt(jnp.where(hm, q, 0.0).astype(BF16), kb) * dmask_ref[h]
            oh = jnp.dot(att.astype(BF16), vb, preferred_element_type=F32)
            o = o + jnp.where(hm, oh, 0.0)
        S = S * gch + jnp.where(same_head, _tn((k * dout).astype(BF16), vb), 0.0)
        ms = jnp.dot(o * o, seg_ones, preferred_element_type=F32, precision=HIGHEST) * (1.0 / DV_B)
        g = gb_ref[pl.ds(start, C), :]
        y = o * lax.rsqrt(ms + EPS) * rg * (g * _sigmoid(g))
        o_ref[pl.ds(start, C), :] = y.astype(o_ref.dtype)
        return S

    s_ref[...] = lax.fori_loop(0, n_chunks, chunk, jnp.zeros((LANES, LANES), F32), unroll=True)


def _ret_tables(chunk):
    log_g = jnp.log1p(-jnp.exp2(-5.0 - jnp.arange(H_B, dtype=F32)))
    i = jnp.arange(chunk, dtype=F32)
    d_in = jnp.exp((i[:, None] + 1.0) * log_g)
    d_out = jnp.exp((chunk - 1.0 - i)[:, None] * log_g)
    diff = i[:, None] - i[None, :]
    d_mask = jnp.where(diff >= 0, jnp.exp(jnp.maximum(diff, 0.0)[None] * log_g[:, None, None]), 0.0)
    g_chunk = jnp.exp(chunk * log_g)
    return d_in, d_out, d_mask, g_chunk


def _ret_prefill(z, ret_g, B, T):
    C = RET_CHUNK
    assert T % C == 0
    d_in, d_out, d_mask, g_chunk = _ret_tables(C)
    npair = W_B // LANES
    lanes = lambda t: jnp.repeat(t, DK_B, axis=-1)
    din_l = lanes(d_in).reshape(C, npair, LANES).transpose(1, 0, 2)
    dout_l = lanes(d_out).reshape(C, npair, LANES).transpose(1, 0, 2)
    gch_l = lanes(g_chunk).reshape(npair, 1, LANES)
    cpt = AB_TILE // LANES
    col = lambda t: (lambda b, p: (b, t * cpt + p))
    return pl.pallas_call(
        functools.partial(_ret_prefill_kernel, n_chunks=T // C),
        grid=(B, npair),
        in_specs=[pl.BlockSpec((T, LANES), col(3)), pl.BlockSpec((T, LANES), col(4)),
                  pl.BlockSpec((T, LANES), col(5)), pl.BlockSpec((T, LANES), col(6)),
                  pl.BlockSpec((1, LANES), lambda b, p: (0, p)),
                  pl.BlockSpec((LANES // DK_B, C, C), lambda b, p: (p, 0, 0)),
                  pl.BlockSpec((None, C, LANES), lambda b, p: (p, 0, 0)),
                  pl.BlockSpec((None, C, LANES), lambda b, p: (p, 0, 0)),
                  pl.BlockSpec((None, 1, LANES), lambda b, p: (p, 0, 0))],
        out_specs=[pl.BlockSpec((T, LANES), lambda b, p: (b, p)),
                   pl.BlockSpec((None, None, LANES, LANES), lambda b, p: (b, p, 0, 0))],
        out_shape=[jax.ShapeDtypeStruct((B * T, W_B), BF16),
                   jax.ShapeDtypeStruct((B, npair, LANES, LANES), F32)],
        compiler_params=_cparams(("parallel", "parallel")),
        name="retention_prefill",
    )(z, z, z, z, ret_g.reshape(1, W_B), d_mask, din_l, dout_l, gch_l)


def _mlstm_prefill_kernel(bias_ref, q_ref, k_ref, v_ref, og_ref, ig_ref, fg_ref, igc_ref, fgc_ref, ng_ref,
                          h_ref, c_ref, n_ref, m_ref, *, n_chunks):
    L = MLSTM_CHUNK
    hd = pl.program_id(1)
    r_id = lax.broadcasted_iota(jnp.int32, (L, L), 0)
    c_id = lax.broadcasted_iota(jnp.int32, (L, L), 1)
    causal = c_id <= r_id
    upper = jnp.where(r_id <= c_id, 1.0, 0.0).astype(F32)
    lower = jnp.where(c_id <= r_id, 1.0, 0.0).astype(F32)
    i_rows = ig_ref[...] + bias_ref[0, hd]
    b_rows = jnp.dot(_log_sigmoid(fg_ref[...] + bias_ref[1, hd]), upper, preferred_element_type=F32, precision=HIGHEST)
    i_cols = igc_ref[...] + bias_ref[0, hd]
    b_cols = jnp.dot(lower, _log_sigmoid(fgc_ref[...] + bias_ref[1, hd]), preferred_element_type=F32, precision=HIGHEST)
    ng = ng_ref[...]

    Cm = jnp.zeros((L, L), F32)
    n = jnp.zeros((1, L), F32)
    m = jnp.zeros((1, 1), F32)
    for j in range(n_chunks):
        rows = slice(j * L, (j + 1) * L)
        q = q_ref[rows, :]
        k = k_ref[rows, :] * (DH_C ** -0.5)
        v = v_ref[rows, :]
        b_row, i_row = b_rows[j:j + 1, :], i_rows[j:j + 1, :]
        b_col, i_col = b_cols[:, j:j + 1], i_cols[:, j:j + 1]
        b_last = b_row[:, L - 1:L]
        qb = q.astype(BF16)
        kb = k.astype(BF16)
        b_rep = jnp.broadcast_to(b_col, (L, L))
        log_d = jnp.where(causal, (b_rep - b_row) + i_row, NEG_INF)
        m_row = jnp.broadcast_to(jnp.max(log_d, axis=-1, keepdims=True), (L, L))
        s = _nt(qb, kb) * jnp.exp(log_d - m_row)
        sv = jnp.dot(s.astype(BF16), v.astype(BF16), preferred_element_type=F32)
        s_sum = _rowsum_rep(s)
        log_w = (b_last - b_col) + i_col
        m_loc = jnp.max(log_w, axis=0, keepdims=True)
        e_k = jnp.broadcast_to(jnp.exp(log_w - m_loc), (L, L))
        U = _tn((v * e_k).astype(BF16), kb)
        nk = jnp.sum(k * e_k, axis=0, keepdims=True)
        log_inter = b_rep + m
        m_t = jnp.maximum(log_inter, m_row)
        f_intra = jnp.exp(m_row - m_t)
        w_inter = jnp.exp(log_inter - m_t)
        num = sv * f_intra + _nt(qb, Cm.astype(BF16)) * w_inter
        den = s_sum * f_intra + w_inter * _rowsum_rep(q * n)
        h = num / jnp.maximum(jnp.abs(den), jnp.exp(-m_t))
        m_new = jnp.maximum(b_last + m, m_loc)
        decay = jnp.exp(b_last + m - m_new)
        f_k = jnp.exp(m_loc - m_new)
        Cm = decay * Cm + f_k * U
        n = decay * n + f_k * nk
        m = m_new
        hn = h * lax.rsqrt(_rowsum_rep(h * h) * (1.0 / L) + EPS) * ng
        h_ref[rows, :] = (hn * _sigmoid(og_ref[rows, :])).astype(h_ref.dtype)
    c_ref[...] = Cm
    n_ref[...] = n
    m_ref[...] = jnp.broadcast_to(m, m_ref.shape)


def _mlstm_prefill(z, ig, fg, gate_bias, out_g, B, T):
    L = MLSTM_CHUNK
    assert T % L == 0 and DH_C == LANES
    nc = T // L
    col = lambda t: (lambda b, h, bias: (b, t * H_C + h))
    gspec = pl.BlockSpec((None, None, nc, L), lambda b, h, bias: (b, h, 0, 0))
    gcspec = pl.BlockSpec((None, None, L, nc), lambda b, h, bias: (b, h, 0, 0))
    grid_spec = pltpu.PrefetchScalarGridSpec(
        num_scalar_prefetch=1,
        grid=(B, H_C),
        in_specs=[pl.BlockSpec((T, LANES), col(0)), pl.BlockSpec((T, LANES), col(1)),
                  pl.BlockSpec((T, LANES), col(2)), pl.BlockSpec((T, LANES), col(3)),
                  gspec, gspec, gcspec, gcspec,
                  pl.BlockSpec((1, LANES), lambda b, h, bias: (0, h))],
        out_specs=[pl.BlockSpec((T, LANES), lambda b, h, bias: (b, h)),
                   pl.BlockSpec((None, None, L, L), lambda b, h, bias: (b, h, 0, 0)),
                   pl.BlockSpec((None, None, 1, L), lambda b, h, bias: (b, h, 0, 0)),
                   pl.BlockSpec((None, None, 1, LANES), lambda b, h, bias: (b, h, 0, 0))],
    )
    return pl.pallas_call(
        functools.partial(_mlstm_prefill_kernel, n_chunks=nc),
        grid_spec=grid_spec,
        out_shape=[jax.ShapeDtypeStruct((B * T, W_C), BF16),
                   jax.ShapeDtypeStruct((B, H_C, L, L), F32),
                   jax.ShapeDtypeStruct((B, H_C, 1, L), F32),
                   jax.ShapeDtypeStruct((B, H_C, 1, LANES), F32)],
        compiler_params=_cparams(("parallel", "parallel")),
        name="mlstm_prefill",
    )(gate_bias, z, z, z, z, ig, fg, ig.swapaxes(2, 3), fg.swapaxes(2, 3), out_g.reshape(1, W_C))


SELECT_PAGES = 16


def _moba_select_kernel(pt_ref, *refs, pages_per_block):
    k_refs = refs[:SELECT_PAGES]
    q_ref, sel_ref, gate_ref = refs[SELECT_PAGES:]
    s = pl.program_id(1)
    ppb = pages_per_block
    blocks_per_step = SELECT_PAGES // ppb
    qb = jnp.broadcast_to(q_ref[...], k_refs[0].shape)
    for i in range(blocks_per_step):
        acc = k_refs[i * ppb][...]
        for g in range(1, ppb):
            acc = acc + k_refs[i * ppb + g][...]
        prod = acc * qb
        part = prod[:, 0:8, :]
        for r in range(1, prod.shape[1] // 8):
            part = part + prod[:, 8 * r:8 * (r + 1), :]
        gate_ref[:, s * blocks_per_step + i] = part

    @pl.when(s == pl.num_programs(1) - 1)
    def _():
        H, nb, _, page = gate_ref.shape
        part_sum = jnp.sum(gate_ref[...], axis=2).reshape(H * nb, page)
        col = jnp.sum(part_sum, axis=-1, keepdims=True) * (1.0 / MOBA_BLOCK)
        eye = _eye(nb)
        head_row = lax.broadcasted_iota(jnp.int32, (H, 1), 0)
        gate = jnp.zeros((H, nb), F32)
        for h in range(H):
            gate = jnp.where(head_row == h, _col_to_row(col[h * nb:(h + 1) * nb, :], eye), gate)
        blk_id = lax.broadcasted_iota(jnp.int32, (1, nb), 1)
        rank = jnp.zeros(gate.shape, jnp.int32)
        for m in range(nb):
            gm = gate[:, m:m + 1]
            beats = jnp.logical_or(gm > gate, jnp.logical_and(gm == gate, m < blk_id))
            rank = rank + jnp.where(beats, 1, 0)
        out_col = lax.broadcasted_iota(jnp.int32, sel_ref.shape, 1)
        out = jnp.zeros(sel_ref.shape, jnp.int32)
        for r in range(MOBA_TOPK):
            idx = jnp.sum(jnp.where(rank == r, blk_id, 0), axis=1, keepdims=True)
            out = jnp.where(out_col == r, idx, out)
        sel_ref[...] = out


def _moba_select(cache_kt, page_table, q_col):
    Bs, n_pages = page_table.shape
    _, H, dh, page = cache_kt.shape
    ppb = MOBA_BLOCK // page
    nb = n_pages // ppb
    assert nb >= MOBA_TOPK and n_pages % SELECT_PAGES == 0 and SELECT_PAGES % ppb == 0

    def page_spec(i):
        return pl.BlockSpec((None, H, dh, page), lambda b, s, pt: (pt[b, s * SELECT_PAGES + i], 0, 0, 0))

    grid_spec = pltpu.PrefetchScalarGridSpec(
        num_scalar_prefetch=1,
        grid=(Bs, n_pages // SELECT_PAGES),
        in_specs=[page_spec(i) for i in range(SELECT_PAGES)]
        + [pl.BlockSpec((None, H, dh, 1), lambda b, s, pt: (b, 0, 0, 0))],
        out_specs=pl.BlockSpec((None, H, 8), lambda b, s, pt: (b, 0, 0)),
        scratch_shapes=[pltpu.VMEM((H, nb, 8, page), F32)],
    )
    return pl.pallas_call(
        functools.partial(_moba_select_kernel, pages_per_block=ppb),
        grid_spec=grid_spec,
        out_shape=jax.ShapeDtypeStruct((Bs, H, 8), jnp.int32),
        compiler_params=_cparams(("parallel", "arbitrary")),
        name="moba_select",
    )(page_table, *([cache_kt] * SELECT_PAGES), q_col)


def _moba_decode_kernel(pt_ref, sel_ref, q_ref, kn_ref, vn_ref, ck_ref, cv_ref, o_ref, kbuf, vbuf, sem,
                        *, pages_per_block, page):
    b = pl.program_id(0)
    nb_ = pl.num_programs(0)
    H = q_ref.shape[0]
    ppb = pages_per_block

    def copies(bb, slot):
        out = []
        for h in range(H):
            for r in range(MOBA_TOPK):
                blk = sel_ref[bb, h, r]
                for g in range(ppb):
                    pg = pt_ref[bb, blk * ppb + g]
                    dst = pl.ds((r * ppb + g) * page, page)
                    out.append(pltpu.make_async_copy(ck_ref.at[pg, h], kbuf.at[slot, h, :, dst], sem.at[slot, 0]))
                    out.append(pltpu.make_async_copy(cv_ref.at[pg, h], vbuf.at[slot, h, :, dst], sem.at[slot, 1]))
        return out

    slot = b % 2

    @pl.when(b == 0)
    def _():
        for c in copies(b, slot):
            c.start()

    @pl.when(b + 1 < nb_)
    def _():
        for c in copies(b + 1, 1 - slot):
            c.start()

    for c in copies(b, slot):
        c.wait()

    row = lax.broadcasted_iota(jnp.int32, (H, 1), 0)
    out = jnp.zeros(o_ref.shape, F32)
    q = q_ref[...]
    kn = kn_ref[...]
    vn = vn_ref[...]
    for h in range(H):
        qh = q[h:h + 1, :] * (DH_A ** -0.5)
        q8 = jnp.broadcast_to(qh, (8, qh.shape[1])).astype(BF16)
        kh = kbuf[slot, h].astype(BF16)
        vh = vbuf[slot, h].astype(BF16)
        s = jnp.dot(q8, kh, preferred_element_type=F32)[0:1, :]
        s_self = jnp.sum(qh * kn[h:h + 1, :], axis=-1, keepdims=True)
        m = jnp.maximum(jnp.max(s, axis=-1, keepdims=True), s_self)
        p = jnp.exp(s - m)
        p_self = jnp.exp(s_self - m)
        l = jnp.sum(p, axis=-1, keepdims=True) + p_self
        p8 = jnp.broadcast_to(p, (8, p.shape[1])).astype(BF16)
        pv = _nt(p8, vh)[0:1, :]
        oh = (pv + p_self * vn[h:h + 1, :]) / l
        out = jnp.where(row == h, oh, out)
    o_ref[...] = out


def _moba_decode(cache_kt, cache_vt, page_table, sel, q, k_new, v_new):
    Bs, n_pages = page_table.shape
    _, H, dh, page = cache_kt.shape
    ppb = MOBA_BLOCK // page
    rows = MOBA_TOPK * MOBA_BLOCK
    vec = pl.BlockSpec((None, H, dh), lambda b, pt, sl: (b, 0, 0))
    grid_spec = pltpu.PrefetchScalarGridSpec(
        num_scalar_prefetch=2,
        grid=(Bs,),
        in_specs=[vec, vec, vec, pl.BlockSpec(memory_space=pl.ANY), pl.BlockSpec(memory_space=pl.ANY)],
        out_specs=vec,
        scratch_shapes=[pltpu.VMEM((2, H, dh, rows), F32), pltpu.VMEM((2, H, dh, rows), F32),
                        pltpu.SemaphoreType.DMA((2, 2))],
    )
    return pl.pallas_call(
        functools.partial(_moba_decode_kernel, pages_per_block=ppb, page=page),
        grid_spec=grid_spec,
        out_shape=jax.ShapeDtypeStruct((Bs, H, dh), F32),
        compiler_params=_cparams(("arbitrary",)),
        name="moba_decode",
    )(page_table, sel, q, k_new, v_new, cache_kt, cache_vt)


def _ret_decode_kernel(q_ref, k_ref, v_ref, gb_ref, rg_ref, g_ref, s_ref, o_ref, so_ref):
    H, dk = q_ref.shape
    eye = _eye(dk)
    row = lax.broadcasted_iota(jnp.int32, (H, 1), 0)
    q = q_ref[...]
    k = k_ref[...]
    v = v_ref[...]
    g_all = g_ref[...]
    out = jnp.zeros(o_ref.shape, F32)
    for h in range(H):
        qh, kh, vh = q[h:h + 1, :], k[h:h + 1, :], v[h:h + 1, :]
        g = g_all[h:h + 1, 0:1]
        S = s_ref[h]
        att = jnp.sum(qh * kh, axis=-1, keepdims=True)
        cross = jnp.sum(_row_to_col(qh * g, eye) * S, axis=0, keepdims=True)
        out = jnp.where(row == h, att * vh + cross, out)
        so_ref[h] = S * g + _row_to_col(kh, eye) * vh
    gb = gb_ref[...]
    y = out * lax.rsqrt(jnp.mean(out * out, axis=-1, keepdims=True) + EPS) * rg_ref[...]
    o_ref[...] = y * (gb * _sigmoid(gb))


def _ret_decode(q, k, v, gb, ret_g, state):
    Bs, H, dk = q.shape
    log_g = jnp.log1p(-jnp.exp2(-5.0 - jnp.arange(H_B, dtype=F32)))
    g = jnp.broadcast_to(jnp.exp(1.0 * log_g)[:, None], (H, LANES))
    vec = pl.BlockSpec((None, H, dk), lambda b: (b, 0, 0))
    st = pl.BlockSpec((None, H, dk, DV_B), lambda b: (b, 0, 0, 0))
    return pl.pallas_call(
        _ret_decode_kernel,
        grid=(Bs,),
        in_specs=[vec, vec, vec, vec, pl.BlockSpec((H, dk), lambda b: (0, 0)),
                  pl.BlockSpec((H, LANES), lambda b: (0, 0)), st],
        out_specs=[vec, st],
        out_shape=[jax.ShapeDtypeStruct((Bs, H, DV_B), F32), jax.ShapeDtypeStruct(state.shape, F32)],
        compiler_params=_cparams(("parallel",)),
        name="retention_decode",
    )(q, k, v, gb, ret_g.reshape(H, DV_B), g, state)


def _mlstm_decode_kernel(q_ref, k_ref, v_ref, og_ref, ig_ref, fg_ref, bi_ref, bf_ref, ng_ref, c_ref, n_ref,
                         m_ref, h_ref, co_ref, no_ref, mo_ref):
    H, d = q_ref.shape
    eye = _eye(d)
    row = lax.broadcasted_iota(jnp.int32, (H, 1), 0)
    q = q_ref[...]
    k = k_ref[...] * (DH_C ** -0.5)
    v = v_ref[...]
    n0 = n_ref[...]
    ig = ig_ref[...] + bi_ref[...]
    b = _log_sigmoid(fg_ref[...] + bf_ref[...])
    m0 = m_ref[...]
    log_inter = b + m0
    m_t = jnp.maximum(log_inter, ig)
    w_intra = jnp.exp(ig - m_t)
    w_inter = jnp.exp(log_inter - m_t)
    s = jnp.sum(q * k, axis=-1, keepdims=True) * w_intra
    den = s + w_inter * jnp.sum(q * n0, axis=-1, keepdims=True)
    scale = 1.0 / jnp.maximum(jnp.abs(den), jnp.exp(-m_t))
    w_k = w_intra
    decay = w_inter
    hs = jnp.zeros(h_ref.shape, F32)
    for h in range(H):
        qh, kh, vh = q[h:h + 1, :], k[h:h + 1, :], v[h:h + 1, :]
        Cm = c_ref[h]
        cq = _col_to_row(jnp.sum(Cm * qh, axis=1, keepdims=True), eye)
        num = s[h:h + 1, :] * vh + cq * w_inter[h:h + 1, :]
        hs = jnp.where(row == h, num * scale[h:h + 1, :], hs)
        co_ref[h] = decay[h:h + 1, :] * Cm + _row_to_col(vh * w_k[h:h + 1, :], eye) * kh
    no_ref[...] = decay * n0 + k * w_k
    mo_ref[...] = m_t
    hn = hs * lax.rsqrt(jnp.mean(hs * hs, axis=-1, keepdims=True) + EPS) * ng_ref[...]
    h_ref[...] = hn * _sigmoid(og_ref[...])


def _mlstm_decode(q, k, v, og, ig, fg, gate_bias, out_g, C0, n0, m0):
    Bs, H, d = q.shape
    vec = pl.BlockSpec((None, H, d), lambda b: (b, 0, 0))
    sc = pl.BlockSpec((None, H, 1), lambda b: (b, 0, 0))
    st = pl.BlockSpec((None, H, d, d), lambda b: (b, 0, 0, 0))
    bias = pl.BlockSpec((H, 1), lambda b: (0, 0))
    return pl.pallas_call(
        _mlstm_decode_kernel,
        grid=(Bs,),
        in_specs=[vec, vec, vec, vec, sc, sc, bias, bias, pl.BlockSpec((H, d), lambda b: (0, 0)), st, vec, sc],
        out_specs=[vec, st, vec, sc],
        out_shape=[jax.ShapeDtypeStruct((Bs, H, d), F32), jax.ShapeDtypeStruct(C0.shape, F32),
                   jax.ShapeDtypeStruct((Bs, H, d), F32), jax.ShapeDtypeStruct((Bs, H, 1), F32)],
        compiler_params=_cparams(("parallel",)),
        name="mlstm_decode",
    )(q, k, v, og, ig, fg, gate_bias[0].reshape(H, 1), gate_bias[1].reshape(H, 1), out_g.reshape(H, d), C0, n0, m0)


def _rope_tables(pos):
    half = DH_A // 2
    inv = ROPE_THETA ** (-jnp.arange(half, dtype=F32) / half)
    ang = pos[:, None] * inv[None, :]
    cos = jnp.cos(ang)
    sin = jnp.sin(ang)
    reps = LANES // DH_A
    cos_t = jnp.tile(jnp.concatenate([cos, cos], axis=-1), (1, reps))
    sin_t = jnp.tile(jnp.concatenate([-sin, sin], axis=-1), (1, reps))
    return cos_t, sin_t


def _row_tile(m, pref):
    return pref if m % pref == 0 else m


def kernel(x_prompt, x_sample, cache_k, cache_v, page_table, state_ret, state_mlstm_C, state_mlstm_n,
           state_mlstm_m, state_ffn_conv, ab_norm_g, ab_w_in, ab_ret_norm_g, ab_w_out, c_norm_g, c_w_in,
           c_gate_bias, c_out_norm_g, c_w_out, ffn_norm_g, ffn_w1, ffn_w3, ffn_conv_w, ffn_conv_b, ffn_w2,
           final_norm_g):
    Bp, Tp, D = x_prompt.shape
    Bs, Ts, _ = x_sample.shape
    assert Ts == 1
    past_len = page_table.shape[1] * cache_k.shape[2]
    Mp = Bp * Tp
    xp = x_prompt.reshape(Mp, D)
    xs = x_sample.reshape(Bs, D)
    tm_p = _row_tile(Tp, 1024)
    tm_ffn = _row_tile(Tp, 1024)
    tm_s = Bs
    tf = 256
    rope_p = _rope_tables(jnp.arange(Tp, dtype=jnp.int32).astype(F32))
    rope_s = _rope_tables(jnp.full((Bs,), past_len, jnp.int32).astype(F32))
    gw = 2 * H_C

    outs = {}

    w_in = ab_w_in[0].astype(BF16)
    w_out = ab_w_out[0].astype(BF16)
    w_out_a, w_out_b = w_out[:W_A], w_out[W_A:]

    z = _norm_matmul(xp, ab_norm_g[0], w_in, tm_p, AB_TILE, rope=rope_p)
    oa, kt, vt = _moba_prefill(z, Bp, Tp)
    ob, s_pair = _ret_prefill(z, ab_ret_norm_g[0], Bp, Tp)
    xp = _proj_residual([oa, ob], [w_out_a, w_out_b], xp, tm_p)
    outs["k_prompt"] = kt.reshape(Bp, H_A, DH_A, Tp).transpose(0, 3, 1, 2)[None]
    outs["v_prompt"] = vt.reshape(Bp, H_A, DH_A, Tp).transpose(0, 3, 1, 2)[None]
    hb = LANES // DK_B
    s_heads = jnp.stack([s_pair[:, :, i * DK_B:(i + 1) * DK_B, i * DV_B:(i + 1) * DV_B] for i in range(hb)], axis=2)
    outs["ret_prompt"] = s_heads.reshape(1, Bp, H_B, DK_B, DV_B)

    zs = _norm_matmul(xs, ab_norm_g[0], w_in, tm_s, AB_TILE, rope=rope_s)
    seg = lambda t: zs[:, t * AB_TILE:(t + 1) * AB_TILE].reshape(Bs, H_A, DH_A)
    qa_s, ka_s, va_s, qb_s, kb_s, vb_s, gb_s = (seg(t) for t in range(7))
    cache_kt = cache_k[0].transpose(0, 2, 3, 1)
    cache_vt = cache_v[0].transpose(0, 2, 3, 1)
    sel = _moba_select(cache_kt, page_table, qa_s[..., None])
    oa_s = _moba_decode(cache_kt, cache_vt, page_table, sel, qa_s, ka_s, va_s)
    ob_s, s_new = _ret_decode(qb_s, kb_s, vb_s, gb_s, ab_ret_norm_g[0], state_ret[0])
    xs = _proj_residual([oa_s.reshape(Bs, W_A).astype(BF16), ob_s.reshape(Bs, W_B).astype(BF16)],
                        [w_out_a, w_out_b], xs, tm_s)
    outs["k_sample"] = ka_s.reshape(1, Bs, 1, H_A, DH_A)
    outs["v_sample"] = va_s.reshape(1, Bs, 1, H_A, DH_A)
    outs["ret_sample"] = s_new[None]

    conv_p, conv_s = [], []

    def ffn_both(l, xp, xs, final_g):
        w1 = ffn_w1[l].astype(BF16)
        w3 = ffn_w3[l].astype(BF16)
        w2 = ffn_w2[l].astype(BF16)
        tail = xp.reshape(Bp, Tp, D)[:, Tp - 8:, :].reshape(Bp * 8, D)
        a_tail = _norm_matmul(tail, ffn_norm_g[l], w1, Bp * 8, tf)
        conv_p.append(a_tail.reshape(Bp, 8, D_FF)[:, 8 - (CONV_W - 1):, :])
        a_s = _norm_matmul(xs, ffn_norm_g[l], w1, tm_s, tf)
        st = state_ffn_conv[l]
        conv_s.append(jnp.stack([st[:, 1, :], a_s], axis=1))
        xp = _ffn(xp, ffn_norm_g[l], w1, w3, ffn_conv_w[l], ffn_conv_b[l], w2, final_g, tm_ffn, seq_len=Tp)
        xs = _ffn(xs, ffn_norm_g[l], w1, w3, ffn_conv_w[l], ffn_conv_b[l], w2, final_g, tm_s,
                  state=(st[:, 0, :], st[:, 1, :]))
        return xp, xs

    xp, xs = ffn_both(0, xp, xs, None)

    w_in = c_w_in[0]
    w_main = w_in[:, :4 * W_C].astype(BF16)
    w_gate = jnp.pad(w_in[:, 4 * W_C:], ((0, 0), (0, LANES - gw))).astype(BF16)
    w_out = c_w_out[0].astype(BF16)
    L = MLSTM_CHUNK

    z = _norm_matmul(xp, c_norm_g[0], w_main, tm_p, 1024)
    zg = _norm_matmul(xp, c_norm_g[0], w_gate, tm_p, LANES)
    gates = zg[:, :gw].reshape(Bp, Tp // L, L, 2, H_C).transpose(3, 0, 4, 1, 2)
    h, C_p, n_p, m_p = _mlstm_prefill(z, gates[0], gates[1], c_gate_bias[0], c_out_norm_g[0], Bp, Tp)
    xp = _proj_residual([h], [w_out], xp, tm_p)
    outs["C_prompt"] = C_p[None]
    outs["n_prompt"] = n_p.reshape(1, Bp, H_C, DH_C)
    outs["m_prompt"] = m_p[:, :, 0, 0][None]

    zs = _norm_matmul(xs, c_norm_g[0], w_main, tm_s, 1024)
    zgs = _norm_matmul(xs, c_norm_g[0], w_gate, tm_s, LANES)
    segc = lambda t: zs[:, t * W_C:(t + 1) * W_C].reshape(Bs, H_C, DH_C)
    ig_s = zgs[:, :H_C][:, :, None]
    fg_s = zgs[:, H_C:gw][:, :, None]
    h_s, C_s, n_s, m_s = _mlstm_decode(segc(0), segc(1), segc(2), segc(3), ig_s, fg_s, c_gate_bias[0], c_out_norm_g[0],
                                       state_mlstm_C[0], state_mlstm_n[0], state_mlstm_m[0][:, :, None])
    xs = _proj_residual([h_s.reshape(Bs, W_C).astype(BF16)], [w_out], xs, tm_s)
    outs["C_sample"] = C_s[None]
    outs["n_sample"] = n_s[None]
    outs["m_sample"] = m_s[:, :, 0][None]

    xp, xs = ffn_both(1, xp, xs, final_norm_g)

    return (xp.reshape(Bp, Tp, D), xs.reshape(Bs, 1, D),
            outs["k_prompt"], outs["v_prompt"], outs["k_sample"], outs["v_sample"],
            outs["ret_prompt"], outs["ret_sample"], outs["C_prompt"], outs["C_sample"],
            outs["n_prompt"], outs["n_sample"], outs["m_prompt"], outs["m_sample"],
            jnp.stack(conv_p), jnp.stack(conv_s))
```

```python
import functools

import jax
import jax.numpy as jnp
from jax import lax
from jax.experimental import pallas as pl
from jax.experimental.pallas import tpu as pltpu

F32 = jnp.float32
BF16 = jnp.bfloat16
HIGHEST = lax.Precision.HIGHEST

LANES = 128
D_MODEL = 1024
H_A = 8
DH_A = 64
MOBA_BLOCK = 256
MOBA_TOPK = 3
H_B = 8
DK_B = 64
DV_B = 64
RET_CHUNK = 128
H_C = 8
DH_C = D_MODEL // H_C
MLSTM_CHUNK = 128
D_FF = 11 * D_MODEL // 4
CONV_W = 3
ROPE_THETA = 10000.0
EPS = 1e-6
W_A = H_A * DH_A
W_B = H_B * DV_B
W_C = H_C * DH_C
IN_AB = 3 * W_A + 2 * H_B * DK_B + 2 * W_B
AB_TILE = 512
AB_ROPE_TILES = (0, 1, 3, 4)
AB_KB_TILE = 4
VMEM_LIMIT = 56 * 1024 * 1024

NEG_INF = float("-inf")
LOG2_E = 1.4426950408889634


def _cparams(sem):
    return pltpu.CompilerParams(dimension_semantics=sem, vmem_limit_bytes=VMEM_LIMIT)


def _nt(a, b, **kw):
    return lax.dot_general(a, b, (((1,), (1,)), ((), ())), preferred_element_type=F32, **kw)


def _tn(a, b, **kw):
    return lax.dot_general(a, b, (((0,), (0,)), ((), ())), preferred_element_type=F32, **kw)


def _rms_rows(x, g):
    ms = jnp.mean(x * x, axis=-1, keepdims=True)
    return x * lax.rsqrt(ms + EPS) * g


def _eye(n):
    return lax.broadcasted_iota(jnp.int32, (n, n), 0) == lax.broadcasted_iota(jnp.int32, (n, n), 1)


def _row_to_col(row, eye):
    return jnp.sum(jnp.where(eye, row, 0.0), axis=1, keepdims=True)


def _col_to_row(col, eye):
    return jnp.sum(jnp.where(eye, col, 0.0), axis=0, keepdims=True)


def _rowsum_rep(x, ones=None):
    if ones is None:
        ones = jnp.ones((x.shape[1], LANES), BF16)
    hi = x.astype(BF16)
    lo = (x - hi.astype(F32)).astype(BF16)
    return (jnp.dot(hi, ones, preferred_element_type=F32) + jnp.dot(lo, ones, preferred_element_type=F32))


def _log_sigmoid(x):
    return jnp.minimum(x, 0.0) - jnp.log1p(jnp.exp(-jnp.abs(x)))


def _sigmoid(x):
    return 1.0 / (1.0 + jnp.exp(-x))


def _gelu_tanh(x):
    c = 0.7978845608028654
    return (0.5 * x) * (1.0 + jnp.tanh(x * (c + (0.044715 * c) * (x * x))))


def _norm_matmul_kernel(x_ref, g_ref, w_ref, cos_ref, sin_ref, o_ref, xn_ref, *, rope_tiles, scale_tile, scale):
    j = pl.program_id(1)

    @pl.when(j == 0)
    def _():
        xn_ref[...] = _rms_rows(x_ref[...], g_ref[...]).astype(BF16)

    z = jnp.dot(xn_ref[...], w_ref[...], preferred_element_type=F32)
    if not rope_tiles:
        o_ref[...] = z
        return

    is_rope = functools.reduce(jnp.logical_or, [j == t for t in rope_tiles])

    @pl.when(is_rope)
    def _():
        sc = jnp.where(j == scale_tile, scale, 1.0).astype(F32)
        cos = cos_ref[...]
        sin = sin_ref[...]
        for c in range(z.shape[1] // LANES):
            cols = slice(c * LANES, (c + 1) * LANES)
            o_ref[:, cols] = _rope_lanes(z[:, cols], cos, sin) * sc

    @pl.when(jnp.logical_not(is_rope))
    def _():
        o_ref[...] = z


def _norm_matmul(x, g, w, tm, tn, rope=None):
    M, D = x.shape
    N = w.shape[1]
    assert M % tm == 0 and N % tn == 0
    if rope is None:
        cos = sin = jnp.zeros((8, LANES), F32)
        tab_spec = pl.BlockSpec((8, LANES), lambda i, j: (0, 0))
        kern = functools.partial(_norm_matmul_kernel, rope_tiles=(), scale_tile=-1, scale=1.0)
    else:
        cos, sin = rope
        nt = cos.shape[0] // tm
        tab_spec = pl.BlockSpec((tm, LANES), lambda i, j: (i % nt, 0))
        kern = functools.partial(_norm_matmul_kernel, rope_tiles=AB_ROPE_TILES, scale_tile=AB_KB_TILE,
                                 scale=DK_B ** -0.5)
    return pl.pallas_call(
        kern,
        grid=(M // tm, N // tn),
        in_specs=[pl.BlockSpec((tm, D), lambda i, j: (i, 0)),
                  pl.BlockSpec((1, D), lambda i, j: (0, 0)),
                  pl.BlockSpec((D, tn), lambda i, j: (0, j)),
                  tab_spec, tab_spec],
        out_specs=pl.BlockSpec((tm, tn), lambda i, j: (i, j)),
        out_shape=jax.ShapeDtypeStruct((M, N), F32),
        scratch_shapes=[pltpu.VMEM((tm, D), BF16)],
        compiler_params=_cparams(("parallel", "arbitrary")),
        name="norm_matmul",
    )(x, g.reshape(1, D), w, cos, sin)


def _proj_residual_kernel(*refs, n_in):
    a_refs = refs[:n_in]
    w_refs = refs[n_in:2 * n_in]
    res_ref = refs[2 * n_in]
    o_ref = refs[2 * n_in + 1]
    y = res_ref[...]
    acc = None
    for a_ref, w_ref in zip(a_refs, w_refs):
        d = jnp.dot(a_ref[...], w_ref[...], preferred_element_type=F32)
        acc = d if acc is None else acc + d
    o_ref[...] = y + acc


def _proj_residual(acts, ws, res, tm):
    M, D = res.shape
    n_in = len(acts)
    in_specs = ([pl.BlockSpec((tm, a.shape[1]), lambda i: (i, 0)) for a in acts]
                + [pl.BlockSpec(w.shape, lambda i: (0, 0)) for w in ws]
                + [pl.BlockSpec((tm, D), lambda i: (i, 0))])
    return pl.pallas_call(
        functools.partial(_proj_residual_kernel, n_in=n_in),
        grid=(M // tm,),
        in_specs=in_specs,
        out_specs=pl.BlockSpec((tm, D), lambda i: (i, 0)),
        out_shape=jax.ShapeDtypeStruct((M, D), F32),
        compiler_params=_cparams(("parallel",)),
        name="proj_residual",
    )(*acts, *ws, res)


PREV_ROWS = 16
FFN_CHUNK = 256


def _ffn_kernel(*refs, seq_mode, tiles_per_seq, final_norm):
    if seq_mode:
        (x_ref, xp_ref, g_ref, w1_ref, w3_ref, cw_ref, cb_ref, w2_ref, fg_ref, o_ref, y_ref) = refs
    else:
        (x_ref, s0_ref, s1_ref, g_ref, w1_ref, w3_ref, cw_ref, cb_ref, w2_ref, fg_ref, o_ref, y_ref) = refs
    i = pl.program_id(0)
    x = x_ref[...]
    tm = x.shape[0]
    F = w1_ref.shape[1]
    xn = _rms_rows(x, g_ref[...]).astype(BF16)
    if seq_mode:
        xpn = _rms_rows(xp_ref[...], g_ref[...]).astype(BF16)
        has_prev = ((i % tiles_per_seq) != 0).astype(F32)
        row8 = lax.broadcasted_iota(jnp.int32, (8, 1), 0)
    for c0 in range(0, F, FFN_CHUNK):
        cols = slice(c0, min(c0 + FFN_CHUNK, F))
        w1c = w1_ref[:, cols]
        a = jnp.dot(xn, w1c, preferred_element_type=F32)
        gate = jnp.dot(xn, w3_ref[:, cols], preferred_element_type=F32)
        if seq_mode:
            ap = jnp.dot(xpn, w1c, preferred_element_type=F32)
            p1 = ap[PREV_ROWS - 1:PREV_ROWS, :] * has_prev
            p2 = ap[PREV_ROWS - 2:PREV_ROWS - 1, :] * has_prev
            a1 = pltpu.roll(a, 1, 0)
            a2 = pltpu.roll(a, 2, 0)
            top1 = jnp.where(row8 == 0, p1, a1[0:8, :])
            top2 = jnp.where(row8 == 0, p2, jnp.where(row8 == 1, p1, a2[0:8, :]))
            a1 = jnp.concatenate([top1, a1[8:, :]], axis=0)
            a2 = jnp.concatenate([top2, a2[8:, :]], axis=0)
        else:
            a1 = s1_ref[:, cols]
            a2 = s0_ref[:, cols]
        ac = cb_ref[:, cols] + a2 * cw_ref[0:1, cols]
        ac = ac + a1 * cw_ref[1:2, cols]
        ac = ac + a * cw_ref[2:3, cols]
        y_ref[:, cols] = (_gelu_tanh(ac) * gate).astype(BF16)
    acc = x + jnp.dot(y_ref[...], w2_ref[...], preferred_element_type=F32)
    if final_norm:
        acc = _rms_rows(acc, fg_ref[...])
    o_ref[...] = acc


def _ffn(x, norm_g, w1, w3, conv_w, conv_b, w2, final_g, tm, seq_len=None, state=None):
    M, D = x.shape
    F = w1.shape[1]
    seq_mode = state is None
    final_norm = final_g is not None
    fg = (final_g if final_norm else jnp.ones((D,), F32)).reshape(1, D)
    whole = lambda shape: pl.BlockSpec(shape, lambda i: (0, 0), pipeline_mode=pl.Buffered(1))
    common = [whole((1, D)), whole((D, F)), whole((D, F)), whole((CONV_W, F)), whole((1, F)), whole((F, D)),
              whole((1, D))]
    common_args = (norm_g.reshape(1, D), w1, w3, conv_w, conv_b.reshape(1, F), w2, fg)
    x_spec = pl.BlockSpec((tm, D), lambda i: (i, 0))
    if seq_mode:
        assert seq_len % tm == 0 and tm % PREV_ROWS == 0
        r = tm // PREV_ROWS
        in_specs = [x_spec, pl.BlockSpec((PREV_ROWS, D), lambda i: (jnp.maximum(i * r - 1, 0), 0))] + common
        args = (x, x) + common_args
        tiles_per_seq = seq_len // tm
    else:
        s_spec = pl.BlockSpec((tm, F), lambda i: (i, 0))
        in_specs = [x_spec, s_spec, s_spec] + common
        args = (x, state[0], state[1]) + common_args
        tiles_per_seq = 1
    return pl.pallas_call(
        functools.partial(_ffn_kernel, seq_mode=seq_mode, tiles_per_seq=tiles_per_seq, final_norm=final_norm),
        grid=(M // tm,),
        in_specs=in_specs,
        out_specs=pl.BlockSpec((tm, D), lambda i: (i, 0)),
        out_shape=jax.ShapeDtypeStruct((M, D), F32),
        scratch_shapes=[pltpu.VMEM((tm, F), BF16)],
        compiler_params=_cparams(("parallel",)),
        name="conv_ffn",
    )(*args)


def _rope_lanes(x, cos, sin):
    lane = lax.broadcasted_iota(jnp.int32, (1, LANES), 1)
    first_half = (lane % DH_A) < (DH_A // 2)
    partner = jnp.where(first_half, pltpu.roll(x, LANES - DH_A // 2, 1), pltpu.roll(x, DH_A // 2, 1))
    return x * cos + partner * sin


def _moba_prefill_kernel(q_ref, k_ref, v_ref, cos_ref, sin_ref, o_ref, kt_ref, vt_ref, kb_ref, vb_ref, *, nb):
    blk = MOBA_BLOCK
    heads = LANES // DH_A
    T = q_ref.shape[0]
    cos = cos_ref[...]
    sin = sin_ref[...]
    k = _rope_lanes(k_ref[...], cos, sin)
    kt_ref[...] = k.T
    vt_ref[...] = v_ref[...].T
    kb_ref[...] = k.astype(BF16)
    vb_ref[...] = v_ref[...].astype(BF16)
    kmean = jnp.concatenate(
        [jnp.sum(k[n * blk:(n + 1) * blk, :], axis=0, keepdims=True) * (1.0 / blk) for n in range(nb)]
        + [jnp.zeros((8 - nb, LANES), F32)] * (nb < 8), axis=0)

    q2 = _rope_lanes(q_ref[...], cos, sin)
    lane = lax.broadcasted_iota(jnp.int32, (1, LANES), 1)
    blk_id = lax.broadcasted_iota(jnp.int32, (8, 1), 0)
    q_blk = lax.broadcasted_iota(jnp.int32, (1, T), 1) // blk
    past = blk_id < q_blk
    causal = (lax.broadcasted_iota(jnp.int32, (blk, blk), 1) <= lax.broadcasted_iota(jnp.int32, (blk, blk), 0))
    hms, qss, sels = [], [], []
    for h in range(heads):
        hm = (lane // DH_A) == h
        qh = jnp.where(hm, q2, 0.0)
        gt = _nt(kmean, qh, precision=HIGHEST)
        sel_t = jnp.zeros_like(gt)
        for n in range(nb - 1):
            gn = gt[n:n + 1, :]
            beats = jnp.logical_and(past, jnp.logical_or(gt > gn, jnp.logical_and(gt == gn, blk_id < n)))
            rank = jnp.sum(beats.astype(F32), axis=0, keepdims=True)
            sel_n = jnp.where(jnp.logical_and(rank < MOBA_TOPK, n < q_blk), 1.0, 0.0)
            sel_t = jnp.where(blk_id == n, sel_n, sel_t)
        hms.append(hm)
        qss.append((qh * (DH_A ** -0.5 * LOG2_E)).astype(BF16))
        sels.append(sel_t.T)

    for qi in range(nb):
        rows = slice(qi * blk, (qi + 1) * blk)
        outs = []
        for h in range(heads):
            qs = qss[h][rows, :]
            pieces = []
            for n in range(qi + 1):
                s = _nt(qs, kb_ref[n * blk:(n + 1) * blk, :])
                keep = causal if n == qi else (sels[h][rows, n:n + 1] > 0.5)
                pieces.append(jnp.where(keep, s, NEG_INF))
            m = functools.reduce(jnp.maximum, pieces)
            m = jnp.max(m, axis=-1, keepdims=True)
            lsum = None
            acc = None
            for n in range(qi + 1):
                p = jnp.exp2(pieces[n] - m)
                lsum = p if lsum is None else lsum + p
                d = jnp.dot(p.astype(BF16), vb_ref[n * blk:(n + 1) * blk, :], preferred_element_type=F32)
                acc = d if acc is None else acc + d
            outs.append(acc / jnp.sum(lsum, axis=-1, keepdims=True))
        out = outs[0]
        for h in range(1, heads):
            out = jnp.where(hms[h], outs[h], out)
        o_ref[rows, :] = out.astype(o_ref.dtype)


def _moba_prefill(z, rope, B, T):
    blk = MOBA_BLOCK
    assert T % blk == 0
    nb = T // blk
    assert nb <= 8
    cpt = AB_TILE // LANES
    return pl.pallas_call(
        functools.partial(_moba_prefill_kernel, nb=nb),
        grid=(B, W_A // LANES),
        in_specs=[pl.BlockSpec((T, LANES), lambda b, p: (b, p)),
                  pl.BlockSpec((T, LANES), lambda b, p: (b, cpt + p)),
                  pl.BlockSpec((T, LANES), lambda b, p: (b, 2 * cpt + p)),
                  pl.BlockSpec((T, LANES), lambda b, p: (0, 0)),
                  pl.BlockSpec((T, LANES), lambda b, p: (0, 0))],
        out_specs=[pl.BlockSpec((T, LANES), lambda b, p: (b, p)),
                   pl.BlockSpec((None, LANES, T), lambda b, p: (b, p, 0)),
                   pl.BlockSpec((None, LANES, T), lambda b, p: (b, p, 0))],
        out_shape=[jax.ShapeDtypeStruct((B * T, W_A), BF16),
                   jax.ShapeDtypeStruct((B, W_A, T), F32),
                   jax.ShapeDtypeStruct((B, W_A, T), F32)],
        scratch_shapes=[pltpu.VMEM((T, LANES), BF16), pltpu.VMEM((T, LANES), BF16)],
        compiler_params=_cparams(("parallel", "parallel")),
        name="moba_prefill",
    )(z, z, z, rope[0], rope[1])


def _ret_prefill_kernel(q_ref, k_ref, v_ref, gb_ref, cos_ref, sin_ref, rg_ref, dmask_ref, din_ref, dout_ref,
                        gch_ref, o_ref, s_ref, *, n_chunks):
    C = RET_CHUNK
    heads = LANES // DK_B
    lane = lax.broadcasted_iota(jnp.int32, (1, LANES), 1)
    hms = [(lane // DK_B) == h for h in range(heads)]
    row_h = lax.broadcasted_iota(jnp.int32, (LANES, LANES), 0) // DK_B
    col_h = lax.broadcasted_iota(jnp.int32, (LANES, LANES), 1) // DV_B
    same_head = row_h == col_h
    seg_ones = jnp.where(same_head, 1.0, 0.0).astype(BF16)
    din = din_ref[...]
    dout = dout_ref[...]
    gch = gch_ref[...]
    rg = rg_ref[...]
    dmask = dmask_ref[...]

    S = jnp.zeros((LANES, LANES), F32)
    for j in range(n_chunks):
        rows = slice(j * C, (j + 1) * C)
        cos = cos_ref[rows, :]
        sin = sin_ref[rows, :]
        q = _rope_lanes(q_ref[rows, :], cos, sin)
        k = _rope_lanes(k_ref[rows, :], cos, sin) * (DK_B ** -0.5)
        kb = k.astype(BF16)
        vb = v_ref[rows, :].astype(BF16)
        o = jnp.dot((q * din).astype(BF16), S.astype(BF16), preferred_element_type=F32)
        q_st = jnp.concatenate([jnp.where(hm, q, 0.0) for hm in hms], axis=0).astype(BF16)
        att = _nt(q_st, kb) * dmask
        res = jnp.dot(att.astype(BF16), vb, preferred_element_type=F32)
        intra = res[0:C, :]
        for h in range(1, heads):
            intra = jnp.where(hms[h], res[h * C:(h + 1) * C, :], intra)
        o = o + intra
        S = S * gch + jnp.where(same_head, _tn((k * dout).astype(BF16), vb), 0.0)
        ms = _rowsum_rep(o * o, seg_ones) * (1.0 / DV_B)
        g = gb_ref[rows, :]
        y = o * lax.rsqrt(ms + EPS) * rg * (g * _sigmoid(g))
        o_ref[rows, :] = y.astype(o_ref.dtype)
    s_ref[...] = S


def _ret_tables(chunk):
    log_g = jnp.log1p(-jnp.exp2(-5.0 - jnp.arange(H_B, dtype=F32)))
    i = jnp.arange(chunk, dtype=F32)
    d_in = jnp.exp((i[:, None] + 1.0) * log_g)
    d_out = jnp.exp((chunk - 1.0 - i)[:, None] * log_g)
    diff = i[:, None] - i[None, :]
    d_mask = jnp.where(diff >= 0, jnp.exp(jnp.maximum(diff, 0.0)[None] * log_g[:, None, None]), 0.0)
    g_chunk = jnp.exp(chunk * log_g)
    return d_in, d_out, d_mask, g_chunk


def _ret_prefill(z, rope, ret_g, B, T):
    C = RET_CHUNK
    assert T % C == 0
    d_in, d_out, d_mask, g_chunk = _ret_tables(C)
    npair = W_B // LANES
    lanes = lambda t: jnp.repeat(t, DK_B, axis=-1)
    din_l = lanes(d_in).reshape(C, npair, LANES).transpose(1, 0, 2)
    dout_l = lanes(d_out).reshape(C, npair, LANES).transpose(1, 0, 2)
    gch_l = lanes(g_chunk).reshape(npair, 1, LANES)
    cpt = AB_TILE // LANES
    col = lambda t: (lambda b, p: (b, t * cpt + p))
    return pl.pallas_call(
        functools.partial(_ret_prefill_kernel, n_chunks=T // C),
        grid=(B, npair),
        in_specs=[pl.BlockSpec((T, LANES), col(3)), pl.BlockSpec((T, LANES), col(4)),
                  pl.BlockSpec((T, LANES), col(5)), pl.BlockSpec((T, LANES), col(6)),
                  pl.BlockSpec((T, LANES), lambda b, p: (0, 0)),
                  pl.BlockSpec((T, LANES), lambda b, p: (0, 0)),
                  pl.BlockSpec((1, LANES), lambda b, p: (0, p)),
                  pl.BlockSpec((None, (LANES // DK_B) * C, C), lambda b, p: (p, 0, 0)),
                  pl.BlockSpec((None, C, LANES), lambda b, p: (p, 0, 0)),
                  pl.BlockSpec((None, C, LANES), lambda b, p: (p, 0, 0)),
                  pl.BlockSpec((None, 1, LANES), lambda b, p: (p, 0, 0))],
        out_specs=[pl.BlockSpec((T, LANES), lambda b, p: (b, p)),
                   pl.BlockSpec((None, None, LANES, LANES), lambda b, p: (b, p, 0, 0))],
        out_shape=[jax.ShapeDtypeStruct((B * T, W_B), BF16),
                   jax.ShapeDtypeStruct((B, npair, LANES, LANES), F32)],
        compiler_params=_cparams(("parallel", "parallel")),
        name="retention_prefill",
    )(z, z, z, z, rope[0], rope[1], ret_g.reshape(1, W_B), d_mask.reshape(npair, -1, C), din_l, dout_l, gch_l)


def _mlstm_prefill_kernel(bias_ref, q_ref, k_ref, v_ref, og_ref, ig_ref, fg_ref, igc_ref, fgc_ref, ng_ref,
                          h_ref, c_ref, n_ref, m_ref, *, n_chunks):
    L = MLSTM_CHUNK
    hd = pl.program_id(1)
    r_id = lax.broadcasted_iota(jnp.int32, (L, L), 0)
    c_id = lax.broadcasted_iota(jnp.int32, (L, L), 1)
    causal = c_id <= r_id
    upper = jnp.where(r_id <= c_id, 1.0, 0.0).astype(F32)
    lower = jnp.where(c_id <= r_id, 1.0, 0.0).astype(F32)
    i_rows = ig_ref[...] + bias_ref[0, hd]
    b_rows = jnp.dot(_log_sigmoid(fg_ref[...] + bias_ref[1, hd]), upper, preferred_element_type=F32, precision=HIGHEST)
    i_cols = igc_ref[...] + bias_ref[0, hd]
    b_cols = jnp.dot(lower, _log_sigmoid(fgc_ref[...] + bias_ref[1, hd]), preferred_element_type=F32, precision=HIGHEST)
    ng = ng_ref[...]

    Cm = jnp.zeros((L, L), F32)
    n = jnp.zeros((1, L), F32)
    m = jnp.zeros((1, 1), F32)
    for j in range(n_chunks):
        rows = slice(j * L, (j + 1) * L)
        q = q_ref[rows, :]
        k = k_ref[rows, :] * (DH_C ** -0.5)
        v = v_ref[rows, :]
        b_row, i_row = b_rows[j:j + 1, :], i_rows[j:j + 1, :]
        b_col, i_col = b_cols[:, j:j + 1], i_cols[:, j:j + 1]
        b_last = b_row[:, L - 1:L]
        qb = q.astype(BF16)
        kb = k.astype(BF16)
        b_rep = jnp.broadcast_to(b_col, (L, L))
        log_d = jnp.where(causal, (b_rep - b_row) + i_row, NEG_INF)
        m_row = jnp.broadcast_to(jnp.max(log_d, axis=-1, keepdims=True), (L, L))
        s = _nt(qb, kb) * jnp.exp(log_d - m_row)
        sv = jnp.dot(s.astype(BF16), v.astype(BF16), preferred_element_type=F32)
        s_sum = _rowsum_rep(s)
        log_w = (b_last - b_col) + i_col
        m_loc = jnp.max(log_w, axis=0, keepdims=True)
        e_k = jnp.broadcast_to(jnp.exp(log_w - m_loc), (L, L))
        U = _tn((v * e_k).astype(BF16), kb)
        nk = jnp.sum(k * e_k, axis=0, keepdims=True)
        log_inter = b_rep + m
        m_t = jnp.maximum(log_inter, m_row)
        f_intra = jnp.exp(m_row - m_t)
        w_inter = jnp.exp(log_inter - m_t)
        num = sv * f_intra + _nt(qb, Cm.astype(BF16)) * w_inter
        den = s_sum * f_intra + w_inter * _rowsum_rep(q * n)
        h = num / jnp.maximum(jnp.abs(den), jnp.exp(-m_t))
        m_new = jnp.maximum(b_last + m, m_loc)
        decay = jnp.exp(b_last + m - m_new)
        f_k = jnp.exp(m_loc - m_new)
        Cm = decay * Cm + f_k * U
        n = decay * n + f_k * nk
        m = m_new
        hn = h * lax.rsqrt(_rowsum_rep(h * h) * (1.0 / L) + EPS) * ng
        h_ref[rows, :] = (hn * _sigmoid(og_ref[rows, :])).astype(h_ref.dtype)
    c_ref[...] = Cm
    n_ref[...] = n
    m_ref[...] = jnp.broadcast_to(m, m_ref.shape)


def _mlstm_prefill(z, ig, fg, gate_bias, out_g, B, T):
    L = MLSTM_CHUNK
    assert T % L == 0 and DH_C == LANES
    nc = T // L
    col = lambda t: (lambda b, h, bias: (b, t * H_C + h))
    gspec = pl.BlockSpec((None, None, nc, L), lambda b, h, bias: (b, h, 0, 0))
    gcspec = pl.BlockSpec((None, None, L, nc), lambda b, h, bias: (b, h, 0, 0))
    grid_spec = pltpu.PrefetchScalarGridSpec(
        num_scalar_prefetch=1,
        grid=(B, H_C),
        in_specs=[pl.BlockSpec((T, LANES), col(0)), pl.BlockSpec((T, LANES), col(1)),
                  pl.BlockSpec((T, LANES), col(2)), pl.BlockSpec((T, LANES), col(3)),
                  gspec, gspec, gcspec, gcspec,
                  pl.BlockSpec((1, LANES), lambda b, h, bias: (0, h))],
        out_specs=[pl.BlockSpec((T, LANES), lambda b, h, bias: (b, h)),
                   pl.BlockSpec((None, None, L, L), lambda b, h, bias: (b, h, 0, 0)),
                   pl.BlockSpec((None, None, 1, L), lambda b, h, bias: (b, h, 0, 0)),
                   pl.BlockSpec((None, None, 1, LANES), lambda b, h, bias: (b, h, 0, 0))],
    )
    return pl.pallas_call(
        functools.partial(_mlstm_prefill_kernel, n_chunks=nc),
        grid_spec=grid_spec,
        out_shape=[jax.ShapeDtypeStruct((B * T, W_C), BF16),
                   jax.ShapeDtypeStruct((B, H_C, L, L), F32),
                   jax.ShapeDtypeStruct((B, H_C, 1, L), F32),
                   jax.ShapeDtypeStruct((B, H_C, 1, LANES), F32)],
        compiler_params=_cparams(("parallel", "parallel")),
        name="mlstm_prefill",
    )(gate_bias, z, z, z, z, ig, fg, ig.swapaxes(2, 3), fg.swapaxes(2, 3), out_g.reshape(1, W_C))


SELECT_PAGES = 16


def _moba_select_kernel(pt_ref, *refs, pages_per_block):
    k_refs = refs[:SELECT_PAGES]
    q_ref, sel_ref, gate_ref = refs[SELECT_PAGES:]
    s = pl.program_id(1)
    ppb = pages_per_block
    blocks_per_step = SELECT_PAGES // ppb
    qb = jnp.broadcast_to(q_ref[...], k_refs[0].shape)
    for i in range(blocks_per_step):
        acc = k_refs[i * ppb][...]
        for g in range(1, ppb):
            acc = acc + k_refs[i * ppb + g][...]
        prod = acc * qb
        part = prod[:, 0:8, :]
        for r in range(1, prod.shape[1] // 8):
            part = part + prod[:, 8 * r:8 * (r + 1), :]
        gate_ref[:, s * blocks_per_step + i] = part

    @pl.when(s == pl.num_programs(1) - 1)
    def _():
        H, nb, _, page = gate_ref.shape
        part_sum = jnp.sum(gate_ref[...], axis=2).reshape(H * nb, page)
        col = jnp.sum(part_sum, axis=-1, keepdims=True) * (1.0 / MOBA_BLOCK)
        eye = _eye(nb)
        head_row = lax.broadcasted_iota(jnp.int32, (H, 1), 0)
        gate = jnp.zeros((H, nb), F32)
        for h in range(H):
            gate = jnp.where(head_row == h, _col_to_row(col[h * nb:(h + 1) * nb, :], eye), gate)
        blk_id = lax.broadcasted_iota(jnp.int32, (1, nb), 1)
        rank = jnp.zeros(gate.shape, jnp.int32)
        for m in range(nb):
            gm = gate[:, m:m + 1]
            beats = jnp.logical_or(gm > gate, jnp.logical_and(gm == gate, m < blk_id))
            rank = rank + jnp.where(beats, 1, 0)
        out_col = lax.broadcasted_iota(jnp.int32, sel_ref.shape, 1)
        out = jnp.zeros(sel_ref.shape, jnp.int32)
        for r in range(MOBA_TOPK):
            idx = jnp.sum(jnp.where(rank == r, blk_id, 0), axis=1, keepdims=True)
            out = jnp.where(out_col == r, idx, out)
        sel_ref[...] = out


def _moba_select(cache_kt, page_table, q_col):
    Bs, n_pages = page_table.shape
    _, H, dh, page = cache_kt.shape
    ppb = MOBA_BLOCK // page
    nb = n_pages // ppb
    assert nb >= MOBA_TOPK and n_pages % SELECT_PAGES == 0 and SELECT_PAGES % ppb == 0

    def page_spec(i):
        return pl.BlockSpec((None, H, dh, page), lambda b, s, pt: (pt[b, s * SELECT_PAGES + i], 0, 0, 0))

    grid_spec = pltpu.PrefetchScalarGridSpec(
        num_scalar_prefetch=1,
        grid=(Bs, n_pages // SELECT_PAGES),
        in_specs=[page_spec(i) for i in range(SELECT_PAGES)]
        + [pl.BlockSpec((None, H, dh, 1), lambda b, s, pt: (b, 0, 0, 0))],
        out_specs=pl.BlockSpec((None, H, 8), lambda b, s, pt: (b, 0, 0)),
        scratch_shapes=[pltpu.VMEM((H, nb, 8, page), F32)],
    )
    return pl.pallas_call(
        functools.partial(_moba_select_kernel, pages_per_block=ppb),
        grid_spec=grid_spec,
        out_shape=jax.ShapeDtypeStruct((Bs, H, 8), jnp.int32),
        compiler_params=_cparams(("parallel", "arbitrary")),
        name="moba_select",
    )(page_table, *([cache_kt] * SELECT_PAGES), q_col)


def _moba_decode_kernel(pt_ref, sel_ref, q_ref, kn_ref, vn_ref, ck_ref, cv_ref, o_ref, kbuf, vbuf, sem,
                        *, pages_per_block, page):
    b = pl.program_id(0)
    nb_ = pl.num_programs(0)
    H = q_ref.shape[0]
    ppb = pages_per_block

    def copies(bb, slot):
        out = []
        for h in range(H):
            for r in range(MOBA_TOPK):
                blk = sel_ref[bb, h, r]
                for g in range(ppb):
                    pg = pt_ref[bb, blk * ppb + g]
                    dst = pl.ds((r * ppb + g) * page, page)
                    out.append(pltpu.make_async_copy(ck_ref.at[pg, h], kbuf.at[slot, h, :, dst], sem.at[slot, 0]))
                    out.append(pltpu.make_async_copy(cv_ref.at[pg, h], vbuf.at[slot, h, :, dst], sem.at[slot, 1]))
        return out

    slot = b % 2

    @pl.when(b == 0)
    def _():
        for c in copies(b, slot):
            c.start()

    @pl.when(b + 1 < nb_)
    def _():
        for c in copies(b + 1, 1 - slot):
            c.start()

    for c in copies(b, slot):
        c.wait()

    row = lax.broadcasted_iota(jnp.int32, (H, 1), 0)
    out = jnp.zeros(o_ref.shape, F32)
    q = q_ref[...]
    kn = kn_ref[...]
    vn = vn_ref[...]
    for h in range(H):
        qh = q[h:h + 1, :] * (DH_A ** -0.5)
        q8 = jnp.broadcast_to(qh, (8, qh.shape[1])).astype(BF16)
        kh = kbuf[slot, h].astype(BF16)
        vh = vbuf[slot, h].astype(BF16)
        s = jnp.dot(q8, kh, preferred_element_type=F32)[0:1, :]
        s_self = jnp.sum(qh * kn[h:h + 1, :], axis=-1, keepdims=True)
        m = jnp.maximum(jnp.max(s, axis=-1, keepdims=True), s_self)
        p = jnp.exp(s - m)
        p_self = jnp.exp(s_self - m)
        l = jnp.sum(p, axis=-1, keepdims=True) + p_self
        p8 = jnp.broadcast_to(p, (8, p.shape[1])).astype(BF16)
        pv = _nt(p8, vh)[0:1, :]
        oh = (pv + p_self * vn[h:h + 1, :]) / l
        out = jnp.where(row == h, oh, out)
    o_ref[...] = out


def _moba_decode(cache_kt, cache_vt, page_table, sel, q, k_new, v_new):
    Bs, n_pages = page_table.shape
    _, H, dh, page = cache_kt.shape
    ppb = MOBA_BLOCK // page
    rows = MOBA_TOPK * MOBA_BLOCK
    vec = pl.BlockSpec((None, H, dh), lambda b, pt, sl: (b, 0, 0))
    grid_spec = pltpu.PrefetchScalarGridSpec(
        num_scalar_prefetch=2,
        grid=(Bs,),
        in_specs=[vec, vec, vec, pl.BlockSpec(memory_space=pl.ANY), pl.BlockSpec(memory_space=pl.ANY)],
        out_specs=vec,
        scratch_shapes=[pltpu.VMEM((2, H, dh, rows), F32), pltpu.VMEM((2, H, dh, rows), F32),
                        pltpu.SemaphoreType.DMA((2, 2))],
    )
    return pl.pallas_call(
        functools.partial(_moba_decode_kernel, pages_per_block=ppb, page=page),
        grid_spec=grid_spec,
        out_shape=jax.ShapeDtypeStruct((Bs, H, dh), F32),
        compiler_params=_cparams(("arbitrary",)),
        name="moba_decode",
    )(page_table, sel, q, k_new, v_new, cache_kt, cache_vt)


def _ret_decode_kernel(q_ref, k_ref, v_ref, gb_ref, rg_ref, g_ref, s_ref, o_ref, so_ref):
    H, dk = q_ref.shape
    eye = _eye(dk)
    row = lax.broadcasted_iota(jnp.int32, (H, 1), 0)
    q = q_ref[...]
    k = k_ref[...]
    v = v_ref[...]
    g_all = g_ref[...]
    out = jnp.zeros(o_ref.shape, F32)
    for h in range(H):
        qh, kh, vh = q[h:h + 1, :], k[h:h + 1, :], v[h:h + 1, :]
        g = g_all[h:h + 1, 0:1]
        S = s_ref[h]
        att = jnp.sum(qh * kh, axis=-1, keepdims=True)
        cross = jnp.sum(_row_to_col(qh * g, eye) * S, axis=0, keepdims=True)
        out = jnp.where(row == h, att * vh + cross, out)
        so_ref[h] = S * g + _row_to_col(kh, eye) * vh
    gb = gb_ref[...]
    y = out * lax.rsqrt(jnp.mean(out * out, axis=-1, keepdims=True) + EPS) * rg_ref[...]
    o_ref[...] = y * (gb * _sigmoid(gb))


def _ret_decode(q, k, v, gb, ret_g, state):
    Bs, H, dk = q.shape
    log_g = jnp.log1p(-jnp.exp2(-5.0 - jnp.arange(H_B, dtype=F32)))
    g = jnp.broadcast_to(jnp.exp(1.0 * log_g)[:, None], (H, LANES))
    vec = pl.BlockSpec((None, H, dk), lambda b: (b, 0, 0))
    st = pl.BlockSpec((None, H, dk, DV_B), lambda b: (b, 0, 0, 0))
    return pl.pallas_call(
        _ret_decode_kernel,
        grid=(Bs,),
        in_specs=[vec, vec, vec, vec, pl.BlockSpec((H, dk), lambda b: (0, 0)),
                  pl.BlockSpec((H, LANES), lambda b: (0, 0)), st],
        out_specs=[vec, st],
        out_shape=[jax.ShapeDtypeStruct((Bs, H, DV_B), F32), jax.ShapeDtypeStruct(state.shape, F32)],
        compiler_params=_cparams(("parallel",)),
        name="retention_decode",
    )(q, k, v, gb, ret_g.reshape(H, DV_B), g, state)


def _mlstm_decode_kernel(q_ref, k_ref, v_ref, og_ref, ig_ref, fg_ref, bi_ref, bf_ref, ng_ref, c_ref, n_ref,
                         m_ref, h_ref, co_ref, no_ref, mo_ref):
    H, d = q_ref.shape
    eye = _eye(d)
    row = lax.broadcasted_iota(jnp.int32, (H, 1), 0)
    q = q_ref[...]
    k = k_ref[...] * (DH_C ** -0.5)
    v = v_ref[...]
    n0 = n_ref[...]
    ig = ig_ref[...] + bi_ref[...]
    b = _log_sigmoid(fg_ref[...] + bf_ref[...])
    m0 = m_ref[...]
    log_inter = b + m0
    m_t = jnp.maximum(log_inter, ig)
    w_intra = jnp.exp(ig - m_t)
    w_inter = jnp.exp(log_inter - m_t)
    s = jnp.sum(q * k, axis=-1, keepdims=True) * w_intra
    den = s + w_inter * jnp.sum(q * n0, axis=-1, keepdims=True)
    scale = 1.0 / jnp.maximum(jnp.abs(den), jnp.exp(-m_t))
    w_k = w_intra
    decay = w_inter
    hs = jnp.zeros(h_ref.shape, F32)
    for h in range(H):
        qh, kh, vh = q[h:h + 1, :], k[h:h + 1, :], v[h:h + 1, :]
        Cm = c_ref[h]
        cq = _col_to_row(jnp.sum(Cm * qh, axis=1, keepdims=True), eye)
        num = s[h:h + 1, :] * vh + cq * w_inter[h:h + 1, :]
        hs = jnp.where(row == h, num * scale[h:h + 1, :], hs)
        co_ref[h] = decay[h:h + 1, :] * Cm + _row_to_col(vh * w_k[h:h + 1, :], eye) * kh
    no_ref[...] = decay * n0 + k * w_k
    mo_ref[...] = m_t
    hn = hs * lax.rsqrt(jnp.mean(hs * hs, axis=-1, keepdims=True) + EPS) * ng_ref[...]
    h_ref[...] = hn * _sigmoid(og_ref[...])


def _mlstm_decode(q, k, v, og, ig, fg, gate_bias, out_g, C0, n0, m0):
    Bs, H, d = q.shape
    vec = pl.BlockSpec((None, H, d), lambda b: (b, 0, 0))
    sc = pl.BlockSpec((None, H, 1), lambda b: (b, 0, 0))
    st = pl.BlockSpec((None, H, d, d), lambda b: (b, 0, 0, 0))
    bias = pl.BlockSpec((H, 1), lambda b: (0, 0))
    return pl.pallas_call(
        _mlstm_decode_kernel,
        grid=(Bs,),
        in_specs=[vec, vec, vec, vec, sc, sc, bias, bias, pl.BlockSpec((H, d), lambda b: (0, 0)), st, vec, sc],
        out_specs=[vec, st, vec, sc],
        out_shape=[jax.ShapeDtypeStruct((Bs, H, d), F32), jax.ShapeDtypeStruct(C0.shape, F32),
                   jax.ShapeDtypeStruct((Bs, H, d), F32), jax.ShapeDtypeStruct((Bs, H, 1), F32)],
        compiler_params=_cparams(("parallel",)),
        name="mlstm_decode",
    )(q, k, v, og, ig, fg, gate_bias[0].reshape(H, 1), gate_bias[1].reshape(H, 1), out_g.reshape(H, d), C0, n0, m0)


def _rope_tables(pos):
    half = DH_A // 2
    inv = ROPE_THETA ** (-jnp.arange(half, dtype=F32) / half)
    ang = pos[:, None] * inv[None, :]
    cos = jnp.cos(ang)
    sin = jnp.sin(ang)
    reps = LANES // DH_A
    cos_t = jnp.tile(jnp.concatenate([cos, cos], axis=-1), (1, reps))
    sin_t = jnp.tile(jnp.concatenate([-sin, sin], axis=-1), (1, reps))
    return cos_t, sin_t


def _row_tile(m, pref):
    return pref if m % pref == 0 else m


def kernel(x_prompt, x_sample, cache_k, cache_v, page_table, state_ret, state_mlstm_C, state_mlstm_n,
           state_mlstm_m, state_ffn_conv, ab_norm_g, ab_w_in, ab_ret_norm_g, ab_w_out, c_norm_g, c_w_in,
           c_gate_bias, c_out_norm_g, c_w_out, ffn_norm_g, ffn_w1, ffn_w3, ffn_conv_w, ffn_conv_b, ffn_w2,
           final_norm_g):
    Bp, Tp, D = x_prompt.shape
    Bs, Ts, _ = x_sample.shape
    assert Ts == 1
    past_len = page_table.shape[1] * cache_k.shape[2]
    Mp = Bp * Tp
    xp = x_prompt.reshape(Mp, D)
    xs = x_sample.reshape(Bs, D)
    tm_p = _row_tile(Tp, 1024)
    tm_ffn = _row_tile(Tp, 1024)
    tm_s = Bs
    tf = 256
    rope_p = _rope_tables(jnp.arange(Tp, dtype=jnp.int32).astype(F32))
    rope_s = _rope_tables(jnp.full((Bs,), past_len, jnp.int32).astype(F32))
    gw = 2 * H_C

    outs = {}

    w_in = ab_w_in[0].astype(BF16)
    w_out = ab_w_out[0].astype(BF16)
    w_out_a, w_out_b = w_out[:W_A], w_out[W_A:]

    z = _norm_matmul(xp, ab_norm_g[0], w_in, tm_p, IN_AB // 2)
    oa, kt, vt = _moba_prefill(z, rope_p, Bp, Tp)
    ob, s_pair = _ret_prefill(z, rope_p, ab_ret_norm_g[0], Bp, Tp)
    xp = _proj_residual([oa, ob], [w_out_a, w_out_b], xp, tm_p)
    outs["k_prompt"] = kt.reshape(Bp, H_A, DH_A, Tp).transpose(0, 3, 1, 2)[None]
    outs["v_prompt"] = vt.reshape(Bp, H_A, DH_A, Tp).transpose(0, 3, 1, 2)[None]
    hb = LANES // DK_B
    s_heads = jnp.stack([s_pair[:, :, i * DK_B:(i + 1) * DK_B, i * DV_B:(i + 1) * DV_B] for i in range(hb)], axis=2)
    outs["ret_prompt"] = s_heads.reshape(1, Bp, H_B, DK_B, DV_B)

    zs = _norm_matmul(xs, ab_norm_g[0], w_in, tm_s, AB_TILE, rope=rope_s)
    seg = lambda t: zs[:, t * AB_TILE:(t + 1) * AB_TILE].reshape(Bs, H_A, DH_A)
    qa_s, ka_s, va_s, qb_s, kb_s, vb_s, gb_s = (seg(t) for t in range(7))
    cache_kt = cache_k[0].transpose(0, 2, 3, 1)
    cache_vt = cache_v[0].transpose(0, 2, 3, 1)
    sel = _moba_select(cache_kt, page_table, qa_s[..., None])
    oa_s = _moba_decode(cache_kt, cache_vt, page_table, sel, qa_s, ka_s, va_s)
    ob_s, s_new = _ret_decode(qb_s, kb_s, vb_s, gb_s, ab_ret_norm_g[0], state_ret[0])
    xs = _proj_residual([oa_s.reshape(Bs, W_A).astype(BF16), ob_s.reshape(Bs, W_B).astype(BF16)],
                        [w_out_a, w_out_b], xs, tm_s)
    outs["k_sample"] = ka_s.reshape(1, Bs, 1, H_A, DH_A)
    outs["v_sample"] = va_s.reshape(1, Bs, 1, H_A, DH_A)
    outs["ret_sample"] = s_new[None]

    conv_p, conv_s = [], []

    def ffn_both(l, xp, xs, final_g):
        w1 = ffn_w1[l].astype(BF16)
        w3 = ffn_w3[l].astype(BF16)
        w2 = ffn_w2[l].astype(BF16)
        tail = xp.reshape(Bp, Tp, D)[:, Tp - 8:, :].reshape(Bp * 8, D)
        a_tail = _norm_matmul(tail, ffn_norm_g[l], w1, Bp * 8, tf)
        conv_p.append(a_tail.reshape(Bp, 8, D_FF)[:, 8 - (CONV_W - 1):, :])
        a_s = _norm_matmul(xs, ffn_norm_g[l], w1, tm_s, tf)
        st = state_ffn_conv[l]
        conv_s.append(jnp.stack([st[:, 1, :], a_s], axis=1))
        xp = _ffn(xp, ffn_norm_g[l], w1, w3, ffn_conv_w[l], ffn_conv_b[l], w2, final_g, tm_ffn, seq_len=Tp)
        xs = _ffn(xs, ffn_norm_g[l], w1, w3, ffn_conv_w[l], ffn_conv_b[l], w2, final_g, tm_s,
                  state=(st[:, 0, :], st[:, 1, :]))
        return xp, xs

    xp, xs = ffn_both(0, xp, xs, None)

    w_in = c_w_in[0]
    w_main = w_in[:, :4 * W_C].astype(BF16)
    w_gate = jnp.pad(w_in[:, 4 * W_C:], ((0, 0), (0, LANES - gw))).astype(BF16)
    w_out = c_w_out[0].astype(BF16)
    L = MLSTM_CHUNK

    z = _norm_matmul(xp, c_norm_g[0], w_main, tm_p, 1024)
    zg = _norm_matmul(xp, c_norm_g[0], w_gate, tm_p, LANES)
    gates = zg[:, :gw].reshape(Bp, Tp // L, L, 2, H_C).transpose(3, 0, 4, 1, 2)
    h, C_p, n_p, m_p = _mlstm_prefill(z, gates[0], gates[1], c_gate_bias[0], c_out_norm_g[0], Bp, Tp)
    xp = _proj_residual([h], [w_out], xp, tm_p)
    outs["C_prompt"] = C_p[None]
    outs["n_prompt"] = n_p.reshape(1, Bp, H_C, DH_C)
    outs["m_prompt"] = m_p[:, :, 0, 0][None]

    zs = _norm_matmul(xs, c_norm_g[0], w_main, tm_s, 1024)
    zgs = _norm_matmul(xs, c_norm_g[0], w_gate, tm_s, LANES)
    segc = lambda t: zs[:, t * W_C:(t + 1) * W_C].reshape(Bs, H_C, DH_C)
    ig_s = zgs[:, :H_C][:, :, None]
    fg_s = zgs[:, H_C:gw][:, :, None]
    h_s, C_s, n_s, m_s = _mlstm_decode(segc(0), segc(1), segc(2), segc(3), ig_s, fg_s, c_gate_bias[0], c_out_norm_g[0],
                                       state_mlstm_C[0], state_mlstm_n[0], state_mlstm_m[0][:, :, None])
    xs = _proj_residual([h_s.reshape(Bs, W_C).astype(BF16)], [w_out], xs, tm_s)
    outs["C_sample"] = C_s[None]
    outs["n_sample"] = n_s[None]
    outs["m_sample"] = m_s[:, :, 0][None]

    xp, xs = ffn_both(1, xp, xs, final_norm_g)

    return (xp.reshape(Bp, Tp, D), xs.reshape(Bs, 1, D),
            outs["k_prompt"], outs["v_prompt"], outs["k_sample"], outs["v_sample"],
            outs["ret_prompt"], outs["ret_sample"], outs["C_prompt"], outs["C_sample"],
            outs["n_prompt"], outs["n_sample"], outs["m_prompt"], outs["m_sample"],
            jnp.stack(conv_p), jnp.stack(conv_s))
```

```python
import functools

import jax
import jax.numpy as jnp
from jax import lax
from jax.experimental import pallas as pl
from jax.experimental.pallas import tpu as pltpu

F32 = jnp.float32
BF16 = jnp.bfloat16
HIGHEST = lax.Precision.HIGHEST

LANES = 128
D_MODEL = 1024
H_A = 8
DH_A = 64
MOBA_BLOCK = 256
MOBA_TOPK = 3
H_B = 8
DK_B = 64
DV_B = 64
RET_CHUNK = 128
H_C = 8
DH_C = D_MODEL // H_C
MLSTM_CHUNK = 128
D_FF = 11 * D_MODEL // 4
CONV_W = 3
ROPE_THETA = 10000.0
EPS = 1e-6
W_A = H_A * DH_A
W_B = H_B * DV_B
W_C = H_C * DH_C
IN_AB = 3 * W_A + 2 * H_B * DK_B + 2 * W_B
AB_TILE = 512
AB_ROPE_TILES = (0, 1, 3, 4)
AB_KB_TILE = 4
VMEM_LIMIT = 56 * 1024 * 1024

NEG_INF = float("-inf")
LOG2_E = 1.4426950408889634


def _cparams(sem):
    return pltpu.CompilerParams(dimension_semantics=sem, vmem_limit_bytes=VMEM_LIMIT)


def _nt(a, b, **kw):
    return lax.dot_general(a, b, (((1,), (1,)), ((), ())), preferred_element_type=F32, **kw)


def _tn(a, b, **kw):
    return lax.dot_general(a, b, (((0,), (0,)), ((), ())), preferred_element_type=F32, **kw)


def _rms_rows(x, g):
    ms = jnp.mean(x * x, axis=-1, keepdims=True)
    return x * lax.rsqrt(ms + EPS) * g


def _eye(n):
    return lax.broadcasted_iota(jnp.int32, (n, n), 0) == lax.broadcasted_iota(jnp.int32, (n, n), 1)


def _row_to_col(row, eye):
    return jnp.sum(jnp.where(eye, row, 0.0), axis=1, keepdims=True)


def _col_to_row(col, eye):
    return jnp.sum(jnp.where(eye, col, 0.0), axis=0, keepdims=True)


def _rowsum_rep(x, ones=None):
    if ones is None:
        ones = jnp.ones((x.shape[1], LANES), BF16)
    hi = x.astype(BF16)
    lo = (x - hi.astype(F32)).astype(BF16)
    return (jnp.dot(hi, ones, preferred_element_type=F32) + jnp.dot(lo, ones, preferred_element_type=F32))


def _log_sigmoid(x):
    return jnp.minimum(x, 0.0) - jnp.log1p(jnp.exp(-jnp.abs(x)))


def _sigmoid(x):
    return 1.0 / (1.0 + jnp.exp(-x))


def _gelu_tanh(x):
    c = 0.7978845608028654
    return (0.5 * x) * (1.0 + jnp.tanh(x * (c + (0.044715 * c) * (x * x))))


def _norm_matmul_kernel(x_ref, g_ref, w_ref, cos_ref, sin_ref, o_ref, xn_ref, *, rope_tiles, scale_tile, scale):
    j = pl.program_id(1)

    @pl.when(j == 0)
    def _():
        xn_ref[...] = _rms_rows(x_ref[...], g_ref[...]).astype(BF16)

    z = jnp.dot(xn_ref[...], w_ref[...], preferred_element_type=F32)
    if not rope_tiles:
        o_ref[...] = z
        return

    is_rope = functools.reduce(jnp.logical_or, [j == t for t in rope_tiles])

    @pl.when(is_rope)
    def _():
        sc = jnp.where(j == scale_tile, scale, 1.0).astype(F32)
        cos = cos_ref[...]
        sin = sin_ref[...]
        for c in range(z.shape[1] // LANES):
            cols = slice(c * LANES, (c + 1) * LANES)
            o_ref[:, cols] = _rope_lanes(z[:, cols], cos, sin) * sc

    @pl.when(jnp.logical_not(is_rope))
    def _():
        o_ref[...] = z


def _norm_matmul_side_kernel(x_ref, g_ref, w_ref, ws_ref, o_ref, os_ref, xn_ref):
    @pl.when(pl.program_id(1) == 0)
    def _():
        xn_ref[...] = _rms_rows(x_ref[...], g_ref[...]).astype(BF16)
        os_ref[...] = jnp.dot(xn_ref[...], ws_ref[...], preferred_element_type=F32)

    o_ref[...] = jnp.dot(xn_ref[...], w_ref[...], preferred_element_type=F32)


def _norm_matmul_side(x, g, w, w_side, tm, tn):
    M, D = x.shape
    N = w.shape[1]
    Ns = w_side.shape[1]
    assert M % tm == 0 and N % tn == 0
    return pl.pallas_call(
        _norm_matmul_side_kernel,
        grid=(M // tm, N // tn),
        in_specs=[pl.BlockSpec((tm, D), lambda i, j: (i, 0)),
                  pl.BlockSpec((1, D), lambda i, j: (0, 0)),
                  pl.BlockSpec((D, tn), lambda i, j: (0, j)),
                  pl.BlockSpec((D, Ns), lambda i, j: (0, 0))],
        out_specs=[pl.BlockSpec((tm, tn), lambda i, j: (i, j)),
                   pl.BlockSpec((tm, Ns), lambda i, j: (i, 0))],
        out_shape=[jax.ShapeDtypeStruct((M, N), F32), jax.ShapeDtypeStruct((M, Ns), F32)],
        scratch_shapes=[pltpu.VMEM((tm, D), BF16)],
        compiler_params=_cparams(("parallel", "arbitrary")),
        name="norm_matmul_side",
    )(x, g.reshape(1, D), w, w_side)


def _norm_matmul(x, g, w, tm, tn, rope=None):
    M, D = x.shape
    N = w.shape[1]
    assert M % tm == 0 and N % tn == 0
    if rope is None:
        cos = sin = jnp.zeros((8, LANES), F32)
        tab_spec = pl.BlockSpec((8, LANES), lambda i, j: (0, 0))
        kern = functools.partial(_norm_matmul_kernel, rope_tiles=(), scale_tile=-1, scale=1.0)
    else:
        cos, sin = rope
        nt = cos.shape[0] // tm
        tab_spec = pl.BlockSpec((tm, LANES), lambda i, j: (i % nt, 0))
        kern = functools.partial(_norm_matmul_kernel, rope_tiles=AB_ROPE_TILES, scale_tile=AB_KB_TILE,
                                 scale=DK_B ** -0.5)
    return pl.pallas_call(
        kern,
        grid=(M // tm, N // tn),
        in_specs=[pl.BlockSpec((tm, D), lambda i, j: (i, 0)),
                  pl.BlockSpec((1, D), lambda i, j: (0, 0)),
                  pl.BlockSpec((D, tn), lambda i, j: (0, j)),
                  tab_spec, tab_spec],
        out_specs=pl.BlockSpec((tm, tn), lambda i, j: (i, j)),
        out_shape=jax.ShapeDtypeStruct((M, N), F32),
        scratch_shapes=[pltpu.VMEM((tm, D), BF16)],
        compiler_params=_cparams(("parallel", "arbitrary")),
        name="norm_matmul",
    )(x, g.reshape(1, D), w, cos, sin)


def _proj_residual_kernel(*refs, n_in):
    a_refs = refs[:n_in]
    w_refs = refs[n_in:2 * n_in]
    res_ref = refs[2 * n_in]
    o_ref = refs[2 * n_in + 1]
    y = res_ref[...]
    acc = None
    for a_ref, w_ref in zip(a_refs, w_refs):
        d = jnp.dot(a_ref[...], w_ref[...], preferred_element_type=F32)
        acc = d if acc is None else acc + d
    o_ref[...] = y + acc


def _proj_residual(acts, ws, res, tm):
    M, D = res.shape
    n_in = len(acts)
    in_specs = ([pl.BlockSpec((tm, a.shape[1]), lambda i: (i, 0)) for a in acts]
                + [pl.BlockSpec(w.shape, lambda i: (0, 0)) for w in ws]
                + [pl.BlockSpec((tm, D), lambda i: (i, 0))])
    return pl.pallas_call(
        functools.partial(_proj_residual_kernel, n_in=n_in),
        grid=(M // tm,),
        in_specs=in_specs,
        out_specs=pl.BlockSpec((tm, D), lambda i: (i, 0)),
        out_shape=jax.ShapeDtypeStruct((M, D), F32),
        compiler_params=_cparams(("parallel",)),
        name="proj_residual",
    )(*acts, *ws, res)


PREV_ROWS = 16
FFN_CHUNK = 256


def _ffn_kernel(*refs, seq_mode, tiles_per_seq, final_norm):
    if seq_mode:
        (x_ref, xp_ref, g_ref, w1_ref, w3_ref, cw_ref, cb_ref, w2_ref, fg_ref, o_ref, a_ref, y_ref) = refs
    else:
        (x_ref, s0_ref, s1_ref, g_ref, w1_ref, w3_ref, cw_ref, cb_ref, w2_ref, fg_ref, o_ref, a_ref, y_ref) = refs
    i = pl.program_id(0)
    x = x_ref[...]
    tm = x.shape[0]
    F = w1_ref.shape[1]
    xn = _rms_rows(x, g_ref[...]).astype(BF16)
    if seq_mode:
        xpn = _rms_rows(xp_ref[...], g_ref[...]).astype(BF16)
        has_prev = ((i % tiles_per_seq) != 0).astype(F32)
        row8 = lax.broadcasted_iota(jnp.int32, (8, 1), 0)
    for c0 in range(0, F, FFN_CHUNK):
        cols = slice(c0, min(c0 + FFN_CHUNK, F))
        w1c = w1_ref[:, cols]
        a = jnp.dot(xn, w1c, preferred_element_type=F32)
        gate = jnp.dot(xn, w3_ref[:, cols], preferred_element_type=F32)
        a_ref[:, cols] = a[tm - a_ref.shape[0]:, :]
        if seq_mode:
            ap = jnp.dot(xpn, w1c, preferred_element_type=F32)
            p1 = ap[PREV_ROWS - 1:PREV_ROWS, :] * has_prev
            p2 = ap[PREV_ROWS - 2:PREV_ROWS - 1, :] * has_prev
            a1 = pltpu.roll(a, 1, 0)
            a2 = pltpu.roll(a, 2, 0)
            top1 = jnp.where(row8 == 0, p1, a1[0:8, :])
            top2 = jnp.where(row8 == 0, p2, jnp.where(row8 == 1, p1, a2[0:8, :]))
            a1 = jnp.concatenate([top1, a1[8:, :]], axis=0)
            a2 = jnp.concatenate([top2, a2[8:, :]], axis=0)
        else:
            a1 = s1_ref[:, cols]
            a2 = s0_ref[:, cols]
        ac = cb_ref[:, cols] + a2 * cw_ref[0:1, cols]
        ac = ac + a1 * cw_ref[1:2, cols]
        ac = ac + a * cw_ref[2:3, cols]
        y_ref[:, cols] = (_gelu_tanh(ac) * gate).astype(BF16)
    acc = x + jnp.dot(y_ref[...], w2_ref[...], preferred_element_type=F32)
    if final_norm:
        acc = _rms_rows(acc, fg_ref[...])
    o_ref[...] = acc


def _ffn(x, norm_g, w1, w3, conv_w, conv_b, w2, final_g, tm, seq_len=None, state=None):
    M, D = x.shape
    F = w1.shape[1]
    seq_mode = state is None
    final_norm = final_g is not None
    fg = (final_g if final_norm else jnp.ones((D,), F32)).reshape(1, D)
    whole = lambda shape: pl.BlockSpec(shape, lambda i: (0, 0), pipeline_mode=pl.Buffered(1))
    common = [whole((1, D)), whole((D, F)), whole((D, F)), whole((CONV_W, F)), whole((1, F)), whole((F, D)),
              whole((1, D))]
    common_args = (norm_g.reshape(1, D), w1, w3, conv_w, conv_b.reshape(1, F), w2, fg)
    x_spec = pl.BlockSpec((tm, D), lambda i: (i, 0))
    if seq_mode:
        assert seq_len % tm == 0 and tm % PREV_ROWS == 0
        r = tm // PREV_ROWS
        in_specs = [x_spec, pl.BlockSpec((PREV_ROWS, D), lambda i: (jnp.maximum(i * r - 1, 0), 0))] + common
        args = (x, x) + common_args
        tiles_per_seq = seq_len // tm
    else:
        s_spec = pl.BlockSpec((tm, F), lambda i: (i, 0))
        in_specs = [x_spec, s_spec, s_spec] + common
        args = (x, state[0], state[1]) + common_args
        tiles_per_seq = 1
    if seq_mode:
        a_spec = pl.BlockSpec((None, 8, F), lambda i: (i // tiles_per_seq, 0, 0))
        a_shape = jax.ShapeDtypeStruct((M // seq_len, 8, F), F32)
    else:
        a_spec = pl.BlockSpec((tm, F), lambda i: (i, 0))
        a_shape = jax.ShapeDtypeStruct((M, F), F32)
    return pl.pallas_call(
        functools.partial(_ffn_kernel, seq_mode=seq_mode, tiles_per_seq=tiles_per_seq, final_norm=final_norm),
        grid=(M // tm,),
        in_specs=in_specs,
        out_specs=[pl.BlockSpec((tm, D), lambda i: (i, 0)), a_spec],
        out_shape=[jax.ShapeDtypeStruct((M, D), F32), a_shape],
        scratch_shapes=[pltpu.VMEM((tm, F), BF16)],
        compiler_params=_cparams(("arbitrary",)),
        name="conv_ffn",
    )(*args)


def _rope_lanes(x, cos, sin):
    lane = lax.broadcasted_iota(jnp.int32, (1, LANES), 1)
    first_half = (lane % DH_A) < (DH_A // 2)
    partner = jnp.where(first_half, pltpu.roll(x, LANES - DH_A // 2, 1), pltpu.roll(x, DH_A // 2, 1))
    return x * cos + partner * sin


def _moba_prefill_kernel(q_ref, k_ref, v_ref, cos_ref, sin_ref, o_ref, kt_ref, vt_ref, kb_ref, vb_ref, *, nb):
    blk = MOBA_BLOCK
    heads = LANES // DH_A
    T = q_ref.shape[0]
    cos = cos_ref[...]
    sin = sin_ref[...]
    k = _rope_lanes(k_ref[...], cos, sin)
    kt_ref[...] = k.T
    vt_ref[...] = v_ref[...].T
    kb_ref[...] = k.astype(BF16)
    vb_ref[:, 0:LANES] = v_ref[...].astype(BF16)
    vb_ref[:, LANES:2 * LANES] = jnp.ones((T, LANES), BF16)
    kmean = jnp.concatenate(
        [jnp.sum(k[n * blk:(n + 1) * blk, :], axis=0, keepdims=True) * (1.0 / blk) for n in range(nb)]
        + [jnp.zeros((8 - nb, LANES), F32)] * (nb < 8), axis=0)

    q2 = _rope_lanes(q_ref[...], cos, sin)
    lane = lax.broadcasted_iota(jnp.int32, (1, LANES), 1)
    blk_id = lax.broadcasted_iota(jnp.int32, (8, 1), 0)
    q_blk = lax.broadcasted_iota(jnp.int32, (1, T), 1) // blk
    past = blk_id < q_blk
    causal = (lax.broadcasted_iota(jnp.int32, (blk, blk), 1) <= lax.broadcasted_iota(jnp.int32, (blk, blk), 0))
    hms, qss, sels = [], [], []
    for h in range(heads):
        hm = (lane // DH_A) == h
        qh = jnp.where(hm, q2, 0.0)
        gt = _nt(kmean, qh, precision=HIGHEST)
        sel_t = jnp.zeros_like(gt)
        for n in range(nb - 1):
            gn = gt[n:n + 1, :]
            beats = jnp.logical_and(past, jnp.logical_or(gt > gn, jnp.logical_and(gt == gn, blk_id < n)))
            rank = jnp.sum(beats.astype(F32), axis=0, keepdims=True)
            sel_n = jnp.where(jnp.logical_and(rank < MOBA_TOPK, n < q_blk), 1.0, 0.0)
            sel_t = jnp.where(blk_id == n, sel_n, sel_t)
        hms.append(hm)
        qss.append((qh * (DH_A ** -0.5 * LOG2_E)).astype(BF16))
        sels.append(jnp.where(sel_t.T > 0.5, 0.0, NEG_INF))

    for qi in range(nb):
        rows = slice(qi * blk, (qi + 1) * blk)
        outs = []
        for h in range(heads):
            qs = qss[h][rows, :]
            pieces = []
            for n in range(qi + 1):
                s = _nt(qs, kb_ref[n * blk:(n + 1) * blk, :])
                pieces.append(jnp.where(causal, s, NEG_INF) if n == qi else s + sels[h][rows, n:n + 1])
            m = functools.reduce(jnp.maximum, pieces)
            m = jnp.max(m, axis=-1, keepdims=True)
            p_all = jnp.concatenate([jnp.exp2((s - m).astype(BF16)) for s in pieces], axis=1)
            acc = jnp.dot(p_all, vb_ref[0:(qi + 1) * blk, :], preferred_element_type=F32)
            outs.append(acc[:, 0:LANES] / acc[:, LANES:2 * LANES])
        out = outs[0]
        for h in range(1, heads):
            out = jnp.where(hms[h], outs[h], out)
        o_ref[rows, :] = out.astype(o_ref.dtype)


def _moba_prefill(z, rope, B, T):
    blk = MOBA_BLOCK
    assert T % blk == 0
    nb = T // blk
    assert nb <= 8
    cpt = AB_TILE // LANES
    return pl.pallas_call(
        functools.partial(_moba_prefill_kernel, nb=nb),
        grid=(B, W_A // LANES),
        in_specs=[pl.BlockSpec((T, LANES), lambda b, p: (b, p)),
                  pl.BlockSpec((T, LANES), lambda b, p: (b, cpt + p)),
                  pl.BlockSpec((T, LANES), lambda b, p: (b, 2 * cpt + p)),
                  pl.BlockSpec((T, LANES), lambda b, p: (0, 0)),
                  pl.BlockSpec((T, LANES), lambda b, p: (0, 0))],
        out_specs=[pl.BlockSpec((T, LANES), lambda b, p: (b, p)),
                   pl.BlockSpec((None, LANES, T), lambda b, p: (b, p, 0)),
                   pl.BlockSpec((None, LANES, T), lambda b, p: (b, p, 0))],
        out_shape=[jax.ShapeDtypeStruct((B * T, W_A), BF16),
                   jax.ShapeDtypeStruct((B, W_A, T), F32),
                   jax.ShapeDtypeStruct((B, W_A, T), F32)],
        scratch_shapes=[pltpu.VMEM((T, LANES), BF16), pltpu.VMEM((T, 2 * LANES), BF16)],
        compiler_params=_cparams(("parallel", "parallel")),
        name="moba_prefill",
    )(z, z, z, rope[0], rope[1])


def _ret_prefill_kernel(q_ref, k_ref, v_ref, gb_ref, cos_ref, sin_ref, rg_ref, dmask_ref, din_ref, dout_ref,
                        gch_ref, o_ref, s_ref, *, n_chunks):
    C = RET_CHUNK
    heads = LANES // DK_B
    lane = lax.broadcasted_iota(jnp.int32, (1, LANES), 1)
    hms = [(lane // DK_B) == h for h in range(heads)]
    row_h = lax.broadcasted_iota(jnp.int32, (LANES, LANES), 0) // DK_B
    col_h = lax.broadcasted_iota(jnp.int32, (LANES, LANES), 1) // DV_B
    same_head = row_h == col_h
    seg_ones = jnp.where(same_head, 1.0, 0.0).astype(BF16)
    din = din_ref[...]
    dout = dout_ref[...]
    gch = gch_ref[...]
    rg = rg_ref[...]
    dmask = dmask_ref[...]

    S = jnp.zeros((LANES, LANES), F32)
    for j in range(n_chunks):
        rows = slice(j * C, (j + 1) * C)
        cos = cos_ref[rows, :]
        sin = sin_ref[rows, :]
        q = _rope_lanes(q_ref[rows, :], cos, sin)
        k = _rope_lanes(k_ref[rows, :], cos, sin) * (DK_B ** -0.5)
        kb = k.astype(BF16)
        vb = v_ref[rows, :].astype(BF16)
        o = jnp.dot((q * din).astype(BF16), S.astype(BF16), preferred_element_type=F32)
        q_st = jnp.concatenate([jnp.where(hm, q, 0.0) for hm in hms], axis=0).astype(BF16)
        att = _nt(q_st, kb) * dmask
        res = jnp.dot(att.astype(BF16), vb, preferred_element_type=F32)
        intra = res[0:C, :]
        for h in range(1, heads):
            intra = jnp.where(hms[h], res[h * C:(h + 1) * C, :], intra)
        o = o + intra
        S = S * gch + jnp.where(same_head, _tn((k * dout).astype(BF16), vb), 0.0)
        ms = _rowsum_rep(o * o, seg_ones) * (1.0 / DV_B)
        g = gb_ref[rows, :]
        y = o * lax.rsqrt(ms + EPS) * rg * (g * _sigmoid(g))
        o_ref[rows, :] = y.astype(o_ref.dtype)
    s_ref[...] = S


def _ret_tables(chunk):
    log_g = jnp.log1p(-jnp.exp2(-5.0 - jnp.arange(H_B, dtype=F32)))
    i = jnp.arange(chunk, dtype=F32)
    d_in = jnp.exp((i[:, None] + 1.0) * log_g)
    d_out = jnp.exp((chunk - 1.0 - i)[:, None] * log_g)
    diff = i[:, None] - i[None, :]
    d_mask = jnp.where(diff >= 0, jnp.exp(jnp.maximum(diff, 0.0)[None] * log_g[:, None, None]), 0.0)
    g_chunk = jnp.exp(chunk * log_g)
    return d_in, d_out, d_mask, g_chunk


def _ret_prefill(z, rope, ret_g, B, T):
    C = RET_CHUNK
    assert T % C == 0
    d_in, d_out, d_mask, g_chunk = _ret_tables(C)
    npair = W_B // LANES
    lanes = lambda t: jnp.repeat(t, DK_B, axis=-1)
    din_l = lanes(d_in).reshape(C, npair, LANES).transpose(1, 0, 2)
    dout_l = lanes(d_out).reshape(C, npair, LANES).transpose(1, 0, 2)
    gch_l = lanes(g_chunk).reshape(npair, 1, LANES)
    cpt = AB_TILE // LANES
    col = lambda t: (lambda b, p: (b, t * cpt + p))
    return pl.pallas_call(
        functools.partial(_ret_prefill_kernel, n_chunks=T // C),
        grid=(B, npair),
        in_specs=[pl.BlockSpec((T, LANES), col(3)), pl.BlockSpec((T, LANES), col(4)),
                  pl.BlockSpec((T, LANES), col(5)), pl.BlockSpec((T, LANES), col(6)),
                  pl.BlockSpec((T, LANES), lambda b, p: (0, 0)),
                  pl.BlockSpec((T, LANES), lambda b, p: (0, 0)),
                  pl.BlockSpec((1, LANES), lambda b, p: (0, p)),
                  pl.BlockSpec((None, (LANES // DK_B) * C, C), lambda b, p: (p, 0, 0)),
                  pl.BlockSpec((None, C, LANES), lambda b, p: (p, 0, 0)),
                  pl.BlockSpec((None, C, LANES), lambda b, p: (p, 0, 0)),
                  pl.BlockSpec((None, 1, LANES), lambda b, p: (p, 0, 0))],
        out_specs=[pl.BlockSpec((T, LANES), lambda b, p: (b, p)),
                   pl.BlockSpec((None, None, LANES, LANES), lambda b, p: (b, p, 0, 0))],
        out_shape=[jax.ShapeDtypeStruct((B * T, W_B), BF16),
                   jax.ShapeDtypeStruct((B, npair, LANES, LANES), F32)],
        compiler_params=_cparams(("parallel", "parallel")),
        name="retention_prefill",
    )(z, z, z, z, rope[0], rope[1], ret_g.reshape(1, W_B), d_mask.reshape(npair, -1, C), din_l, dout_l, gch_l)


def _mlstm_prefill_kernel(bias_ref, q_ref, k_ref, v_ref, og_ref, ig_ref, fg_ref, igc_ref, fgc_ref, ng_ref,
                          h_ref, c_ref, n_ref, m_ref, *, n_chunks):
    L = MLSTM_CHUNK
    hd = pl.program_id(1)
    r_id = lax.broadcasted_iota(jnp.int32, (L, L), 0)
    c_id = lax.broadcasted_iota(jnp.int32, (L, L), 1)
    causal = c_id <= r_id
    upper = jnp.where(r_id <= c_id, 1.0, 0.0).astype(F32)
    lower = jnp.where(c_id <= r_id, 1.0, 0.0).astype(F32)
    i_rows = ig_ref[...] + bias_ref[0, hd]
    b_rows = jnp.dot(_log_sigmoid(fg_ref[...] + bias_ref[1, hd]), upper, preferred_element_type=F32, precision=HIGHEST)
    i_cols = igc_ref[...] + bias_ref[0, hd]
    b_cols = jnp.dot(lower, _log_sigmoid(fgc_ref[...] + bias_ref[1, hd]), preferred_element_type=F32, precision=HIGHEST)
    ng = ng_ref[...]

    Cm = jnp.zeros((L, L), F32)
    n = jnp.zeros((1, L), F32)
    m = jnp.zeros((1, 1), F32)
    for j in range(n_chunks):
        rows = slice(j * L, (j + 1) * L)
        q = q_ref[rows, :]
        k = k_ref[rows, :] * (DH_C ** -0.5)
        v = v_ref[rows, :]
        b_row, i_row = b_rows[j:j + 1, :], i_rows[j:j + 1, :]
        b_col, i_col = b_cols[:, j:j + 1], i_cols[:, j:j + 1]
        b_last = b_row[:, L - 1:L]
        qb = q.astype(BF16)
        kb = k.astype(BF16)
        b_rep = jnp.broadcast_to(b_col, (L, L))
        log_d = jnp.where(causal, (b_rep - b_row) + i_row, NEG_INF)
        m_row = jnp.broadcast_to(jnp.max(log_d, axis=-1, keepdims=True), (L, L))
        s = _nt(qb, kb) * jnp.exp(log_d - m_row)
        sv = jnp.dot(s.astype(BF16), v.astype(BF16), preferred_element_type=F32)
        s_sum = _rowsum_rep(s)
        log_w = (b_last - b_col) + i_col
        m_loc = jnp.max(log_w, axis=0, keepdims=True)
        e_k = jnp.broadcast_to(jnp.exp(log_w - m_loc), (L, L))
        U = _tn((v * e_k).astype(BF16), kb)
        nk = jnp.sum(k * e_k, axis=0, keepdims=True)
        log_inter = b_rep + m
        m_t = jnp.maximum(log_inter, m_row)
        f_intra = jnp.exp(m_row - m_t)
        w_inter = jnp.exp(log_inter - m_t)
        num = sv * f_intra + _nt(qb, Cm.astype(BF16)) * w_inter
        den = s_sum * f_intra + w_inter * _rowsum_rep(q * n)
        h = num / jnp.maximum(jnp.abs(den), jnp.exp(-m_t))
        m_new = jnp.maximum(b_last + m, m_loc)
        decay = jnp.exp(b_last + m - m_new)
        f_k = jnp.exp(m_loc - m_new)
        Cm = decay * Cm + f_k * U
        n = decay * n + f_k * nk
        m = m_new
        hn = h * lax.rsqrt(_rowsum_rep(h * h) * (1.0 / L) + EPS) * ng
        h_ref[rows, :] = (hn * _sigmoid(og_ref[rows, :])).astype(h_ref.dtype)
    c_ref[...] = Cm
    n_ref[...] = n
    m_ref[...] = jnp.broadcast_to(m, m_ref.shape)


def _mlstm_prefill(z, ig, fg, gate_bias, out_g, B, T):
    L = MLSTM_CHUNK
    assert T % L == 0 and DH_C == LANES
    nc = T // L
    col = lambda t: (lambda b, h, bias: (b, t * H_C + h))
    gspec = pl.BlockSpec((None, None, nc, L), lambda b, h, bias: (b, h, 0, 0))
    gcspec = pl.BlockSpec((None, None, L, nc), lambda b, h, bias: (b, h, 0, 0))
    grid_spec = pltpu.PrefetchScalarGridSpec(
        num_scalar_prefetch=1,
        grid=(B, H_C),
        in_specs=[pl.BlockSpec((T, LANES), col(0)), pl.BlockSpec((T, LANES), col(1)),
                  pl.BlockSpec((T, LANES), col(2)), pl.BlockSpec((T, LANES), col(3)),
                  gspec, gspec, gcspec, gcspec,
                  pl.BlockSpec((1, LANES), lambda b, h, bias: (0, h))],
        out_specs=[pl.BlockSpec((T, LANES), lambda b, h, bias: (b, h)),
                   pl.BlockSpec((None, None, L, L), lambda b, h, bias: (b, h, 0, 0)),
                   pl.BlockSpec((None, None, 1, L), lambda b, h, bias: (b, h, 0, 0)),
                   pl.BlockSpec((None, None, 1, LANES), lambda b, h, bias: (b, h, 0, 0))],
    )
    return pl.pallas_call(
        functools.partial(_mlstm_prefill_kernel, n_chunks=nc),
        grid_spec=grid_spec,
        out_shape=[jax.ShapeDtypeStruct((B * T, W_C), BF16),
                   jax.ShapeDtypeStruct((B, H_C, L, L), F32),
                   jax.ShapeDtypeStruct((B, H_C, 1, L), F32),
                   jax.ShapeDtypeStruct((B, H_C, 1, LANES), F32)],
        compiler_params=_cparams(("parallel", "parallel")),
        name="mlstm_prefill",
    )(gate_bias, z, z, z, z, ig, fg, ig.swapaxes(2, 3), fg.swapaxes(2, 3), out_g.reshape(1, W_C))


SELECT_PAGES = 16


def _moba_select_kernel(pt_ref, *refs, pages_per_block):
    k_refs = refs[:SELECT_PAGES]
    q_ref, sel_ref, gate_ref = refs[SELECT_PAGES:]
    s = pl.program_id(1)
    ppb = pages_per_block
    blocks_per_step = SELECT_PAGES // ppb
    qb = jnp.broadcast_to(q_ref[...], k_refs[0].shape)
    for i in range(blocks_per_step):
        acc = k_refs[i * ppb][...]
        for g in range(1, ppb):
            acc = acc + k_refs[i * ppb + g][...]
        prod = acc * qb
        part = prod[:, 0:8, :]
        for r in range(1, prod.shape[1] // 8):
            part = part + prod[:, 8 * r:8 * (r + 1), :]
        gate_ref[:, s * blocks_per_step + i] = part

    @pl.when(s == pl.num_programs(1) - 1)
    def _():
        H, nb, _, page = gate_ref.shape
        part_sum = jnp.sum(gate_ref[...], axis=2).reshape(H * nb, page)
        col = jnp.sum(part_sum, axis=-1, keepdims=True) * (1.0 / MOBA_BLOCK)
        eye = _eye(nb)
        head_row = lax.broadcasted_iota(jnp.int32, (H, 1), 0)
        gate = jnp.zeros((H, nb), F32)
        for h in range(H):
            gate = jnp.where(head_row == h, _col_to_row(col[h * nb:(h + 1) * nb, :], eye), gate)
        blk_id = lax.broadcasted_iota(jnp.int32, (1, nb), 1)
        rank = jnp.zeros(gate.shape, jnp.int32)
        for m in range(nb):
            gm = gate[:, m:m + 1]
            beats = jnp.logical_or(gm > gate, jnp.logical_and(gm == gate, m < blk_id))
            rank = rank + jnp.where(beats, 1, 0)
        out_col = lax.broadcasted_iota(jnp.int32, sel_ref.shape, 1)
        out = jnp.zeros(sel_ref.shape, jnp.int32)
        for r in range(MOBA_TOPK):
            idx = jnp.sum(jnp.where(rank == r, blk_id, 0), axis=1, keepdims=True)
            out = jnp.where(out_col == r, idx, out)
        sel_ref[...] = out


def _moba_select(cache_kt, page_table, q_col):
    Bs, n_pages = page_table.shape
    _, H, dh, page = cache_kt.shape
    ppb = MOBA_BLOCK // page
    nb = n_pages // ppb
    assert nb >= MOBA_TOPK and n_pages % SELECT_PAGES == 0 and SELECT_PAGES % ppb == 0

    def page_spec(i):
        return pl.BlockSpec((None, H, dh, page), lambda b, s, pt: (pt[b, s * SELECT_PAGES + i], 0, 0, 0))

    grid_spec = pltpu.PrefetchScalarGridSpec(
        num_scalar_prefetch=1,
        grid=(Bs, n_pages // SELECT_PAGES),
        in_specs=[page_spec(i) for i in range(SELECT_PAGES)]
        + [pl.BlockSpec((None, H, dh, 1), lambda b, s, pt: (b, 0, 0, 0))],
        out_specs=pl.BlockSpec((None, H, 8), lambda b, s, pt: (b, 0, 0)),
        scratch_shapes=[pltpu.VMEM((H, nb, 8, page), F32)],
    )
    return pl.pallas_call(
        functools.partial(_moba_select_kernel, pages_per_block=ppb),
        grid_spec=grid_spec,
        out_shape=jax.ShapeDtypeStruct((Bs, H, 8), jnp.int32),
        compiler_params=_cparams(("parallel", "arbitrary")),
        name="moba_select",
    )(page_table, *([cache_kt] * SELECT_PAGES), q_col)


def _moba_decode_kernel(pt_ref, sel_ref, q_ref, kn_ref, vn_ref, ck_ref, cv_ref, o_ref, kbuf, vbuf, sem,
                        *, pages_per_block, page):
    b = pl.program_id(0)
    nb_ = pl.num_programs(0)
    H = q_ref.shape[0]
    ppb = pages_per_block

    def copies(bb, slot):
        out = []
        for h in range(H):
            for r in range(MOBA_TOPK):
                blk = sel_ref[bb, h, r]
                for g in range(ppb):
                    pg = pt_ref[bb, blk * ppb + g]
                    dst = pl.ds((r * ppb + g) * page, page)
                    out.append(pltpu.make_async_copy(ck_ref.at[pg, h], kbuf.at[slot, h, :, dst], sem.at[slot, 0]))
                    out.append(pltpu.make_async_copy(cv_ref.at[pg, h], vbuf.at[slot, h, :, dst], sem.at[slot, 1]))
        return out

    slot = b % 2

    @pl.when(b == 0)
    def _():
        for c in copies(b, slot):
            c.start()

    @pl.when(b + 1 < nb_)
    def _():
        for c in copies(b + 1, 1 - slot):
            c.start()

    for c in copies(b, slot):
        c.wait()

    row = lax.broadcasted_iota(jnp.int32, (H, 1), 0)
    out = jnp.zeros(o_ref.shape, F32)
    q = q_ref[...]
    kn = kn_ref[...]
    vn = vn_ref[...]
    for h in range(H):
        qh = q[h:h + 1, :] * (DH_A ** -0.5)
        q8 = jnp.broadcast_to(qh, (8, qh.shape[1])).astype(BF16)
        kh = kbuf[slot, h].astype(BF16)
        vh = vbuf[slot, h].astype(BF16)
        s = jnp.dot(q8, kh, preferred_element_type=F32)[0:1, :]
        s_self = jnp.sum(qh * kn[h:h + 1, :], axis=-1, keepdims=True)
        m = jnp.maximum(jnp.max(s, axis=-1, keepdims=True), s_self)
        p = jnp.exp(s - m)
        p_self = jnp.exp(s_self - m)
        l = jnp.sum(p, axis=-1, keepdims=True) + p_self
        p8 = jnp.broadcast_to(p, (8, p.shape[1])).astype(BF16)
        pv = _nt(p8, vh)[0:1, :]
        oh = (pv + p_self * vn[h:h + 1, :]) / l
        out = jnp.where(row == h, oh, out)
    o_ref[...] = out


def _moba_decode(cache_kt, cache_vt, page_table, sel, q, k_new, v_new):
    Bs, n_pages = page_table.shape
    _, H, dh, page = cache_kt.shape
    ppb = MOBA_BLOCK // page
    rows = MOBA_TOPK * MOBA_BLOCK
    vec = pl.BlockSpec((None, H, dh), lambda b, pt, sl: (b, 0, 0))
    grid_spec = pltpu.PrefetchScalarGridSpec(
        num_scalar_prefetch=2,
        grid=(Bs,),
        in_specs=[vec, vec, vec, pl.BlockSpec(memory_space=pl.ANY), pl.BlockSpec(memory_space=pl.ANY)],
        out_specs=vec,
        scratch_shapes=[pltpu.VMEM((2, H, dh, rows), F32), pltpu.VMEM((2, H, dh, rows), F32),
                        pltpu.SemaphoreType.DMA((2, 2))],
    )
    return pl.pallas_call(
        functools.partial(_moba_decode_kernel, pages_per_block=ppb, page=page),
        grid_spec=grid_spec,
        out_shape=jax.ShapeDtypeStruct((Bs, H, dh), F32),
        compiler_params=_cparams(("arbitrary",)),
        name="moba_decode",
    )(page_table, sel, q, k_new, v_new, cache_kt, cache_vt)


def _ret_decode_kernel(q_ref, k_ref, v_ref, gb_ref, rg_ref, g_ref, s_ref, o_ref, so_ref):
    H, dk = q_ref.shape
    eye = _eye(dk)
    row = lax.broadcasted_iota(jnp.int32, (H, 1), 0)
    q = q_ref[...]
    k = k_ref[...]
    v = v_ref[...]
    g_all = g_ref[...]
    out = jnp.zeros(o_ref.shape, F32)
    for h in range(H):
        qh, kh, vh = q[h:h + 1, :], k[h:h + 1, :], v[h:h + 1, :]
        g = g_all[h:h + 1, 0:1]
        S = s_ref[h]
        att = jnp.sum(qh * kh, axis=-1, keepdims=True)
        cross = jnp.sum(_row_to_col(qh * g, eye) * S, axis=0, keepdims=True)
        out = jnp.where(row == h, att * vh + cross, out)
        so_ref[h] = S * g + _row_to_col(kh, eye) * vh
    gb = gb_ref[...]
    y = out * lax.rsqrt(jnp.mean(out * out, axis=-1, keepdims=True) + EPS) * rg_ref[...]
    o_ref[...] = y * (gb * _sigmoid(gb))


def _ret_decode(q, k, v, gb, ret_g, state):
    Bs, H, dk = q.shape
    log_g = jnp.log1p(-jnp.exp2(-5.0 - jnp.arange(H_B, dtype=F32)))
    g = jnp.broadcast_to(jnp.exp(1.0 * log_g)[:, None], (H, LANES))
    vec = pl.BlockSpec((None, H, dk), lambda b: (b, 0, 0))
    st = pl.BlockSpec((None, H, dk, DV_B), lambda b: (b, 0, 0, 0))
    return pl.pallas_call(
        _ret_decode_kernel,
        grid=(Bs,),
        in_specs=[vec, vec, vec, vec, pl.BlockSpec((H, dk), lambda b: (0, 0)),
                  pl.BlockSpec((H, LANES), lambda b: (0, 0)), st],
        out_specs=[vec, st],
        out_shape=[jax.ShapeDtypeStruct((Bs, H, DV_B), F32), jax.ShapeDtypeStruct(state.shape, F32)],
        compiler_params=_cparams(("parallel",)),
        name="retention_decode",
    )(q, k, v, gb, ret_g.reshape(H, DV_B), g, state)


def _mlstm_decode_kernel(q_ref, k_ref, v_ref, og_ref, ig_ref, fg_ref, bi_ref, bf_ref, ng_ref, c_ref, n_ref,
                         m_ref, h_ref, co_ref, no_ref, mo_ref):
    H, d = q_ref.shape
    eye = _eye(d)
    row = lax.broadcasted_iota(jnp.int32, (H, 1), 0)
    q = q_ref[...]
    k = k_ref[...] * (DH_C ** -0.5)
    v = v_ref[...]
    n0 = n_ref[...]
    ig = ig_ref[...] + bi_ref[...]
    b = _log_sigmoid(fg_ref[...] + bf_ref[...])
    m0 = m_ref[...]
    log_inter = b + m0
    m_t = jnp.maximum(log_inter, ig)
    w_intra = jnp.exp(ig - m_t)
    w_inter = jnp.exp(log_inter - m_t)
    s = jnp.sum(q * k, axis=-1, keepdims=True) * w_intra
    den = s + w_inter * jnp.sum(q * n0, axis=-1, keepdims=True)
    scale = 1.0 / jnp.maximum(jnp.abs(den), jnp.exp(-m_t))
    w_k = w_intra
    decay = w_inter
    hs = jnp.zeros(h_ref.shape, F32)
    for h in range(H):
        qh, kh, vh = q[h:h + 1, :], k[h:h + 1, :], v[h:h + 1, :]
        Cm = c_ref[h]
        cq = _col_to_row(jnp.sum(Cm * qh, axis=1, keepdims=True), eye)
        num = s[h:h + 1, :] * vh + cq * w_inter[h:h + 1, :]
        hs = jnp.where(row == h, num * scale[h:h + 1, :], hs)
        co_ref[h] = decay[h:h + 1, :] * Cm + _row_to_col(vh * w_k[h:h + 1, :], eye) * kh
    no_ref[...] = decay * n0 + k * w_k
    mo_ref[...] = m_t
    hn = hs * lax.rsqrt(jnp.mean(hs * hs, axis=-1, keepdims=True) + EPS) * ng_ref[...]
    h_ref[...] = hn * _sigmoid(og_ref[...])


def _mlstm_decode(q, k, v, og, ig, fg, gate_bias, out_g, C0, n0, m0):
    Bs, H, d = q.shape
    vec = pl.BlockSpec((None, H, d), lambda b: (b, 0, 0))
    sc = pl.BlockSpec((None, H, 1), lambda b: (b, 0, 0))
    st = pl.BlockSpec((None, H, d, d), lambda b: (b, 0, 0, 0))
    bias = pl.BlockSpec((H, 1), lambda b: (0, 0))
    return pl.pallas_call(
        _mlstm_decode_kernel,
        grid=(Bs,),
        in_specs=[vec, vec, vec, vec, sc, sc, bias, bias, pl.BlockSpec((H, d), lambda b: (0, 0)), st, vec, sc],
        out_specs=[vec, st, vec, sc],
        out_shape=[jax.ShapeDtypeStruct((Bs, H, d), F32), jax.ShapeDtypeStruct(C0.shape, F32),
                   jax.ShapeDtypeStruct((Bs, H, d), F32), jax.ShapeDtypeStruct((Bs, H, 1), F32)],
        compiler_params=_cparams(("parallel",)),
        name="mlstm_decode",
    )(q, k, v, og, ig, fg, gate_bias[0].reshape(H, 1), gate_bias[1].reshape(H, 1), out_g.reshape(H, d), C0, n0, m0)


def _rope_tables(pos):
    half = DH_A // 2
    inv = ROPE_THETA ** (-jnp.arange(half, dtype=F32) / half)
    ang = pos[:, None] * inv[None, :]
    cos = jnp.cos(ang)
    sin = jnp.sin(ang)
    reps = LANES // DH_A
    cos_t = jnp.tile(jnp.concatenate([cos, cos], axis=-1), (1, reps))
    sin_t = jnp.tile(jnp.concatenate([-sin, sin], axis=-1), (1, reps))
    return cos_t, sin_t


def _row_tile(m, pref):
    return pref if m % pref == 0 else m


def kernel(x_prompt, x_sample, cache_k, cache_v, page_table, state_ret, state_mlstm_C, state_mlstm_n,
           state_mlstm_m, state_ffn_conv, ab_norm_g, ab_w_in, ab_ret_norm_g, ab_w_out, c_norm_g, c_w_in,
           c_gate_bias, c_out_norm_g, c_w_out, ffn_norm_g, ffn_w1, ffn_w3, ffn_conv_w, ffn_conv_b, ffn_w2,
           final_norm_g):
    Bp, Tp, D = x_prompt.shape
    Bs, Ts, _ = x_sample.shape
    assert Ts == 1
    past_len = page_table.shape[1] * cache_k.shape[2]
    Mp = Bp * Tp
    xp = x_prompt.reshape(Mp, D)
    xs = x_sample.reshape(Bs, D)
    tm_p = _row_tile(Tp, 1024)
    tm_ffn = _row_tile(Tp, 1024)
    tm_s = Bs
    rope_p = _rope_tables(jnp.arange(Tp, dtype=jnp.int32).astype(F32))
    rope_s = _rope_tables(jnp.full((Bs,), past_len, jnp.int32).astype(F32))
    gw = 2 * H_C

    outs = {}

    w_in = ab_w_in[0].astype(BF16)
    w_out = ab_w_out[0].astype(BF16)
    w_out_a, w_out_b = w_out[:W_A], w_out[W_A:]

    z = _norm_matmul(xp, ab_norm_g[0], w_in, tm_p, IN_AB // 2)
    oa, kt, vt = _moba_prefill(z, rope_p, Bp, Tp)
    ob, s_pair = _ret_prefill(z, rope_p, ab_ret_norm_g[0], Bp, Tp)
    xp = _proj_residual([oa, ob], [w_out_a, w_out_b], xp, tm_p)
    outs["k_prompt"] = kt.reshape(Bp, H_A, DH_A, Tp).transpose(0, 3, 1, 2)[None]
    outs["v_prompt"] = vt.reshape(Bp, H_A, DH_A, Tp).transpose(0, 3, 1, 2)[None]
    hb = LANES // DK_B
    s_heads = jnp.stack([s_pair[:, :, i * DK_B:(i + 1) * DK_B, i * DV_B:(i + 1) * DV_B] for i in range(hb)], axis=2)
    outs["ret_prompt"] = s_heads.reshape(1, Bp, H_B, DK_B, DV_B)

    zs = _norm_matmul(xs, ab_norm_g[0], w_in, tm_s, AB_TILE, rope=rope_s)
    seg = lambda t: zs[:, t * AB_TILE:(t + 1) * AB_TILE].reshape(Bs, H_A, DH_A)
    qa_s, ka_s, va_s, qb_s, kb_s, vb_s, gb_s = (seg(t) for t in range(7))
    cache_kt = cache_k[0].transpose(0, 2, 3, 1)
    cache_vt = cache_v[0].transpose(0, 2, 3, 1)
    sel = _moba_select(cache_kt, page_table, qa_s[..., None])
    oa_s = _moba_decode(cache_kt, cache_vt, page_table, sel, qa_s, ka_s, va_s)
    ob_s, s_new = _ret_decode(qb_s, kb_s, vb_s, gb_s, ab_ret_norm_g[0], state_ret[0])
    xs = _proj_residual([oa_s.reshape(Bs, W_A).astype(BF16), ob_s.reshape(Bs, W_B).astype(BF16)],
                        [w_out_a, w_out_b], xs, tm_s)
    outs["k_sample"] = ka_s.reshape(1, Bs, 1, H_A, DH_A)
    outs["v_sample"] = va_s.reshape(1, Bs, 1, H_A, DH_A)
    outs["ret_sample"] = s_new[None]

    conv_p, conv_s = [], []

    def ffn_both(l, xp, xs, final_g):
        w1 = ffn_w1[l].astype(BF16)
        w3 = ffn_w3[l].astype(BF16)
        w2 = ffn_w2[l].astype(BF16)
        st = state_ffn_conv[l]
        xp, a_tail = _ffn(xp, ffn_norm_g[l], w1, w3, ffn_conv_w[l], ffn_conv_b[l], w2, final_g, tm_ffn, seq_len=Tp)
        xs, a_s = _ffn(xs, ffn_norm_g[l], w1, w3, ffn_conv_w[l], ffn_conv_b[l], w2, final_g, tm_s,
                       state=(st[:, 0, :], st[:, 1, :]))
        conv_p.append(a_tail[:, 8 - (CONV_W - 1):, :])
        conv_s.append(jnp.stack([st[:, 1, :], a_s], axis=1))
        return xp, xs

    xp, xs = ffn_both(0, xp, xs, None)

    w_in = c_w_in[0]
    w_main = w_in[:, :4 * W_C].astype(BF16)
    w_gate = jnp.pad(w_in[:, 4 * W_C:], ((0, 0), (0, LANES - gw))).astype(BF16)
    w_out = c_w_out[0].astype(BF16)
    L = MLSTM_CHUNK

    z, zg = _norm_matmul_side(xp, c_norm_g[0], w_main, w_gate, tm_p, 1024)
    gates = zg[:, :gw].reshape(Bp, Tp // L, L, 2, H_C).transpose(3, 0, 4, 1, 2)
    h, C_p, n_p, m_p = _mlstm_prefill(z, gates[0], gates[1], c_gate_bias[0], c_out_norm_g[0], Bp, Tp)
    xp = _proj_residual([h], [w_out], xp, tm_p)
    outs["C_prompt"] = C_p[None]
    outs["n_prompt"] = n_p.reshape(1, Bp, H_C, DH_C)
    outs["m_prompt"] = m_p[:, :, 0, 0][None]

    zs, zgs = _norm_matmul_side(xs, c_norm_g[0], w_main, w_gate, tm_s, 1024)
    segc = lambda t: zs[:, t * W_C:(t + 1) * W_C].reshape(Bs, H_C, DH_C)
    ig_s = zgs[:, :H_C][:, :, None]
    fg_s = zgs[:, H_C:gw][:, :, None]
    h_s, C_s, n_s, m_s = _mlstm_decode(segc(0), segc(1), segc(2), segc(3), ig_s, fg_s, c_gate_bias[0], c_out_norm_g[0],
                                       state_mlstm_C[0], state_mlstm_n[0], state_mlstm_m[0][:, :, None])
    xs = _proj_residual([h_s.reshape(Bs, W_C).astype(BF16)], [w_out], xs, tm_s)
    outs["C_sample"] = C_s[None]
    outs["n_sample"] = n_s[None]
    outs["m_sample"] = m_s[:, :, 0][None]

    xp, xs = ffn_both(1, xp, xs, final_norm_g)

    return (xp.reshape(Bp, Tp, D), xs.reshape(Bs, 1, D),
            outs["k_prompt"], outs["v_prompt"], outs["k_sample"], outs["v_sample"],
            outs["ret_prompt"], outs["ret_sample"], outs["C_prompt"], outs["C_sample"],
            outs["n_prompt"], outs["n_sample"], outs["m_prompt"], outs["m_sample"],
            jnp.stack(conv_p), jnp.stack(conv_s))
```

```python
import functools

import jax
import jax.numpy as jnp
from jax import lax
from jax.experimental import pallas as pl
from jax.experimental.pallas import tpu as pltpu

F32 = jnp.float32
BF16 = jnp.bfloat16
HIGHEST = lax.Precision.HIGHEST

LANES = 128
D_MODEL = 1024
H_A = 8
DH_A = 64
MOBA_BLOCK = 256
MOBA_TOPK = 3
H_B = 8
DK_B = 64
DV_B = 64
RET_CHUNK = 128
H_C = 8
DH_C = D_MODEL // H_C
MLSTM_CHUNK = 128
D_FF = 11 * D_MODEL // 4
CONV_W = 3
ROPE_THETA = 10000.0
EPS = 1e-6
W_A = H_A * DH_A
W_B = H_B * DV_B
W_C = H_C * DH_C
IN_AB = 3 * W_A + 2 * H_B * DK_B + 2 * W_B
AB_TILE = 512
AB_ROPE_TILES = (0, 1, 3, 4)
AB_KB_TILE = 4
VMEM_LIMIT = 56 * 1024 * 1024

NEG_INF = float("-inf")
LOG2_E = 1.4426950408889634


def _cparams(sem):
    return pltpu.CompilerParams(dimension_semantics=sem, vmem_limit_bytes=VMEM_LIMIT)


def _nt(a, b, **kw):
    return lax.dot_general(a, b, (((1,), (1,)), ((), ())), preferred_element_type=F32, **kw)


def _tn(a, b, **kw):
    return lax.dot_general(a, b, (((0,), (0,)), ((), ())), preferred_element_type=F32, **kw)


def _rms_rows(x, g):
    ms = jnp.mean(x * x, axis=-1, keepdims=True)
    return x * lax.rsqrt(ms + EPS) * g


def _eye(n):
    return lax.broadcasted_iota(jnp.int32, (n, n), 0) == lax.broadcasted_iota(jnp.int32, (n, n), 1)


def _row_to_col(row, eye):
    return jnp.sum(jnp.where(eye, row, 0.0), axis=1, keepdims=True)


def _col_to_row(col, eye):
    return jnp.sum(jnp.where(eye, col, 0.0), axis=0, keepdims=True)


def _rowsum_rep(x, ones=None):
    if ones is None:
        ones = jnp.ones((x.shape[1], LANES), BF16)
    hi = x.astype(BF16)
    lo = (x - hi.astype(F32)).astype(BF16)
    return (jnp.dot(hi, ones, preferred_element_type=F32) + jnp.dot(lo, ones, preferred_element_type=F32))


def _log_sigmoid(x):
    return jnp.minimum(x, 0.0) - jnp.log1p(jnp.exp(-jnp.abs(x)))


def _sigmoid(x):
    return 1.0 / (1.0 + jnp.exp(-x))


def _gelu_tanh(x):
    c = 0.7978845608028654
    return (0.5 * x) * (1.0 + jnp.tanh(x * (c + (0.044715 * c) * (x * x))))


def _norm_matmul_kernel(x_ref, g_ref, w_ref, cos_ref, sin_ref, o_ref, xn_ref, *, rope_tiles, scale_tile, scale):
    j = pl.program_id(1)

    @pl.when(j == 0)
    def _():
        xn_ref[...] = _rms_rows(x_ref[...], g_ref[...]).astype(BF16)

    z = jnp.dot(xn_ref[...], w_ref[...], preferred_element_type=F32)
    if not rope_tiles:
        o_ref[...] = z
        return

    is_rope = functools.reduce(jnp.logical_or, [j == t for t in rope_tiles])

    @pl.when(is_rope)
    def _():
        sc = jnp.where(j == scale_tile, scale, 1.0).astype(F32)
        cos = cos_ref[...]
        sin = sin_ref[...]
        for c in range(z.shape[1] // LANES):
            cols = slice(c * LANES, (c + 1) * LANES)
            o_ref[:, cols] = _rope_lanes(z[:, cols], cos, sin) * sc

    @pl.when(jnp.logical_not(is_rope))
    def _():
        o_ref[...] = z


def _norm_matmul_side_kernel(x_ref, g_ref, w_ref, ws_ref, o_ref, os_ref, xn_ref):
    @pl.when(pl.program_id(1) == 0)
    def _():
        xn_ref[...] = _rms_rows(x_ref[...], g_ref[...]).astype(BF16)
        os_ref[...] = jnp.dot(xn_ref[...], ws_ref[...], preferred_element_type=F32)

    o_ref[...] = jnp.dot(xn_ref[...], w_ref[...], preferred_element_type=F32)


def _norm_matmul_side(x, g, w, n_cols, w_side, tm, tn):
    M, D = x.shape
    N = n_cols
    Ns = w_side.shape[1]
    assert M % tm == 0 and N % tn == 0
    return pl.pallas_call(
        _norm_matmul_side_kernel,
        grid=(M // tm, N // tn),
        in_specs=[pl.BlockSpec((tm, D), lambda i, j: (i, 0)),
                  pl.BlockSpec((1, D), lambda i, j: (0, 0)),
                  pl.BlockSpec((D, tn), lambda i, j: (0, j)),
                  pl.BlockSpec((D, Ns), lambda i, j: (0, 0))],
        out_specs=[pl.BlockSpec((tm, tn), lambda i, j: (i, j)),
                   pl.BlockSpec((tm, Ns), lambda i, j: (i, 0))],
        out_shape=[jax.ShapeDtypeStruct((M, N), F32), jax.ShapeDtypeStruct((M, Ns), F32)],
        scratch_shapes=[pltpu.VMEM((tm, D), BF16)],
        compiler_params=_cparams(("parallel", "arbitrary")),
        name="norm_matmul_side",
    )(x, g.reshape(1, D), w, w_side)


def _norm_matmul(x, g, w, tm, tn, rope=None):
    M, D = x.shape
    N = w.shape[1]
    assert M % tm == 0 and N % tn == 0
    if rope is None:
        cos = sin = jnp.zeros((8, LANES), F32)
        tab_spec = pl.BlockSpec((8, LANES), lambda i, j: (0, 0))
        kern = functools.partial(_norm_matmul_kernel, rope_tiles=(), scale_tile=-1, scale=1.0)
    else:
        cos, sin = rope
        nt = cos.shape[0] // tm
        tab_spec = pl.BlockSpec((tm, LANES), lambda i, j: (i % nt, 0))
        kern = functools.partial(_norm_matmul_kernel, rope_tiles=AB_ROPE_TILES, scale_tile=AB_KB_TILE,
                                 scale=DK_B ** -0.5)
    return pl.pallas_call(
        kern,
        grid=(M // tm, N // tn),
        in_specs=[pl.BlockSpec((tm, D), lambda i, j: (i, 0)),
                  pl.BlockSpec((1, D), lambda i, j: (0, 0)),
                  pl.BlockSpec((D, tn), lambda i, j: (0, j)),
                  tab_spec, tab_spec],
        out_specs=pl.BlockSpec((tm, tn), lambda i, j: (i, j)),
        out_shape=jax.ShapeDtypeStruct((M, N), F32),
        scratch_shapes=[pltpu.VMEM((tm, D), BF16)],
        compiler_params=_cparams(("parallel", "arbitrary")),
        name="norm_matmul",
    )(x, g.reshape(1, D), w, cos, sin)


def _proj_residual_kernel(*refs, n_in):
    a_refs = refs[:n_in]
    w_refs = refs[n_in:2 * n_in]
    res_ref = refs[2 * n_in]
    o_ref = refs[2 * n_in + 1]
    y = res_ref[...]
    acc = None
    for a_ref, w_ref in zip(a_refs, w_refs):
        d = jnp.dot(a_ref[...], w_ref[...], preferred_element_type=F32)
        acc = d if acc is None else acc + d
    o_ref[...] = y + acc


def _proj_residual(acts, ws, res, tm):
    M, D = res.shape
    n_in = len(acts)
    in_specs = ([pl.BlockSpec((tm, a.shape[1]), lambda i: (i, 0)) for a in acts]
                + [pl.BlockSpec(w.shape, lambda i: (0, 0)) for w in ws]
                + [pl.BlockSpec((tm, D), lambda i: (i, 0))])
    return pl.pallas_call(
        functools.partial(_proj_residual_kernel, n_in=n_in),
        grid=(M // tm,),
        in_specs=in_specs,
        out_specs=pl.BlockSpec((tm, D), lambda i: (i, 0)),
        out_shape=jax.ShapeDtypeStruct((M, D), F32),
        compiler_params=_cparams(("parallel",)),
        name="proj_residual",
    )(*acts, *ws, res)


PREV_ROWS = 16
FFN_CHUNK = 256


def _ffn_kernel(*refs, seq_mode, tiles_per_seq, final_norm):
    if seq_mode:
        (x_ref, xp_ref, g_ref, w1_ref, w3_ref, cw_ref, cb_ref, w2_ref, fg_ref, o_ref, a_ref, y_ref) = refs
    else:
        (x_ref, s0_ref, s1_ref, g_ref, w1_ref, w3_ref, cw_ref, cb_ref, w2_ref, fg_ref, o_ref, a_ref, y_ref) = refs
    i = pl.program_id(0)
    x = x_ref[...]
    tm = x.shape[0]
    F = w1_ref.shape[1]
    xn = _rms_rows(x, g_ref[...]).astype(BF16)
    if seq_mode:
        xpn = _rms_rows(xp_ref[...], g_ref[...]).astype(BF16)
        has_prev = ((i % tiles_per_seq) != 0).astype(F32)
        row8 = lax.broadcasted_iota(jnp.int32, (8, 1), 0)
    for c0 in range(0, F, FFN_CHUNK):
        cols = slice(c0, min(c0 + FFN_CHUNK, F))
        w1c = w1_ref[:, cols]
        a = jnp.dot(xn, w1c, preferred_element_type=F32)
        gate = jnp.dot(xn, w3_ref[:, cols], preferred_element_type=F32)
        a_ref[:, cols] = a[tm - a_ref.shape[0]:, :]
        if seq_mode:
            ap = jnp.dot(xpn, w1c, preferred_element_type=F32)
            p1 = ap[PREV_ROWS - 1:PREV_ROWS, :] * has_prev
            p2 = ap[PREV_ROWS - 2:PREV_ROWS - 1, :] * has_prev
            a1 = pltpu.roll(a, 1, 0)
            a2 = pltpu.roll(a, 2, 0)
            top1 = jnp.where(row8 == 0, p1, a1[0:8, :])
            top2 = jnp.where(row8 == 0, p2, jnp.where(row8 == 1, p1, a2[0:8, :]))
            a1 = jnp.concatenate([top1, a1[8:, :]], axis=0)
            a2 = jnp.concatenate([top2, a2[8:, :]], axis=0)
        else:
            a1 = s1_ref[:, cols]
            a2 = s0_ref[:, cols]
        ac = cb_ref[:, cols] + a2 * cw_ref[0:1, cols]
        ac = ac + a1 * cw_ref[1:2, cols]
        ac = ac + a * cw_ref[2:3, cols]
        y_ref[:, cols] = (_gelu_tanh(ac) * gate).astype(BF16)
    acc = x + jnp.dot(y_ref[...], w2_ref[...], preferred_element_type=F32)
    if final_norm:
        acc = _rms_rows(acc, fg_ref[...])
    o_ref[...] = acc


def _ffn(x, norm_g, layer, w1, w3, conv_w, conv_b, w2, final_g, tm, seq_len=None, state=None):
    M, D = x.shape
    F = w1.shape[2]
    seq_mode = state is None
    final_norm = final_g is not None
    fg = (final_g if final_norm else jnp.ones((D,), F32)).reshape(1, D)
    whole = lambda shape: pl.BlockSpec(shape, lambda i: (0, 0), pipeline_mode=pl.Buffered(1))
    stacked = lambda shape: pl.BlockSpec((None,) + shape, lambda i: (layer, 0, 0), pipeline_mode=pl.Buffered(1))
    common = [whole((1, D)), stacked((D, F)), stacked((D, F)), whole((CONV_W, F)), whole((1, F)), stacked((F, D)),
              whole((1, D))]
    common_args = (norm_g.reshape(1, D), w1, w3, conv_w, conv_b.reshape(1, F), w2, fg)
    x_spec = pl.BlockSpec((tm, D), lambda i: (i, 0))
    if seq_mode:
        assert seq_len % tm == 0 and tm % PREV_ROWS == 0
        r = tm // PREV_ROWS
        in_specs = [x_spec, pl.BlockSpec((PREV_ROWS, D), lambda i: (jnp.maximum(i * r - 1, 0), 0))] + common
        args = (x, x) + common_args
        tiles_per_seq = seq_len // tm
    else:
        s_spec = pl.BlockSpec((tm, F), lambda i: (i, 0))
        in_specs = [x_spec, s_spec, s_spec] + common
        args = (x, state[0], state[1]) + common_args
        tiles_per_seq = 1
    if seq_mode:
        a_spec = pl.BlockSpec((None, 8, F), lambda i: (i // tiles_per_seq, 0, 0))
        a_shape = jax.ShapeDtypeStruct((M // seq_len, 8, F), F32)
    else:
        a_spec = pl.BlockSpec((tm, F), lambda i: (i, 0))
        a_shape = jax.ShapeDtypeStruct((M, F), F32)
    return pl.pallas_call(
        functools.partial(_ffn_kernel, seq_mode=seq_mode, tiles_per_seq=tiles_per_seq, final_norm=final_norm),
        grid=(M // tm,),
        in_specs=in_specs,
        out_specs=[pl.BlockSpec((tm, D), lambda i: (i, 0)), a_spec],
        out_shape=[jax.ShapeDtypeStruct((M, D), F32), a_shape],
        scratch_shapes=[pltpu.VMEM((tm, F), BF16)],
        compiler_params=_cparams(("arbitrary",)),
        name="conv_ffn",
    )(*args)


def _rope_lanes(x, cos, sin):
    lane = lax.broadcasted_iota(jnp.int32, (1, LANES), 1)
    first_half = (lane % DH_A) < (DH_A // 2)
    partner = jnp.where(first_half, pltpu.roll(x, LANES - DH_A // 2, 1), pltpu.roll(x, DH_A // 2, 1))
    return x * cos + partner * sin


def _moba_prefill_kernel(q_ref, k_ref, v_ref, cos_ref, sin_ref, o_ref, kt_ref, vt_ref, kb_ref, vb_ref, *, nb):
    blk = MOBA_BLOCK
    heads = LANES // DH_A
    T = q_ref.shape[0]
    cos = cos_ref[...]
    sin = sin_ref[...]
    k = _rope_lanes(k_ref[...], cos, sin)
    kt_ref[...] = k.T
    vt_ref[...] = v_ref[...].T
    kb_ref[...] = k.astype(BF16)
    vb_ref[:, 0:LANES] = v_ref[...].astype(BF16)
    vb_ref[:, LANES:2 * LANES] = jnp.ones((T, LANES), BF16)
    kmean = jnp.concatenate(
        [jnp.sum(k[n * blk:(n + 1) * blk, :], axis=0, keepdims=True) * (1.0 / blk) for n in range(nb)]
        + [jnp.zeros((8 - nb, LANES), F32)] * (nb < 8), axis=0)

    q2 = _rope_lanes(q_ref[...], cos, sin)
    lane = lax.broadcasted_iota(jnp.int32, (1, LANES), 1)
    blk_id = lax.broadcasted_iota(jnp.int32, (8, 1), 0)
    q_blk = lax.broadcasted_iota(jnp.int32, (1, T), 1) // blk
    past = blk_id < q_blk
    causal = (lax.broadcasted_iota(jnp.int32, (blk, blk), 1) <= lax.broadcasted_iota(jnp.int32, (blk, blk), 0))
    hms, qss, sels = [], [], []
    for h in range(heads):
        hm = (lane // DH_A) == h
        qh = jnp.where(hm, q2, 0.0)
        gt = _nt(kmean, qh, precision=HIGHEST)
        sel_t = jnp.zeros_like(gt)
        for n in range(nb - 1):
            gn = gt[n:n + 1, :]
            beats = jnp.logical_and(past, jnp.logical_or(gt > gn, jnp.logical_and(gt == gn, blk_id < n)))
            rank = jnp.sum(beats.astype(F32), axis=0, keepdims=True)
            sel_n = jnp.where(jnp.logical_and(rank < MOBA_TOPK, n < q_blk), 1.0, 0.0)
            sel_t = jnp.where(blk_id == n, sel_n, sel_t)
        hms.append(hm)
        qss.append((qh * (DH_A ** -0.5 * LOG2_E)).astype(BF16))
        sels.append(jnp.where(sel_t.T > 0.5, 0.0, NEG_INF))

    for qi in range(nb):
        rows = slice(qi * blk, (qi + 1) * blk)
        outs = []
        for h in range(heads):
            qs = qss[h][rows, :]
            pieces = []
            for n in range(qi + 1):
                s = _nt(qs, kb_ref[n * blk:(n + 1) * blk, :])
                pieces.append(jnp.where(causal, s, NEG_INF) if n == qi else s + sels[h][rows, n:n + 1])
            m = functools.reduce(jnp.maximum, pieces)
            m = jnp.max(m, axis=-1, keepdims=True)
            p_all = jnp.concatenate([jnp.exp2((s - m).astype(BF16)) for s in pieces], axis=1)
            acc = jnp.dot(p_all, vb_ref[0:(qi + 1) * blk, :], preferred_element_type=F32)
            outs.append(acc[:, 0:LANES] / acc[:, LANES:2 * LANES])
        out = outs[0]
        for h in range(1, heads):
            out = jnp.where(hms[h], outs[h], out)
        o_ref[rows, :] = out.astype(o_ref.dtype)


def _moba_prefill(z, rope, B, T):
    blk = MOBA_BLOCK
    assert T % blk == 0
    nb = T // blk
    assert nb <= 8
    cpt = AB_TILE // LANES
    return pl.pallas_call(
        functools.partial(_moba_prefill_kernel, nb=nb),
        grid=(B, W_A // LANES),
        in_specs=[pl.BlockSpec((T, LANES), lambda b, p: (b, p)),
                  pl.BlockSpec((T, LANES), lambda b, p: (b, cpt + p)),
                  pl.BlockSpec((T, LANES), lambda b, p: (b, 2 * cpt + p)),
                  pl.BlockSpec((T, LANES), lambda b, p: (0, 0)),
                  pl.BlockSpec((T, LANES), lambda b, p: (0, 0))],
        out_specs=[pl.BlockSpec((T, LANES), lambda b, p: (b, p)),
                   pl.BlockSpec((None, LANES, T), lambda b, p: (b, p, 0)),
                   pl.BlockSpec((None, LANES, T), lambda b, p: (b, p, 0))],
        out_shape=[jax.ShapeDtypeStruct((B * T, W_A), BF16),
                   jax.ShapeDtypeStruct((B, W_A, T), F32),
                   jax.ShapeDtypeStruct((B, W_A, T), F32)],
        scratch_shapes=[pltpu.VMEM((T, LANES), BF16), pltpu.VMEM((T, 2 * LANES), BF16)],
        compiler_params=_cparams(("parallel", "parallel")),
        name="moba_prefill",
    )(z, z, z, rope[0], rope[1])


def _ret_prefill_kernel(q_ref, k_ref, v_ref, gb_ref, cos_ref, sin_ref, rg_ref, dmask_ref, din_ref, dout_ref,
                        gch_ref, o_ref, s_ref, *, n_chunks):
    C = RET_CHUNK
    heads = LANES // DK_B
    lane = lax.broadcasted_iota(jnp.int32, (1, LANES), 1)
    hms = [(lane // DK_B) == h for h in range(heads)]
    row_h = lax.broadcasted_iota(jnp.int32, (LANES, LANES), 0) // DK_B
    col_h = lax.broadcasted_iota(jnp.int32, (LANES, LANES), 1) // DV_B
    same_head = row_h == col_h
    seg_ones = jnp.where(same_head, 1.0, 0.0).astype(BF16)
    din = din_ref[...]
    dout = dout_ref[...]
    gch = gch_ref[...]
    rg = rg_ref[...]
    dmask = dmask_ref[...]

    S = jnp.zeros((LANES, LANES), F32)
    for j in range(n_chunks):
        rows = slice(j * C, (j + 1) * C)
        cos = cos_ref[rows, :]
        sin = sin_ref[rows, :]
        q = _rope_lanes(q_ref[rows, :], cos, sin)
        k = _rope_lanes(k_ref[rows, :], cos, sin) * (DK_B ** -0.5)
        kb = k.astype(BF16)
        vb = v_ref[rows, :].astype(BF16)
        o = jnp.dot((q * din).astype(BF16), S.astype(BF16), preferred_element_type=F32)
        q_st = jnp.concatenate([jnp.where(hm, q, 0.0) for hm in hms], axis=0).astype(BF16)
        att = _nt(q_st, kb) * dmask
        res = jnp.dot(att.astype(BF16), vb, preferred_element_type=F32)
        intra = res[0:C, :]
        for h in range(1, heads):
            intra = jnp.where(hms[h], res[h * C:(h + 1) * C, :], intra)
        o = o + intra
        S = S * gch + jnp.where(same_head, _tn((k * dout).astype(BF16), vb), 0.0)
        ms = _rowsum_rep(o * o, seg_ones) * (1.0 / DV_B)
        g = gb_ref[rows, :]
        y = o * lax.rsqrt(ms + EPS) * rg * (g * _sigmoid(g))
        o_ref[rows, :] = y.astype(o_ref.dtype)
    s_ref[...] = S


def _ret_tables(chunk):
    log_g = jnp.log1p(-jnp.exp2(-5.0 - jnp.arange(H_B, dtype=F32)))
    i = jnp.arange(chunk, dtype=F32)
    d_in = jnp.exp((i[:, None] + 1.0) * log_g)
    d_out = jnp.exp((chunk - 1.0 - i)[:, None] * log_g)
    diff = i[:, None] - i[None, :]
    d_mask = jnp.where(diff >= 0, jnp.exp(jnp.maximum(diff, 0.0)[None] * log_g[:, None, None]), 0.0)
    g_chunk = jnp.exp(chunk * log_g)
    return d_in, d_out, d_mask, g_chunk


def _ret_prefill(z, rope, ret_g, B, T):
    C = RET_CHUNK
    assert T % C == 0
    d_in, d_out, d_mask, g_chunk = _ret_tables(C)
    npair = W_B // LANES
    lanes = lambda t: jnp.repeat(t, DK_B, axis=-1)
    din_l = lanes(d_in).reshape(C, npair, LANES).transpose(1, 0, 2)
    dout_l = lanes(d_out).reshape(C, npair, LANES).transpose(1, 0, 2)
    gch_l = lanes(g_chunk).reshape(npair, 1, LANES)
    cpt = AB_TILE // LANES
    col = lambda t: (lambda b, p: (b, t * cpt + p))
    return pl.pallas_call(
        functools.partial(_ret_prefill_kernel, n_chunks=T // C),
        grid=(B, npair),
        in_specs=[pl.BlockSpec((T, LANES), col(3)), pl.BlockSpec((T, LANES), col(4)),
                  pl.BlockSpec((T, LANES), col(5)), pl.BlockSpec((T, LANES), col(6)),
                  pl.BlockSpec((T, LANES), lambda b, p: (0, 0)),
                  pl.BlockSpec((T, LANES), lambda b, p: (0, 0)),
                  pl.BlockSpec((1, LANES), lambda b, p: (0, p)),
                  pl.BlockSpec((None, (LANES // DK_B) * C, C), lambda b, p: (p, 0, 0)),
                  pl.BlockSpec((None, C, LANES), lambda b, p: (p, 0, 0)),
                  pl.BlockSpec((None, C, LANES), lambda b, p: (p, 0, 0)),
                  pl.BlockSpec((None, 1, LANES), lambda b, p: (p, 0, 0))],
        out_specs=[pl.BlockSpec((T, LANES), lambda b, p: (b, p)),
                   pl.BlockSpec((None, None, LANES, LANES), lambda b, p: (b, p, 0, 0))],
        out_shape=[jax.ShapeDtypeStruct((B * T, W_B), BF16),
                   jax.ShapeDtypeStruct((B, npair, LANES, LANES), F32)],
        compiler_params=_cparams(("parallel", "parallel")),
        name="retention_prefill",
    )(z, z, z, z, rope[0], rope[1], ret_g.reshape(1, W_B), d_mask.reshape(npair, -1, C), din_l, dout_l, gch_l)


def _mlstm_prefill_kernel(bias_ref, q_ref, k_ref, v_ref, og_ref, ig_ref, fg_ref, igc_ref, fgc_ref, ng_ref,
                          h_ref, c_ref, n_ref, m_ref, *, n_chunks):
    L = MLSTM_CHUNK
    hd = pl.program_id(1)
    r_id = lax.broadcasted_iota(jnp.int32, (L, L), 0)
    c_id = lax.broadcasted_iota(jnp.int32, (L, L), 1)
    causal = c_id <= r_id
    upper = jnp.where(r_id <= c_id, 1.0, 0.0).astype(F32)
    lower = jnp.where(c_id <= r_id, 1.0, 0.0).astype(F32)
    ones_b = jnp.ones((L, L), BF16)
    i_rows = ig_ref[...] + bias_ref[0, hd]
    b_rows = jnp.dot(_log_sigmoid(fg_ref[...] + bias_ref[1, hd]), upper, preferred_element_type=F32, precision=HIGHEST)
    i_cols = igc_ref[...] + bias_ref[0, hd]
    b_cols = jnp.dot(lower, _log_sigmoid(fgc_ref[...] + bias_ref[1, hd]), preferred_element_type=F32, precision=HIGHEST)
    ng = ng_ref[...]

    Cm = jnp.zeros((L, L), F32)
    n = jnp.zeros((1, L), F32)
    m = jnp.zeros((1, 1), F32)
    for j in range(n_chunks):
        rows = slice(j * L, (j + 1) * L)
        q = q_ref[rows, :]
        k = k_ref[rows, :] * (DH_C ** -0.5)
        v = v_ref[rows, :]
        b_row, i_row = b_rows[j:j + 1, :], i_rows[j:j + 1, :]
        b_col, i_col = b_cols[:, j:j + 1], i_cols[:, j:j + 1]
        b_last = b_row[:, L - 1:L]
        qb = q.astype(BF16)
        kb = k.astype(BF16)
        b_rep = jnp.broadcast_to(b_col, (L, L))
        log_d = jnp.where(causal, (b_rep - b_row) + i_row, NEG_INF)
        m_row = jnp.broadcast_to(jnp.max(log_d, axis=-1, keepdims=True), (L, L))
        s = _nt(qb, kb) * jnp.exp(log_d - m_row)
        sv_ext = jnp.dot(s.astype(BF16), jnp.concatenate([v.astype(BF16), ones_b], axis=1),
                         preferred_element_type=F32)
        sv, s_sum = sv_ext[:, 0:L], sv_ext[:, L:2 * L]
        log_w = (b_last - b_col) + i_col
        m_loc = jnp.max(log_w, axis=0, keepdims=True)
        e_k = jnp.broadcast_to(jnp.exp(log_w - m_loc), (L, L))
        U = _tn((v * e_k).astype(BF16), kb)
        nk = jnp.sum(k * e_k, axis=0, keepdims=True)
        log_inter = b_rep + m
        m_t = jnp.maximum(log_inter, m_row)
        f_intra = jnp.exp(m_row - m_t)
        w_inter = jnp.exp(log_inter - m_t)
        num = sv * f_intra + _nt(qb, Cm.astype(BF16)) * w_inter
        den = s_sum * f_intra + w_inter * _rowsum_rep(q * n)
        h = num / jnp.maximum(jnp.abs(den), jnp.exp(-m_t))
        m_new = jnp.maximum(b_last + m, m_loc)
        decay = jnp.exp(b_last + m - m_new)
        f_k = jnp.exp(m_loc - m_new)
        Cm = decay * Cm + f_k * U
        n = decay * n + f_k * nk
        m = m_new
        hn = h * lax.rsqrt(_rowsum_rep(h * h) * (1.0 / L) + EPS) * ng
        h_ref[rows, :] = (hn * _sigmoid(og_ref[rows, :])).astype(h_ref.dtype)
    c_ref[...] = Cm
    n_ref[...] = n
    m_ref[...] = jnp.broadcast_to(m, m_ref.shape)


def _mlstm_prefill(z, ig, fg, gate_bias, out_g, B, T):
    L = MLSTM_CHUNK
    assert T % L == 0 and DH_C == LANES
    nc = T // L
    col = lambda t: (lambda b, h, bias: (b, t * H_C + h))
    gspec = pl.BlockSpec((None, None, nc, L), lambda b, h, bias: (b, h, 0, 0))
    gcspec = pl.BlockSpec((None, None, L, nc), lambda b, h, bias: (b, h, 0, 0))
    grid_spec = pltpu.PrefetchScalarGridSpec(
        num_scalar_prefetch=1,
        grid=(B, H_C),
        in_specs=[pl.BlockSpec((T, LANES), col(0)), pl.BlockSpec((T, LANES), col(1)),
                  pl.BlockSpec((T, LANES), col(2)), pl.BlockSpec((T, LANES), col(3)),
                  gspec, gspec, gcspec, gcspec,
                  pl.BlockSpec((1, LANES), lambda b, h, bias: (0, h))],
        out_specs=[pl.BlockSpec((T, LANES), lambda b, h, bias: (b, h)),
                   pl.BlockSpec((None, None, L, L), lambda b, h, bias: (b, h, 0, 0)),
                   pl.BlockSpec((None, None, 1, L), lambda b, h, bias: (b, h, 0, 0)),
                   pl.BlockSpec((None, None, 1, LANES), lambda b, h, bias: (b, h, 0, 0))],
    )
    return pl.pallas_call(
        functools.partial(_mlstm_prefill_kernel, n_chunks=nc),
        grid_spec=grid_spec,
        out_shape=[jax.ShapeDtypeStruct((B * T, W_C), BF16),
                   jax.ShapeDtypeStruct((B, H_C, L, L), F32),
                   jax.ShapeDtypeStruct((B, H_C, 1, L), F32),
                   jax.ShapeDtypeStruct((B, H_C, 1, LANES), F32)],
        compiler_params=_cparams(("parallel", "parallel")),
        name="mlstm_prefill",
    )(gate_bias, z, z, z, z, ig, fg, ig.swapaxes(2, 3), fg.swapaxes(2, 3), out_g.reshape(1, W_C))


SELECT_PAGES = 16


def _moba_select_kernel(pt_ref, *refs, pages_per_block):
    k_refs = refs[:SELECT_PAGES]
    q_ref, sel_ref, gate_ref, qb_ref = refs[SELECT_PAGES:]
    s = pl.program_id(1)
    ppb = pages_per_block
    blocks_per_step = SELECT_PAGES // ppb
    H, dh, page = k_refs[0].shape

    @pl.when(s == 0)
    def _():
        eye = _eye(dh)
        q = q_ref[...]
        for h in range(H):
            qb_ref[h] = jnp.broadcast_to(_row_to_col(q[h:h + 1, :], eye), (dh, page))

    qb = qb_ref[...]
    for i in range(blocks_per_step):
        acc = k_refs[i * ppb][...]
        for g in range(1, ppb):
            acc = acc + k_refs[i * ppb + g][...]
        prod = acc * qb
        part = prod[:, 0:8, :]
        for r in range(1, prod.shape[1] // 8):
            part = part + prod[:, 8 * r:8 * (r + 1), :]
        gate_ref[:, s * blocks_per_step + i] = part

    @pl.when(s == pl.num_programs(1) - 1)
    def _():
        nb = gate_ref.shape[1]
        part_sum = jnp.sum(gate_ref[...], axis=2).reshape(H * nb, page)
        col = jnp.sum(part_sum, axis=-1, keepdims=True) * (1.0 / MOBA_BLOCK)
        eye = _eye(nb)
        head_row = lax.broadcasted_iota(jnp.int32, (H, 1), 0)
        gate = jnp.zeros((H, nb), F32)
        for h in range(H):
            gate = jnp.where(head_row == h, _col_to_row(col[h * nb:(h + 1) * nb, :], eye), gate)
        blk_id = lax.broadcasted_iota(jnp.int32, (1, nb), 1)
        rank = jnp.zeros(gate.shape, jnp.int32)
        for m in range(nb):
            gm = gate[:, m:m + 1]
            beats = jnp.logical_or(gm > gate, jnp.logical_and(gm == gate, m < blk_id))
            rank = rank + jnp.where(beats, 1, 0)
        out_col = lax.broadcasted_iota(jnp.int32, sel_ref.shape, 1)
        out = jnp.zeros(sel_ref.shape, jnp.int32)
        for r in range(MOBA_TOPK):
            idx = jnp.sum(jnp.where(rank == r, blk_id, 0), axis=1, keepdims=True)
            out = jnp.where(out_col == r, idx, out)
        sel_ref[...] = out


def _moba_select(cache_kt, page_table, q):
    Bs, n_pages = page_table.shape
    _, H, dh, page = cache_kt.shape
    ppb = MOBA_BLOCK // page
    nb = n_pages // ppb
    assert nb >= MOBA_TOPK and n_pages % SELECT_PAGES == 0 and SELECT_PAGES % ppb == 0

    def page_spec(i):
        return pl.BlockSpec((None, H, dh, page), lambda b, s, pt: (pt[b, s * SELECT_PAGES + i], 0, 0, 0))

    grid_spec = pltpu.PrefetchScalarGridSpec(
        num_scalar_prefetch=1,
        grid=(Bs, n_pages // SELECT_PAGES),
        in_specs=[page_spec(i) for i in range(SELECT_PAGES)]
        + [pl.BlockSpec((None, H, dh), lambda b, s, pt: (b, 0, 0))],
        out_specs=pl.BlockSpec((None, H, 8), lambda b, s, pt: (b, 0, 0)),
        scratch_shapes=[pltpu.VMEM((H, nb, 8, page), F32), pltpu.VMEM((H, dh, page), F32)],
    )
    return pl.pallas_call(
        functools.partial(_moba_select_kernel, pages_per_block=ppb),
        grid_spec=grid_spec,
        out_shape=jax.ShapeDtypeStruct((Bs, H, 8), jnp.int32),
        compiler_params=_cparams(("parallel", "arbitrary")),
        name="moba_select",
    )(page_table, *([cache_kt] * SELECT_PAGES), q)


def _moba_decode_kernel(pt_ref, sel_ref, q_ref, kn_ref, vn_ref, ck_ref, cv_ref, o_ref, kbuf, vbuf, sem,
                        *, pages_per_block, page):
    b = pl.program_id(0)
    nb_ = pl.num_programs(0)
    H = q_ref.shape[0]
    ppb = pages_per_block

    def copies(bb, slot):
        out = []
        for h in range(H):
            for r in range(MOBA_TOPK):
                blk = sel_ref[bb, h, r]
                for g in range(ppb):
                    pg = pt_ref[bb, blk * ppb + g]
                    dst = pl.ds((r * ppb + g) * page, page)
                    out.append(pltpu.make_async_copy(ck_ref.at[pg, h], kbuf.at[slot, h, :, dst], sem.at[slot, 0]))
                    out.append(pltpu.make_async_copy(cv_ref.at[pg, h], vbuf.at[slot, h, :, dst], sem.at[slot, 1]))
        return out

    slot = b % 2

    @pl.when(b == 0)
    def _():
        for c in copies(b, slot):
            c.start()

    @pl.when(b + 1 < nb_)
    def _():
        for c in copies(b + 1, 1 - slot):
            c.start()

    for c in copies(b, slot):
        c.wait()

    row = lax.broadcasted_iota(jnp.int32, (H, 1), 0)
    out = jnp.zeros(o_ref.shape, F32)
    q = q_ref[...]
    kn = kn_ref[...]
    vn = vn_ref[...]
    for h in range(H):
        qh = q[h:h + 1, :] * (DH_A ** -0.5)
        q8 = jnp.broadcast_to(qh, (8, qh.shape[1])).astype(BF16)
        kh = kbuf[slot, h].astype(BF16)
        vh = vbuf[slot, h].astype(BF16)
        s = jnp.dot(q8, kh, preferred_element_type=F32)[0:1, :]
        s_self = jnp.sum(qh * kn[h:h + 1, :], axis=-1, keepdims=True)
        m = jnp.maximum(jnp.max(s, axis=-1, keepdims=True), s_self)
        p = jnp.exp(s - m)
        p_self = jnp.exp(s_self - m)
        l = jnp.sum(p, axis=-1, keepdims=True) + p_self
        p8 = jnp.broadcast_to(p, (8, p.shape[1])).astype(BF16)
        pv = _nt(p8, vh)[0:1, :]
        oh = (pv + p_self * vn[h:h + 1, :]) / l
        out = jnp.where(row == h, oh, out)
    o_ref[...] = out


def _moba_decode(cache_kt, cache_vt, page_table, sel, q, k_new, v_new):
    Bs, n_pages = page_table.shape
    _, H, dh, page = cache_kt.shape
    ppb = MOBA_BLOCK // page
    rows = MOBA_TOPK * MOBA_BLOCK
    vec = pl.BlockSpec((None, H, dh), lambda b, pt, sl: (b, 0, 0))
    grid_spec = pltpu.PrefetchScalarGridSpec(
        num_scalar_prefetch=2,
        grid=(Bs,),
        in_specs=[vec, vec, vec, pl.BlockSpec(memory_space=pl.ANY), pl.BlockSpec(memory_space=pl.ANY)],
        out_specs=vec,
        scratch_shapes=[pltpu.VMEM((2, H, dh, rows), F32), pltpu.VMEM((2, H, dh, rows), F32),
                        pltpu.SemaphoreType.DMA((2, 2))],
    )
    return pl.pallas_call(
        functools.partial(_moba_decode_kernel, pages_per_block=ppb, page=page),
        grid_spec=grid_spec,
        out_shape=jax.ShapeDtypeStruct((Bs, H, dh), F32),
        compiler_params=_cparams(("arbitrary",)),
        name="moba_decode",
    )(page_table, sel, q, k_new, v_new, cache_kt, cache_vt)


def _ret_decode_kernel(q_ref, k_ref, v_ref, gb_ref, rg_ref, g_ref, s_ref, o_ref, so_ref):
    H, dk = q_ref.shape
    eye = _eye(dk)
    row = lax.broadcasted_iota(jnp.int32, (H, 1), 0)
    q = q_ref[...]
    k = k_ref[...]
    v = v_ref[...]
    g_all = g_ref[...]
    out = jnp.zeros(o_ref.shape, F32)
    for h in range(H):
        qh, kh, vh = q[h:h + 1, :], k[h:h + 1, :], v[h:h + 1, :]
        g = g_all[h:h + 1, 0:1]
        S = s_ref[h]
        att = jnp.sum(qh * kh, axis=-1, keepdims=True)
        cross = jnp.sum(_row_to_col(qh * g, eye) * S, axis=0, keepdims=True)
        out = jnp.where(row == h, att * vh + cross, out)
        so_ref[h] = S * g + _row_to_col(kh, eye) * vh
    gb = gb_ref[...]
    y = out * lax.rsqrt(jnp.mean(out * out, axis=-1, keepdims=True) + EPS) * rg_ref[...]
    o_ref[...] = y * (gb * _sigmoid(gb))


def _ret_decode(q, k, v, gb, ret_g, state):
    Bs, H, dk = q.shape
    log_g = jnp.log1p(-jnp.exp2(-5.0 - jnp.arange(H_B, dtype=F32)))
    g = jnp.broadcast_to(jnp.exp(1.0 * log_g)[:, None], (H, LANES))
    vec = pl.BlockSpec((None, H, dk), lambda b: (b, 0, 0))
    st = pl.BlockSpec((None, H, dk, DV_B), lambda b: (b, 0, 0, 0))
    return pl.pallas_call(
        _ret_decode_kernel,
        grid=(Bs,),
        in_specs=[vec, vec, vec, vec, pl.BlockSpec((H, dk), lambda b: (0, 0)),
                  pl.BlockSpec((H, LANES), lambda b: (0, 0)), st],
        out_specs=[vec, st],
        out_shape=[jax.ShapeDtypeStruct((Bs, H, DV_B), F32), jax.ShapeDtypeStruct(state.shape, F32)],
        compiler_params=_cparams(("parallel",)),
        name="retention_decode",
    )(q, k, v, gb, ret_g.reshape(H, DV_B), g, state)


def _mlstm_decode_kernel(q_ref, k_ref, v_ref, og_ref, ig_ref, fg_ref, bi_ref, bf_ref, ng_ref, c_ref, n_ref,
                         m_ref, h_ref, co_ref, no_ref, mo_ref):
    H, d = q_ref.shape
    eye = _eye(d)
    row = lax.broadcasted_iota(jnp.int32, (H, 1), 0)
    q = q_ref[...]
    k = k_ref[...] * (DH_C ** -0.5)
    v = v_ref[...]
    n0 = n_ref[...]
    ig = ig_ref[...] + bi_ref[...]
    b = _log_sigmoid(fg_ref[...] + bf_ref[...])
    m0 = m_ref[...]
    log_inter = b + m0
    m_t = jnp.maximum(log_inter, ig)
    w_intra = jnp.exp(ig - m_t)
    w_inter = jnp.exp(log_inter - m_t)
    s = jnp.sum(q * k, axis=-1, keepdims=True) * w_intra
    den = s + w_inter * jnp.sum(q * n0, axis=-1, keepdims=True)
    scale = 1.0 / jnp.maximum(jnp.abs(den), jnp.exp(-m_t))
    w_k = w_intra
    decay = w_inter
    hs = jnp.zeros(h_ref.shape, F32)
    for h in range(H):
        qh, kh, vh = q[h:h + 1, :], k[h:h + 1, :], v[h:h + 1, :]
        Cm = c_ref[h]
        cq = _col_to_row(jnp.sum(Cm * qh, axis=1, keepdims=True), eye)
        num = s[h:h + 1, :] * vh + cq * w_inter[h:h + 1, :]
        hs = jnp.where(row == h, num * scale[h:h + 1, :], hs)
        co_ref[h] = decay[h:h + 1, :] * Cm + _row_to_col(vh * w_k[h:h + 1, :], eye) * kh
    no_ref[...] = decay * n0 + k * w_k
    mo_ref[...] = m_t
    hn = hs * lax.rsqrt(jnp.mean(hs * hs, axis=-1, keepdims=True) + EPS) * ng_ref[...]
    h_ref[...] = hn * _sigmoid(og_ref[...])


def _mlstm_decode(q, k, v, og, ig, fg, gate_bias, out_g, C0, n0, m0):
    Bs, H, d = q.shape
    vec = pl.BlockSpec((None, H, d), lambda b: (b, 0, 0))
    sc = pl.BlockSpec((None, H, 1), lambda b: (b, 0, 0))
    st = pl.BlockSpec((None, H, d, d), lambda b: (b, 0, 0, 0))
    bias = pl.BlockSpec((H, 1), lambda b: (0, 0))
    return pl.pallas_call(
        _mlstm_decode_kernel,
        grid=(Bs,),
        in_specs=[vec, vec, vec, vec, sc, sc, bias, bias, pl.BlockSpec((H, d), lambda b: (0, 0)), st, vec, sc],
        out_specs=[vec, st, vec, sc],
        out_shape=[jax.ShapeDtypeStruct((Bs, H, d), F32), jax.ShapeDtypeStruct(C0.shape, F32),
                   jax.ShapeDtypeStruct((Bs, H, d), F32), jax.ShapeDtypeStruct((Bs, H, 1), F32)],
        compiler_params=_cparams(("parallel",)),
        name="mlstm_decode",
    )(q, k, v, og, ig, fg, gate_bias[0].reshape(H, 1), gate_bias[1].reshape(H, 1), out_g.reshape(H, d), C0, n0, m0)


def _rope_tables(pos):
    half = DH_A // 2
    inv = ROPE_THETA ** (-jnp.arange(half, dtype=F32) / half)
    ang = pos[:, None] * inv[None, :]
    cos = jnp.cos(ang)
    sin = jnp.sin(ang)
    reps = LANES // DH_A
    cos_t = jnp.tile(jnp.concatenate([cos, cos], axis=-1), (1, reps))
    sin_t = jnp.tile(jnp.concatenate([-sin, sin], axis=-1), (1, reps))
    return cos_t, sin_t


def _row_tile(m, pref):
    return pref if m % pref == 0 else m


def kernel(x_prompt, x_sample, cache_k, cache_v, page_table, state_ret, state_mlstm_C, state_mlstm_n,
           state_mlstm_m, state_ffn_conv, ab_norm_g, ab_w_in, ab_ret_norm_g, ab_w_out, c_norm_g, c_w_in,
           c_gate_bias, c_out_norm_g, c_w_out, ffn_norm_g, ffn_w1, ffn_w3, ffn_conv_w, ffn_conv_b, ffn_w2,
           final_norm_g):
    Bp, Tp, D = x_prompt.shape
    Bs, Ts, _ = x_sample.shape
    assert Ts == 1
    past_len = page_table.shape[1] * cache_k.shape[2]
    Mp = Bp * Tp
    xp = x_prompt.reshape(Mp, D)
    xs = x_sample.reshape(Bs, D)
    tm_p = _row_tile(Tp, 1024)
    tm_ffn = _row_tile(Tp, 1024)
    tm_s = Bs
    rope_p = _rope_tables(jnp.arange(Tp, dtype=jnp.int32).astype(F32))
    rope_s = _rope_tables(jnp.full((Bs,), past_len, jnp.int32).astype(F32))
    gw = 2 * H_C

    outs = {}

    w_in = ab_w_in[0].astype(BF16)
    w_out = ab_w_out[0].astype(BF16)
    w_out_a, w_out_b = w_out[:W_A], w_out[W_A:]

    z = _norm_matmul(xp, ab_norm_g[0], w_in, tm_p, IN_AB // 2)
    oa, kt, vt = _moba_prefill(z, rope_p, Bp, Tp)
    ob, s_pair = _ret_prefill(z, rope_p, ab_ret_norm_g[0], Bp, Tp)
    xp = _proj_residual([oa, ob], [w_out_a, w_out_b], xp, tm_p)
    outs["k_prompt"] = kt.reshape(Bp, H_A, DH_A, Tp).transpose(0, 3, 1, 2)[None]
    outs["v_prompt"] = vt.reshape(Bp, H_A, DH_A, Tp).transpose(0, 3, 1, 2)[None]
    hb = LANES // DK_B
    s_heads = jnp.stack([s_pair[:, :, i * DK_B:(i + 1) * DK_B, i * DV_B:(i + 1) * DV_B] for i in range(hb)], axis=2)
    outs["ret_prompt"] = s_heads.reshape(1, Bp, H_B, DK_B, DV_B)

    zs = _norm_matmul(xs, ab_norm_g[0], w_in, tm_s, AB_TILE, rope=rope_s)
    seg = lambda t: zs[:, t * AB_TILE:(t + 1) * AB_TILE].reshape(Bs, H_A, DH_A)
    qa_s, ka_s, va_s, qb_s, kb_s, vb_s, gb_s = (seg(t) for t in range(7))
    cache_kt = cache_k[0].transpose(0, 2, 3, 1)
    cache_vt = cache_v[0].transpose(0, 2, 3, 1)
    sel = _moba_select(cache_kt, page_table, qa_s)
    oa_s = _moba_decode(cache_kt, cache_vt, page_table, sel, qa_s, ka_s, va_s)
    ob_s, s_new = _ret_decode(qb_s, kb_s, vb_s, gb_s, ab_ret_norm_g[0], state_ret[0])
    xs = _proj_residual([oa_s.reshape(Bs, W_A).astype(BF16), ob_s.reshape(Bs, W_B).astype(BF16)],
                        [w_out_a, w_out_b], xs, tm_s)
    outs["k_sample"] = ka_s.reshape(1, Bs, 1, H_A, DH_A)
    outs["v_sample"] = va_s.reshape(1, Bs, 1, H_A, DH_A)
    outs["ret_sample"] = s_new[None]

    conv_p, conv_s = [], []

    w1_b, w3_b, w2_b = ffn_w1.astype(BF16), ffn_w3.astype(BF16), ffn_w2.astype(BF16)

    def ffn_both(l, xp, xs, final_g):
        st = state_ffn_conv[l]
        xp, a_tail = _ffn(xp, ffn_norm_g[l], l, w1_b, w3_b, ffn_conv_w[l], ffn_conv_b[l], w2_b, final_g, tm_ffn,
                          seq_len=Tp)
        xs, a_s = _ffn(xs, ffn_norm_g[l], l, w1_b, w3_b, ffn_conv_w[l], ffn_conv_b[l], w2_b, final_g, tm_s,
                       state=(st[:, 0, :], st[:, 1, :]))
        conv_p.append(a_tail[:, 8 - (CONV_W - 1):, :])
        conv_s.append(jnp.stack([st[:, 1, :], a_s], axis=1))
        return xp, xs

    xp, xs = ffn_both(0, xp, xs, None)

    w_in = c_w_in[0]
    w_all = w_in.astype(BF16)
    w_gate = jnp.pad(w_in[:, 4 * W_C:], ((0, 0), (0, LANES - gw))).astype(BF16)
    w_out = c_w_out[0].astype(BF16)
    L = MLSTM_CHUNK

    z, zg = _norm_matmul_side(xp, c_norm_g[0], w_all, 4 * W_C, w_gate, tm_p, 1024)
    gates = zg[:, :gw].reshape(Bp, Tp // L, L, 2, H_C).transpose(3, 0, 4, 1, 2)
    h, C_p, n_p, m_p = _mlstm_prefill(z, gates[0], gates[1], c_gate_bias[0], c_out_norm_g[0], Bp, Tp)
    xp = _proj_residual([h], [w_out], xp, tm_p)
    outs["C_prompt"] = C_p[None]
    outs["n_prompt"] = n_p.reshape(1, Bp, H_C, DH_C)
    outs["m_prompt"] = m_p[:, :, 0, 0][None]

    zs, zgs = _norm_matmul_side(xs, c_norm_g[0], w_all, 4 * W_C, w_gate, tm_s, 1024)
    segc = lambda t: zs[:, t * W_C:(t + 1) * W_C].reshape(Bs, H_C, DH_C)
    ig_s = zgs[:, :H_C][:, :, None]
    fg_s = zgs[:, H_C:gw][:, :, None]
    h_s, C_s, n_s, m_s = _mlstm_decode(segc(0), segc(1), segc(2), segc(3), ig_s, fg_s, c_gate_bias[0], c_out_norm_g[0],
                                       state_mlstm_C[0], state_mlstm_n[0], state_mlstm_m[0][:, :, None])
    xs = _proj_residual([h_s.reshape(Bs, W_C).astype(BF16)], [w_out], xs, tm_s)
    outs["C_sample"] = C_s[None]
    outs["n_sample"] = n_s[None]
    outs["m_sample"] = m_s[:, :, 0][None]

    xp, xs = ffn_both(1, xp, xs, final_norm_g)

    return (xp.reshape(Bp, Tp, D), xs.reshape(Bs, 1, D),
            outs["k_prompt"], outs["v_prompt"], outs["k_sample"], outs["v_sample"],
            outs["ret_prompt"], outs["ret_sample"], outs["C_prompt"], outs["C_sample"],
            outs["n_prompt"], outs["n_sample"], outs["m_prompt"], outs["m_sample"],
            jnp.stack(conv_p), jnp.stack(conv_s))
```

```python
import functools

import jax
import jax.numpy as jnp
from jax import lax
from jax.experimental import pallas as pl
from jax.experimental.pallas import tpu as pltpu

F32 = jnp.float32
BF16 = jnp.bfloat16
HIGHEST = lax.Precision.HIGHEST

LANES = 128
D_MODEL = 1024
H_A = 8
DH_A = 64
MOBA_BLOCK = 256
MOBA_TOPK = 3
H_B = 8
DK_B = 64
DV_B = 64
RET_CHUNK = 128
H_C = 8
DH_C = D_MODEL // H_C
MLSTM_CHUNK = 128
D_FF = 11 * D_MODEL // 4
CONV_W = 3
ROPE_THETA = 10000.0
EPS = 1e-6
W_A = H_A * DH_A
W_B = H_B * DV_B
W_C = H_C * DH_C
IN_AB = 3 * W_A + 2 * H_B * DK_B + 2 * W_B
AB_TILE = 512
AB_ROPE_TILES = (0, 1, 3, 4)
AB_KB_TILE = 4
VMEM_LIMIT = 56 * 1024 * 1024

NEG_INF = float("-inf")
LOG2_E = 1.4426950408889634


def _cparams(sem):
    return pltpu.CompilerParams(dimension_semantics=sem, vmem_limit_bytes=VMEM_LIMIT)


def _nt(a, b, **kw):
    return lax.dot_general(a, b, (((1,), (1,)), ((), ())), preferred_element_type=F32, **kw)


def _tn(a, b, **kw):
    return lax.dot_general(a, b, (((0,), (0,)), ((), ())), preferred_element_type=F32, **kw)


def _rms_rows(x, g):
    ms = jnp.mean(x * x, axis=-1, keepdims=True)
    return x * lax.rsqrt(ms + EPS) * g


def _eye(n):
    return lax.broadcasted_iota(jnp.int32, (n, n), 0) == lax.broadcasted_iota(jnp.int32, (n, n), 1)


def _row_to_col(row, eye):
    return jnp.sum(jnp.where(eye, row, 0.0), axis=1, keepdims=True)


def _col_to_row(col, eye):
    return jnp.sum(jnp.where(eye, col, 0.0), axis=0, keepdims=True)


def _rowsum_rep(x, ones=None):
    if ones is None:
        ones = jnp.ones((x.shape[1], LANES), BF16)
    hi = x.astype(BF16)
    lo = (x - hi.astype(F32)).astype(BF16)
    return (jnp.dot(hi, ones, preferred_element_type=F32) + jnp.dot(lo, ones, preferred_element_type=F32))


def _log_sigmoid(x):
    return jnp.minimum(x, 0.0) - jnp.log1p(jnp.exp(-jnp.abs(x)))


def _sigmoid(x):
    return 1.0 / (1.0 + jnp.exp(-x))


def _gelu_tanh(x):
    c = 0.7978845608028654
    return (0.5 * x) * (1.0 + jnp.tanh(x * (c + (0.044715 * c) * (x * x))))


def _norm_matmul_kernel(x_ref, g_ref, w_ref, cos_ref, sin_ref, o_ref, xn_ref, *, rope_tiles, scale_tile, scale):
    j = pl.program_id(1)

    @pl.when(j == 0)
    def _():
        xn_ref[...] = _rms_rows(x_ref[...], g_ref[...]).astype(BF16)

    z = jnp.dot(xn_ref[...], w_ref[...], preferred_element_type=F32)
    if not rope_tiles:
        o_ref[...] = z
        return

    is_rope = functools.reduce(jnp.logical_or, [j == t for t in rope_tiles])

    @pl.when(is_rope)
    def _():
        sc = jnp.where(j == scale_tile, scale, 1.0).astype(F32)
        cos = cos_ref[...]
        sin = sin_ref[...]
        for c in range(z.shape[1] // LANES):
            cols = slice(c * LANES, (c + 1) * LANES)
            o_ref[:, cols] = _rope_lanes(z[:, cols], cos, sin) * sc

    @pl.when(jnp.logical_not(is_rope))
    def _():
        o_ref[...] = z


def _norm_matmul_side_kernel(x_ref, g_ref, w_ref, ws_ref, o_ref, os_ref, xn_ref):
    @pl.when(pl.program_id(1) == 0)
    def _():
        xn_ref[...] = _rms_rows(x_ref[...], g_ref[...]).astype(BF16)
        os_ref[...] = jnp.dot(xn_ref[...], ws_ref[...], preferred_element_type=F32)

    o_ref[...] = jnp.dot(xn_ref[...], w_ref[...], preferred_element_type=F32)


def _norm_matmul_side(x, g, w, n_cols, w_side, tm, tn):
    M, D = x.shape
    N = n_cols
    Ns = w_side.shape[1]
    assert M % tm == 0 and N % tn == 0
    return pl.pallas_call(
        _norm_matmul_side_kernel,
        grid=(M // tm, N // tn),
        in_specs=[pl.BlockSpec((tm, D), lambda i, j: (i, 0)),
                  pl.BlockSpec((1, D), lambda i, j: (0, 0)),
                  pl.BlockSpec((D, tn), lambda i, j: (0, j)),
                  pl.BlockSpec((D, Ns), lambda i, j: (0, 0))],
        out_specs=[pl.BlockSpec((tm, tn), lambda i, j: (i, j)),
                   pl.BlockSpec((tm, Ns), lambda i, j: (i, 0))],
        out_shape=[jax.ShapeDtypeStruct((M, N), F32), jax.ShapeDtypeStruct((M, Ns), F32)],
        scratch_shapes=[pltpu.VMEM((tm, D), BF16)],
        compiler_params=_cparams(("parallel", "arbitrary")),
        name="norm_matmul_side",
    )(x, g.reshape(1, D), w, w_side)


def _norm_matmul(x, g, w, tm, tn, rope=None):
    M, D = x.shape
    N = w.shape[1]
    assert M % tm == 0 and N % tn == 0
    if rope is None:
        cos = sin = jnp.zeros((8, LANES), F32)
        tab_spec = pl.BlockSpec((8, LANES), lambda i, j: (0, 0))
        kern = functools.partial(_norm_matmul_kernel, rope_tiles=(), scale_tile=-1, scale=1.0)
    else:
        cos, sin = rope
        nt = cos.shape[0] // tm
        tab_spec = pl.BlockSpec((tm, LANES), lambda i, j: (i % nt, 0))
        kern = functools.partial(_norm_matmul_kernel, rope_tiles=AB_ROPE_TILES, scale_tile=AB_KB_TILE,
                                 scale=DK_B ** -0.5)
    return pl.pallas_call(
        kern,
        grid=(M // tm, N // tn),
        in_specs=[pl.BlockSpec((tm, D), lambda i, j: (i, 0)),
                  pl.BlockSpec((1, D), lambda i, j: (0, 0)),
                  pl.BlockSpec((D, tn), lambda i, j: (0, j)),
                  tab_spec, tab_spec],
        out_specs=pl.BlockSpec((tm, tn), lambda i, j: (i, j)),
        out_shape=jax.ShapeDtypeStruct((M, N), F32),
        scratch_shapes=[pltpu.VMEM((tm, D), BF16)],
        compiler_params=_cparams(("parallel", "arbitrary")),
        name="norm_matmul",
    )(x, g.reshape(1, D), w, cos, sin)


def _proj_residual_kernel(*refs, n_in):
    a_refs = refs[:n_in]
    w_refs = refs[n_in:2 * n_in]
    res_ref = refs[2 * n_in]
    o_ref = refs[2 * n_in + 1]
    y = res_ref[...]
    acc = None
    for a_ref, w_ref in zip(a_refs, w_refs):
        d = jnp.dot(a_ref[...], w_ref[...], preferred_element_type=F32)
        acc = d if acc is None else acc + d
    o_ref[...] = y + acc


def _proj_residual(acts, ws, res, tm):
    M, D = res.shape
    n_in = len(acts)
    in_specs = ([pl.BlockSpec((tm, a.shape[1]), lambda i: (i, 0)) for a in acts]
                + [pl.BlockSpec(w.shape, lambda i: (0, 0)) for w in ws]
                + [pl.BlockSpec((tm, D), lambda i: (i, 0))])
    return pl.pallas_call(
        functools.partial(_proj_residual_kernel, n_in=n_in),
        grid=(M // tm,),
        in_specs=in_specs,
        out_specs=pl.BlockSpec((tm, D), lambda i: (i, 0)),
        out_shape=jax.ShapeDtypeStruct((M, D), F32),
        compiler_params=_cparams(("parallel",)),
        name="proj_residual",
    )(*acts, *ws, res)


PREV_ROWS = 16
FFN_CHUNK = 256


def _ffn_kernel(*refs, seq_mode, tiles_per_seq, final_norm):
    if seq_mode:
        (x_ref, xp_ref, g_ref, w1_ref, w3_ref, cw_ref, cb_ref, w2_ref, fg_ref, o_ref, a_ref, y_ref) = refs
    else:
        (x_ref, s0_ref, s1_ref, g_ref, w1_ref, w3_ref, cw_ref, cb_ref, w2_ref, fg_ref, o_ref, a_ref, y_ref) = refs
    i = pl.program_id(0)
    x = x_ref[...]
    tm = x.shape[0]
    F = w1_ref.shape[1]
    xn = _rms_rows(x, g_ref[...]).astype(BF16)
    if seq_mode:
        xpn = _rms_rows(xp_ref[...], g_ref[...]).astype(BF16)
        has_prev = ((i % tiles_per_seq) != 0).astype(F32)
        row8 = lax.broadcasted_iota(jnp.int32, (8, 1), 0)
    for c0 in range(0, F, FFN_CHUNK):
        cols = slice(c0, min(c0 + FFN_CHUNK, F))
        w1c = w1_ref[:, cols]
        a = jnp.dot(xn, w1c, preferred_element_type=F32)
        gate = jnp.dot(xn, w3_ref[:, cols], preferred_element_type=F32)
        a_ref[:, cols] = a[tm - a_ref.shape[0]:, :]
        if seq_mode:
            ap = jnp.dot(xpn, w1c, preferred_element_type=F32)
            p1 = ap[PREV_ROWS - 1:PREV_ROWS, :] * has_prev
            p2 = ap[PREV_ROWS - 2:PREV_ROWS - 1, :] * has_prev
            a1 = pltpu.roll(a, 1, 0)
            a2 = pltpu.roll(a, 2, 0)
            top1 = jnp.where(row8 == 0, p1, a1[0:8, :])
            top2 = jnp.where(row8 == 0, p2, jnp.where(row8 == 1, p1, a2[0:8, :]))
            a1 = jnp.concatenate([top1, a1[8:, :]], axis=0)
            a2 = jnp.concatenate([top2, a2[8:, :]], axis=0)
        else:
            a1 = s1_ref[:, cols]
            a2 = s0_ref[:, cols]
        ac = cb_ref[:, cols] + a2 * cw_ref[0:1, cols]
        ac = ac + a1 * cw_ref[1:2, cols]
        ac = ac + a * cw_ref[2:3, cols]
        y_ref[:, cols] = (_gelu_tanh(ac) * gate).astype(BF16)
    acc = x + jnp.dot(y_ref[...], w2_ref[...], preferred_element_type=F32)
    if final_norm:
        acc = _rms_rows(acc, fg_ref[...])
    o_ref[...] = acc


def _ffn(x, norm_g, layer, w1, w3, conv_w, conv_b, w2, final_g, tm, seq_len=None, state=None):
    M, D = x.shape
    F = w1.shape[2]
    seq_mode = state is None
    final_norm = final_g is not None
    fg = (final_g if final_norm else jnp.ones((D,), F32)).reshape(1, D)
    whole = lambda shape: pl.BlockSpec(shape, lambda i: (0, 0), pipeline_mode=pl.Buffered(1))
    stacked = lambda shape: pl.BlockSpec((None,) + shape, lambda i: (layer, 0, 0), pipeline_mode=pl.Buffered(1))
    common = [whole((1, D)), stacked((D, F)), stacked((D, F)), whole((CONV_W, F)), whole((1, F)), stacked((F, D)),
              whole((1, D))]
    common_args = (norm_g.reshape(1, D), w1, w3, conv_w, conv_b.reshape(1, F), w2, fg)
    x_spec = pl.BlockSpec((tm, D), lambda i: (i, 0))
    if seq_mode:
        assert seq_len % tm == 0 and tm % PREV_ROWS == 0
        r = tm // PREV_ROWS
        in_specs = [x_spec, pl.BlockSpec((PREV_ROWS, D), lambda i: (jnp.maximum(i * r - 1, 0), 0))] + common
        args = (x, x) + common_args
        tiles_per_seq = seq_len // tm
    else:
        s_spec = pl.BlockSpec((tm, F), lambda i: (i, 0))
        in_specs = [x_spec, s_spec, s_spec] + common
        args = (x, state[0], state[1]) + common_args
        tiles_per_seq = 1
    if seq_mode:
        a_spec = pl.BlockSpec((None, 8, F), lambda i: (i // tiles_per_seq, 0, 0))
        a_shape = jax.ShapeDtypeStruct((M // seq_len, 8, F), F32)
    else:
        a_spec = pl.BlockSpec((tm, F), lambda i: (i, 0))
        a_shape = jax.ShapeDtypeStruct((M, F), F32)
    return pl.pallas_call(
        functools.partial(_ffn_kernel, seq_mode=seq_mode, tiles_per_seq=tiles_per_seq, final_norm=final_norm),
        grid=(M // tm,),
        in_specs=in_specs,
        out_specs=[pl.BlockSpec((tm, D), lambda i: (i, 0)), a_spec],
        out_shape=[jax.ShapeDtypeStruct((M, D), F32), a_shape],
        scratch_shapes=[pltpu.VMEM((tm, F), BF16)],
        compiler_params=_cparams(("arbitrary",)),
        name="conv_ffn",
    )(*args)


def _rope_lanes(x, cos, sin):
    lane = lax.broadcasted_iota(jnp.int32, (1, LANES), 1)
    first_half = (lane % DH_A) < (DH_A // 2)
    partner = jnp.where(first_half, pltpu.roll(x, LANES - DH_A // 2, 1), pltpu.roll(x, DH_A // 2, 1))
    return x * cos + partner * sin


def _moba_prefill_kernel(q_ref, k_ref, v_ref, cos_ref, sin_ref, o_ref, kt_ref, vt_ref, kb_ref, vb_ref, *, nb):
    blk = MOBA_BLOCK
    heads = LANES // DH_A
    T = q_ref.shape[0]
    cos = cos_ref[...]
    sin = sin_ref[...]
    k = _rope_lanes(k_ref[...], cos, sin)
    kt_ref[...] = k.T
    vt_ref[...] = v_ref[...].T
    kb_ref[...] = k.astype(BF16)
    vb_ref[:, 0:LANES] = v_ref[...].astype(BF16)
    vb_ref[:, LANES:2 * LANES] = jnp.ones((T, LANES), BF16)
    kmean = jnp.concatenate(
        [jnp.sum(k[n * blk:(n + 1) * blk, :], axis=0, keepdims=True) * (1.0 / blk) for n in range(nb)]
        + [jnp.zeros((8 - nb, LANES), F32)] * (nb < 8), axis=0)

    q2 = _rope_lanes(q_ref[...], cos, sin)
    lane = lax.broadcasted_iota(jnp.int32, (1, LANES), 1)
    blk_id = lax.broadcasted_iota(jnp.int32, (8, 1), 0)
    q_blk = lax.broadcasted_iota(jnp.int32, (1, T), 1) // blk
    past = blk_id < q_blk
    causal = (lax.broadcasted_iota(jnp.int32, (blk, blk), 1) <= lax.broadcasted_iota(jnp.int32, (blk, blk), 0))
    hms, qss, sels = [], [], []
    for h in range(heads):
        hm = (lane // DH_A) == h
        qh = jnp.where(hm, q2, 0.0)
        gt = _nt(kmean, qh, precision=HIGHEST)
        sel_t = jnp.zeros_like(gt)
        for n in range(nb - 1):
            gn = gt[n:n + 1, :]
            beats = jnp.logical_and(past, jnp.logical_or(gt > gn, jnp.logical_and(gt == gn, blk_id < n)))
            rank = jnp.sum(beats.astype(F32), axis=0, keepdims=True)
            sel_n = jnp.where(jnp.logical_and(rank < MOBA_TOPK, n < q_blk), 1.0, 0.0)
            sel_t = jnp.where(blk_id == n, sel_n, sel_t)
        hms.append(hm)
        qss.append((qh * (DH_A ** -0.5 * LOG2_E)).astype(BF16))
        sels.append(jnp.where(sel_t.T > 0.5, 0.0, NEG_INF))

    for qi in range(nb):
        rows = slice(qi * blk, (qi + 1) * blk)
        outs = []
        for h in range(heads):
            qs = qss[h][rows, :]
            pieces = []
            for n in range(qi + 1):
                s = _nt(qs, kb_ref[n * blk:(n + 1) * blk, :])
                pieces.append(jnp.where(causal, s, NEG_INF) if n == qi else s + sels[h][rows, n:n + 1])
            m = functools.reduce(jnp.maximum, pieces)
            m = jnp.max(m, axis=-1, keepdims=True)
            p_all = jnp.concatenate([jnp.exp2((s - m).astype(BF16)) for s in pieces], axis=1)
            acc = jnp.dot(p_all, vb_ref[0:(qi + 1) * blk, :], preferred_element_type=F32)
            outs.append(acc[:, 0:LANES] / acc[:, LANES:2 * LANES])
        out = outs[0]
        for h in range(1, heads):
            out = jnp.where(hms[h], outs[h], out)
        o_ref[rows, :] = out.astype(o_ref.dtype)


def _moba_prefill(z, rope, B, T):
    blk = MOBA_BLOCK
    assert T % blk == 0
    nb = T // blk
    assert nb <= 8
    cpt = AB_TILE // LANES
    return pl.pallas_call(
        functools.partial(_moba_prefill_kernel, nb=nb),
        grid=(B, W_A // LANES),
        in_specs=[pl.BlockSpec((T, LANES), lambda b, p: (b, p)),
                  pl.BlockSpec((T, LANES), lambda b, p: (b, cpt + p)),
                  pl.BlockSpec((T, LANES), lambda b, p: (b, 2 * cpt + p)),
                  pl.BlockSpec((T, LANES), lambda b, p: (0, 0)),
                  pl.BlockSpec((T, LANES), lambda b, p: (0, 0))],
        out_specs=[pl.BlockSpec((T, LANES), lambda b, p: (b, p)),
                   pl.BlockSpec((None, LANES, T), lambda b, p: (b, p, 0)),
                   pl.BlockSpec((None, LANES, T), lambda b, p: (b, p, 0))],
        out_shape=[jax.ShapeDtypeStruct((B * T, W_A), BF16),
                   jax.ShapeDtypeStruct((B, W_A, T), F32),
                   jax.ShapeDtypeStruct((B, W_A, T), F32)],
        scratch_shapes=[pltpu.VMEM((T, LANES), BF16), pltpu.VMEM((T, 2 * LANES), BF16)],
        compiler_params=_cparams(("parallel", "parallel")),
        name="moba_prefill",
    )(z, z, z, rope[0], rope[1])


def _ret_prefill_kernel(q_ref, k_ref, v_ref, gb_ref, cos_ref, sin_ref, rg_ref, dmask_ref, din_ref, dout_ref,
                        gch_ref, o_ref, s_ref, *, n_chunks):
    C = RET_CHUNK
    heads = LANES // DK_B
    lane = lax.broadcasted_iota(jnp.int32, (1, LANES), 1)
    hms = [(lane // DK_B) == h for h in range(heads)]
    row_h = lax.broadcasted_iota(jnp.int32, (LANES, LANES), 0) // DK_B
    col_h = lax.broadcasted_iota(jnp.int32, (LANES, LANES), 1) // DV_B
    same_head = row_h == col_h
    seg_ones = jnp.where(same_head, 1.0, 0.0).astype(BF16)
    din = din_ref[...]
    dout = dout_ref[...]
    gch = gch_ref[...]
    rg = rg_ref[...]
    dmask = dmask_ref[...]

    S = jnp.zeros((LANES, LANES), F32)
    for j in range(n_chunks):
        rows = slice(j * C, (j + 1) * C)
        cos = cos_ref[rows, :]
        sin = sin_ref[rows, :]
        q = _rope_lanes(q_ref[rows, :], cos, sin)
        k = _rope_lanes(k_ref[rows, :], cos, sin) * (DK_B ** -0.5)
        kb = k.astype(BF16)
        vb = v_ref[rows, :].astype(BF16)
        o = jnp.dot((q * din).astype(BF16), S.astype(BF16), preferred_element_type=F32)
        q_st = jnp.concatenate([jnp.where(hm, q, 0.0) for hm in hms], axis=0).astype(BF16)
        att = _nt(q_st, kb) * dmask
        res = jnp.dot(att.astype(BF16), vb, preferred_element_type=F32)
        intra = res[0:C, :]
        for h in range(1, heads):
            intra = jnp.where(hms[h], res[h * C:(h + 1) * C, :], intra)
        o = o + intra
        S = S * gch + jnp.where(same_head, _tn((k * dout).astype(BF16), vb), 0.0)
        ms = _rowsum_rep(o * o, seg_ones) * (1.0 / DV_B)
        g = gb_ref[rows, :]
        y = o * lax.rsqrt(ms + EPS) * rg * (g * _sigmoid(g))
        o_ref[rows, :] = y.astype(o_ref.dtype)
    s_ref[...] = S


def _ret_tables(chunk):
    log_g = jnp.log1p(-jnp.exp2(-5.0 - jnp.arange(H_B, dtype=F32)))
    i = jnp.arange(chunk, dtype=F32)
    d_in = jnp.exp((i[:, None] + 1.0) * log_g)
    d_out = jnp.exp((chunk - 1.0 - i)[:, None] * log_g)
    diff = i[:, None] - i[None, :]
    d_mask = jnp.where(diff >= 0, jnp.exp(jnp.maximum(diff, 0.0)[None] * log_g[:, None, None]), 0.0)
    g_chunk = jnp.exp(chunk * log_g)
    return d_in, d_out, d_mask, g_chunk


def _ret_prefill(z, rope, ret_g, B, T):
    C = RET_CHUNK
    assert T % C == 0
    d_in, d_out, d_mask, g_chunk = _ret_tables(C)
    npair = W_B // LANES
    lanes = lambda t: jnp.repeat(t, DK_B, axis=-1)
    din_l = lanes(d_in).reshape(C, npair, LANES).transpose(1, 0, 2)
    dout_l = lanes(d_out).reshape(C, npair, LANES).transpose(1, 0, 2)
    gch_l = lanes(g_chunk).reshape(npair, 1, LANES)
    cpt = AB_TILE // LANES
    col = lambda t: (lambda b, p: (b, t * cpt + p))
    return pl.pallas_call(
        functools.partial(_ret_prefill_kernel, n_chunks=T // C),
        grid=(B, npair),
        in_specs=[pl.BlockSpec((T, LANES), col(3)), pl.BlockSpec((T, LANES), col(4)),
                  pl.BlockSpec((T, LANES), col(5)), pl.BlockSpec((T, LANES), col(6)),
                  pl.BlockSpec((T, LANES), lambda b, p: (0, 0)),
                  pl.BlockSpec((T, LANES), lambda b, p: (0, 0)),
                  pl.BlockSpec((1, LANES), lambda b, p: (0, p)),
                  pl.BlockSpec((None, (LANES // DK_B) * C, C), lambda b, p: (p, 0, 0)),
                  pl.BlockSpec((None, C, LANES), lambda b, p: (p, 0, 0)),
                  pl.BlockSpec((None, C, LANES), lambda b, p: (p, 0, 0)),
                  pl.BlockSpec((None, 1, LANES), lambda b, p: (p, 0, 0))],
        out_specs=[pl.BlockSpec((T, LANES), lambda b, p: (b, p)),
                   pl.BlockSpec((None, None, LANES, LANES), lambda b, p: (b, p, 0, 0))],
        out_shape=[jax.ShapeDtypeStruct((B * T, W_B), BF16),
                   jax.ShapeDtypeStruct((B, npair, LANES, LANES), F32)],
        compiler_params=_cparams(("parallel", "parallel")),
        name="retention_prefill",
    )(z, z, z, z, rope[0], rope[1], ret_g.reshape(1, W_B), d_mask.reshape(npair, -1, C), din_l, dout_l, gch_l)


def _mlstm_prefill_kernel(bias_ref, q_ref, k_ref, v_ref, og_ref, ig_ref, fg_ref, igc_ref, fgc_ref, ng_ref,
                          h_ref, c_ref, n_ref, m_ref, *, n_chunks):
    L = MLSTM_CHUNK
    hd = pl.program_id(1)
    r_id = lax.broadcasted_iota(jnp.int32, (L, L), 0)
    c_id = lax.broadcasted_iota(jnp.int32, (L, L), 1)
    causal = c_id <= r_id
    upper = jnp.where(r_id <= c_id, 1.0, 0.0).astype(F32)
    lower = jnp.where(c_id <= r_id, 1.0, 0.0).astype(F32)
    ones_b = jnp.ones((L, L), BF16)
    i_rows = ig_ref[...] + bias_ref[0, hd]
    b_rows = jnp.dot(_log_sigmoid(fg_ref[...] + bias_ref[1, hd]), upper, preferred_element_type=F32, precision=HIGHEST)
    i_cols = igc_ref[...] + bias_ref[0, hd]
    b_cols = jnp.dot(lower, _log_sigmoid(fgc_ref[...] + bias_ref[1, hd]), preferred_element_type=F32, precision=HIGHEST)
    ng = ng_ref[...]

    Cm = jnp.zeros((L, L), F32)
    n = jnp.zeros((1, L), F32)
    m = jnp.zeros((1, 1), F32)
    for j in range(n_chunks):
        rows = slice(j * L, (j + 1) * L)
        q = q_ref[rows, :]
        k = k_ref[rows, :] * (DH_C ** -0.5)
        v = v_ref[rows, :]
        b_row, i_row = b_rows[j:j + 1, :], i_rows[j:j + 1, :]
        b_col, i_col = b_cols[:, j:j + 1], i_cols[:, j:j + 1]
        b_last = b_row[:, L - 1:L]
        qb = q.astype(BF16)
        kb = k.astype(BF16)
        b_rep = jnp.broadcast_to(b_col, (L, L))
        log_d = jnp.where(causal, (b_rep - b_row) + i_row, NEG_INF)
        m_row = jnp.broadcast_to(jnp.max(log_d, axis=-1, keepdims=True), (L, L))
        s = _nt(qb, kb) * jnp.exp(log_d - m_row)
        sv_ext = jnp.dot(s.astype(BF16), jnp.concatenate([v.astype(BF16), ones_b], axis=1),
                         preferred_element_type=F32)
        sv, s_sum = sv_ext[:, 0:L], sv_ext[:, L:2 * L]
        log_w = (b_last - b_col) + i_col
        m_loc = jnp.max(log_w, axis=0, keepdims=True)
        e_k = jnp.broadcast_to(jnp.exp(log_w - m_loc), (L, L))
        U = _tn((v * e_k).astype(BF16), kb)
        nk = jnp.sum(k * e_k, axis=0, keepdims=True)
        log_inter = b_rep + m
        m_t = jnp.maximum(log_inter, m_row)
        f_intra = jnp.exp(m_row - m_t)
        w_inter = jnp.exp(log_inter - m_t)
        num = sv * f_intra + _nt(qb, Cm.astype(BF16)) * w_inter
        den = s_sum * f_intra + w_inter * _rowsum_rep(q * n)
        h = num / jnp.maximum(jnp.abs(den), jnp.exp(-m_t))
        m_new = jnp.maximum(b_last + m, m_loc)
        decay = jnp.exp(b_last + m - m_new)
        f_k = jnp.exp(m_loc - m_new)
        Cm = decay * Cm + f_k * U
        n = decay * n + f_k * nk
        m = m_new
        hn = h * lax.rsqrt(_rowsum_rep(h * h) * (1.0 / L) + EPS) * ng
        h_ref[rows, :] = (hn * _sigmoid(og_ref[rows, :])).astype(h_ref.dtype)
    c_ref[...] = Cm
    n_ref[...] = n
    m_ref[...] = jnp.broadcast_to(m, m_ref.shape)


def _mlstm_prefill(z, ig, fg, gate_bias, out_g, B, T):
    L = MLSTM_CHUNK
    assert T % L == 0 and DH_C == LANES
    nc = T // L
    col = lambda t: (lambda b, h, bias: (b, t * H_C + h))
    gspec = pl.BlockSpec((None, None, nc, L), lambda b, h, bias: (b, h, 0, 0))
    gcspec = pl.BlockSpec((None, None, L, nc), lambda b, h, bias: (b, h, 0, 0))
    grid_spec = pltpu.PrefetchScalarGridSpec(
        num_scalar_prefetch=1,
        grid=(B, H_C),
        in_specs=[pl.BlockSpec((T, LANES), col(0)), pl.BlockSpec((T, LANES), col(1)),
                  pl.BlockSpec((T, LANES), col(2)), pl.BlockSpec((T, LANES), col(3)),
                  gspec, gspec, gcspec, gcspec,
                  pl.BlockSpec((1, LANES), lambda b, h, bias: (0, h))],
        out_specs=[pl.BlockSpec((T, LANES), lambda b, h, bias: (b, h)),
                   pl.BlockSpec((None, None, L, L), lambda b, h, bias: (b, h, 0, 0)),
                   pl.BlockSpec((None, None, 1, L), lambda b, h, bias: (b, h, 0, 0)),
                   pl.BlockSpec((None, None, 1, LANES), lambda b, h, bias: (b, h, 0, 0))],
    )
    return pl.pallas_call(
        functools.partial(_mlstm_prefill_kernel, n_chunks=nc),
        grid_spec=grid_spec,
        out_shape=[jax.ShapeDtypeStruct((B * T, W_C), BF16),
                   jax.ShapeDtypeStruct((B, H_C, L, L), F32),
                   jax.ShapeDtypeStruct((B, H_C, 1, L), F32),
                   jax.ShapeDtypeStruct((B, H_C, 1, LANES), F32)],
        compiler_params=_cparams(("parallel", "parallel")),
        name="mlstm_prefill",
    )(gate_bias, z, z, z, z, ig, fg, ig.swapaxes(2, 3), fg.swapaxes(2, 3), out_g.reshape(1, W_C))


SELECT_PAGES = 16


def _moba_select_kernel(pt_ref, *refs, pages_per_block):
    k_refs = refs[:SELECT_PAGES]
    q_ref, sel_ref, gate_ref, qb_ref = refs[SELECT_PAGES:]
    s = pl.program_id(1)
    ppb = pages_per_block
    blocks_per_step = SELECT_PAGES // ppb
    H, dh, page = k_refs[0].shape

    @pl.when(s == 0)
    def _():
        eye = _eye(dh)
        q = q_ref[...]
        for h in range(H):
            qb_ref[h] = jnp.broadcast_to(_row_to_col(q[h:h + 1, :], eye), (dh, page))

    qb = qb_ref[...]
    for i in range(blocks_per_step):
        acc = k_refs[i * ppb][...]
        for g in range(1, ppb):
            acc = acc + k_refs[i * ppb + g][...]
        prod = acc * qb
        part = prod[:, 0:8, :]
        for r in range(1, prod.shape[1] // 8):
            part = part + prod[:, 8 * r:8 * (r + 1), :]
        gate_ref[:, s * blocks_per_step + i] = part

    @pl.when(s == pl.num_programs(1) - 1)
    def _():
        nb = gate_ref.shape[1]
        part_sum = jnp.sum(gate_ref[...], axis=2).reshape(H * nb, page)
        col = jnp.sum(part_sum, axis=-1, keepdims=True) * (1.0 / MOBA_BLOCK)
        eye = _eye(nb)
        head_row = lax.broadcasted_iota(jnp.int32, (H, 1), 0)
        gate = jnp.zeros((H, nb), F32)
        for h in range(H):
            gate = jnp.where(head_row == h, _col_to_row(col[h * nb:(h + 1) * nb, :], eye), gate)
        blk_id = lax.broadcasted_iota(jnp.int32, (1, nb), 1)
        rank = jnp.zeros(gate.shape, jnp.int32)
        for m in range(nb):
            gm = gate[:, m:m + 1]
            beats = jnp.logical_or(gm > gate, jnp.logical_and(gm == gate, m < blk_id))
            rank = rank + jnp.where(beats, 1, 0)
        out_col = lax.broadcasted_iota(jnp.int32, sel_ref.shape, 1)
        out = jnp.zeros(sel_ref.shape, jnp.int32)
        for r in range(MOBA_TOPK):
            idx = jnp.sum(jnp.where(rank == r, blk_id, 0), axis=1, keepdims=True)
            out = jnp.where(out_col == r, idx, out)
        sel_ref[...] = out


def _moba_select(cache_kt, page_table, q):
    Bs, n_pages = page_table.shape
    _, H, dh, page = cache_kt.shape
    ppb = MOBA_BLOCK // page
    nb = n_pages // ppb
    assert nb >= MOBA_TOPK and n_pages % SELECT_PAGES == 0 and SELECT_PAGES % ppb == 0

    def page_spec(i):
        return pl.BlockSpec((None, H, dh, page), lambda b, s, pt: (pt[b, s * SELECT_PAGES + i], 0, 0, 0))

    grid_spec = pltpu.PrefetchScalarGridSpec(
        num_scalar_prefetch=1,
        grid=(Bs, n_pages // SELECT_PAGES),
        in_specs=[page_spec(i) for i in range(SELECT_PAGES)]
        + [pl.BlockSpec((None, H, dh), lambda b, s, pt: (b, 0, 0))],
        out_specs=pl.BlockSpec((None, H, 8), lambda b, s, pt: (b, 0, 0)),
        scratch_shapes=[pltpu.VMEM((H, nb, 8, page), F32), pltpu.VMEM((H, dh, page), F32)],
    )
    return pl.pallas_call(
        functools.partial(_moba_select_kernel, pages_per_block=ppb),
        grid_spec=grid_spec,
        out_shape=jax.ShapeDtypeStruct((Bs, H, 8), jnp.int32),
        compiler_params=_cparams(("parallel", "arbitrary")),
        name="moba_select",
    )(page_table, *([cache_kt] * SELECT_PAGES), q)


def _moba_decode_kernel(pt_ref, sel_ref, q_ref, kn_ref, vn_ref, ck_ref, cv_ref, o_ref, kbuf, vbuf, sem,
                        *, pages_per_block, page):
    b = pl.program_id(0)
    nb_ = pl.num_programs(0)
    H = q_ref.shape[0]
    ppb = pages_per_block

    def copies(bb, slot):
        out = []
        for h in range(H):
            for r in range(MOBA_TOPK):
                blk = sel_ref[bb, h, r]
                for g in range(ppb):
                    pg = pt_ref[bb, blk * ppb + g]
                    dst = pl.ds((r * ppb + g) * page, page)
                    out.append(pltpu.make_async_copy(ck_ref.at[pg, h], kbuf.at[slot, h, :, dst], sem.at[slot, 0]))
                    out.append(pltpu.make_async_copy(cv_ref.at[pg, h], vbuf.at[slot, h, :, dst], sem.at[slot, 1]))
        return out

    slot = b % 2

    @pl.when(b == 0)
    def _():
        for c in copies(b, slot):
            c.start()

    @pl.when(b + 1 < nb_)
    def _():
        for c in copies(b + 1, 1 - slot):
            c.start()

    for c in copies(b, slot):
        c.wait()

    row = lax.broadcasted_iota(jnp.int32, (H, 1), 0)
    out = jnp.zeros(o_ref.shape, F32)
    q = q_ref[...]
    kn = kn_ref[...]
    vn = vn_ref[...]
    for h in range(H):
        qh = q[h:h + 1, :] * (DH_A ** -0.5)
        q8 = jnp.broadcast_to(qh, (8, qh.shape[1])).astype(BF16)
        kh = kbuf[slot, h].astype(BF16)
        vh = vbuf[slot, h].astype(BF16)
        s = jnp.dot(q8, kh, preferred_element_type=F32)[0:1, :]
        s_self = jnp.sum(qh * kn[h:h + 1, :], axis=-1, keepdims=True)
        m = jnp.maximum(jnp.max(s, axis=-1, keepdims=True), s_self)
        p = jnp.exp(s - m)
        p_self = jnp.exp(s_self - m)
        l = jnp.sum(p, axis=-1, keepdims=True) + p_self
        p8 = jnp.broadcast_to(p, (8, p.shape[1])).astype(BF16)
        pv = _nt(p8, vh)[0:1, :]
        oh = (pv + p_self * vn[h:h + 1, :]) / l
        out = jnp.where(row == h, oh, out)
    o_ref[...] = out


def _moba_decode(cache_kt, cache_vt, page_table, sel, q, k_new, v_new):
    Bs, n_pages = page_table.shape
    _, H, dh, page = cache_kt.shape
    ppb = MOBA_BLOCK // page
    rows = MOBA_TOPK * MOBA_BLOCK
    vec = pl.BlockSpec((None, H, dh), lambda b, pt, sl: (b, 0, 0))
    grid_spec = pltpu.PrefetchScalarGridSpec(
        num_scalar_prefetch=2,
        grid=(Bs,),
        in_specs=[vec, vec, vec, pl.BlockSpec(memory_space=pl.ANY), pl.BlockSpec(memory_space=pl.ANY)],
        out_specs=vec,
        scratch_shapes=[pltpu.VMEM((2, H, dh, rows), F32), pltpu.VMEM((2, H, dh, rows), F32),
                        pltpu.SemaphoreType.DMA((2, 2))],
    )
    return pl.pallas_call(
        functools.partial(_moba_decode_kernel, pages_per_block=ppb, page=page),
        grid_spec=grid_spec,
        out_shape=jax.ShapeDtypeStruct((Bs, H, dh), F32),
        compiler_params=_cparams(("arbitrary",)),
        name="moba_decode",
    )(page_table, sel, q, k_new, v_new, cache_kt, cache_vt)


def _ret_decode_kernel(q_ref, k_ref, v_ref, gb_ref, rg_ref, g_ref, s_ref, o_ref, so_ref):
    H, dk = q_ref.shape
    eye = _eye(dk)
    row = lax.broadcasted_iota(jnp.int32, (H, 1), 0)
    q = q_ref[...]
    k = k_ref[...]
    v = v_ref[...]
    g_all = g_ref[...]
    out = jnp.zeros(o_ref.shape, F32)
    for h in range(H):
        qh, kh, vh = q[h:h + 1, :], k[h:h + 1, :], v[h:h + 1, :]
        g = g_all[h:h + 1, 0:1]
        S = s_ref[h]
        att = jnp.sum(qh * kh, axis=-1, keepdims=True)
        cross = jnp.sum(_row_to_col(qh * g, eye) * S, axis=0, keepdims=True)
        out = jnp.where(row == h, att * vh + cross, out)
        so_ref[h] = S * g + _row_to_col(kh, eye) * vh
    gb = gb_ref[...]
    y = out * lax.rsqrt(jnp.mean(out * out, axis=-1, keepdims=True) + EPS) * rg_ref[...]
    o_ref[...] = y * (gb * _sigmoid(gb))


def _ret_decode(q, k, v, gb, ret_g, state):
    Bs, H, dk = q.shape
    log_g = jnp.log1p(-jnp.exp2(-5.0 - jnp.arange(H_B, dtype=F32)))
    g = jnp.broadcast_to(jnp.exp(1.0 * log_g)[:, None], (H, LANES))
    vec = pl.BlockSpec((None, H, dk), lambda b: (b, 0, 0))
    st = pl.BlockSpec((None, H, dk, DV_B), lambda b: (b, 0, 0, 0))
    return pl.pallas_call(
        _ret_decode_kernel,
        grid=(Bs,),
        in_specs=[vec, vec, vec, vec, pl.BlockSpec((H, dk), lambda b: (0, 0)),
                  pl.BlockSpec((H, LANES), lambda b: (0, 0)), st],
        out_specs=[vec, st],
        out_shape=[jax.ShapeDtypeStruct((Bs, H, DV_B), F32), jax.ShapeDtypeStruct(state.shape, F32)],
        compiler_params=_cparams(("parallel",)),
        name="retention_decode",
    )(q, k, v, gb, ret_g.reshape(H, DV_B), g, state)


def _mlstm_decode_kernel(q_ref, k_ref, v_ref, og_ref, ig_ref, fg_ref, bi_ref, bf_ref, ng_ref, c_ref, n_ref,
                         m_ref, h_ref, co_ref, no_ref, mo_ref):
    H, d = q_ref.shape
    eye = _eye(d)
    row = lax.broadcasted_iota(jnp.int32, (H, 1), 0)
    q = q_ref[...]
    k = k_ref[...] * (DH_C ** -0.5)
    v = v_ref[...]
    n0 = n_ref[...]
    ig = ig_ref[...] + bi_ref[...]
    b = _log_sigmoid(fg_ref[...] + bf_ref[...])
    m0 = m_ref[...]
    log_inter = b + m0
    m_t = jnp.maximum(log_inter, ig)
    w_intra = jnp.exp(ig - m_t)
    w_inter = jnp.exp(log_inter - m_t)
    s = jnp.sum(q * k, axis=-1, keepdims=True) * w_intra
    den = s + w_inter * jnp.sum(q * n0, axis=-1, keepdims=True)
    scale = 1.0 / jnp.maximum(jnp.abs(den), jnp.exp(-m_t))
    w_k = w_intra
    decay = w_inter
    hs = jnp.zeros(h_ref.shape, F32)
    for h in range(H):
        qh, kh, vh = q[h:h + 1, :], k[h:h + 1, :], v[h:h + 1, :]
        Cm = c_ref[h]
        cq = _col_to_row(jnp.sum(Cm * qh, axis=1, keepdims=True), eye)
        num = s[h:h + 1, :] * vh + cq * w_inter[h:h + 1, :]
        hs = jnp.where(row == h, num * scale[h:h + 1, :], hs)
        co_ref[h] = decay[h:h + 1, :] * Cm + _row_to_col(vh * w_k[h:h + 1, :], eye) * kh
    no_ref[...] = decay * n0 + k * w_k
    mo_ref[...] = m_t
    hn = hs * lax.rsqrt(jnp.mean(hs * hs, axis=-1, keepdims=True) + EPS) * ng_ref[...]
    h_ref[...] = hn * _sigmoid(og_ref[...])


def _mlstm_decode(q, k, v, og, ig, fg, gate_bias, out_g, C0, n0, m0):
    Bs, H, d = q.shape
    vec = pl.BlockSpec((None, H, d), lambda b: (b, 0, 0))
    sc = pl.BlockSpec((None, H, 1), lambda b: (b, 0, 0))
    st = pl.BlockSpec((None, H, d, d), lambda b: (b, 0, 0, 0))
    bias = pl.BlockSpec((H, 1), lambda b: (0, 0))
    return pl.pallas_call(
        _mlstm_decode_kernel,
        grid=(Bs,),
        in_specs=[vec, vec, vec, vec, sc, sc, bias, bias, pl.BlockSpec((H, d), lambda b: (0, 0)), st, vec, sc],
        out_specs=[vec, st, vec, sc],
        out_shape=[jax.ShapeDtypeStruct((Bs, H, d), F32), jax.ShapeDtypeStruct(C0.shape, F32),
                   jax.ShapeDtypeStruct((Bs, H, d), F32), jax.ShapeDtypeStruct((Bs, H, 1), F32)],
        compiler_params=_cparams(("parallel",)),
        name="mlstm_decode",
    )(q, k, v, og, ig, fg, gate_bias[0].reshape(H, 1), gate_bias[1].reshape(H, 1), out_g.reshape(H, d), C0, n0, m0)


def _rope_tables(pos):
    half = DH_A // 2
    inv = ROPE_THETA ** (-jnp.arange(half, dtype=F32) / half)
    ang = pos[:, None] * inv[None, :]
    cos = jnp.cos(ang)
    sin = jnp.sin(ang)
    reps = LANES // DH_A
    cos_t = jnp.tile(jnp.concatenate([cos, cos], axis=-1), (1, reps))
    sin_t = jnp.tile(jnp.concatenate([-sin, sin], axis=-1), (1, reps))
    return cos_t, sin_t


def _row_tile(m, pref):
    return pref if m % pref == 0 else m


def kernel(x_prompt, x_sample, cache_k, cache_v, page_table, state_ret, state_mlstm_C, state_mlstm_n,
           state_mlstm_m, state_ffn_conv, ab_norm_g, ab_w_in, ab_ret_norm_g, ab_w_out, c_norm_g, c_w_in,
           c_gate_bias, c_out_norm_g, c_w_out, ffn_norm_g, ffn_w1, ffn_w3, ffn_conv_w, ffn_conv_b, ffn_w2,
           final_norm_g):
    Bp, Tp, D = x_prompt.shape
    Bs, Ts, _ = x_sample.shape
    assert Ts == 1
    past_len = page_table.shape[1] * cache_k.shape[2]
    Mp = Bp * Tp
    xp = x_prompt.reshape(Mp, D)
    xs = x_sample.reshape(Bs, D)
    tm_p = _row_tile(Tp, 1024)
    tm_in = _row_tile(Tp, 2048)
    tm_ffn = _row_tile(Tp, 1024)
    tm_s = Bs
    rope_p = _rope_tables(jnp.arange(Tp, dtype=jnp.int32).astype(F32))
    rope_s = _rope_tables(jnp.full((Bs,), past_len, jnp.int32).astype(F32))
    gw = 2 * H_C

    outs = {}

    w_in = ab_w_in[0].astype(BF16)
    w_out = ab_w_out[0].astype(BF16)
    w_out_a, w_out_b = w_out[:W_A], w_out[W_A:]

    z = _norm_matmul(xp, ab_norm_g[0], w_in, tm_in, IN_AB // 4)
    oa, kt, vt = _moba_prefill(z, rope_p, Bp, Tp)
    ob, s_pair = _ret_prefill(z, rope_p, ab_ret_norm_g[0], Bp, Tp)
    xp = _proj_residual([oa, ob], [w_out_a, w_out_b], xp, tm_p)
    outs["k_prompt"] = kt.reshape(Bp, H_A, DH_A, Tp).transpose(0, 3, 1, 2)[None]
    outs["v_prompt"] = vt.reshape(Bp, H_A, DH_A, Tp).transpose(0, 3, 1, 2)[None]
    hb = LANES // DK_B
    s_heads = jnp.stack([s_pair[:, :, i * DK_B:(i + 1) * DK_B, i * DV_B:(i + 1) * DV_B] for i in range(hb)], axis=2)
    outs["ret_prompt"] = s_heads.reshape(1, Bp, H_B, DK_B, DV_B)

    zs = _norm_matmul(xs, ab_norm_g[0], w_in, tm_s, AB_TILE, rope=rope_s)
    seg = lambda t: zs[:, t * AB_TILE:(t + 1) * AB_TILE].reshape(Bs, H_A, DH_A)
    qa_s, ka_s, va_s, qb_s, kb_s, vb_s, gb_s = (seg(t) for t in range(7))
    cache_kt = cache_k[0].transpose(0, 2, 3, 1)
    cache_vt = cache_v[0].transpose(0, 2, 3, 1)
    sel = _moba_select(cache_kt, page_table, qa_s)
    oa_s = _moba_decode(cache_kt, cache_vt, page_table, sel, qa_s, ka_s, va_s)
    ob_s, s_new = _ret_decode(qb_s, kb_s, vb_s, gb_s, ab_ret_norm_g[0], state_ret[0])
    xs = _proj_residual([oa_s.reshape(Bs, W_A).astype(BF16), ob_s.reshape(Bs, W_B).astype(BF16)],
                        [w_out_a, w_out_b], xs, tm_s)
    outs["k_sample"] = ka_s.reshape(1, Bs, 1, H_A, DH_A)
    outs["v_sample"] = va_s.reshape(1, Bs, 1, H_A, DH_A)
    outs["ret_sample"] = s_new[None]

    conv_p, conv_s = [], []

    w1_b, w3_b, w2_b = ffn_w1.astype(BF16), ffn_w3.astype(BF16), ffn_w2.astype(BF16)

    def ffn_both(l, xp, xs, final_g):
        st = state_ffn_conv[l]
        xp, a_tail = _ffn(xp, ffn_norm_g[l], l, w1_b, w3_b, ffn_conv_w[l], ffn_conv_b[l], w2_b, final_g, tm_ffn,
                          seq_len=Tp)
        xs, a_s = _ffn(xs, ffn_norm_g[l], l, w1_b, w3_b, ffn_conv_w[l], ffn_conv_b[l], w2_b, final_g, tm_s,
                       state=(st[:, 0, :], st[:, 1, :]))
        conv_p.append(a_tail[:, 8 - (CONV_W - 1):, :])
        conv_s.append(jnp.stack([st[:, 1, :], a_s], axis=1))
        return xp, xs

    xp, xs = ffn_both(0, xp, xs, None)

    w_in = c_w_in[0]
    w_all = w_in.astype(BF16)
    w_gate = jnp.pad(w_in[:, 4 * W_C:], ((0, 0), (0, LANES - gw))).astype(BF16)
    w_out = c_w_out[0].astype(BF16)
    L = MLSTM_CHUNK

    z, zg = _norm_matmul_side(xp, c_norm_g[0], w_all, 4 * W_C, w_gate, tm_in, 1024)
    gates = zg[:, :gw].reshape(Bp, Tp // L, L, 2, H_C).transpose(3, 0, 4, 1, 2)
    h, C_p, n_p, m_p = _mlstm_prefill(z, gates[0], gates[1], c_gate_bias[0], c_out_norm_g[0], Bp, Tp)
    xp = _proj_residual([h], [w_out], xp, tm_p)
    outs["C_prompt"] = C_p[None]
    outs["n_prompt"] = n_p.reshape(1, Bp, H_C, DH_C)
    outs["m_prompt"] = m_p[:, :, 0, 0][None]

    zs, zgs = _norm_matmul_side(xs, c_norm_g[0], w_all, 4 * W_C, w_gate, tm_s, 1024)
    segc = lambda t: zs[:, t * W_C:(t + 1) * W_C].reshape(Bs, H_C, DH_C)
    ig_s = zgs[:, :H_C][:, :, None]
    fg_s = zgs[:, H_C:gw][:, :, None]
    h_s, C_s, n_s, m_s = _mlstm_decode(segc(0), segc(1), segc(2), segc(3), ig_s, fg_s, c_gate_bias[0], c_out_norm_g[0],
                                       state_mlstm_C[0], state_mlstm_n[0], state_mlstm_m[0][:, :, None])
    xs = _proj_residual([h_s.reshape(Bs, W_C).astype(BF16)], [w_out], xs, tm_s)
    outs["C_sample"] = C_s[None]
    outs["n_sample"] = n_s[None]
    outs["m_sample"] = m_s[:, :, 0][None]

    xp, xs = ffn_both(1, xp, xs, final_norm_g)

    return (xp.reshape(Bp, Tp, D), xs.reshape(Bs, 1, D),
            outs["k_prompt"], outs["v_prompt"], outs["k_sample"], outs["v_sample"],
            outs["ret_prompt"], outs["ret_sample"], outs["C_prompt"], outs["C_sample"],
            outs["n_prompt"], outs["n_sample"], outs["m_prompt"], outs["m_sample"],
            jnp.stack(conv_p), jnp.stack(conv_s))
```

```python
import functools

import jax
import jax.numpy as jnp
from jax import lax
from jax.experimental import pallas as pl
from jax.experimental.pallas import tpu as pltpu

F32 = jnp.float32
BF16 = jnp.bfloat16
HIGHEST = lax.Precision.HIGHEST

LANES = 128
D_MODEL = 1024
H_A = 8
DH_A = 64
MOBA_BLOCK = 256
MOBA_TOPK = 3
H_B = 8
DK_B = 64
DV_B = 64
RET_CHUNK = 128
H_C = 8
DH_C = D_MODEL // H_C
MLSTM_CHUNK = 128
D_FF = 11 * D_MODEL // 4
CONV_W = 3
ROPE_THETA = 10000.0
EPS = 1e-6
W_A = H_A * DH_A
W_B = H_B * DV_B
W_C = H_C * DH_C
IN_AB = 3 * W_A + 2 * H_B * DK_B + 2 * W_B
AB_TILE = 512
AB_ROPE_TILES = (0, 1, 3, 4)
AB_KB_TILE = 4
VMEM_LIMIT = 56 * 1024 * 1024

NEG_INF = float("-inf")
LOG2_E = 1.4426950408889634


def _cparams(sem):
    return pltpu.CompilerParams(dimension_semantics=sem, vmem_limit_bytes=VMEM_LIMIT)


def _nt(a, b, **kw):
    return lax.dot_general(a, b, (((1,), (1,)), ((), ())), preferred_element_type=F32, **kw)


def _tn(a, b, **kw):
    return lax.dot_general(a, b, (((0,), (0,)), ((), ())), preferred_element_type=F32, **kw)


def _rms_rows(x, g):
    ms = jnp.mean(x * x, axis=-1, keepdims=True)
    return x * lax.rsqrt(ms + EPS) * g


def _eye(n):
    return lax.broadcasted_iota(jnp.int32, (n, n), 0) == lax.broadcasted_iota(jnp.int32, (n, n), 1)


def _row_to_col(row, eye):
    return jnp.sum(jnp.where(eye, row, 0.0), axis=1, keepdims=True)


def _col_to_row(col, eye):
    return jnp.sum(jnp.where(eye, col, 0.0), axis=0, keepdims=True)


def _rowsum_rep(x, ones=None):
    if ones is None:
        ones = jnp.ones((x.shape[1], LANES), BF16)
    hi = x.astype(BF16)
    lo = (x - hi.astype(F32)).astype(BF16)
    return (jnp.dot(hi, ones, preferred_element_type=F32) + jnp.dot(lo, ones, preferred_element_type=F32))


def _log_sigmoid(x):
    return jnp.minimum(x, 0.0) - jnp.log1p(jnp.exp(-jnp.abs(x)))


def _sigmoid(x):
    return 1.0 / (1.0 + jnp.exp(-x))


def _gelu_tanh(x):
    c = 0.7978845608028654
    return (0.5 * x) * (1.0 + jnp.tanh(x * (c + (0.044715 * c) * (x * x))))


def _norm_matmul_kernel(x_ref, g_ref, w_ref, cos_ref, sin_ref, o_ref, xn_ref, *, rope_tiles, scale_tile, scale):
    j = pl.program_id(1)

    @pl.when(j == 0)
    def _():
        xn_ref[...] = _rms_rows(x_ref[...], g_ref[...]).astype(BF16)

    z = jnp.dot(xn_ref[...], w_ref[...], preferred_element_type=F32)
    if not rope_tiles:
        o_ref[...] = z
        return

    is_rope = functools.reduce(jnp.logical_or, [j == t for t in rope_tiles])

    @pl.when(is_rope)
    def _():
        sc = jnp.where(j == scale_tile, scale, 1.0).astype(F32)
        cos = cos_ref[...]
        sin = sin_ref[...]
        for c in range(z.shape[1] // LANES):
            cols = slice(c * LANES, (c + 1) * LANES)
            o_ref[:, cols] = _rope_lanes(z[:, cols], cos, sin) * sc

    @pl.when(jnp.logical_not(is_rope))
    def _():
        o_ref[...] = z


def _norm_matmul_side_kernel(x_ref, g_ref, w_ref, ws_ref, o_ref, os_ref, xn_ref):
    @pl.when(pl.program_id(1) == 0)
    def _():
        xn_ref[...] = _rms_rows(x_ref[...], g_ref[...]).astype(BF16)
        os_ref[...] = jnp.dot(xn_ref[...], ws_ref[...], preferred_element_type=F32)

    o_ref[...] = jnp.dot(xn_ref[...], w_ref[...], preferred_element_type=F32)


def _norm_matmul_side(x, g, w, n_cols, w_side, tm, tn):
    M, D = x.shape
    N = n_cols
    Ns = w_side.shape[1]
    assert M % tm == 0 and N % tn == 0
    return pl.pallas_call(
        _norm_matmul_side_kernel,
        grid=(M // tm, N // tn),
        in_specs=[pl.BlockSpec((tm, D), lambda i, j: (i, 0)),
                  pl.BlockSpec((1, D), lambda i, j: (0, 0)),
                  pl.BlockSpec((D, tn), lambda i, j: (0, j)),
                  pl.BlockSpec((D, Ns), lambda i, j: (0, 0))],
        out_specs=[pl.BlockSpec((tm, tn), lambda i, j: (i, j)),
                   pl.BlockSpec((tm, Ns), lambda i, j: (i, 0))],
        out_shape=[jax.ShapeDtypeStruct((M, N), F32), jax.ShapeDtypeStruct((M, Ns), F32)],
        scratch_shapes=[pltpu.VMEM((tm, D), BF16)],
        compiler_params=_cparams(("parallel", "arbitrary")),
        name="norm_matmul_side",
    )(x, g.reshape(1, D), w, w_side)


def _norm_matmul(x, g, w, tm, tn, rope=None):
    M, D = x.shape
    N = w.shape[1]
    assert M % tm == 0 and N % tn == 0
    if rope is None:
        cos = sin = jnp.zeros((8, LANES), F32)
        tab_spec = pl.BlockSpec((8, LANES), lambda i, j: (0, 0))
        kern = functools.partial(_norm_matmul_kernel, rope_tiles=(), scale_tile=-1, scale=1.0)
    else:
        cos, sin = rope
        nt = cos.shape[0] // tm
        tab_spec = pl.BlockSpec((tm, LANES), lambda i, j: (i % nt, 0))
        kern = functools.partial(_norm_matmul_kernel, rope_tiles=AB_ROPE_TILES, scale_tile=AB_KB_TILE,
                                 scale=DK_B ** -0.5)
    return pl.pallas_call(
        kern,
        grid=(M // tm, N // tn),
        in_specs=[pl.BlockSpec((tm, D), lambda i, j: (i, 0)),
                  pl.BlockSpec((1, D), lambda i, j: (0, 0)),
                  pl.BlockSpec((D, tn), lambda i, j: (0, j)),
                  tab_spec, tab_spec],
        out_specs=pl.BlockSpec((tm, tn), lambda i, j: (i, j)),
        out_shape=jax.ShapeDtypeStruct((M, N), F32),
        scratch_shapes=[pltpu.VMEM((tm, D), BF16)],
        compiler_params=_cparams(("parallel", "arbitrary")),
        name="norm_matmul",
    )(x, g.reshape(1, D), w, cos, sin)


def _proj_residual_kernel(*refs, n_in):
    a_refs = refs[:n_in]
    w_refs = refs[n_in:2 * n_in]
    res_ref = refs[2 * n_in]
    o_ref = refs[2 * n_in + 1]
    y = res_ref[...]
    acc = None
    for a_ref, w_ref in zip(a_refs, w_refs):
        d = jnp.dot(a_ref[...], w_ref[...], preferred_element_type=F32)
        acc = d if acc is None else acc + d
    o_ref[...] = y + acc


def _proj_residual(acts, ws, res, tm):
    M, D = res.shape
    n_in = len(acts)
    in_specs = ([pl.BlockSpec((tm, a.shape[1]), lambda i: (i, 0)) for a in acts]
                + [pl.BlockSpec(w.shape, lambda i: (0, 0)) for w in ws]
                + [pl.BlockSpec((tm, D), lambda i: (i, 0))])
    return pl.pallas_call(
        functools.partial(_proj_residual_kernel, n_in=n_in),
        grid=(M // tm,),
        in_specs=in_specs,
        out_specs=pl.BlockSpec((tm, D), lambda i: (i, 0)),
        out_shape=jax.ShapeDtypeStruct((M, D), F32),
        compiler_params=_cparams(("parallel",)),
        name="proj_residual",
    )(*acts, *ws, res)


PREV_ROWS = 16
FFN_CHUNK = 256


def _ffn_kernel(*refs, seq_mode, tiles_per_seq, final_norm):
    if seq_mode:
        (x_ref, xp_ref, g_ref, w1_ref, w3_ref, cw_ref, cb_ref, w2_ref, fg_ref, o_ref, a_ref, y_ref) = refs
    else:
        (x_ref, s0_ref, s1_ref, g_ref, w1_ref, w3_ref, cw_ref, cb_ref, w2_ref, fg_ref, o_ref, a_ref, y_ref) = refs
    i = pl.program_id(0)
    x = x_ref[...]
    tm = x.shape[0]
    F = w1_ref.shape[1]
    xn = _rms_rows(x, g_ref[...]).astype(BF16)
    if seq_mode:
        xpn = _rms_rows(xp_ref[...], g_ref[...]).astype(BF16)
        has_prev = ((i % tiles_per_seq) != 0).astype(F32)
        row8 = lax.broadcasted_iota(jnp.int32, (8, 1), 0)
    for c0 in range(0, F, FFN_CHUNK):
        cols = slice(c0, min(c0 + FFN_CHUNK, F))
        w1c = w1_ref[:, cols]
        a = jnp.dot(xn, w1c, preferred_element_type=F32)
        gate = jnp.dot(xn, w3_ref[:, cols], preferred_element_type=F32)
        a_ref[:, cols] = a[tm - a_ref.shape[0]:, :]
        if seq_mode:
            ap = jnp.dot(xpn, w1c, preferred_element_type=F32)
            p1 = ap[PREV_ROWS - 1:PREV_ROWS, :] * has_prev
            p2 = ap[PREV_ROWS - 2:PREV_ROWS - 1, :] * has_prev
            a1 = pltpu.roll(a, 1, 0)
            a2 = pltpu.roll(a, 2, 0)
            top1 = jnp.where(row8 == 0, p1, a1[0:8, :])
            top2 = jnp.where(row8 == 0, p2, jnp.where(row8 == 1, p1, a2[0:8, :]))
            a1 = jnp.concatenate([top1, a1[8:, :]], axis=0)
            a2 = jnp.concatenate([top2, a2[8:, :]], axis=0)
        else:
            a1 = s1_ref[:, cols]
            a2 = s0_ref[:, cols]
        ac = cb_ref[:, cols] + a2 * cw_ref[0:1, cols]
        ac = ac + a1 * cw_ref[1:2, cols]
        ac = ac + a * cw_ref[2:3, cols]
        y_ref[:, cols] = (_gelu_tanh(ac) * gate).astype(BF16)
    acc = x + jnp.dot(y_ref[...], w2_ref[...], preferred_element_type=F32)
    if final_norm:
        acc = _rms_rows(acc, fg_ref[...])
    o_ref[...] = acc


def _ffn(x, norm_g, layer, w1, w3, conv_w, conv_b, w2, final_g, tm, seq_len=None, state=None):
    M, D = x.shape
    F = w1.shape[2]
    seq_mode = state is None
    final_norm = final_g is not None
    fg = (final_g if final_norm else jnp.ones((D,), F32)).reshape(1, D)
    whole = lambda shape: pl.BlockSpec(shape, lambda i: (0, 0), pipeline_mode=pl.Buffered(1))
    stacked = lambda shape: pl.BlockSpec((None,) + shape, lambda i: (layer, 0, 0), pipeline_mode=pl.Buffered(1))
    common = [whole((1, D)), stacked((D, F)), stacked((D, F)), whole((CONV_W, F)), whole((1, F)), stacked((F, D)),
              whole((1, D))]
    common_args = (norm_g.reshape(1, D), w1, w3, conv_w, conv_b.reshape(1, F), w2, fg)
    x_spec = pl.BlockSpec((tm, D), lambda i: (i, 0))
    if seq_mode:
        assert seq_len % tm == 0 and tm % PREV_ROWS == 0
        r = tm // PREV_ROWS
        in_specs = [x_spec, pl.BlockSpec((PREV_ROWS, D), lambda i: (jnp.maximum(i * r - 1, 0), 0))] + common
        args = (x, x) + common_args
        tiles_per_seq = seq_len // tm
    else:
        s_spec = pl.BlockSpec((tm, F), lambda i: (i, 0))
        in_specs = [x_spec, s_spec, s_spec] + common
        args = (x, state[0], state[1]) + common_args
        tiles_per_seq = 1
    if seq_mode:
        a_spec = pl.BlockSpec((None, 8, F), lambda i: (i // tiles_per_seq, 0, 0))
        a_shape = jax.ShapeDtypeStruct((M // seq_len, 8, F), F32)
    else:
        a_spec = pl.BlockSpec((tm, F), lambda i: (i, 0))
        a_shape = jax.ShapeDtypeStruct((M, F), F32)
    return pl.pallas_call(
        functools.partial(_ffn_kernel, seq_mode=seq_mode, tiles_per_seq=tiles_per_seq, final_norm=final_norm),
        grid=(M // tm,),
        in_specs=in_specs,
        out_specs=[pl.BlockSpec((tm, D), lambda i: (i, 0)), a_spec],
        out_shape=[jax.ShapeDtypeStruct((M, D), F32), a_shape],
        scratch_shapes=[pltpu.VMEM((tm, F), BF16)],
        compiler_params=_cparams(("arbitrary",)),
        name="conv_ffn",
    )(*args)


def _rope_lanes(x, cos, sin):
    lane = lax.broadcasted_iota(jnp.int32, (1, LANES), 1)
    first_half = (lane % DH_A) < (DH_A // 2)
    partner = jnp.where(first_half, pltpu.roll(x, LANES - DH_A // 2, 1), pltpu.roll(x, DH_A // 2, 1))
    return x * cos + partner * sin


def _moba_prefill_kernel(q_ref, k_ref, v_ref, cos_ref, sin_ref, o_ref, kt_ref, vt_ref, kb_ref, vb_ref, *, nb):
    blk = MOBA_BLOCK
    heads = LANES // DH_A
    T = q_ref.shape[0]
    cos = cos_ref[...]
    sin = sin_ref[...]
    k = _rope_lanes(k_ref[...], cos, sin)
    kt_ref[...] = k.T
    vt_ref[...] = v_ref[...].T
    kb_ref[...] = k.astype(BF16)
    vb_ref[:, 0:LANES] = v_ref[...].astype(BF16)
    vb_ref[:, LANES:2 * LANES] = jnp.ones((T, LANES), BF16)
    kmean = jnp.concatenate(
        [jnp.sum(k[n * blk:(n + 1) * blk, :], axis=0, keepdims=True) * (1.0 / blk) for n in range(nb)]
        + [jnp.zeros((8 - nb, LANES), F32)] * (nb < 8), axis=0)

    q2 = _rope_lanes(q_ref[...], cos, sin)
    lane = lax.broadcasted_iota(jnp.int32, (1, LANES), 1)
    blk_id = lax.broadcasted_iota(jnp.int32, (8, 1), 0)
    q_blk = lax.broadcasted_iota(jnp.int32, (1, T), 1) // blk
    past = blk_id < q_blk
    causal = (lax.broadcasted_iota(jnp.int32, (blk, blk), 1) <= lax.broadcasted_iota(jnp.int32, (blk, blk), 0))
    hms, qss, sels = [], [], []
    for h in range(heads):
        hm = (lane // DH_A) == h
        qh = jnp.where(hm, q2, 0.0)
        gt = _nt(kmean, qh, precision=HIGHEST)
        sel_t = jnp.zeros_like(gt)
        for n in range(nb - 1):
            gn = gt[n:n + 1, :]
            beats = jnp.logical_and(past, jnp.logical_or(gt > gn, jnp.logical_and(gt == gn, blk_id < n)))
            rank = jnp.sum(beats.astype(F32), axis=0, keepdims=True)
            sel_n = jnp.where(jnp.logical_and(rank < MOBA_TOPK, n < q_blk), 1.0, 0.0)
            sel_t = jnp.where(blk_id == n, sel_n, sel_t)
        hms.append(hm)
        qss.append((qh * (DH_A ** -0.5 * LOG2_E)).astype(BF16))
        sels.append(jnp.where(sel_t.T > 0.5, 0.0, NEG_INF))

    for qi in range(nb):
        rows = slice(qi * blk, (qi + 1) * blk)
        outs = []
        for h in range(heads):
            qs = qss[h][rows, :]
            pieces = []
            for n in range(qi + 1):
                s = _nt(qs, kb_ref[n * blk:(n + 1) * blk, :])
                pieces.append(jnp.where(causal, s, NEG_INF) if n == qi else s + sels[h][rows, n:n + 1])
            m = functools.reduce(jnp.maximum, pieces)
            m = jnp.max(m, axis=-1, keepdims=True)
            p_all = jnp.concatenate([jnp.exp2((s - m).astype(BF16)) for s in pieces], axis=1)
            acc = jnp.dot(p_all, vb_ref[0:(qi + 1) * blk, :], preferred_element_type=F32)
            outs.append(acc[:, 0:LANES] / acc[:, LANES:2 * LANES])
        out = outs[0]
        for h in range(1, heads):
            out = jnp.where(hms[h], outs[h], out)
        o_ref[rows, :] = out.astype(o_ref.dtype)


def _moba_prefill(z, rope, B, T):
    blk = MOBA_BLOCK
    assert T % blk == 0
    nb = T // blk
    assert nb <= 8
    cpt = AB_TILE // LANES
    return pl.pallas_call(
        functools.partial(_moba_prefill_kernel, nb=nb),
        grid=(B, W_A // LANES),
        in_specs=[pl.BlockSpec((T, LANES), lambda b, p: (b, p)),
                  pl.BlockSpec((T, LANES), lambda b, p: (b, cpt + p)),
                  pl.BlockSpec((T, LANES), lambda b, p: (b, 2 * cpt + p)),
                  pl.BlockSpec((T, LANES), lambda b, p: (0, 0)),
                  pl.BlockSpec((T, LANES), lambda b, p: (0, 0))],
        out_specs=[pl.BlockSpec((T, LANES), lambda b, p: (b, p)),
                   pl.BlockSpec((None, LANES, T), lambda b, p: (b, p, 0)),
                   pl.BlockSpec((None, LANES, T), lambda b, p: (b, p, 0))],
        out_shape=[jax.ShapeDtypeStruct((B * T, W_A), BF16),
                   jax.ShapeDtypeStruct((B, W_A, T), F32),
                   jax.ShapeDtypeStruct((B, W_A, T), F32)],
        scratch_shapes=[pltpu.VMEM((T, LANES), BF16), pltpu.VMEM((T, 2 * LANES), BF16)],
        compiler_params=_cparams(("parallel", "parallel")),
        name="moba_prefill",
    )(z, z, z, rope[0], rope[1])


def _ret_prefill_kernel(q_ref, k_ref, v_ref, gb_ref, cos_ref, sin_ref, rg_ref, dmask_ref, din_ref, dout_ref,
                        gch_ref, o_ref, s_ref, *, n_chunks):
    C = RET_CHUNK
    heads = LANES // DK_B
    lane = lax.broadcasted_iota(jnp.int32, (1, LANES), 1)
    hms = [(lane // DK_B) == h for h in range(heads)]
    row_h = lax.broadcasted_iota(jnp.int32, (LANES, LANES), 0) // DK_B
    col_h = lax.broadcasted_iota(jnp.int32, (LANES, LANES), 1) // DV_B
    same_head = row_h == col_h
    seg_ones = jnp.where(same_head, 1.0, 0.0).astype(BF16)
    din = din_ref[...]
    dout = dout_ref[...]
    gch = gch_ref[...]
    rg = rg_ref[...]
    dmask = dmask_ref[...]

    S = jnp.zeros((LANES, LANES), F32)
    for j in range(n_chunks):
        rows = slice(j * C, (j + 1) * C)
        cos = cos_ref[rows, :]
        sin = sin_ref[rows, :]
        q = _rope_lanes(q_ref[rows, :], cos, sin)
        k = _rope_lanes(k_ref[rows, :], cos, sin) * (DK_B ** -0.5)
        kb = k.astype(BF16)
        vb = v_ref[rows, :].astype(BF16)
        o = jnp.dot((q * din).astype(BF16), S.astype(BF16), preferred_element_type=F32)
        q_st = jnp.concatenate([jnp.where(hm, q, 0.0) for hm in hms], axis=0).astype(BF16)
        att = _nt(q_st, kb) * dmask
        res = jnp.dot(att.astype(BF16), vb, preferred_element_type=F32)
        intra = res[0:C, :]
        for h in range(1, heads):
            intra = jnp.where(hms[h], res[h * C:(h + 1) * C, :], intra)
        o = o + intra
        S = S * gch + jnp.where(same_head, _tn((k * dout).astype(BF16), vb), 0.0)
        ms = _rowsum_rep(o * o, seg_ones) * (1.0 / DV_B)
        g = gb_ref[rows, :]
        y = o * lax.rsqrt(ms + EPS) * rg * (g * _sigmoid(g))
        o_ref[rows, :] = y.astype(o_ref.dtype)
    s_ref[...] = S


def _ret_tables(chunk):
    log_g = jnp.log1p(-jnp.exp2(-5.0 - jnp.arange(H_B, dtype=F32)))
    i = jnp.arange(chunk, dtype=F32)
    d_in = jnp.exp((i[:, None] + 1.0) * log_g)
    d_out = jnp.exp((chunk - 1.0 - i)[:, None] * log_g)
    diff = i[:, None] - i[None, :]
    d_mask = jnp.where(diff >= 0, jnp.exp(jnp.maximum(diff, 0.0)[None] * log_g[:, None, None]), 0.0)
    g_chunk = jnp.exp(chunk * log_g)
    return d_in, d_out, d_mask, g_chunk


def _ret_prefill(z, rope, ret_g, B, T):
    C = RET_CHUNK
    assert T % C == 0
    d_in, d_out, d_mask, g_chunk = _ret_tables(C)
    npair = W_B // LANES
    lanes = lambda t: jnp.repeat(t, DK_B, axis=-1)
    din_l = lanes(d_in).reshape(C, npair, LANES).transpose(1, 0, 2)
    dout_l = lanes(d_out).reshape(C, npair, LANES).transpose(1, 0, 2)
    gch_l = lanes(g_chunk).reshape(npair, 1, LANES)
    cpt = AB_TILE // LANES
    col = lambda t: (lambda b, p: (b, t * cpt + p))
    return pl.pallas_call(
        functools.partial(_ret_prefill_kernel, n_chunks=T // C),
        grid=(B, npair),
        in_specs=[pl.BlockSpec((T, LANES), col(3)), pl.BlockSpec((T, LANES), col(4)),
                  pl.BlockSpec((T, LANES), col(5)), pl.BlockSpec((T, LANES), col(6)),
                  pl.BlockSpec((T, LANES), lambda b, p: (0, 0)),
                  pl.BlockSpec((T, LANES), lambda b, p: (0, 0)),
                  pl.BlockSpec((1, LANES), lambda b, p: (0, p)),
                  pl.BlockSpec((None, (LANES // DK_B) * C, C), lambda b, p: (p, 0, 0)),
                  pl.BlockSpec((None, C, LANES), lambda b, p: (p, 0, 0)),
                  pl.BlockSpec((None, C, LANES), lambda b, p: (p, 0, 0)),
                  pl.BlockSpec((None, 1, LANES), lambda b, p: (p, 0, 0))],
        out_specs=[pl.BlockSpec((T, LANES), lambda b, p: (b, p)),
                   pl.BlockSpec((None, None, LANES, LANES), lambda b, p: (b, p, 0, 0))],
        out_shape=[jax.ShapeDtypeStruct((B * T, W_B), BF16),
                   jax.ShapeDtypeStruct((B, npair, LANES, LANES), F32)],
        compiler_params=_cparams(("parallel", "parallel")),
        name="retention_prefill",
    )(z, z, z, z, rope[0], rope[1], ret_g.reshape(1, W_B), d_mask.reshape(npair, -1, C), din_l, dout_l, gch_l)


def _mlstm_prefill_kernel(bias_ref, q_ref, k_ref, v_ref, og_ref, ig_ref, fg_ref, igc_ref, fgc_ref, ng_ref,
                          h_ref, c_ref, n_ref, m_ref, *, n_chunks):
    L = MLSTM_CHUNK
    hd = pl.program_id(1)
    r_id = lax.broadcasted_iota(jnp.int32, (L, L), 0)
    c_id = lax.broadcasted_iota(jnp.int32, (L, L), 1)
    causal = c_id <= r_id
    upper = jnp.where(r_id <= c_id, 1.0, 0.0).astype(F32)
    lower = jnp.where(c_id <= r_id, 1.0, 0.0).astype(F32)
    ones_b = jnp.ones((L, L), BF16)
    i_rows = ig_ref[...] + bias_ref[0, hd]
    b_rows = jnp.dot(_log_sigmoid(fg_ref[...] + bias_ref[1, hd]), upper, preferred_element_type=F32, precision=HIGHEST)
    i_cols = igc_ref[...] + bias_ref[0, hd]
    b_cols = jnp.dot(lower, _log_sigmoid(fgc_ref[...] + bias_ref[1, hd]), preferred_element_type=F32, precision=HIGHEST)
    ng = ng_ref[...]

    Cm = jnp.zeros((L, L), F32)
    n = jnp.zeros((1, L), F32)
    m = jnp.zeros((1, 1), F32)
    for j in range(n_chunks):
        rows = slice(j * L, (j + 1) * L)
        q = q_ref[rows, :]
        k = k_ref[rows, :] * (DH_C ** -0.5)
        v = v_ref[rows, :]
        b_row, i_row = b_rows[j:j + 1, :], i_rows[j:j + 1, :]
        b_col, i_col = b_cols[:, j:j + 1], i_cols[:, j:j + 1]
        b_last = b_row[:, L - 1:L]
        qb = q.astype(BF16)
        kb = k.astype(BF16)
        b_rep = jnp.broadcast_to(b_col, (L, L))
        log_d = jnp.where(causal, (b_rep - b_row) + i_row, NEG_INF)
        m_row = jnp.broadcast_to(jnp.max(log_d, axis=-1, keepdims=True), (L, L))
        s = _nt(qb, kb) * jnp.exp(log_d - m_row)
        sv_ext = jnp.dot(s.astype(BF16), jnp.concatenate([v.astype(BF16), ones_b], axis=1),
                         preferred_element_type=F32)
        sv, s_sum = sv_ext[:, 0:L], sv_ext[:, L:2 * L]
        log_w = (b_last - b_col) + i_col
        m_loc = jnp.max(log_w, axis=0, keepdims=True)
        e_k = jnp.broadcast_to(jnp.exp(log_w - m_loc), (L, L))
        U = _tn((v * e_k).astype(BF16), kb)
        nk = jnp.sum(k * e_k, axis=0, keepdims=True)
        log_inter = b_rep + m
        m_t = jnp.maximum(log_inter, m_row)
        f_intra = jnp.exp(m_row - m_t)
        w_inter = jnp.exp(log_inter - m_t)
        num = sv * f_intra + _nt(qb, Cm.astype(BF16)) * w_inter
        den = s_sum * f_intra + w_inter * _rowsum_rep(q * n)
        h = num / jnp.maximum(jnp.abs(den), jnp.exp(-m_t))
        m_new = jnp.maximum(b_last + m, m_loc)
        decay = jnp.exp(b_last + m - m_new)
        f_k = jnp.exp(m_loc - m_new)
        Cm = decay * Cm + f_k * U
        n = decay * n + f_k * nk
        m = m_new
        hn = h * lax.rsqrt(_rowsum_rep(h * h) * (1.0 / L) + EPS) * ng
        h_ref[rows, :] = (hn * _sigmoid(og_ref[rows, :])).astype(h_ref.dtype)
    c_ref[...] = Cm
    n_ref[...] = n
    m_ref[...] = jnp.broadcast_to(m, m_ref.shape)


def _mlstm_prefill(z, ig, fg, gate_bias, out_g, B, T):
    L = MLSTM_CHUNK
    assert T % L == 0 and DH_C == LANES
    nc = T // L
    col = lambda t: (lambda b, h, bias: (b, t * H_C + h))
    gspec = pl.BlockSpec((None, None, nc, L), lambda b, h, bias: (b, h, 0, 0))
    gcspec = pl.BlockSpec((None, None, L, nc), lambda b, h, bias: (b, h, 0, 0))
    grid_spec = pltpu.PrefetchScalarGridSpec(
        num_scalar_prefetch=1,
        grid=(B, H_C),
        in_specs=[pl.BlockSpec((T, LANES), col(0)), pl.BlockSpec((T, LANES), col(1)),
                  pl.BlockSpec((T, LANES), col(2)), pl.BlockSpec((T, LANES), col(3)),
                  gspec, gspec, gcspec, gcspec,
                  pl.BlockSpec((1, LANES), lambda b, h, bias: (0, h))],
        out_specs=[pl.BlockSpec((T, LANES), lambda b, h, bias: (b, h)),
                   pl.BlockSpec((None, None, L, L), lambda b, h, bias: (b, h, 0, 0)),
                   pl.BlockSpec((None, None, 1, L), lambda b, h, bias: (b, h, 0, 0)),
                   pl.BlockSpec((None, None, 1, LANES), lambda b, h, bias: (b, h, 0, 0))],
    )
    return pl.pallas_call(
        functools.partial(_mlstm_prefill_kernel, n_chunks=nc),
        grid_spec=grid_spec,
        out_shape=[jax.ShapeDtypeStruct((B * T, W_C), BF16),
                   jax.ShapeDtypeStruct((B, H_C, L, L), F32),
                   jax.ShapeDtypeStruct((B, H_C, 1, L), F32),
                   jax.ShapeDtypeStruct((B, H_C, 1, LANES), F32)],
        compiler_params=_cparams(("parallel", "parallel")),
        name="mlstm_prefill",
    )(gate_bias, z, z, z, z, ig, fg, ig.swapaxes(2, 3), fg.swapaxes(2, 3), out_g.reshape(1, W_C))


SELECT_PAGES = 16


def _moba_select_kernel(pt_ref, *refs, pages_per_block):
    k_refs = refs[:SELECT_PAGES]
    q_ref, sel_ref, gate_ref, qb_ref = refs[SELECT_PAGES:]
    s = pl.program_id(1)
    ppb = pages_per_block
    blocks_per_step = SELECT_PAGES // ppb
    H, dh, page = k_refs[0].shape

    @pl.when(s == 0)
    def _():
        eye = _eye(dh)
        q = q_ref[...]
        for h in range(H):
            qb_ref[h] = jnp.broadcast_to(_row_to_col(q[h:h + 1, :], eye), (dh, page))

    qb = qb_ref[...]
    for i in range(blocks_per_step):
        acc = k_refs[i * ppb][...]
        for g in range(1, ppb):
            acc = acc + k_refs[i * ppb + g][...]
        prod = acc * qb
        part = prod[:, 0:8, :]
        for r in range(1, prod.shape[1] // 8):
            part = part + prod[:, 8 * r:8 * (r + 1), :]
        gate_ref[:, s * blocks_per_step + i] = part

    @pl.when(s == pl.num_programs(1) - 1)
    def _():
        nb = gate_ref.shape[1]
        part_sum = jnp.sum(gate_ref[...], axis=2).reshape(H * nb, page)
        col = jnp.sum(part_sum, axis=-1, keepdims=True) * (1.0 / MOBA_BLOCK)
        eye = _eye(nb)
        head_row = lax.broadcasted_iota(jnp.int32, (H, 1), 0)
        gate = jnp.zeros((H, nb), F32)
        for h in range(H):
            gate = jnp.where(head_row == h, _col_to_row(col[h * nb:(h + 1) * nb, :], eye), gate)
        blk_id = lax.broadcasted_iota(jnp.int32, (1, nb), 1)
        rank = jnp.zeros(gate.shape, jnp.int32)
        for m in range(nb):
            gm = gate[:, m:m + 1]
            beats = jnp.logical_or(gm > gate, jnp.logical_and(gm == gate, m < blk_id))
            rank = rank + jnp.where(beats, 1, 0)
        out_col = lax.broadcasted_iota(jnp.int32, sel_ref.shape, 1)
        out = jnp.zeros(sel_ref.shape, jnp.int32)
        for r in range(MOBA_TOPK):
            idx = jnp.sum(jnp.where(rank == r, blk_id, 0), axis=1, keepdims=True)
            out = jnp.where(out_col == r, idx, out)
        sel_ref[...] = out


def _moba_select(cache_kt, page_table, q):
    Bs, n_pages = page_table.shape
    _, H, dh, page = cache_kt.shape
    ppb = MOBA_BLOCK // page
    nb = n_pages // ppb
    assert nb >= MOBA_TOPK and n_pages % SELECT_PAGES == 0 and SELECT_PAGES % ppb == 0

    def page_spec(i):
        return pl.BlockSpec((None, H, dh, page), lambda b, s, pt: (pt[b, s * SELECT_PAGES + i], 0, 0, 0))

    grid_spec = pltpu.PrefetchScalarGridSpec(
        num_scalar_prefetch=1,
        grid=(Bs, n_pages // SELECT_PAGES),
        in_specs=[page_spec(i) for i in range(SELECT_PAGES)]
        + [pl.BlockSpec((None, H, dh), lambda b, s, pt: (b, 0, 0))],
        out_specs=pl.BlockSpec((None, H, 8), lambda b, s, pt: (b, 0, 0)),
        scratch_shapes=[pltpu.VMEM((H, nb, 8, page), F32), pltpu.VMEM((H, dh, page), F32)],
    )
    return pl.pallas_call(
        functools.partial(_moba_select_kernel, pages_per_block=ppb),
        grid_spec=grid_spec,
        out_shape=jax.ShapeDtypeStruct((Bs, H, 8), jnp.int32),
        compiler_params=_cparams(("parallel", "arbitrary")),
        name="moba_select",
    )(page_table, *([cache_kt] * SELECT_PAGES), q)


def _moba_decode_kernel(pt_ref, sel_ref, q_ref, kn_ref, vn_ref, ck_ref, cv_ref, o_ref, kbuf, vbuf, sem,
                        *, pages_per_block, page):
    b = pl.program_id(0)
    nb_ = pl.num_programs(0)
    H = q_ref.shape[0]
    ppb = pages_per_block

    def copies(bb, slot):
        out = []
        for h in range(H):
            for r in range(MOBA_TOPK):
                blk = sel_ref[bb, h, r]
                for g in range(ppb):
                    pg = pt_ref[bb, blk * ppb + g]
                    dst = pl.ds((r * ppb + g) * page, page)
                    out.append(pltpu.make_async_copy(ck_ref.at[pg, h], kbuf.at[slot, h, :, dst], sem.at[slot, 0]))
                    out.append(pltpu.make_async_copy(cv_ref.at[pg, h], vbuf.at[slot, h, :, dst], sem.at[slot, 1]))
        return out

    slot = b % 2

    @pl.when(b == 0)
    def _():
        for c in copies(b, slot):
            c.start()

    @pl.when(b + 1 < nb_)
    def _():
        for c in copies(b + 1, 1 - slot):
            c.start()

    for c in copies(b, slot):
        c.wait()

    row = lax.broadcasted_iota(jnp.int32, (H, 1), 0)
    out = jnp.zeros(o_ref.shape, F32)
    q = q_ref[...]
    kn = kn_ref[...]
    vn = vn_ref[...]
    for h in range(H):
        qh = q[h:h + 1, :] * (DH_A ** -0.5)
        q8 = jnp.broadcast_to(qh, (8, qh.shape[1])).astype(BF16)
        kh = kbuf[slot, h].astype(BF16)
        vh = vbuf[slot, h].astype(BF16)
        s = jnp.dot(q8, kh, preferred_element_type=F32)[0:1, :]
        s_self = jnp.sum(qh * kn[h:h + 1, :], axis=-1, keepdims=True)
        m = jnp.maximum(jnp.max(s, axis=-1, keepdims=True), s_self)
        p = jnp.exp(s - m)
        p_self = jnp.exp(s_self - m)
        l = jnp.sum(p, axis=-1, keepdims=True) + p_self
        p8 = jnp.broadcast_to(p, (8, p.shape[1])).astype(BF16)
        pv = _nt(p8, vh)[0:1, :]
        oh = (pv + p_self * vn[h:h + 1, :]) / l
        out = jnp.where(row == h, oh, out)
    o_ref[...] = out


def _moba_decode(cache_kt, cache_vt, page_table, sel, q, k_new, v_new):
    Bs, n_pages = page_table.shape
    _, H, dh, page = cache_kt.shape
    ppb = MOBA_BLOCK // page
    rows = MOBA_TOPK * MOBA_BLOCK
    vec = pl.BlockSpec((None, H, dh), lambda b, pt, sl: (b, 0, 0))
    grid_spec = pltpu.PrefetchScalarGridSpec(
        num_scalar_prefetch=2,
        grid=(Bs,),
        in_specs=[vec, vec, vec, pl.BlockSpec(memory_space=pl.ANY), pl.BlockSpec(memory_space=pl.ANY)],
        out_specs=vec,
        scratch_shapes=[pltpu.VMEM((2, H, dh, rows), F32), pltpu.VMEM((2, H, dh, rows), F32),
                        pltpu.SemaphoreType.DMA((2, 2))],
    )
    return pl.pallas_call(
        functools.partial(_moba_decode_kernel, pages_per_block=ppb, page=page),
        grid_spec=grid_spec,
        out_shape=jax.ShapeDtypeStruct((Bs, H, dh), F32),
        compiler_params=_cparams(("arbitrary",)),
        name="moba_decode",
    )(page_table, sel, q, k_new, v_new, cache_kt, cache_vt)


DECODE_ROWS = 4


def _per_row(kernel, shared):
    def wrapped(*refs):
        rows = next(r.shape[0] for i, r in enumerate(refs) if i not in shared)
        for r in range(rows):
            kernel(*[ref if i in shared else ref.at[r] for i, ref in enumerate(refs)])
    return wrapped


def _ret_decode_kernel(q_ref, k_ref, v_ref, gb_ref, rg_ref, g_ref, s_ref, o_ref, so_ref):
    H, dk = q_ref.shape
    eye = _eye(dk)
    row = lax.broadcasted_iota(jnp.int32, (H, 1), 0)
    q = q_ref[...]
    k = k_ref[...]
    v = v_ref[...]
    g_all = g_ref[...]
    out = jnp.zeros(o_ref.shape, F32)
    for h in range(H):
        qh, kh, vh = q[h:h + 1, :], k[h:h + 1, :], v[h:h + 1, :]
        g = g_all[h:h + 1, 0:1]
        S = s_ref[h]
        att = jnp.sum(qh * kh, axis=-1, keepdims=True)
        cross = jnp.sum(_row_to_col(qh * g, eye) * S, axis=0, keepdims=True)
        out = jnp.where(row == h, att * vh + cross, out)
        so_ref[h] = S * g + _row_to_col(kh, eye) * vh
    gb = gb_ref[...]
    y = out * lax.rsqrt(jnp.mean(out * out, axis=-1, keepdims=True) + EPS) * rg_ref[...]
    o_ref[...] = y * (gb * _sigmoid(gb))


def _ret_decode(q, k, v, gb, ret_g, state):
    Bs, H, dk = q.shape
    log_g = jnp.log1p(-jnp.exp2(-5.0 - jnp.arange(H_B, dtype=F32)))
    g = jnp.broadcast_to(jnp.exp(1.0 * log_g)[:, None], (H, LANES))
    R = DECODE_ROWS if Bs % DECODE_ROWS == 0 else 1
    vec = pl.BlockSpec((R, H, dk), lambda b: (b, 0, 0))
    st = pl.BlockSpec((R, H, dk, DV_B), lambda b: (b, 0, 0, 0))
    return pl.pallas_call(
        _per_row(_ret_decode_kernel, shared=(4, 5)),
        grid=(Bs // R,),
        in_specs=[vec, vec, vec, vec, pl.BlockSpec((H, dk), lambda b: (0, 0)),
                  pl.BlockSpec((H, LANES), lambda b: (0, 0)), st],
        out_specs=[vec, st],
        out_shape=[jax.ShapeDtypeStruct((Bs, H, DV_B), F32), jax.ShapeDtypeStruct(state.shape, F32)],
        compiler_params=_cparams(("parallel",)),
        name="retention_decode",
    )(q, k, v, gb, ret_g.reshape(H, DV_B), g, state)


def _mlstm_decode_kernel(q_ref, k_ref, v_ref, og_ref, ig_ref, fg_ref, bi_ref, bf_ref, ng_ref, c_ref, n_ref,
                         m_ref, h_ref, co_ref, no_ref, mo_ref):
    H, d = q_ref.shape
    eye = _eye(d)
    row = lax.broadcasted_iota(jnp.int32, (H, 1), 0)
    q = q_ref[...]
    k = k_ref[...] * (DH_C ** -0.5)
    v = v_ref[...]
    n0 = n_ref[...]
    ig = ig_ref[...] + bi_ref[...]
    b = _log_sigmoid(fg_ref[...] + bf_ref[...])
    m0 = m_ref[...]
    log_inter = b + m0
    m_t = jnp.maximum(log_inter, ig)
    w_intra = jnp.exp(ig - m_t)
    w_inter = jnp.exp(log_inter - m_t)
    s = jnp.sum(q * k, axis=-1, keepdims=True) * w_intra
    den = s + w_inter * jnp.sum(q * n0, axis=-1, keepdims=True)
    scale = 1.0 / jnp.maximum(jnp.abs(den), jnp.exp(-m_t))
    w_k = w_intra
    decay = w_inter
    hs = jnp.zeros(h_ref.shape, F32)
    for h in range(H):
        qh, kh, vh = q[h:h + 1, :], k[h:h + 1, :], v[h:h + 1, :]
        Cm = c_ref[h]
        cq = _col_to_row(jnp.sum(Cm * qh, axis=1, keepdims=True), eye)
        num = s[h:h + 1, :] * vh + cq * w_inter[h:h + 1, :]
        hs = jnp.where(row == h, num * scale[h:h + 1, :], hs)
        co_ref[h] = decay[h:h + 1, :] * Cm + _row_to_col(vh * w_k[h:h + 1, :], eye) * kh
    no_ref[...] = decay * n0 + k * w_k
    mo_ref[...] = m_t
    hn = hs * lax.rsqrt(jnp.mean(hs * hs, axis=-1, keepdims=True) + EPS) * ng_ref[...]
    h_ref[...] = hn * _sigmoid(og_ref[...])


def _mlstm_decode(q, k, v, og, ig, fg, gate_bias, out_g, C0, n0, m0):
    Bs, H, d = q.shape
    R = DECODE_ROWS if Bs % DECODE_ROWS == 0 else 1
    vec = pl.BlockSpec((R, H, d), lambda b: (b, 0, 0))
    sc = pl.BlockSpec((R, H, 1), lambda b: (b, 0, 0))
    st = pl.BlockSpec((R, H, d, d), lambda b: (b, 0, 0, 0))
    bias = pl.BlockSpec((H, 1), lambda b: (0, 0))
    return pl.pallas_call(
        _per_row(_mlstm_decode_kernel, shared=(6, 7, 8)),
        grid=(Bs // R,),
        in_specs=[vec, vec, vec, vec, sc, sc, bias, bias, pl.BlockSpec((H, d), lambda b: (0, 0)), st, vec, sc],
        out_specs=[vec, st, vec, sc],
        out_shape=[jax.ShapeDtypeStruct((Bs, H, d), F32), jax.ShapeDtypeStruct(C0.shape, F32),
                   jax.ShapeDtypeStruct((Bs, H, d), F32), jax.ShapeDtypeStruct((Bs, H, 1), F32)],
        compiler_params=_cparams(("parallel",)),
        name="mlstm_decode",
    )(q, k, v, og, ig, fg, gate_bias[0].reshape(H, 1), gate_bias[1].reshape(H, 1), out_g.reshape(H, d), C0, n0, m0)


def _rope_tables(pos):
    half = DH_A // 2
    inv = ROPE_THETA ** (-jnp.arange(half, dtype=F32) / half)
    ang = pos[:, None] * inv[None, :]
    cos = jnp.cos(ang)
    sin = jnp.sin(ang)
    reps = LANES // DH_A
    cos_t = jnp.tile(jnp.concatenate([cos, cos], axis=-1), (1, reps))
    sin_t = jnp.tile(jnp.concatenate([-sin, sin], axis=-1), (1, reps))
    return cos_t, sin_t


def _row_tile(m, pref):
    return pref if m % pref == 0 else m


def kernel(x_prompt, x_sample, cache_k, cache_v, page_table, state_ret, state_mlstm_C, state_mlstm_n,
           state_mlstm_m, state_ffn_conv, ab_norm_g, ab_w_in, ab_ret_norm_g, ab_w_out, c_norm_g, c_w_in,
           c_gate_bias, c_out_norm_g, c_w_out, ffn_norm_g, ffn_w1, ffn_w3, ffn_conv_w, ffn_conv_b, ffn_w2,
           final_norm_g):
    Bp, Tp, D = x_prompt.shape
    Bs, Ts, _ = x_sample.shape
    assert Ts == 1
    past_len = page_table.shape[1] * cache_k.shape[2]
    Mp = Bp * Tp
    xp = x_prompt.reshape(Mp, D)
    xs = x_sample.reshape(Bs, D)
    tm_p = _row_tile(Tp, 1024)
    tm_in = _row_tile(Tp, 2048)
    tm_ffn = _row_tile(Tp, 1024)
    tm_s = Bs
    rope_p = _rope_tables(jnp.arange(Tp, dtype=jnp.int32).astype(F32))
    rope_s = _rope_tables(jnp.full((Bs,), past_len, jnp.int32).astype(F32))
    gw = 2 * H_C

    outs = {}

    w_in = ab_w_in[0].astype(BF16)
    w_out = ab_w_out[0].astype(BF16)
    w_out_a, w_out_b = w_out[:W_A], w_out[W_A:]

    z = _norm_matmul(xp, ab_norm_g[0], w_in, tm_in, AB_TILE)
    oa, kt, vt = _moba_prefill(z, rope_p, Bp, Tp)
    ob, s_pair = _ret_prefill(z, rope_p, ab_ret_norm_g[0], Bp, Tp)
    xp = _proj_residual([oa, ob], [w_out_a, w_out_b], xp, tm_p)
    outs["k_prompt"] = kt.reshape(Bp, H_A, DH_A, Tp).transpose(0, 3, 1, 2)[None]
    outs["v_prompt"] = vt.reshape(Bp, H_A, DH_A, Tp).transpose(0, 3, 1, 2)[None]
    hb = LANES // DK_B
    s_heads = jnp.stack([s_pair[:, :, i * DK_B:(i + 1) * DK_B, i * DV_B:(i + 1) * DV_B] for i in range(hb)], axis=2)
    outs["ret_prompt"] = s_heads.reshape(1, Bp, H_B, DK_B, DV_B)

    zs = _norm_matmul(xs, ab_norm_g[0], w_in, tm_s, AB_TILE, rope=rope_s)
    seg = lambda t: zs[:, t * AB_TILE:(t + 1) * AB_TILE].reshape(Bs, H_A, DH_A)
    qa_s, ka_s, va_s, qb_s, kb_s, vb_s, gb_s = (seg(t) for t in range(7))
    cache_kt = cache_k[0].transpose(0, 2, 3, 1)
    cache_vt = cache_v[0].transpose(0, 2, 3, 1)
    sel = _moba_select(cache_kt, page_table, qa_s)
    oa_s = _moba_decode(cache_kt, cache_vt, page_table, sel, qa_s, ka_s, va_s)
    ob_s, s_new = _ret_decode(qb_s, kb_s, vb_s, gb_s, ab_ret_norm_g[0], state_ret[0])
    xs = _proj_residual([oa_s.reshape(Bs, W_A).astype(BF16), ob_s.reshape(Bs, W_B).astype(BF16)],
                        [w_out_a, w_out_b], xs, tm_s)
    outs["k_sample"] = ka_s.reshape(1, Bs, 1, H_A, DH_A)
    outs["v_sample"] = va_s.reshape(1, Bs, 1, H_A, DH_A)
    outs["ret_sample"] = s_new[None]

    conv_p, conv_s = [], []

    w1_b, w3_b, w2_b = ffn_w1.astype(BF16), ffn_w3.astype(BF16), ffn_w2.astype(BF16)

    def ffn_both(l, xp, xs, final_g):
        st = state_ffn_conv[l]
        xp, a_tail = _ffn(xp, ffn_norm_g[l], l, w1_b, w3_b, ffn_conv_w[l], ffn_conv_b[l], w2_b, final_g, tm_ffn,
                          seq_len=Tp)
        xs, a_s = _ffn(xs, ffn_norm_g[l], l, w1_b, w3_b, ffn_conv_w[l], ffn_conv_b[l], w2_b, final_g, tm_s,
                       state=(st[:, 0, :], st[:, 1, :]))
        conv_p.append(a_tail[:, 8 - (CONV_W - 1):, :])
        conv_s.append(jnp.stack([st[:, 1, :], a_s], axis=1))
        return xp, xs

    xp, xs = ffn_both(0, xp, xs, None)

    w_in = c_w_in[0]
    w_all = w_in.astype(BF16)
    w_gate = jnp.pad(w_in[:, 4 * W_C:], ((0, 0), (0, LANES - gw))).astype(BF16)
    w_out = c_w_out[0].astype(BF16)
    L = MLSTM_CHUNK

    z, zg = _norm_matmul_side(xp, c_norm_g[0], w_all, 4 * W_C, w_gate, tm_in, 1024)
    gates = zg[:, :gw].reshape(Bp, Tp // L, L, 2, H_C).transpose(3, 0, 4, 1, 2)
    h, C_p, n_p, m_p = _mlstm_prefill(z, gates[0], gates[1], c_gate_bias[0], c_out_norm_g[0], Bp, Tp)
    xp = _proj_residual([h], [w_out], xp, tm_p)
    outs["C_prompt"] = C_p[None]
    outs["n_prompt"] = n_p.reshape(1, Bp, H_C, DH_C)
    outs["m_prompt"] = m_p[:, :, 0, 0][None]

    zs, zgs = _norm_matmul_side(xs, c_norm_g[0], w_all, 4 * W_C, w_gate, tm_s, 1024)
    segc = lambda t: zs[:, t * W_C:(t + 1) * W_C].reshape(Bs, H_C, DH_C)
    ig_s = zgs[:, :H_C][:, :, None]
    fg_s = zgs[:, H_C:gw][:, :, None]
    h_s, C_s, n_s, m_s = _mlstm_decode(segc(0), segc(1), segc(2), segc(3), ig_s, fg_s, c_gate_bias[0], c_out_norm_g[0],
                                       state_mlstm_C[0], state_mlstm_n[0], state_mlstm_m[0][:, :, None])
    xs = _proj_residual([h_s.reshape(Bs, W_C).astype(BF16)], [w_out], xs, tm_s)
    outs["C_sample"] = C_s[None]
    outs["n_sample"] = n_s[None]
    outs["m_sample"] = m_s[:, :, 0][None]

    xp, xs = ffn_both(1, xp, xs, final_norm_g)

    return (xp.reshape(Bp, Tp, D), xs.reshape(Bs, 1, D),
            outs["k_prompt"], outs["v_prompt"], outs["k_sample"], outs["v_sample"],
            outs["ret_prompt"], outs["ret_sample"], outs["C_prompt"], outs["C_sample"],
            outs["n_prompt"], outs["n_sample"], outs["m_prompt"], outs["m_sample"],
            jnp.stack(conv_p), jnp.stack(conv_s))
```

```python
import functools

import jax
import jax.numpy as jnp
from jax import lax
from jax.experimental import pallas as pl
from jax.experimental.pallas import tpu as pltpu

F32 = jnp.float32
BF16 = jnp.bfloat16
HIGHEST = lax.Precision.HIGHEST

LANES = 128
D_MODEL = 1024
H_A = 8
DH_A = 64
MOBA_BLOCK = 256
MOBA_TOPK = 3
H_B = 8
DK_B = 64
DV_B = 64
RET_CHUNK = 128
H_C = 8
DH_C = D_MODEL // H_C
MLSTM_CHUNK = 128
D_FF = 11 * D_MODEL // 4
CONV_W = 3
ROPE_THETA = 10000.0
EPS = 1e-6
W_A = H_A * DH_A
W_B = H_B * DV_B
W_C = H_C * DH_C
IN_AB = 3 * W_A + 2 * H_B * DK_B + 2 * W_B
AB_TILE = 512
AB_ROPE_TILES = (0, 1, 3, 4)
AB_KB_TILE = 4
VMEM_LIMIT = 56 * 1024 * 1024

NEG_INF = float("-inf")
LOG2_E = 1.4426950408889634


def _cparams(sem):
    return pltpu.CompilerParams(dimension_semantics=sem, vmem_limit_bytes=VMEM_LIMIT)


def _nt(a, b, **kw):
    return lax.dot_general(a, b, (((1,), (1,)), ((), ())), preferred_element_type=F32, **kw)


def _tn(a, b, **kw):
    return lax.dot_general(a, b, (((0,), (0,)), ((), ())), preferred_element_type=F32, **kw)


def _rms_rows(x, g):
    ms = jnp.mean(x * x, axis=-1, keepdims=True)
    return x * lax.rsqrt(ms + EPS) * g


def _eye(n):
    return lax.broadcasted_iota(jnp.int32, (n, n), 0) == lax.broadcasted_iota(jnp.int32, (n, n), 1)


def _row_to_col(row, eye):
    return jnp.sum(jnp.where(eye, row, 0.0), axis=1, keepdims=True)


def _col_to_row(col, eye):
    return jnp.sum(jnp.where(eye, col, 0.0), axis=0, keepdims=True)


def _rowsum_rep(x, ones=None):
    if ones is None:
        ones = jnp.ones((x.shape[1], LANES), BF16)
    hi = x.astype(BF16)
    lo = (x - hi.astype(F32)).astype(BF16)
    return (jnp.dot(hi, ones, preferred_element_type=F32) + jnp.dot(lo, ones, preferred_element_type=F32))


def _log_sigmoid(x):
    return jnp.minimum(x, 0.0) - jnp.log1p(jnp.exp(-jnp.abs(x)))


def _sigmoid(x):
    return 1.0 / (1.0 + jnp.exp(-x))


def _gelu_tanh(x):
    c = 0.7978845608028654
    return (0.5 * x) * (1.0 + jnp.tanh(x * (c + (0.044715 * c) * (x * x))))


def _norm_matmul_kernel(x_ref, g_ref, w_ref, cos_ref, sin_ref, o_ref, xn_ref, *, rope_tiles, scale_tile, scale):
    j = pl.program_id(1)

    @pl.when(j == 0)
    def _():
        xn_ref[...] = _rms_rows(x_ref[...], g_ref[...]).astype(BF16)

    z = jnp.dot(xn_ref[...], w_ref[...], preferred_element_type=F32)
    if not rope_tiles:
        o_ref[...] = z
        return

    is_rope = functools.reduce(jnp.logical_or, [j == t for t in rope_tiles])

    @pl.when(is_rope)
    def _():
        sc = jnp.where(j == scale_tile, scale, 1.0).astype(F32)
        cos = cos_ref[...]
        sin = sin_ref[...]
        for c in range(z.shape[1] // LANES):
            cols = slice(c * LANES, (c + 1) * LANES)
            o_ref[:, cols] = _rope_lanes(z[:, cols], cos, sin) * sc

    @pl.when(jnp.logical_not(is_rope))
    def _():
        o_ref[...] = z


def _norm_matmul_side_kernel(x_ref, g_ref, w_ref, ws_ref, o_ref, os_ref, xn_ref):
    @pl.when(pl.program_id(1) == 0)
    def _():
        xn_ref[...] = _rms_rows(x_ref[...], g_ref[...]).astype(BF16)
        os_ref[...] = jnp.dot(xn_ref[...], ws_ref[...], preferred_element_type=F32)

    o_ref[...] = jnp.dot(xn_ref[...], w_ref[...], preferred_element_type=F32)


def _norm_matmul_side(x, g, w, n_cols, w_side, tm, tn):
    M, D = x.shape
    N = n_cols
    Ns = w_side.shape[1]
    assert M % tm == 0 and N % tn == 0
    return pl.pallas_call(
        _norm_matmul_side_kernel,
        grid=(M // tm, N // tn),
        in_specs=[pl.BlockSpec((tm, D), lambda i, j: (i, 0)),
                  pl.BlockSpec((1, D), lambda i, j: (0, 0)),
                  pl.BlockSpec((D, tn), lambda i, j: (0, j)),
                  pl.BlockSpec((D, Ns), lambda i, j: (0, 0))],
        out_specs=[pl.BlockSpec((tm, tn), lambda i, j: (i, j)),
                   pl.BlockSpec((tm, Ns), lambda i, j: (i, 0))],
        out_shape=[jax.ShapeDtypeStruct((M, N), F32), jax.ShapeDtypeStruct((M, Ns), F32)],
        scratch_shapes=[pltpu.VMEM((tm, D), BF16)],
        compiler_params=_cparams(("parallel", "arbitrary")),
        name="norm_matmul_side",
    )(x, g.reshape(1, D), w, w_side)


def _norm_matmul(x, g, w, tm, tn, rope=None):
    M, D = x.shape
    N = w.shape[1]
    assert M % tm == 0 and N % tn == 0
    if rope is None:
        cos = sin = jnp.zeros((8, LANES), F32)
        tab_spec = pl.BlockSpec((8, LANES), lambda i, j: (0, 0))
        kern = functools.partial(_norm_matmul_kernel, rope_tiles=(), scale_tile=-1, scale=1.0)
    else:
        cos, sin = rope
        nt = cos.shape[0] // tm
        tab_spec = pl.BlockSpec((tm, LANES), lambda i, j: (i % nt, 0))
        kern = functools.partial(_norm_matmul_kernel, rope_tiles=AB_ROPE_TILES, scale_tile=AB_KB_TILE,
                                 scale=DK_B ** -0.5)
    return pl.pallas_call(
        kern,
        grid=(M // tm, N // tn),
        in_specs=[pl.BlockSpec((tm, D), lambda i, j: (i, 0)),
                  pl.BlockSpec((1, D), lambda i, j: (0, 0)),
                  pl.BlockSpec((D, tn), lambda i, j: (0, j)),
                  tab_spec, tab_spec],
        out_specs=pl.BlockSpec((tm, tn), lambda i, j: (i, j)),
        out_shape=jax.ShapeDtypeStruct((M, N), F32),
        scratch_shapes=[pltpu.VMEM((tm, D), BF16)],
        compiler_params=_cparams(("parallel", "arbitrary")),
        name="norm_matmul",
    )(x, g.reshape(1, D), w, cos, sin)


def _proj_residual_kernel(*refs, n_in):
    a_refs = refs[:n_in]
    w_refs = refs[n_in:2 * n_in]
    res_ref = refs[2 * n_in]
    o_ref = refs[2 * n_in + 1]
    y = res_ref[...]
    acc = None
    for a_ref, w_ref in zip(a_refs, w_refs):
        d = jnp.dot(a_ref[...], w_ref[...], preferred_element_type=F32)
        acc = d if acc is None else acc + d
    o_ref[...] = y + acc


def _proj_residual(acts, ws, res, tm):
    M, D = res.shape
    n_in = len(acts)
    in_specs = ([pl.BlockSpec((tm, a.shape[1]), lambda i: (i, 0)) for a in acts]
                + [pl.BlockSpec(w.shape, lambda i: (0, 0)) for w in ws]
                + [pl.BlockSpec((tm, D), lambda i: (i, 0))])
    return pl.pallas_call(
        functools.partial(_proj_residual_kernel, n_in=n_in),
        grid=(M // tm,),
        in_specs=in_specs,
        out_specs=pl.BlockSpec((tm, D), lambda i: (i, 0)),
        out_shape=jax.ShapeDtypeStruct((M, D), F32),
        compiler_params=_cparams(("parallel",)),
        name="proj_residual",
    )(*acts, *ws, res)


PREV_ROWS = 16
FFN_CHUNK = 256


def _ffn_kernel(*refs, seq_mode, tiles_per_seq, final_norm):
    if seq_mode:
        (x_ref, xp_ref, g_ref, w1_ref, w3_ref, cw_ref, cb_ref, w2_ref, fg_ref, o_ref, a_ref, y_ref) = refs
    else:
        (x_ref, s0_ref, s1_ref, g_ref, w1_ref, w3_ref, cw_ref, cb_ref, w2_ref, fg_ref, o_ref, a_ref, y_ref) = refs
    i = pl.program_id(0)
    x = x_ref[...]
    tm = x.shape[0]
    F = w1_ref.shape[1]
    xn = _rms_rows(x, g_ref[...]).astype(BF16)
    if seq_mode:
        xpn = _rms_rows(xp_ref[...], g_ref[...]).astype(BF16)
        has_prev = ((i % tiles_per_seq) != 0).astype(F32)
        row8 = lax.broadcasted_iota(jnp.int32, (8, 1), 0)
    for c0 in range(0, F, FFN_CHUNK):
        cols = slice(c0, min(c0 + FFN_CHUNK, F))
        w1c = w1_ref[:, cols]
        a = jnp.dot(xn, w1c, preferred_element_type=F32)
        gate = jnp.dot(xn, w3_ref[:, cols], preferred_element_type=F32)
        a_ref[:, cols] = a[tm - a_ref.shape[0]:, :]
        if seq_mode:
            ap = jnp.dot(xpn, w1c, preferred_element_type=F32)
            p1 = ap[PREV_ROWS - 1:PREV_ROWS, :] * has_prev
            p2 = ap[PREV_ROWS - 2:PREV_ROWS - 1, :] * has_prev
            a1 = pltpu.roll(a, 1, 0)
            a2 = pltpu.roll(a, 2, 0)
            top1 = jnp.where(row8 == 0, p1, a1[0:8, :])
            top2 = jnp.where(row8 == 0, p2, jnp.where(row8 == 1, p1, a2[0:8, :]))
            a1 = jnp.concatenate([top1, a1[8:, :]], axis=0)
            a2 = jnp.concatenate([top2, a2[8:, :]], axis=0)
        else:
            a1 = s1_ref[:, cols]
            a2 = s0_ref[:, cols]
        ac = cb_ref[:, cols] + a2 * cw_ref[0:1, cols]
        ac = ac + a1 * cw_ref[1:2, cols]
        ac = ac + a * cw_ref[2:3, cols]
        y_ref[:, cols] = (_gelu_tanh(ac) * gate).astype(BF16)
    acc = x + jnp.dot(y_ref[...], w2_ref[...], preferred_element_type=F32)
    if final_norm:
        acc = _rms_rows(acc, fg_ref[...])
    o_ref[...] = acc


def _ffn(x, norm_g, layer, w1, w3, conv_w, conv_b, w2, final_g, tm, seq_len=None, state=None):
    M, D = x.shape
    F = w1.shape[2]
    seq_mode = state is None
    final_norm = final_g is not None
    fg = (final_g if final_norm else jnp.ones((D,), F32)).reshape(1, D)
    whole = lambda shape: pl.BlockSpec(shape, lambda i: (0, 0), pipeline_mode=pl.Buffered(1))
    stacked = lambda shape: pl.BlockSpec((None,) + shape, lambda i: (layer, 0, 0), pipeline_mode=pl.Buffered(1))
    common = [whole((1, D)), stacked((D, F)), stacked((D, F)), whole((CONV_W, F)), whole((1, F)), stacked((F, D)),
              whole((1, D))]
    common_args = (norm_g.reshape(1, D), w1, w3, conv_w, conv_b.reshape(1, F), w2, fg)
    x_spec = pl.BlockSpec((tm, D), lambda i: (i, 0))
    if seq_mode:
        assert seq_len % tm == 0 and tm % PREV_ROWS == 0
        r = tm // PREV_ROWS
        in_specs = [x_spec, pl.BlockSpec((PREV_ROWS, D), lambda i: (jnp.maximum(i * r - 1, 0), 0))] + common
        args = (x, x) + common_args
        tiles_per_seq = seq_len // tm
    else:
        s_spec = pl.BlockSpec((tm, F), lambda i: (i, 0))
        in_specs = [x_spec, s_spec, s_spec] + common
        args = (x, state[0], state[1]) + common_args
        tiles_per_seq = 1
    if seq_mode:
        a_spec = pl.BlockSpec((None, 8, F), lambda i: (i // tiles_per_seq, 0, 0))
        a_shape = jax.ShapeDtypeStruct((M // seq_len, 8, F), F32)
    else:
        a_spec = pl.BlockSpec((tm, F), lambda i: (i, 0))
        a_shape = jax.ShapeDtypeStruct((M, F), F32)
    return pl.pallas_call(
        functools.partial(_ffn_kernel, seq_mode=seq_mode, tiles_per_seq=tiles_per_seq, final_norm=final_norm),
        grid=(M // tm,),
        in_specs=in_specs,
        out_specs=[pl.BlockSpec((tm, D), lambda i: (i, 0)), a_spec],
        out_shape=[jax.ShapeDtypeStruct((M, D), F32), a_shape],
        scratch_shapes=[pltpu.VMEM((tm, F), BF16)],
        compiler_params=_cparams(("arbitrary",)),
        name="conv_ffn",
    )(*args)


def _rope_lanes(x, cos, sin):
    lane = lax.broadcasted_iota(jnp.int32, (1, LANES), 1)
    first_half = (lane % DH_A) < (DH_A // 2)
    partner = jnp.where(first_half, pltpu.roll(x, LANES - DH_A // 2, 1), pltpu.roll(x, DH_A // 2, 1))
    return x * cos + partner * sin


def _moba_prefill_kernel(q_ref, k_ref, v_ref, cos_ref, sin_ref, o_ref, kt_ref, vt_ref, kb_ref, vb_ref, *, nb):
    blk = MOBA_BLOCK
    heads = LANES // DH_A
    T = q_ref.shape[0]
    cos = cos_ref[...]
    sin = sin_ref[...]
    k = _rope_lanes(k_ref[...], cos, sin)
    kt_ref[...] = k.T
    vt_ref[...] = v_ref[...].T
    kb_ref[...] = k.astype(BF16)
    vb_ref[:, 0:LANES] = v_ref[...].astype(BF16)
    vb_ref[:, LANES:2 * LANES] = jnp.ones((T, LANES), BF16)
    kmean = jnp.concatenate(
        [jnp.sum(k[n * blk:(n + 1) * blk, :], axis=0, keepdims=True) * (1.0 / blk) for n in range(nb)]
        + [jnp.zeros((8 - nb, LANES), F32)] * (nb < 8), axis=0)

    q2 = _rope_lanes(q_ref[...], cos, sin)
    lane = lax.broadcasted_iota(jnp.int32, (1, LANES), 1)
    blk_id = lax.broadcasted_iota(jnp.int32, (8, 1), 0)
    q_blk = lax.broadcasted_iota(jnp.int32, (1, T), 1) // blk
    past = blk_id < q_blk
    causal = (lax.broadcasted_iota(jnp.int32, (blk, blk), 1) <= lax.broadcasted_iota(jnp.int32, (blk, blk), 0))
    hms, qss, sels = [], [], []
    for h in range(heads):
        hm = (lane // DH_A) == h
        qh = jnp.where(hm, q2, 0.0)
        gt = _nt(kmean, qh, precision=HIGHEST)
        sel_t = jnp.zeros_like(gt)
        for n in range(nb - 1):
            gn = gt[n:n + 1, :]
            beats = jnp.logical_and(past, jnp.logical_or(gt > gn, jnp.logical_and(gt == gn, blk_id < n)))
            rank = jnp.sum(beats.astype(F32), axis=0, keepdims=True)
            sel_n = jnp.where(jnp.logical_and(rank < MOBA_TOPK, n < q_blk), 1.0, 0.0)
            sel_t = jnp.where(blk_id == n, sel_n, sel_t)
        hms.append(hm)
        qss.append((qh * (DH_A ** -0.5 * LOG2_E)).astype(BF16))
        sels.append(jnp.where(sel_t.T > 0.5, 0.0, NEG_INF))

    for qi in range(nb):
        rows = slice(qi * blk, (qi + 1) * blk)
        outs = []
        for h in range(heads):
            qs = qss[h][rows, :]
            pieces = []
            for n in range(qi + 1):
                s = _nt(qs, kb_ref[n * blk:(n + 1) * blk, :])
                pieces.append(jnp.where(causal, s, NEG_INF) if n == qi else s + sels[h][rows, n:n + 1])
            m = functools.reduce(jnp.maximum, pieces)
            m = jnp.max(m, axis=-1, keepdims=True)
            p_all = jnp.concatenate([jnp.exp2((s - m).astype(BF16)) for s in pieces], axis=1)
            acc = jnp.dot(p_all, vb_ref[0:(qi + 1) * blk, :], preferred_element_type=F32)
            outs.append(acc[:, 0:LANES] / acc[:, LANES:2 * LANES])
        out = outs[0]
        for h in range(1, heads):
            out = jnp.where(hms[h], outs[h], out)
        o_ref[rows, :] = out.astype(o_ref.dtype)


def _moba_prefill(z, rope, B, T):
    blk = MOBA_BLOCK
    assert T % blk == 0
    nb = T // blk
    assert nb <= 8
    cpt = AB_TILE // LANES
    return pl.pallas_call(
        functools.partial(_moba_prefill_kernel, nb=nb),
        grid=(B, W_A // LANES),
        in_specs=[pl.BlockSpec((T, LANES), lambda b, p: (b, p)),
                  pl.BlockSpec((T, LANES), lambda b, p: (b, cpt + p)),
                  pl.BlockSpec((T, LANES), lambda b, p: (b, 2 * cpt + p)),
                  pl.BlockSpec((T, LANES), lambda b, p: (0, 0)),
                  pl.BlockSpec((T, LANES), lambda b, p: (0, 0))],
        out_specs=[pl.BlockSpec((T, LANES), lambda b, p: (b, p)),
                   pl.BlockSpec((None, LANES, T), lambda b, p: (b, p, 0)),
                   pl.BlockSpec((None, LANES, T), lambda b, p: (b, p, 0))],
        out_shape=[jax.ShapeDtypeStruct((B * T, W_A), BF16),
                   jax.ShapeDtypeStruct((B, W_A, T), F32),
                   jax.ShapeDtypeStruct((B, W_A, T), F32)],
        scratch_shapes=[pltpu.VMEM((T, LANES), BF16), pltpu.VMEM((T, 2 * LANES), BF16)],
        compiler_params=_cparams(("parallel", "parallel")),
        name="moba_prefill",
    )(z, z, z, rope[0], rope[1])


def _ret_prefill_kernel(q_ref, k_ref, v_ref, gb_ref, cos_ref, sin_ref, rg_ref, dmask_ref, din_ref, dout_ref,
                        gch_ref, o_ref, s_ref, *, n_chunks):
    C = RET_CHUNK
    heads = LANES // DK_B
    lane = lax.broadcasted_iota(jnp.int32, (1, LANES), 1)
    hms = [(lane // DK_B) == h for h in range(heads)]
    row_h = lax.broadcasted_iota(jnp.int32, (LANES, LANES), 0) // DK_B
    col_h = lax.broadcasted_iota(jnp.int32, (LANES, LANES), 1) // DV_B
    same_head = row_h == col_h
    seg_ones = jnp.where(same_head, 1.0, 0.0).astype(BF16)
    din = din_ref[...]
    dout = dout_ref[...]
    gch = gch_ref[...]
    rg = rg_ref[...]
    dmask = dmask_ref[...]

    S = jnp.zeros((LANES, LANES), F32)
    for j in range(n_chunks):
        rows = slice(j * C, (j + 1) * C)
        cos = cos_ref[rows, :]
        sin = sin_ref[rows, :]
        q = _rope_lanes(q_ref[rows, :], cos, sin)
        k = _rope_lanes(k_ref[rows, :], cos, sin) * (DK_B ** -0.5)
        kb = k.astype(BF16)
        vb = v_ref[rows, :].astype(BF16)
        o = jnp.dot((q * din).astype(BF16), S.astype(BF16), preferred_element_type=F32)
        q_st = jnp.concatenate([jnp.where(hm, q, 0.0) for hm in hms], axis=0).astype(BF16)
        att = _nt(q_st, kb) * dmask
        res = jnp.dot(att.astype(BF16), vb, preferred_element_type=F32)
        intra = res[0:C, :]
        for h in range(1, heads):
            intra = jnp.where(hms[h], res[h * C:(h + 1) * C, :], intra)
        o = o + intra
        S = S * gch + jnp.where(same_head, _tn((k * dout).astype(BF16), vb), 0.0)
        ms = _rowsum_rep(o * o, seg_ones) * (1.0 / DV_B)
        g = gb_ref[rows, :]
        y = o * lax.rsqrt(ms + EPS) * rg * (g * _sigmoid(g))
        o_ref[rows, :] = y.astype(o_ref.dtype)
    s_ref[...] = S


def _ret_tables(chunk):
    log_g = jnp.log1p(-jnp.exp2(-5.0 - jnp.arange(H_B, dtype=F32)))
    i = jnp.arange(chunk, dtype=F32)
    d_in = jnp.exp((i[:, None] + 1.0) * log_g)
    d_out = jnp.exp((chunk - 1.0 - i)[:, None] * log_g)
    diff = i[:, None] - i[None, :]
    d_mask = jnp.where(diff >= 0, jnp.exp(jnp.maximum(diff, 0.0)[None] * log_g[:, None, None]), 0.0)
    g_chunk = jnp.exp(chunk * log_g)
    return d_in, d_out, d_mask, g_chunk


def _ret_prefill(z, rope, ret_g, B, T):
    C = RET_CHUNK
    assert T % C == 0
    d_in, d_out, d_mask, g_chunk = _ret_tables(C)
    npair = W_B // LANES
    lanes = lambda t: jnp.repeat(t, DK_B, axis=-1)
    din_l = lanes(d_in).reshape(C, npair, LANES).transpose(1, 0, 2)
    dout_l = lanes(d_out).reshape(C, npair, LANES).transpose(1, 0, 2)
    gch_l = lanes(g_chunk).reshape(npair, 1, LANES)
    cpt = AB_TILE // LANES
    col = lambda t: (lambda b, p: (b, t * cpt + p))
    return pl.pallas_call(
        functools.partial(_ret_prefill_kernel, n_chunks=T // C),
        grid=(B, npair),
        in_specs=[pl.BlockSpec((T, LANES), col(3)), pl.BlockSpec((T, LANES), col(4)),
                  pl.BlockSpec((T, LANES), col(5)), pl.BlockSpec((T, LANES), col(6)),
                  pl.BlockSpec((T, LANES), lambda b, p: (0, 0)),
                  pl.BlockSpec((T, LANES), lambda b, p: (0, 0)),
                  pl.BlockSpec((1, LANES), lambda b, p: (0, p)),
                  pl.BlockSpec((None, (LANES // DK_B) * C, C), lambda b, p: (p, 0, 0)),
                  pl.BlockSpec((None, C, LANES), lambda b, p: (p, 0, 0)),
                  pl.BlockSpec((None, C, LANES), lambda b, p: (p, 0, 0)),
                  pl.BlockSpec((None, 1, LANES), lambda b, p: (p, 0, 0))],
        out_specs=[pl.BlockSpec((T, LANES), lambda b, p: (b, p)),
                   pl.BlockSpec((None, None, LANES, LANES), lambda b, p: (b, p, 0, 0))],
        out_shape=[jax.ShapeDtypeStruct((B * T, W_B), BF16),
                   jax.ShapeDtypeStruct((B, npair, LANES, LANES), F32)],
        compiler_params=_cparams(("parallel", "parallel")),
        name="retention_prefill",
    )(z, z, z, z, rope[0], rope[1], ret_g.reshape(1, W_B), d_mask.reshape(npair, -1, C), din_l, dout_l, gch_l)


def _mlstm_prefill_kernel(bias_ref, q_ref, k_ref, v_ref, og_ref, ig_ref, fg_ref, igc_ref, fgc_ref, ng_ref,
                          h_ref, c_ref, n_ref, m_ref, *, n_chunks):
    L = MLSTM_CHUNK
    hd = pl.program_id(1)
    r_id = lax.broadcasted_iota(jnp.int32, (L, L), 0)
    c_id = lax.broadcasted_iota(jnp.int32, (L, L), 1)
    causal = c_id <= r_id
    upper = jnp.where(r_id <= c_id, 1.0, 0.0).astype(F32)
    lower = jnp.where(c_id <= r_id, 1.0, 0.0).astype(F32)
    ones_b = jnp.ones((L, L), BF16)
    i_rows = ig_ref[...] + bias_ref[0, hd]
    b_rows = jnp.dot(_log_sigmoid(fg_ref[...] + bias_ref[1, hd]), upper, preferred_element_type=F32, precision=HIGHEST)
    i_cols = igc_ref[...] + bias_ref[0, hd]
    b_cols = jnp.dot(lower, _log_sigmoid(fgc_ref[...] + bias_ref[1, hd]), preferred_element_type=F32, precision=HIGHEST)
    ng = ng_ref[...]

    Cm = jnp.zeros((L, L), F32)
    n = jnp.zeros((1, L), F32)
    m = jnp.zeros((1, 1), F32)
    for j in range(n_chunks):
        rows = slice(j * L, (j + 1) * L)
        q = q_ref[rows, :]
        k = k_ref[rows, :] * (DH_C ** -0.5)
        v = v_ref[rows, :]
        b_row, i_row = b_rows[j:j + 1, :], i_rows[j:j + 1, :]
        b_col, i_col = b_cols[:, j:j + 1], i_cols[:, j:j + 1]
        b_last = b_row[:, L - 1:L]
        qb = q.astype(BF16)
        kb = k.astype(BF16)
        b_rep = jnp.broadcast_to(b_col, (L, L))
        log_d = jnp.where(causal, (b_rep - b_row) + i_row, NEG_INF)
        m_row = jnp.broadcast_to(jnp.max(log_d, axis=-1, keepdims=True), (L, L))
        s = _nt(qb, kb) * jnp.exp(log_d - m_row)
        sv_ext = jnp.dot(s.astype(BF16), jnp.concatenate([v.astype(BF16), ones_b], axis=1),
                         preferred_element_type=F32)
        sv, s_sum = sv_ext[:, 0:L], sv_ext[:, L:2 * L]
        log_w = (b_last - b_col) + i_col
        m_loc = jnp.max(log_w, axis=0, keepdims=True)
        e_k = jnp.broadcast_to(jnp.exp(log_w - m_loc), (L, L))
        U = _tn((v * e_k).astype(BF16), kb)
        nk = jnp.sum(k * e_k, axis=0, keepdims=True)
        log_inter = b_rep + m
        m_t = jnp.maximum(log_inter, m_row)
        f_intra = jnp.exp(m_row - m_t)
        w_inter = jnp.exp(log_inter - m_t)
        num = sv * f_intra + _nt(qb, Cm.astype(BF16)) * w_inter
        den = s_sum * f_intra + w_inter * _rowsum_rep(q * n)
        h = num / jnp.maximum(jnp.abs(den), jnp.exp(-m_t))
        m_new = jnp.maximum(b_last + m, m_loc)
        decay = jnp.exp(b_last + m - m_new)
        f_k = jnp.exp(m_loc - m_new)
        Cm = decay * Cm + f_k * U
        n = decay * n + f_k * nk
        m = m_new
        hn = h * lax.rsqrt(_rowsum_rep(h * h) * (1.0 / L) + EPS) * ng
        h_ref[rows, :] = (hn * _sigmoid(og_ref[rows, :])).astype(h_ref.dtype)
    c_ref[...] = Cm
    n_ref[...] = n
    m_ref[...] = jnp.broadcast_to(m, m_ref.shape)


def _mlstm_prefill(z, ig, fg, gate_bias, out_g, B, T):
    L = MLSTM_CHUNK
    assert T % L == 0 and DH_C == LANES
    nc = T // L
    col = lambda t: (lambda b, h, bias: (b, t * H_C + h))
    gspec = pl.BlockSpec((None, None, nc, L), lambda b, h, bias: (b, h, 0, 0))
    gcspec = pl.BlockSpec((None, None, L, nc), lambda b, h, bias: (b, h, 0, 0))
    grid_spec = pltpu.PrefetchScalarGridSpec(
        num_scalar_prefetch=1,
        grid=(B, H_C),
        in_specs=[pl.BlockSpec((T, LANES), col(0)), pl.BlockSpec((T, LANES), col(1)),
                  pl.BlockSpec((T, LANES), col(2)), pl.BlockSpec((T, LANES), col(3)),
                  gspec, gspec, gcspec, gcspec,
                  pl.BlockSpec((1, LANES), lambda b, h, bias: (0, h))],
        out_specs=[pl.BlockSpec((T, LANES), lambda b, h, bias: (b, h)),
                   pl.BlockSpec((None, None, L, L), lambda b, h, bias: (b, h, 0, 0)),
                   pl.BlockSpec((None, None, 1, L), lambda b, h, bias: (b, h, 0, 0)),
                   pl.BlockSpec((None, None, 1, LANES), lambda b, h, bias: (b, h, 0, 0))],
    )
    return pl.pallas_call(
        functools.partial(_mlstm_prefill_kernel, n_chunks=nc),
        grid_spec=grid_spec,
        out_shape=[jax.ShapeDtypeStruct((B * T, W_C), BF16),
                   jax.ShapeDtypeStruct((B, H_C, L, L), F32),
                   jax.ShapeDtypeStruct((B, H_C, 1, L), F32),
                   jax.ShapeDtypeStruct((B, H_C, 1, LANES), F32)],
        compiler_params=_cparams(("parallel", "parallel")),
        name="mlstm_prefill",
    )(gate_bias, z, z, z, z, ig, fg, ig.swapaxes(2, 3), fg.swapaxes(2, 3), out_g.reshape(1, W_C))


SELECT_PAGES = 16


def _moba_select_kernel(pt_ref, *refs, pages_per_block):
    k_refs = refs[:SELECT_PAGES]
    q_ref, sel_ref, gate_ref, qb_ref = refs[SELECT_PAGES:]
    s = pl.program_id(1)
    ppb = pages_per_block
    blocks_per_step = SELECT_PAGES // ppb
    H, dh, page = k_refs[0].shape

    @pl.when(s == 0)
    def _():
        eye = _eye(dh)
        q = q_ref[...]
        for h in range(H):
            qb_ref[h] = jnp.broadcast_to(_row_to_col(q[h:h + 1, :], eye), (dh, page))

    qb = qb_ref[...]
    for i in range(blocks_per_step):
        acc = k_refs[i * ppb][...]
        for g in range(1, ppb):
            acc = acc + k_refs[i * ppb + g][...]
        prod = acc * qb
        part = prod[:, 0:8, :]
        for r in range(1, prod.shape[1] // 8):
            part = part + prod[:, 8 * r:8 * (r + 1), :]
        gate_ref[:, s * blocks_per_step + i] = part

    @pl.when(s == pl.num_programs(1) - 1)
    def _():
        nb = gate_ref.shape[1]
        part_sum = jnp.sum(gate_ref[...], axis=2).reshape(H * nb, page)
        col = jnp.sum(part_sum, axis=-1, keepdims=True) * (1.0 / MOBA_BLOCK)
        eye = _eye(nb)
        head_row = lax.broadcasted_iota(jnp.int32, (H, 1), 0)
        gate = jnp.zeros((H, nb), F32)
        for h in range(H):
            gate = jnp.where(head_row == h, _col_to_row(col[h * nb:(h + 1) * nb, :], eye), gate)
        blk_id = lax.broadcasted_iota(jnp.int32, (1, nb), 1)
        rank = jnp.zeros(gate.shape, jnp.int32)
        for m in range(nb):
            gm = gate[:, m:m + 1]
            beats = jnp.logical_or(gm > gate, jnp.logical_and(gm == gate, m < blk_id))
            rank = rank + jnp.where(beats, 1, 0)
        out_col = lax.broadcasted_iota(jnp.int32, sel_ref.shape, 1)
        out = jnp.zeros(sel_ref.shape, jnp.int32)
        for r in range(MOBA_TOPK):
            idx = jnp.sum(jnp.where(rank == r, blk_id, 0), axis=1, keepdims=True)
            out = jnp.where(out_col == r, idx, out)
        sel_ref[...] = out


def _moba_select(cache_kt, page_table, q):
    Bs, n_pages = page_table.shape
    _, H, dh, page = cache_kt.shape
    ppb = MOBA_BLOCK // page
    nb = n_pages // ppb
    assert nb >= MOBA_TOPK and n_pages % SELECT_PAGES == 0 and SELECT_PAGES % ppb == 0

    def page_spec(i):
        return pl.BlockSpec((None, H, dh, page), lambda b, s, pt: (pt[b, s * SELECT_PAGES + i], 0, 0, 0))

    grid_spec = pltpu.PrefetchScalarGridSpec(
        num_scalar_prefetch=1,
        grid=(Bs, n_pages // SELECT_PAGES),
        in_specs=[page_spec(i) for i in range(SELECT_PAGES)]
        + [pl.BlockSpec((None, H, dh), lambda b, s, pt: (b, 0, 0))],
        out_specs=pl.BlockSpec((None, H, 8), lambda b, s, pt: (b, 0, 0)),
        scratch_shapes=[pltpu.VMEM((H, nb, 8, page), F32), pltpu.VMEM((H, dh, page), F32)],
    )
    return pl.pallas_call(
        functools.partial(_moba_select_kernel, pages_per_block=ppb),
        grid_spec=grid_spec,
        out_shape=jax.ShapeDtypeStruct((Bs, H, 8), jnp.int32),
        compiler_params=_cparams(("parallel", "arbitrary")),
        name="moba_select",
    )(page_table, *([cache_kt] * SELECT_PAGES), q)


def _moba_decode_kernel(pt_ref, sel_ref, q_ref, kn_ref, vn_ref, ck_ref, cv_ref, o_ref, kbuf, vbuf, sem,
                        *, pages_per_block, page):
    b = pl.program_id(0)
    nb_ = pl.num_programs(0)
    H = q_ref.shape[0]
    ppb = pages_per_block

    def copies(bb, slot):
        out = []
        for h in range(H):
            for r in range(MOBA_TOPK):
                blk = sel_ref[bb, h, r]
                for g in range(ppb):
                    pg = pt_ref[bb, blk * ppb + g]
                    dst = pl.ds((r * ppb + g) * page, page)
                    out.append(pltpu.make_async_copy(ck_ref.at[pg, h], kbuf.at[slot, h, :, dst], sem.at[slot, 0]))
                    out.append(pltpu.make_async_copy(cv_ref.at[pg, h], vbuf.at[slot, h, :, dst], sem.at[slot, 1]))
        return out

    slot = b % 2

    @pl.when(b == 0)
    def _():
        for c in copies(b, slot):
            c.start()

    @pl.when(b + 1 < nb_)
    def _():
        for c in copies(b + 1, 1 - slot):
            c.start()

    for c in copies(b, slot):
        c.wait()

    row = lax.broadcasted_iota(jnp.int32, (H, 1), 0)
    out = jnp.zeros(o_ref.shape, F32)
    q = q_ref[...]
    kn = kn_ref[...]
    vn = vn_ref[...]
    for h in range(H):
        qh = q[h:h + 1, :] * (DH_A ** -0.5)
        q8 = jnp.broadcast_to(qh, (8, qh.shape[1])).astype(BF16)
        kh = kbuf[slot, h].astype(BF16)
        vh = vbuf[slot, h].astype(BF16)
        s = jnp.dot(q8, kh, preferred_element_type=F32)[0:1, :]
        s_self = jnp.sum(qh * kn[h:h + 1, :], axis=-1, keepdims=True)
        m = jnp.maximum(jnp.max(s, axis=-1, keepdims=True), s_self)
        p = jnp.exp(s - m)
        p_self = jnp.exp(s_self - m)
        l = jnp.sum(p, axis=-1, keepdims=True) + p_self
        p8 = jnp.broadcast_to(p, (8, p.shape[1])).astype(BF16)
        pv = _nt(p8, vh)[0:1, :]
        oh = (pv + p_self * vn[h:h + 1, :]) / l
        out = jnp.where(row == h, oh, out)
    o_ref[...] = out


def _moba_decode(cache_kt, cache_vt, page_table, sel, q, k_new, v_new):
    Bs, n_pages = page_table.shape
    _, H, dh, page = cache_kt.shape
    ppb = MOBA_BLOCK // page
    rows = MOBA_TOPK * MOBA_BLOCK
    vec = pl.BlockSpec((None, H, dh), lambda b, pt, sl: (b, 0, 0))
    grid_spec = pltpu.PrefetchScalarGridSpec(
        num_scalar_prefetch=2,
        grid=(Bs,),
        in_specs=[vec, vec, vec, pl.BlockSpec(memory_space=pl.ANY), pl.BlockSpec(memory_space=pl.ANY)],
        out_specs=vec,
        scratch_shapes=[pltpu.VMEM((2, H, dh, rows), F32), pltpu.VMEM((2, H, dh, rows), F32),
                        pltpu.SemaphoreType.DMA((2, 2))],
    )
    return pl.pallas_call(
        functools.partial(_moba_decode_kernel, pages_per_block=ppb, page=page),
        grid_spec=grid_spec,
        out_shape=jax.ShapeDtypeStruct((Bs, H, dh), F32),
        compiler_params=_cparams(("arbitrary",)),
        name="moba_decode",
    )(page_table, sel, q, k_new, v_new, cache_kt, cache_vt)


DECODE_ROWS = 4


def _per_row(kernel, shared):
    def wrapped(*refs):
        rows = next(r.shape[0] for i, r in enumerate(refs) if i not in shared)
        for r in range(rows):
            kernel(*[ref if i in shared else ref.at[r] for i, ref in enumerate(refs)])
    return wrapped


def _ret_decode_kernel(q_ref, k_ref, v_ref, gb_ref, rg_ref, g_ref, s_ref, o_ref, so_ref):
    H, dk = q_ref.shape
    eye = _eye(dk)
    row = lax.broadcasted_iota(jnp.int32, (H, 1), 0)
    q = q_ref[...]
    k = k_ref[...]
    v = v_ref[...]
    g_all = g_ref[...]
    out = jnp.zeros(o_ref.shape, F32)
    for h in range(H):
        qh, kh, vh = q[h:h + 1, :], k[h:h + 1, :], v[h:h + 1, :]
        g = g_all[h:h + 1, 0:1]
        S = s_ref[h]
        att = jnp.sum(qh * kh, axis=-1, keepdims=True)
        cross = jnp.sum(_row_to_col(qh * g, eye) * S, axis=0, keepdims=True)
        out = jnp.where(row == h, att * vh + cross, out)
        so_ref[h] = S * g + _row_to_col(kh, eye) * vh
    gb = gb_ref[...]
    y = out * lax.rsqrt(jnp.mean(out * out, axis=-1, keepdims=True) + EPS) * rg_ref[...]
    o_ref[...] = y * (gb * _sigmoid(gb))


def _ret_decode(q, k, v, gb, ret_g, state):
    Bs, H, dk = q.shape
    log_g = jnp.log1p(-jnp.exp2(-5.0 - jnp.arange(H_B, dtype=F32)))
    g = jnp.broadcast_to(jnp.exp(1.0 * log_g)[:, None], (H, LANES))
    R = DECODE_ROWS if Bs % DECODE_ROWS == 0 else 1
    vec = pl.BlockSpec((R, H, dk), lambda b: (b, 0, 0))
    st = pl.BlockSpec((R, H, dk, DV_B), lambda b: (b, 0, 0, 0))
    return pl.pallas_call(
        _per_row(_ret_decode_kernel, shared=(4, 5)),
        grid=(Bs // R,),
        in_specs=[vec, vec, vec, vec, pl.BlockSpec((H, dk), lambda b: (0, 0)),
                  pl.BlockSpec((H, LANES), lambda b: (0, 0)), st],
        out_specs=[vec, st],
        out_shape=[jax.ShapeDtypeStruct((Bs, H, DV_B), F32), jax.ShapeDtypeStruct(state.shape, F32)],
        compiler_params=_cparams(("parallel",)),
        name="retention_decode",
    )(q, k, v, gb, ret_g.reshape(H, DV_B), g, state)


def _mlstm_decode_kernel(q_ref, k_ref, v_ref, og_ref, ig_ref, fg_ref, bi_ref, bf_ref, ng_ref, c_ref, n_ref,
                         m_ref, h_ref, co_ref, no_ref, mo_ref):
    H, d = q_ref.shape
    eye = _eye(d)
    row = lax.broadcasted_iota(jnp.int32, (H, 1), 0)
    q = q_ref[...]
    k = k_ref[...] * (DH_C ** -0.5)
    v = v_ref[...]
    n0 = n_ref[...]
    ig = ig_ref[...] + bi_ref[...]
    b = _log_sigmoid(fg_ref[...] + bf_ref[...])
    m0 = m_ref[...]
    log_inter = b + m0
    m_t = jnp.maximum(log_inter, ig)
    w_intra = jnp.exp(ig - m_t)
    w_inter = jnp.exp(log_inter - m_t)
    s = jnp.sum(q * k, axis=-1, keepdims=True) * w_intra
    den = s + w_inter * jnp.sum(q * n0, axis=-1, keepdims=True)
    scale = 1.0 / jnp.maximum(jnp.abs(den), jnp.exp(-m_t))
    w_k = w_intra
    decay = w_inter
    hs = jnp.zeros(h_ref.shape, F32)
    for h in range(H):
        qh, kh, vh = q[h:h + 1, :], k[h:h + 1, :], v[h:h + 1, :]
        Cm = c_ref[h]
        cq = _col_to_row(jnp.sum(Cm * qh, axis=1, keepdims=True), eye)
        num = s[h:h + 1, :] * vh + cq * w_inter[h:h + 1, :]
        hs = jnp.where(row == h, num * scale[h:h + 1, :], hs)
        co_ref[h] = decay[h:h + 1, :] * Cm + _row_to_col(vh * w_k[h:h + 1, :], eye) * kh
    no_ref[...] = decay * n0 + k * w_k
    mo_ref[...] = m_t
    hn = hs * lax.rsqrt(jnp.mean(hs * hs, axis=-1, keepdims=True) + EPS) * ng_ref[...]
    h_ref[...] = hn * _sigmoid(og_ref[...])


def _mlstm_decode(q, k, v, og, ig, fg, gate_bias, out_g, C0, n0, m0):
    Bs, H, d = q.shape
    R = DECODE_ROWS if Bs % DECODE_ROWS == 0 else 1
    vec = pl.BlockSpec((R, H, d), lambda b: (b, 0, 0))
    sc = pl.BlockSpec((R, H, 1), lambda b: (b, 0, 0))
    st = pl.BlockSpec((R, H, d, d), lambda b: (b, 0, 0, 0))
    bias = pl.BlockSpec((H, 1), lambda b: (0, 0))
    return pl.pallas_call(
        _per_row(_mlstm_decode_kernel, shared=(6, 7, 8)),
        grid=(Bs // R,),
        in_specs=[vec, vec, vec, vec, sc, sc, bias, bias, pl.BlockSpec((H, d), lambda b: (0, 0)), st, vec, sc],
        out_specs=[vec, st, vec, sc],
        out_shape=[jax.ShapeDtypeStruct((Bs, H, d), F32), jax.ShapeDtypeStruct(C0.shape, F32),
                   jax.ShapeDtypeStruct((Bs, H, d), F32), jax.ShapeDtypeStruct((Bs, H, 1), F32)],
        compiler_params=_cparams(("parallel",)),
        name="mlstm_decode",
    )(q, k, v, og, ig, fg, gate_bias[0].reshape(H, 1), gate_bias[1].reshape(H, 1), out_g.reshape(H, d), C0, n0, m0)


def _rope_tables(pos):
    half = DH_A // 2
    inv = ROPE_THETA ** (-jnp.arange(half, dtype=F32) / half)
    ang = pos[:, None] * inv[None, :]
    cos = jnp.cos(ang)
    sin = jnp.sin(ang)
    reps = LANES // DH_A
    cos_t = jnp.tile(jnp.concatenate([cos, cos], axis=-1), (1, reps))
    sin_t = jnp.tile(jnp.concatenate([-sin, sin], axis=-1), (1, reps))
    return cos_t, sin_t


def _row_tile(m, pref):
    return pref if m % pref == 0 else m


def kernel(x_prompt, x_sample, cache_k, cache_v, page_table, state_ret, state_mlstm_C, state_mlstm_n,
           state_mlstm_m, state_ffn_conv, ab_norm_g, ab_w_in, ab_ret_norm_g, ab_w_out, c_norm_g, c_w_in,
           c_gate_bias, c_out_norm_g, c_w_out, ffn_norm_g, ffn_w1, ffn_w3, ffn_conv_w, ffn_conv_b, ffn_w2,
           final_norm_g):
    Bp, Tp, D = x_prompt.shape
    Bs, Ts, _ = x_sample.shape
    assert Ts == 1
    past_len = page_table.shape[1] * cache_k.shape[2]
    Mp = Bp * Tp
    xp = x_prompt.reshape(Mp, D)
    xs = x_sample.reshape(Bs, D)
    tm_p = _row_tile(Tp, 1024)
    tm_in = _row_tile(Tp, 2048)
    tm_ffn = _row_tile(Tp, 1024)
    tm_s = Bs
    rope_p = _rope_tables(jnp.arange(Tp, dtype=jnp.int32).astype(F32))
    rope_s = _rope_tables(jnp.full((Bs,), past_len, jnp.int32).astype(F32))
    gw = 2 * H_C

    outs = {}

    w_in = ab_w_in[0].astype(BF16)
    w_out = ab_w_out[0].astype(BF16)
    w_out_a, w_out_b = w_out[:W_A], w_out[W_A:]

    z = _norm_matmul(xp, ab_norm_g[0], w_in, tm_p, IN_AB // 2)
    oa, kt, vt = _moba_prefill(z, rope_p, Bp, Tp)
    ob, s_pair = _ret_prefill(z, rope_p, ab_ret_norm_g[0], Bp, Tp)
    xp = _proj_residual([oa, ob], [w_out_a, w_out_b], xp, tm_p)
    outs["k_prompt"] = kt.reshape(Bp, H_A, DH_A, Tp).transpose(0, 3, 1, 2)[None]
    outs["v_prompt"] = vt.reshape(Bp, H_A, DH_A, Tp).transpose(0, 3, 1, 2)[None]
    hb = LANES // DK_B
    s_heads = jnp.stack([s_pair[:, :, i * DK_B:(i + 1) * DK_B, i * DV_B:(i + 1) * DV_B] for i in range(hb)], axis=2)
    outs["ret_prompt"] = s_heads.reshape(1, Bp, H_B, DK_B, DV_B)

    zs = _norm_matmul(xs, ab_norm_g[0], w_in, tm_s, AB_TILE, rope=rope_s)
    seg = lambda t: zs[:, t * AB_TILE:(t + 1) * AB_TILE].reshape(Bs, H_A, DH_A)
    qa_s, ka_s, va_s, qb_s, kb_s, vb_s, gb_s = (seg(t) for t in range(7))
    cache_kt = cache_k[0].transpose(0, 2, 3, 1)
    cache_vt = cache_v[0].transpose(0, 2, 3, 1)
    sel = _moba_select(cache_kt, page_table, qa_s)
    oa_s = _moba_decode(cache_kt, cache_vt, page_table, sel, qa_s, ka_s, va_s)
    ob_s, s_new = _ret_decode(qb_s, kb_s, vb_s, gb_s, ab_ret_norm_g[0], state_ret[0])
    xs = _proj_residual([oa_s.reshape(Bs, W_A).astype(BF16), ob_s.reshape(Bs, W_B).astype(BF16)],
                        [w_out_a, w_out_b], xs, tm_s)
    outs["k_sample"] = ka_s.reshape(1, Bs, 1, H_A, DH_A)
    outs["v_sample"] = va_s.reshape(1, Bs, 1, H_A, DH_A)
    outs["ret_sample"] = s_new[None]

    conv_p, conv_s = [], []

    w1_b, w3_b, w2_b = ffn_w1.astype(BF16), ffn_w3.astype(BF16), ffn_w2.astype(BF16)

    def ffn_both(l, xp, xs, final_g):
        st = state_ffn_conv[l]
        xp, a_tail = _ffn(xp, ffn_norm_g[l], l, w1_b, w3_b, ffn_conv_w[l], ffn_conv_b[l], w2_b, final_g, tm_ffn,
                          seq_len=Tp)
        xs, a_s = _ffn(xs, ffn_norm_g[l], l, w1_b, w3_b, ffn_conv_w[l], ffn_conv_b[l], w2_b, final_g, tm_s,
                       state=(st[:, 0, :], st[:, 1, :]))
        conv_p.append(a_tail[:, 8 - (CONV_W - 1):, :])
        conv_s.append(jnp.stack([st[:, 1, :], a_s], axis=1))
        return xp, xs

    xp, xs = ffn_both(0, xp, xs, None)

    w_in = c_w_in[0]
    w_all = w_in.astype(BF16)
    w_gate = jnp.pad(w_in[:, 4 * W_C:], ((0, 0), (0, LANES - gw))).astype(BF16)
    w_out = c_w_out[0].astype(BF16)
    L = MLSTM_CHUNK

    z, zg = _norm_matmul_side(xp, c_norm_g[0], w_all, 4 * W_C, w_gate, tm_in, 1024)
    gates = zg[:, :gw].reshape(Bp, Tp // L, L, 2, H_C).transpose(3, 0, 4, 1, 2)
    h, C_p, n_p, m_p = _mlstm_prefill(z, gates[0], gates[1], c_gate_bias[0], c_out_norm_g[0], Bp, Tp)
    xp = _proj_residual([h], [w_out], xp, tm_p)
    outs["C_prompt"] = C_p[None]
    outs["n_prompt"] = n_p.reshape(1, Bp, H_C, DH_C)
    outs["m_prompt"] = m_p[:, :, 0, 0][None]

    zs, zgs = _norm_matmul_side(xs, c_norm_g[0], w_all, 4 * W_C, w_gate, tm_s, 1024)
    segc = lambda t: zs[:, t * W_C:(t + 1) * W_C].reshape(Bs, H_C, DH_C)
    ig_s = zgs[:, :H_C][:, :, None]
    fg_s = zgs[:, H_C:gw][:, :, None]
    h_s, C_s, n_s, m_s = _mlstm_decode(segc(0), segc(1), segc(2), segc(3), ig_s, fg_s, c_gate_bias[0], c_out_norm_g[0],
                                       state_mlstm_C[0], state_mlstm_n[0], state_mlstm_m[0][:, :, None])
    xs = _proj_residual([h_s.reshape(Bs, W_C).astype(BF16)], [w_out], xs, tm_s)
    outs["C_sample"] = C_s[None]
    outs["n_sample"] = n_s[None]
    outs["m_sample"] = m_s[:, :, 0][None]

    xp, xs = ffn_both(1, xp, xs, final_norm_g)

    return (xp.reshape(Bp, Tp, D), xs.reshape(Bs, 1, D),
            outs["k_prompt"], outs["v_prompt"], outs["k_sample"], outs["v_sample"],
            outs["ret_prompt"], outs["ret_sample"], outs["C_prompt"], outs["C_sample"],
            outs["n_prompt"], outs["n_sample"], outs["m_prompt"], outs["m_sample"],
            jnp.stack(conv_p), jnp.stack(conv_s))
```

```python
import functools

import jax
import jax.numpy as jnp
from jax import lax
from jax.experimental import pallas as pl
from jax.experimental.pallas import tpu as pltpu

F32 = jnp.float32
BF16 = jnp.bfloat16
HIGHEST = lax.Precision.HIGHEST

LANES = 128
D_MODEL = 1024
H_A = 8
DH_A = 64
MOBA_BLOCK = 256
MOBA_TOPK = 3
H_B = 8
DK_B = 64
DV_B = 64
RET_CHUNK = 128
H_C = 8
DH_C = D_MODEL // H_C
MLSTM_CHUNK = 128
D_FF = 11 * D_MODEL // 4
CONV_W = 3
ROPE_THETA = 10000.0
EPS = 1e-6
W_A = H_A * DH_A
W_B = H_B * DV_B
W_C = H_C * DH_C
IN_AB = 3 * W_A + 2 * H_B * DK_B + 2 * W_B
AB_TILE = 512
AB_ROPE_TILES = (0, 1, 3, 4)
AB_KB_TILE = 4
VMEM_LIMIT = 56 * 1024 * 1024

NEG_INF = float("-inf")
LOG2_E = 1.4426950408889634


def _cparams(sem):
    return pltpu.CompilerParams(dimension_semantics=sem, vmem_limit_bytes=VMEM_LIMIT)


def _nt(a, b, **kw):
    return lax.dot_general(a, b, (((1,), (1,)), ((), ())), preferred_element_type=F32, **kw)


def _tn(a, b, **kw):
    return lax.dot_general(a, b, (((0,), (0,)), ((), ())), preferred_element_type=F32, **kw)


def _rms_rows(x, g):
    ms = jnp.mean(x * x, axis=-1, keepdims=True)
    return x * lax.rsqrt(ms + EPS) * g


def _eye(n):
    return lax.broadcasted_iota(jnp.int32, (n, n), 0) == lax.broadcasted_iota(jnp.int32, (n, n), 1)


def _row_to_col(row, eye):
    return jnp.sum(jnp.where(eye, row, 0.0), axis=1, keepdims=True)


def _col_to_row(col, eye):
    return jnp.sum(jnp.where(eye, col, 0.0), axis=0, keepdims=True)


def _rowsum_rep(x, ones=None):
    if ones is None:
        ones = jnp.ones((x.shape[1], LANES), BF16)
    hi = x.astype(BF16)
    lo = (x - hi.astype(F32)).astype(BF16)
    return (jnp.dot(hi, ones, preferred_element_type=F32) + jnp.dot(lo, ones, preferred_element_type=F32))


def _log_sigmoid(x):
    return jnp.minimum(x, 0.0) - jnp.log1p(jnp.exp(-jnp.abs(x)))


def _sigmoid(x):
    return 1.0 / (1.0 + jnp.exp(-x))


def _gelu_tanh(x):
    c = 0.7978845608028654
    return (0.5 * x) * (1.0 + jnp.tanh(x * (c + (0.044715 * c) * (x * x))))


def _norm_matmul_kernel(x_ref, g_ref, w_ref, cos_ref, sin_ref, o_ref, xn_ref, *, rope_tiles, scale_tile, scale):
    j = pl.program_id(1)

    @pl.when(j == 0)
    def _():
        xn_ref[...] = _rms_rows(x_ref[...], g_ref[...]).astype(BF16)

    z = jnp.dot(xn_ref[...], w_ref[...], preferred_element_type=F32)
    if not rope_tiles:
        o_ref[...] = z
        return

    is_rope = functools.reduce(jnp.logical_or, [j == t for t in rope_tiles])

    @pl.when(is_rope)
    def _():
        sc = jnp.where(j == scale_tile, scale, 1.0).astype(F32)
        cos = cos_ref[...]
        sin = sin_ref[...]
        for c in range(z.shape[1] // LANES):
            cols = slice(c * LANES, (c + 1) * LANES)
            o_ref[:, cols] = _rope_lanes(z[:, cols], cos, sin) * sc

    @pl.when(jnp.logical_not(is_rope))
    def _():
        o_ref[...] = z


def _norm_matmul_side_kernel(x_ref, g_ref, w_ref, ws_ref, o_ref, os_ref, xn_ref):
    @pl.when(pl.program_id(1) == 0)
    def _():
        xn_ref[...] = _rms_rows(x_ref[...], g_ref[...]).astype(BF16)
        os_ref[...] = jnp.dot(xn_ref[...], ws_ref[...], preferred_element_type=F32)

    o_ref[...] = jnp.dot(xn_ref[...], w_ref[...], preferred_element_type=F32)


def _norm_matmul_side(x, g, w, n_cols, w_side, tm, tn):
    M, D = x.shape
    N = n_cols
    Ns = w_side.shape[1]
    assert M % tm == 0 and N % tn == 0
    return pl.pallas_call(
        _norm_matmul_side_kernel,
        grid=(M // tm, N // tn),
        in_specs=[pl.BlockSpec((tm, D), lambda i, j: (i, 0)),
                  pl.BlockSpec((1, D), lambda i, j: (0, 0)),
                  pl.BlockSpec((D, tn), lambda i, j: (0, j)),
                  pl.BlockSpec((D, Ns), lambda i, j: (0, 0))],
        out_specs=[pl.BlockSpec((tm, tn), lambda i, j: (i, j)),
                   pl.BlockSpec((tm, Ns), lambda i, j: (i, 0))],
        out_shape=[jax.ShapeDtypeStruct((M, N), F32), jax.ShapeDtypeStruct((M, Ns), F32)],
        scratch_shapes=[pltpu.VMEM((tm, D), BF16)],
        compiler_params=_cparams(("parallel", "arbitrary")),
        name="norm_matmul_side",
    )(x, g.reshape(1, D), w, w_side)


def _norm_matmul(x, g, w, tm, tn, rope=None):
    M, D = x.shape
    N = w.shape[1]
    assert M % tm == 0 and N % tn == 0
    if rope is None:
        cos = sin = jnp.zeros((8, LANES), F32)
        tab_spec = pl.BlockSpec((8, LANES), lambda i, j: (0, 0))
        kern = functools.partial(_norm_matmul_kernel, rope_tiles=(), scale_tile=-1, scale=1.0)
    else:
        cos, sin = rope
        nt = cos.shape[0] // tm
        tab_spec = pl.BlockSpec((tm, LANES), lambda i, j: (i % nt, 0))
        kern = functools.partial(_norm_matmul_kernel, rope_tiles=AB_ROPE_TILES, scale_tile=AB_KB_TILE,
                                 scale=DK_B ** -0.5)
    return pl.pallas_call(
        kern,
        grid=(M // tm, N // tn),
        in_specs=[pl.BlockSpec((tm, D), lambda i, j: (i, 0)),
                  pl.BlockSpec((1, D), lambda i, j: (0, 0)),
                  pl.BlockSpec((D, tn), lambda i, j: (0, j)),
                  tab_spec, tab_spec],
        out_specs=pl.BlockSpec((tm, tn), lambda i, j: (i, j)),
        out_shape=jax.ShapeDtypeStruct((M, N), F32),
        scratch_shapes=[pltpu.VMEM((tm, D), BF16)],
        compiler_params=_cparams(("parallel", "arbitrary")),
        name="norm_matmul",
    )(x, g.reshape(1, D), w, cos, sin)


def _proj_residual_kernel(*refs, n_in):
    a_refs = refs[:n_in]
    w_refs = refs[n_in:2 * n_in]
    res_ref = refs[2 * n_in]
    o_ref = refs[2 * n_in + 1]
    y = res_ref[...]
    acc = None
    for a_ref, w_ref in zip(a_refs, w_refs):
        d = jnp.dot(a_ref[...], w_ref[...], preferred_element_type=F32)
        acc = d if acc is None else acc + d
    o_ref[...] = y + acc


def _proj_residual(acts, ws, res, tm):
    M, D = res.shape
    n_in = len(acts)
    in_specs = ([pl.BlockSpec((tm, a.shape[1]), lambda i: (i, 0)) for a in acts]
                + [pl.BlockSpec(w.shape, lambda i: (0, 0)) for w in ws]
                + [pl.BlockSpec((tm, D), lambda i: (i, 0))])
    return pl.pallas_call(
        functools.partial(_proj_residual_kernel, n_in=n_in),
        grid=(M // tm,),
        in_specs=in_specs,
        out_specs=pl.BlockSpec((tm, D), lambda i: (i, 0)),
        out_shape=jax.ShapeDtypeStruct((M, D), F32),
        compiler_params=_cparams(("parallel",)),
        name="proj_residual",
    )(*acts, *ws, res)


PREV_ROWS = 16
FFN_CHUNK = 256


def _ffn_kernel(*refs, seq_mode, tiles_per_seq, final_norm):
    if seq_mode:
        (x_ref, xp_ref, g_ref, w1_ref, w3_ref, cw_ref, cb_ref, w2_ref, fg_ref, o_ref, a_ref, y_ref) = refs
    else:
        (x_ref, s0_ref, s1_ref, g_ref, w1_ref, w3_ref, cw_ref, cb_ref, w2_ref, fg_ref, o_ref, a_ref, y_ref) = refs
    i = pl.program_id(0)
    x = x_ref[...]
    tm = x.shape[0]
    F = w1_ref.shape[1]
    xn = _rms_rows(x, g_ref[...]).astype(BF16)
    if seq_mode:
        xpn = _rms_rows(xp_ref[...], g_ref[...]).astype(BF16)
        has_prev = ((i % tiles_per_seq) != 0).astype(F32)
        row8 = lax.broadcasted_iota(jnp.int32, (8, 1), 0)
    for c0 in range(0, F, FFN_CHUNK):
        cols = slice(c0, min(c0 + FFN_CHUNK, F))
        w1c = w1_ref[:, cols]
        a = jnp.dot(xn, w1c, preferred_element_type=F32)
        gate = jnp.dot(xn, w3_ref[:, cols], preferred_element_type=F32)
        a_ref[:, cols] = a[tm - a_ref.shape[0]:, :]
        if seq_mode:
            ap = jnp.dot(xpn, w1c, preferred_element_type=F32)
            p1 = ap[PREV_ROWS - 1:PREV_ROWS, :] * has_prev
            p2 = ap[PREV_ROWS - 2:PREV_ROWS - 1, :] * has_prev
            a1 = pltpu.roll(a, 1, 0)
            a2 = pltpu.roll(a, 2, 0)
            top1 = jnp.where(row8 == 0, p1, a1[0:8, :])
            top2 = jnp.where(row8 == 0, p2, jnp.where(row8 == 1, p1, a2[0:8, :]))
            a1 = jnp.concatenate([top1, a1[8:, :]], axis=0)
            a2 = jnp.concatenate([top2, a2[8:, :]], axis=0)
        else:
            a1 = s1_ref[:, cols]
            a2 = s0_ref[:, cols]
        ac = cb_ref[:, cols] + a2 * cw_ref[0:1, cols]
        ac = ac + a1 * cw_ref[1:2, cols]
        ac = ac + a * cw_ref[2:3, cols]
        y_ref[:, cols] = (_gelu_tanh(ac) * gate).astype(BF16)
    acc = x + jnp.dot(y_ref[...], w2_ref[...], preferred_element_type=F32)
    if final_norm:
        acc = _rms_rows(acc, fg_ref[...])
    o_ref[...] = acc


def _ffn(x, norm_g, layer, w1, w3, conv_w, conv_b, w2, final_g, tm, seq_len=None, state=None):
    M, D = x.shape
    F = w1.shape[2]
    seq_mode = state is None
    final_norm = final_g is not None
    fg = (final_g if final_norm else jnp.ones((D,), F32)).reshape(1, D)
    whole = lambda shape: pl.BlockSpec(shape, lambda i: (0, 0), pipeline_mode=pl.Buffered(1))
    stacked = lambda shape: pl.BlockSpec((None,) + shape, lambda i: (layer, 0, 0), pipeline_mode=pl.Buffered(1))
    common = [whole((1, D)), stacked((D, F)), stacked((D, F)), whole((CONV_W, F)), whole((1, F)), stacked((F, D)),
              whole((1, D))]
    common_args = (norm_g.reshape(1, D), w1, w3, conv_w, conv_b.reshape(1, F), w2, fg)
    x_spec = pl.BlockSpec((tm, D), lambda i: (i, 0))
    if seq_mode:
        assert seq_len % tm == 0 and tm % PREV_ROWS == 0
        r = tm // PREV_ROWS
        in_specs = [x_spec, pl.BlockSpec((PREV_ROWS, D), lambda i: (jnp.maximum(i * r - 1, 0), 0))] + common
        args = (x, x) + common_args
        tiles_per_seq = seq_len // tm
    else:
        s_spec = pl.BlockSpec((tm, F), lambda i: (i, 0))
        in_specs = [x_spec, s_spec, s_spec] + common
        args = (x, state[0], state[1]) + common_args
        tiles_per_seq = 1
    if seq_mode:
        a_spec = pl.BlockSpec((None, 8, F), lambda i: (i // tiles_per_seq, 0, 0))
        a_shape = jax.ShapeDtypeStruct((M // seq_len, 8, F), F32)
    else:
        a_spec = pl.BlockSpec((tm, F), lambda i: (i, 0))
        a_shape = jax.ShapeDtypeStruct((M, F), F32)
    return pl.pallas_call(
        functools.partial(_ffn_kernel, seq_mode=seq_mode, tiles_per_seq=tiles_per_seq, final_norm=final_norm),
        grid=(M // tm,),
        in_specs=in_specs,
        out_specs=[pl.BlockSpec((tm, D), lambda i: (i, 0)), a_spec],
        out_shape=[jax.ShapeDtypeStruct((M, D), F32), a_shape],
        scratch_shapes=[pltpu.VMEM((tm, F), BF16)],
        compiler_params=_cparams(("arbitrary",)),
        name="conv_ffn",
    )(*args)


def _rope_lanes(x, cos, sin):
    lane = lax.broadcasted_iota(jnp.int32, (1, LANES), 1)
    first_half = (lane % DH_A) < (DH_A // 2)
    partner = jnp.where(first_half, pltpu.roll(x, LANES - DH_A // 2, 1), pltpu.roll(x, DH_A // 2, 1))
    return x * cos + partner * sin


def _moba_prefill_kernel(q_ref, k_ref, v_ref, cos_ref, sin_ref, o_ref, kt_ref, vt_ref, kb_ref, vb_ref, *, nb):
    blk = MOBA_BLOCK
    heads = LANES // DH_A
    T = q_ref.shape[0]
    cos = cos_ref[...]
    sin = sin_ref[...]
    k = _rope_lanes(k_ref[...], cos, sin)
    kt_ref[...] = k.T
    vt_ref[...] = v_ref[...].T
    kb_ref[...] = k.astype(BF16)
    vb_ref[:, 0:LANES] = v_ref[...].astype(BF16)
    vb_ref[:, LANES:2 * LANES] = jnp.ones((T, LANES), BF16)
    kmean = jnp.concatenate(
        [jnp.sum(k[n * blk:(n + 1) * blk, :], axis=0, keepdims=True) * (1.0 / blk) for n in range(nb)]
        + [jnp.zeros((8 - nb, LANES), F32)] * (nb < 8), axis=0)

    q2 = _rope_lanes(q_ref[...], cos, sin)
    lane = lax.broadcasted_iota(jnp.int32, (1, LANES), 1)
    blk_id = lax.broadcasted_iota(jnp.int32, (8, 1), 0)
    q_blk = lax.broadcasted_iota(jnp.int32, (1, T), 1) // blk
    past = blk_id < q_blk
    causal = (lax.broadcasted_iota(jnp.int32, (blk, blk), 1) <= lax.broadcasted_iota(jnp.int32, (blk, blk), 0))
    hms, qss, sels = [], [], []
    for h in range(heads):
        hm = (lane // DH_A) == h
        qh = jnp.where(hm, q2, 0.0)
        gt = _nt(kmean, qh, precision=HIGHEST)
        sel_t = jnp.zeros_like(gt)
        for n in range(nb - 1):
            gn = gt[n:n + 1, :]
            beats = jnp.logical_and(past, jnp.logical_or(gt > gn, jnp.logical_and(gt == gn, blk_id < n)))
            rank = jnp.sum(beats.astype(F32), axis=0, keepdims=True)
            sel_n = jnp.where(jnp.logical_and(rank < MOBA_TOPK, n < q_blk), 1.0, 0.0)
            sel_t = jnp.where(blk_id == n, sel_n, sel_t)
        hms.append(hm)
        qss.append((qh * (DH_A ** -0.5 * LOG2_E)).astype(BF16))
        sels.append(jnp.where(sel_t.T > 0.5, 0.0, NEG_INF))

    for qi in range(nb):
        rows = slice(qi * blk, (qi + 1) * blk)
        outs = []
        for h in range(heads):
            qs = qss[h][rows, :]
            pieces = []
            for n in range(qi + 1):
                s = _nt(qs, kb_ref[n * blk:(n + 1) * blk, :])
                pieces.append(jnp.where(causal, s, NEG_INF) if n == qi else s + sels[h][rows, n:n + 1])
            m = functools.reduce(jnp.maximum, pieces)
            m = jnp.max(m, axis=-1, keepdims=True)
            p_all = jnp.concatenate([jnp.exp2((s - m).astype(BF16)) for s in pieces], axis=1)
            acc = jnp.dot(p_all, vb_ref[0:(qi + 1) * blk, :], preferred_element_type=F32)
            outs.append(acc[:, 0:LANES] / acc[:, LANES:2 * LANES])
        out = outs[0]
        for h in range(1, heads):
            out = jnp.where(hms[h], outs[h], out)
        o_ref[rows, :] = out.astype(o_ref.dtype)


def _moba_prefill(z, rope, B, T):
    blk = MOBA_BLOCK
    assert T % blk == 0
    nb = T // blk
    assert nb <= 8
    cpt = AB_TILE // LANES
    return pl.pallas_call(
        functools.partial(_moba_prefill_kernel, nb=nb),
        grid=(B, W_A // LANES),
        in_specs=[pl.BlockSpec((T, LANES), lambda b, p: (b, p)),
                  pl.BlockSpec((T, LANES), lambda b, p: (b, cpt + p)),
                  pl.BlockSpec((T, LANES), lambda b, p: (b, 2 * cpt + p)),
                  pl.BlockSpec((T, LANES), lambda b, p: (0, 0)),
                  pl.BlockSpec((T, LANES), lambda b, p: (0, 0))],
        out_specs=[pl.BlockSpec((T, LANES), lambda b, p: (b, p)),
                   pl.BlockSpec((None, LANES, T), lambda b, p: (b, p, 0)),
                   pl.BlockSpec((None, LANES, T), lambda b, p: (b, p, 0))],
        out_shape=[jax.ShapeDtypeStruct((B * T, W_A), BF16),
                   jax.ShapeDtypeStruct((B, W_A, T), F32),
                   jax.ShapeDtypeStruct((B, W_A, T), F32)],
        scratch_shapes=[pltpu.VMEM((T, LANES), BF16), pltpu.VMEM((T, 2 * LANES), BF16)],
        compiler_params=_cparams(("parallel", "parallel")),
        name="moba_prefill",
    )(z, z, z, rope[0], rope[1])


def _ret_prefill_kernel(q_ref, k_ref, v_ref, gb_ref, cos_ref, sin_ref, rg_ref, dmask_ref, din_ref, dout_ref,
                        gch_ref, o_ref, s_ref, *, n_chunks):
    C = RET_CHUNK
    heads = LANES // DK_B
    lane = lax.broadcasted_iota(jnp.int32, (1, LANES), 1)
    hms = [(lane // DK_B) == h for h in range(heads)]
    row_h = lax.broadcasted_iota(jnp.int32, (LANES, LANES), 0) // DK_B
    col_h = lax.broadcasted_iota(jnp.int32, (LANES, LANES), 1) // DV_B
    same_head = row_h == col_h
    seg_ones = jnp.where(same_head, 1.0, 0.0).astype(BF16)
    din = din_ref[...]
    dout = dout_ref[...]
    gch = gch_ref[...]
    rg = rg_ref[...]
    dmask = dmask_ref[...]

    S = jnp.zeros((LANES, LANES), F32)
    for j in range(n_chunks):
        rows = slice(j * C, (j + 1) * C)
        cos = cos_ref[rows, :]
        sin = sin_ref[rows, :]
        q = _rope_lanes(q_ref[rows, :], cos, sin)
        k = _rope_lanes(k_ref[rows, :], cos, sin) * (DK_B ** -0.5)
        kb = k.astype(BF16)
        vb = v_ref[rows, :].astype(BF16)
        o = jnp.dot((q * din).astype(BF16), S.astype(BF16), preferred_element_type=F32)
        q_st = jnp.concatenate([jnp.where(hm, q, 0.0) for hm in hms], axis=0).astype(BF16)
        att = _nt(q_st, kb) * dmask
        res = jnp.dot(att.astype(BF16), vb, preferred_element_type=F32)
        intra = res[0:C, :]
        for h in range(1, heads):
            intra = jnp.where(hms[h], res[h * C:(h + 1) * C, :], intra)
        o = o + intra
        S = S * gch + jnp.where(same_head, _tn((k * dout).astype(BF16), vb), 0.0)
        ms = _rowsum_rep(o * o, seg_ones) * (1.0 / DV_B)
        g = gb_ref[rows, :]
        y = o * lax.rsqrt(ms + EPS) * rg * (g * _sigmoid(g))
        o_ref[rows, :] = y.astype(o_ref.dtype)
    s_ref[...] = S


def _ret_tables(chunk):
    log_g = jnp.log1p(-jnp.exp2(-5.0 - jnp.arange(H_B, dtype=F32)))
    i = jnp.arange(chunk, dtype=F32)
    d_in = jnp.exp((i[:, None] + 1.0) * log_g)
    d_out = jnp.exp((chunk - 1.0 - i)[:, None] * log_g)
    diff = i[:, None] - i[None, :]
    d_mask = jnp.where(diff >= 0, jnp.exp(jnp.maximum(diff, 0.0)[None] * log_g[:, None, None]), 0.0)
    g_chunk = jnp.exp(chunk * log_g)
    return d_in, d_out, d_mask, g_chunk


def _ret_prefill(z, rope, ret_g, B, T):
    C = RET_CHUNK
    assert T % C == 0
    d_in, d_out, d_mask, g_chunk = _ret_tables(C)
    npair = W_B // LANES
    lanes = lambda t: jnp.repeat(t, DK_B, axis=-1)
    din_l = lanes(d_in).reshape(C, npair, LANES).transpose(1, 0, 2)
    dout_l = lanes(d_out).reshape(C, npair, LANES).transpose(1, 0, 2)
    gch_l = lanes(g_chunk).reshape(npair, 1, LANES)
    cpt = AB_TILE // LANES
    col = lambda t: (lambda b, p: (b, t * cpt + p))
    return pl.pallas_call(
        functools.partial(_ret_prefill_kernel, n_chunks=T // C),
        grid=(B, npair),
        in_specs=[pl.BlockSpec((T, LANES), col(3)), pl.BlockSpec((T, LANES), col(4)),
                  pl.BlockSpec((T, LANES), col(5)), pl.BlockSpec((T, LANES), col(6)),
                  pl.BlockSpec((T, LANES), lambda b, p: (0, 0)),
                  pl.BlockSpec((T, LANES), lambda b, p: (0, 0)),
                  pl.BlockSpec((1, LANES), lambda b, p: (0, p)),
                  pl.BlockSpec((None, (LANES // DK_B) * C, C), lambda b, p: (p, 0, 0)),
                  pl.BlockSpec((None, C, LANES), lambda b, p: (p, 0, 0)),
                  pl.BlockSpec((None, C, LANES), lambda b, p: (p, 0, 0)),
                  pl.BlockSpec((None, 1, LANES), lambda b, p: (p, 0, 0))],
        out_specs=[pl.BlockSpec((T, LANES), lambda b, p: (b, p)),
                   pl.BlockSpec((None, None, LANES, LANES), lambda b, p: (b, p, 0, 0))],
        out_shape=[jax.ShapeDtypeStruct((B * T, W_B), BF16),
                   jax.ShapeDtypeStruct((B, npair, LANES, LANES), F32)],
        compiler_params=_cparams(("parallel", "parallel")),
        name="retention_prefill",
    )(z, z, z, z, rope[0], rope[1], ret_g.reshape(1, W_B), d_mask.reshape(npair, -1, C), din_l, dout_l, gch_l)


def _mlstm_prefill_kernel(bias_ref, q_ref, k_ref, v_ref, og_ref, ig_ref, fg_ref, igc_ref, fgc_ref, ng_ref,
                          h_ref, c_ref, n_ref, m_ref, *, n_chunks):
    L = MLSTM_CHUNK
    hd = pl.program_id(1)
    r_id = lax.broadcasted_iota(jnp.int32, (L, L), 0)
    c_id = lax.broadcasted_iota(jnp.int32, (L, L), 1)
    causal = c_id <= r_id
    upper = jnp.where(r_id <= c_id, 1.0, 0.0).astype(F32)
    lower = jnp.where(c_id <= r_id, 1.0, 0.0).astype(F32)
    ones_b = jnp.ones((L, L), BF16)
    i_rows = ig_ref[...] + bias_ref[0, hd]
    b_rows = jnp.dot(_log_sigmoid(fg_ref[...] + bias_ref[1, hd]), upper, preferred_element_type=F32, precision=HIGHEST)
    i_cols = igc_ref[...] + bias_ref[0, hd]
    b_cols = jnp.dot(lower, _log_sigmoid(fgc_ref[...] + bias_ref[1, hd]), preferred_element_type=F32, precision=HIGHEST)
    ng = ng_ref[...]

    Cm = jnp.zeros((L, L), F32)
    n = jnp.zeros((1, L), F32)
    m = jnp.zeros((1, 1), F32)
    for j in range(n_chunks):
        rows = slice(j * L, (j + 1) * L)
        q = q_ref[rows, :]
        k = k_ref[rows, :] * (DH_C ** -0.5)
        v = v_ref[rows, :]
        b_row, i_row = b_rows[j:j + 1, :], i_rows[j:j + 1, :]
        b_col, i_col = b_cols[:, j:j + 1], i_cols[:, j:j + 1]
        b_last = b_row[:, L - 1:L]
        qb = q.astype(BF16)
        kb = k.astype(BF16)
        b_rep = jnp.broadcast_to(b_col, (L, L))
        log_d = jnp.where(causal, (b_rep - b_row) + i_row, NEG_INF)
        m_row = jnp.broadcast_to(jnp.max(log_d, axis=-1, keepdims=True), (L, L))
        s = _nt(qb, kb) * jnp.exp(log_d - m_row)
        sv_ext = jnp.dot(s.astype(BF16), jnp.concatenate([v.astype(BF16), ones_b], axis=1),
                         preferred_element_type=F32)
        sv, s_sum = sv_ext[:, 0:L], sv_ext[:, L:2 * L]
        log_w = (b_last - b_col) + i_col
        m_loc = jnp.max(log_w, axis=0, keepdims=True)
        e_k = jnp.broadcast_to(jnp.exp(log_w - m_loc), (L, L))
        U = _tn((v * e_k).astype(BF16), kb)
        nk = jnp.sum(k * e_k, axis=0, keepdims=True)
        log_inter = b_rep + m
        m_t = jnp.maximum(log_inter, m_row)
        f_intra = jnp.exp(m_row - m_t)
        w_inter = jnp.exp(log_inter - m_t)
        num = sv * f_intra + _nt(qb, Cm.astype(BF16)) * w_inter
        den = s_sum * f_intra + w_inter * _rowsum_rep(q * n)
        h = num / jnp.maximum(jnp.abs(den), jnp.exp(-m_t))
        m_new = jnp.maximum(b_last + m, m_loc)
        decay = jnp.exp(b_last + m - m_new)
        f_k = jnp.exp(m_loc - m_new)
        Cm = decay * Cm + f_k * U
        n = decay * n + f_k * nk
        m = m_new
        hn = h * lax.rsqrt(_rowsum_rep(h * h) * (1.0 / L) + EPS) * ng
        h_ref[rows, :] = (hn * _sigmoid(og_ref[rows, :])).astype(h_ref.dtype)
    c_ref[...] = Cm
    n_ref[...] = n
    m_ref[...] = jnp.broadcast_to(m, m_ref.shape)


def _mlstm_prefill(z, ig, fg, gate_bias, out_g, B, T):
    L = MLSTM_CHUNK
    assert T % L == 0 and DH_C == LANES
    nc = T // L
    col = lambda t: (lambda b, h, bias: (b, t * H_C + h))
    gspec = pl.BlockSpec((None, None, nc, L), lambda b, h, bias: (b, h, 0, 0))
    gcspec = pl.BlockSpec((None, None, L, nc), lambda b, h, bias: (b, h, 0, 0))
    grid_spec = pltpu.PrefetchScalarGridSpec(
        num_scalar_prefetch=1,
        grid=(B, H_C),
        in_specs=[pl.BlockSpec((T, LANES), col(0)), pl.BlockSpec((T, LANES), col(1)),
                  pl.BlockSpec((T, LANES), col(2)), pl.BlockSpec((T, LANES), col(3)),
                  gspec, gspec, gcspec, gcspec,
                  pl.BlockSpec((1, LANES), lambda b, h, bias: (0, h))],
        out_specs=[pl.BlockSpec((T, LANES), lambda b, h, bias: (b, h)),
                   pl.BlockSpec((None, None, L, L), lambda b, h, bias: (b, h, 0, 0)),
                   pl.BlockSpec((None, None, 1, L), lambda b, h, bias: (b, h, 0, 0)),
                   pl.BlockSpec((None, None, 1, LANES), lambda b, h, bias: (b, h, 0, 0))],
    )
    return pl.pallas_call(
        functools.partial(_mlstm_prefill_kernel, n_chunks=nc),
        grid_spec=grid_spec,
        out_shape=[jax.ShapeDtypeStruct((B * T, W_C), BF16),
                   jax.ShapeDtypeStruct((B, H_C, L, L), F32),
                   jax.ShapeDtypeStruct((B, H_C, 1, L), F32),
                   jax.ShapeDtypeStruct((B, H_C, 1, LANES), F32)],
        compiler_params=_cparams(("parallel", "parallel")),
        name="mlstm_prefill",
    )(gate_bias, z, z, z, z, ig, fg, ig.swapaxes(2, 3), fg.swapaxes(2, 3), out_g.reshape(1, W_C))


SELECT_PAGES = 16


def _moba_select_kernel(pt_ref, *refs, pages_per_block):
    k_refs = refs[:SELECT_PAGES]
    q_ref, sel_ref, gate_ref, qb_ref = refs[SELECT_PAGES:]
    s = pl.program_id(1)
    ppb = pages_per_block
    blocks_per_step = SELECT_PAGES // ppb
    H, dh, page = k_refs[0].shape

    @pl.when(s == 0)
    def _():
        eye = _eye(dh)
        q = q_ref[...]
        for h in range(H):
            qb_ref[h] = jnp.broadcast_to(_row_to_col(q[h:h + 1, :], eye), (dh, page))

    qb = qb_ref[...]
    for i in range(blocks_per_step):
        acc = k_refs[i * ppb][...]
        for g in range(1, ppb):
            acc = acc + k_refs[i * ppb + g][...]
        prod = acc * qb
        part = prod[:, 0:8, :]
        for r in range(1, prod.shape[1] // 8):
            part = part + prod[:, 8 * r:8 * (r + 1), :]
        gate_ref[:, s * blocks_per_step + i] = part

    @pl.when(s == pl.num_programs(1) - 1)
    def _():
        nb = gate_ref.shape[1]
        part_sum = jnp.sum(gate_ref[...], axis=2).reshape(H * nb, page)
        col = jnp.sum(part_sum, axis=-1, keepdims=True) * (1.0 / MOBA_BLOCK)
        eye = _eye(nb)
        head_row = lax.broadcasted_iota(jnp.int32, (H, 1), 0)
        gate = jnp.zeros((H, nb), F32)
        for h in range(H):
            gate = jnp.where(head_row == h, _col_to_row(col[h * nb:(h + 1) * nb, :], eye), gate)
        blk_id = lax.broadcasted_iota(jnp.int32, (1, nb), 1)
        rank = jnp.zeros(gate.shape, jnp.int32)
        for m in range(nb):
            gm = gate[:, m:m + 1]
            beats = jnp.logical_or(gm > gate, jnp.logical_and(gm == gate, m < blk_id))
            rank = rank + jnp.where(beats, 1, 0)
        out_col = lax.broadcasted_iota(jnp.int32, sel_ref.shape, 1)
        out = jnp.zeros(sel_ref.shape, jnp.int32)
        for r in range(MOBA_TOPK):
            idx = jnp.sum(jnp.where(rank == r, blk_id, 0), axis=1, keepdims=True)
            out = jnp.where(out_col == r, idx, out)
        sel_ref[...] = out


def _moba_select(cache_kt, page_table, q):
    Bs, n_pages = page_table.shape
    _, H, dh, page = cache_kt.shape
    ppb = MOBA_BLOCK // page
    nb = n_pages // ppb
    assert nb >= MOBA_TOPK and n_pages % SELECT_PAGES == 0 and SELECT_PAGES % ppb == 0

    def page_spec(i):
        return pl.BlockSpec((None, H, dh, page), lambda b, s, pt: (pt[b, s * SELECT_PAGES + i], 0, 0, 0))

    grid_spec = pltpu.PrefetchScalarGridSpec(
        num_scalar_prefetch=1,
        grid=(Bs, n_pages // SELECT_PAGES),
        in_specs=[page_spec(i) for i in range(SELECT_PAGES)]
        + [pl.BlockSpec((None, H, dh), lambda b, s, pt: (b, 0, 0))],
        out_specs=pl.BlockSpec((None, H, 8), lambda b, s, pt: (b, 0, 0)),
        scratch_shapes=[pltpu.VMEM((H, nb, 8, page), F32), pltpu.VMEM((H, dh, page), F32)],
    )
    return pl.pallas_call(
        functools.partial(_moba_select_kernel, pages_per_block=ppb),
        grid_spec=grid_spec,
        out_shape=jax.ShapeDtypeStruct((Bs, H, 8), jnp.int32),
        compiler_params=_cparams(("parallel", "arbitrary")),
        name="moba_select",
    )(page_table, *([cache_kt] * SELECT_PAGES), q)


def _moba_decode_kernel(pt_ref, sel_ref, q_ref, kn_ref, vn_ref, ck_ref, cv_ref, o_ref, kbuf, vbuf, sem,
                        *, pages_per_block, page):
    b = pl.program_id(0)
    nb_ = pl.num_programs(0)
    H = q_ref.shape[0]
    ppb = pages_per_block

    def copies(bb, slot):
        out = []
        for h in range(H):
            for r in range(MOBA_TOPK):
                blk = sel_ref[bb, h, r]
                for g in range(ppb):
                    pg = pt_ref[bb, blk * ppb + g]
                    dst = pl.ds((r * ppb + g) * page, page)
                    out.append(pltpu.make_async_copy(ck_ref.at[pg, h], kbuf.at[slot, h, :, dst], sem.at[slot, 0]))
                    out.append(pltpu.make_async_copy(cv_ref.at[pg, h], vbuf.at[slot, h, :, dst], sem.at[slot, 1]))
        return out

    slot = b % 2

    @pl.when(b == 0)
    def _():
        for c in copies(b, slot):
            c.start()

    @pl.when(b + 1 < nb_)
    def _():
        for c in copies(b + 1, 1 - slot):
            c.start()

    for c in copies(b, slot):
        c.wait()

    row = lax.broadcasted_iota(jnp.int32, (H, 1), 0)
    out = jnp.zeros(o_ref.shape, F32)
    q = q_ref[...]
    kn = kn_ref[...]
    vn = vn_ref[...]
    for h in range(H):
        qh = q[h:h + 1, :] * (DH_A ** -0.5)
        q8 = jnp.broadcast_to(qh, (8, qh.shape[1])).astype(BF16)
        kh = kbuf[slot, h].astype(BF16)
        vh = vbuf[slot, h].astype(BF16)
        s = jnp.dot(q8, kh, preferred_element_type=F32)[0:1, :]
        s_self = jnp.sum(qh * kn[h:h + 1, :], axis=-1, keepdims=True)
        m = jnp.maximum(jnp.max(s, axis=-1, keepdims=True), s_self)
        p = jnp.exp(s - m)
        p_self = jnp.exp(s_self - m)
        l = jnp.sum(p, axis=-1, keepdims=True) + p_self
        p8 = jnp.broadcast_to(p, (8, p.shape[1])).astype(BF16)
        pv = _nt(p8, vh)[0:1, :]
        oh = (pv + p_self * vn[h:h + 1, :]) / l
        out = jnp.where(row == h, oh, out)
    o_ref[...] = out


def _moba_decode(cache_kt, cache_vt, page_table, sel, q, k_new, v_new):
    Bs, n_pages = page_table.shape
    _, H, dh, page = cache_kt.shape
    ppb = MOBA_BLOCK // page
    rows = MOBA_TOPK * MOBA_BLOCK
    vec = pl.BlockSpec((None, H, dh), lambda b, pt, sl: (b, 0, 0))
    grid_spec = pltpu.PrefetchScalarGridSpec(
        num_scalar_prefetch=2,
        grid=(Bs,),
        in_specs=[vec, vec, vec, pl.BlockSpec(memory_space=pl.ANY), pl.BlockSpec(memory_space=pl.ANY)],
        out_specs=vec,
        scratch_shapes=[pltpu.VMEM((2, H, dh, rows), F32), pltpu.VMEM((2, H, dh, rows), F32),
                        pltpu.SemaphoreType.DMA((2, 2))],
    )
    return pl.pallas_call(
        functools.partial(_moba_decode_kernel, pages_per_block=ppb, page=page),
        grid_spec=grid_spec,
        out_shape=jax.ShapeDtypeStruct((Bs, H, dh), F32),
        compiler_params=_cparams(("arbitrary",)),
        name="moba_decode",
    )(page_table, sel, q, k_new, v_new, cache_kt, cache_vt)


DECODE_ROWS = 8


def _per_row(kernel, shared):
    def wrapped(*refs):
        rows = next(r.shape[0] for i, r in enumerate(refs) if i not in shared)
        for r in range(rows):
            kernel(*[ref if i in shared else ref.at[r] for i, ref in enumerate(refs)])
    return wrapped


def _ret_decode_kernel(q_ref, k_ref, v_ref, gb_ref, rg_ref, g_ref, s_ref, o_ref, so_ref):
    H, dk = q_ref.shape
    eye = _eye(dk)
    row = lax.broadcasted_iota(jnp.int32, (H, 1), 0)
    q = q_ref[...]
    k = k_ref[...]
    v = v_ref[...]
    g_all = g_ref[...]
    out = jnp.zeros(o_ref.shape, F32)
    for h in range(H):
        qh, kh, vh = q[h:h + 1, :], k[h:h + 1, :], v[h:h + 1, :]
        g = g_all[h:h + 1, 0:1]
        S = s_ref[h]
        att = jnp.sum(qh * kh, axis=-1, keepdims=True)
        cross = jnp.sum(_row_to_col(qh * g, eye) * S, axis=0, keepdims=True)
        out = jnp.where(row == h, att * vh + cross, out)
        so_ref[h] = S * g + _row_to_col(kh, eye) * vh
    gb = gb_ref[...]
    y = out * lax.rsqrt(jnp.mean(out * out, axis=-1, keepdims=True) + EPS) * rg_ref[...]
    o_ref[...] = y * (gb * _sigmoid(gb))


def _ret_decode(q, k, v, gb, ret_g, state):
    Bs, H, dk = q.shape
    log_g = jnp.log1p(-jnp.exp2(-5.0 - jnp.arange(H_B, dtype=F32)))
    g = jnp.broadcast_to(jnp.exp(1.0 * log_g)[:, None], (H, LANES))
    R = DECODE_ROWS if Bs % DECODE_ROWS == 0 else 1
    vec = pl.BlockSpec((R, H, dk), lambda b: (b, 0, 0))
    st = pl.BlockSpec((R, H, dk, DV_B), lambda b: (b, 0, 0, 0))
    return pl.pallas_call(
        _per_row(_ret_decode_kernel, shared=(4, 5)),
        grid=(Bs // R,),
        in_specs=[vec, vec, vec, vec, pl.BlockSpec((H, dk), lambda b: (0, 0)),
                  pl.BlockSpec((H, LANES), lambda b: (0, 0)), st],
        out_specs=[vec, st],
        out_shape=[jax.ShapeDtypeStruct((Bs, H, DV_B), F32), jax.ShapeDtypeStruct(state.shape, F32)],
        compiler_params=_cparams(("parallel",)),
        name="retention_decode",
    )(q, k, v, gb, ret_g.reshape(H, DV_B), g, state)


def _mlstm_decode_kernel(q_ref, k_ref, v_ref, og_ref, ig_ref, fg_ref, bi_ref, bf_ref, ng_ref, c_ref, n_ref,
                         m_ref, h_ref, co_ref, no_ref, mo_ref):
    H, d = q_ref.shape
    eye = _eye(d)
    row = lax.broadcasted_iota(jnp.int32, (H, 1), 0)
    q = q_ref[...]
    k = k_ref[...] * (DH_C ** -0.5)
    v = v_ref[...]
    n0 = n_ref[...]
    ig = ig_ref[...] + bi_ref[...]
    b = _log_sigmoid(fg_ref[...] + bf_ref[...])
    m0 = m_ref[...]
    log_inter = b + m0
    m_t = jnp.maximum(log_inter, ig)
    w_intra = jnp.exp(ig - m_t)
    w_inter = jnp.exp(log_inter - m_t)
    s = jnp.sum(q * k, axis=-1, keepdims=True) * w_intra
    den = s + w_inter * jnp.sum(q * n0, axis=-1, keepdims=True)
    scale = 1.0 / jnp.maximum(jnp.abs(den), jnp.exp(-m_t))
    w_k = w_intra
    decay = w_inter
    hs = jnp.zeros(h_ref.shape, F32)
    for h in range(H):
        qh, kh, vh = q[h:h + 1, :], k[h:h + 1, :], v[h:h + 1, :]
        Cm = c_ref[h]
        cq = _col_to_row(jnp.sum(Cm * qh, axis=1, keepdims=True), eye)
        num = s[h:h + 1, :] * vh + cq * w_inter[h:h + 1, :]
        hs = jnp.where(row == h, num * scale[h:h + 1, :], hs)
        co_ref[h] = decay[h:h + 1, :] * Cm + _row_to_col(vh * w_k[h:h + 1, :], eye) * kh
    no_ref[...] = decay * n0 + k * w_k
    mo_ref[...] = m_t
    hn = hs * lax.rsqrt(jnp.mean(hs * hs, axis=-1, keepdims=True) + EPS) * ng_ref[...]
    h_ref[...] = hn * _sigmoid(og_ref[...])


def _mlstm_decode(q, k, v, og, ig, fg, gate_bias, out_g, C0, n0, m0):
    Bs, H, d = q.shape
    R = DECODE_ROWS if Bs % DECODE_ROWS == 0 else 1
    vec = pl.BlockSpec((R, H, d), lambda b: (b, 0, 0))
    sc = pl.BlockSpec((R, H, 1), lambda b: (b, 0, 0))
    st = pl.BlockSpec((R, H, d, d), lambda b: (b, 0, 0, 0))
    bias = pl.BlockSpec((H, 1), lambda b: (0, 0))
    return pl.pallas_call(
        _per_row(_mlstm_decode_kernel, shared=(6, 7, 8)),
        grid=(Bs // R,),
        in_specs=[vec, vec, vec, vec, sc, sc, bias, bias, pl.BlockSpec((H, d), lambda b: (0, 0)), st, vec, sc],
        out_specs=[vec, st, vec, sc],
        out_shape=[jax.ShapeDtypeStruct((Bs, H, d), F32), jax.ShapeDtypeStruct(C0.shape, F32),
                   jax.ShapeDtypeStruct((Bs, H, d), F32), jax.ShapeDtypeStruct((Bs, H, 1), F32)],
        compiler_params=_cparams(("parallel",)),
        name="mlstm_decode",
    )(q, k, v, og, ig, fg, gate_bias[0].reshape(H, 1), gate_bias[1].reshape(H, 1), out_g.reshape(H, d), C0, n0, m0)


def _rope_tables(pos):
    half = DH_A // 2
    inv = ROPE_THETA ** (-jnp.arange(half, dtype=F32) / half)
    ang = pos[:, None] * inv[None, :]
    cos = jnp.cos(ang)
    sin = jnp.sin(ang)
    reps = LANES // DH_A
    cos_t = jnp.tile(jnp.concatenate([cos, cos], axis=-1), (1, reps))
    sin_t = jnp.tile(jnp.concatenate([-sin, sin], axis=-1), (1, reps))
    return cos_t, sin_t


def _row_tile(m, pref):
    return pref if m % pref == 0 else m


def kernel(x_prompt, x_sample, cache_k, cache_v, page_table, state_ret, state_mlstm_C, state_mlstm_n,
           state_mlstm_m, state_ffn_conv, ab_norm_g, ab_w_in, ab_ret_norm_g, ab_w_out, c_norm_g, c_w_in,
           c_gate_bias, c_out_norm_g, c_w_out, ffn_norm_g, ffn_w1, ffn_w3, ffn_conv_w, ffn_conv_b, ffn_w2,
           final_norm_g):
    Bp, Tp, D = x_prompt.shape
    Bs, Ts, _ = x_sample.shape
    assert Ts == 1
    past_len = page_table.shape[1] * cache_k.shape[2]
    Mp = Bp * Tp
    xp = x_prompt.reshape(Mp, D)
    xs = x_sample.reshape(Bs, D)
    tm_p = _row_tile(Tp, 1024)
    tm_in = _row_tile(Tp, 2048)
    tm_ffn = _row_tile(Tp, 1024)
    tm_s = Bs
    rope_p = _rope_tables(jnp.arange(Tp, dtype=jnp.int32).astype(F32))
    rope_s = _rope_tables(jnp.full((Bs,), past_len, jnp.int32).astype(F32))
    gw = 2 * H_C

    outs = {}

    w_in = ab_w_in[0].astype(BF16)
    w_out = ab_w_out[0].astype(BF16)
    w_out_a, w_out_b = w_out[:W_A], w_out[W_A:]

    z = _norm_matmul(xp, ab_norm_g[0], w_in, tm_p, IN_AB // 2)
    oa, kt, vt = _moba_prefill(z, rope_p, Bp, Tp)
    ob, s_pair = _ret_prefill(z, rope_p, ab_ret_norm_g[0], Bp, Tp)
    xp = _proj_residual([oa, ob], [w_out_a, w_out_b], xp, tm_p)
    outs["k_prompt"] = kt.reshape(Bp, H_A, DH_A, Tp).transpose(0, 3, 1, 2)[None]
    outs["v_prompt"] = vt.reshape(Bp, H_A, DH_A, Tp).transpose(0, 3, 1, 2)[None]
    hb = LANES // DK_B
    s_heads = jnp.stack([s_pair[:, :, i * DK_B:(i + 1) * DK_B, i * DV_B:(i + 1) * DV_B] for i in range(hb)], axis=2)
    outs["ret_prompt"] = s_heads.reshape(1, Bp, H_B, DK_B, DV_B)

    zs = _norm_matmul(xs, ab_norm_g[0], w_in, tm_s, AB_TILE, rope=rope_s)
    seg = lambda t: zs[:, t * AB_TILE:(t + 1) * AB_TILE].reshape(Bs, H_A, DH_A)
    qa_s, ka_s, va_s, qb_s, kb_s, vb_s, gb_s = (seg(t) for t in range(7))
    cache_kt = cache_k[0].transpose(0, 2, 3, 1)
    cache_vt = cache_v[0].transpose(0, 2, 3, 1)
    sel = _moba_select(cache_kt, page_table, qa_s)
    oa_s = _moba_decode(cache_kt, cache_vt, page_table, sel, qa_s, ka_s, va_s)
    ob_s, s_new = _ret_decode(qb_s, kb_s, vb_s, gb_s, ab_ret_norm_g[0], state_ret[0])
    xs = _proj_residual([oa_s.reshape(Bs, W_A).astype(BF16), ob_s.reshape(Bs, W_B).astype(BF16)],
                        [w_out_a, w_out_b], xs, tm_s)
    outs["k_sample"] = ka_s.reshape(1, Bs, 1, H_A, DH_A)
    outs["v_sample"] = va_s.reshape(1, Bs, 1, H_A, DH_A)
    outs["ret_sample"] = s_new[None]

    conv_p, conv_s = [], []

    w1_b, w3_b, w2_b = ffn_w1.astype(BF16), ffn_w3.astype(BF16), ffn_w2.astype(BF16)

    def ffn_both(l, xp, xs, final_g):
        st = state_ffn_conv[l]
        xp, a_tail = _ffn(xp, ffn_norm_g[l], l, w1_b, w3_b, ffn_conv_w[l], ffn_conv_b[l], w2_b, final_g, tm_ffn,
                          seq_len=Tp)
        xs, a_s = _ffn(xs, ffn_norm_g[l], l, w1_b, w3_b, ffn_conv_w[l], ffn_conv_b[l], w2_b, final_g, tm_s,
                       state=(st[:, 0, :], st[:, 1, :]))
        conv_p.append(a_tail[:, 8 - (CONV_W - 1):, :])
        conv_s.append(jnp.stack([st[:, 1, :], a_s], axis=1))
        return xp, xs

    xp, xs = ffn_both(0, xp, xs, None)

    w_in = c_w_in[0]
    w_all = w_in.astype(BF16)
    w_gate = jnp.pad(w_in[:, 4 * W_C:], ((0, 0), (0, LANES - gw))).astype(BF16)
    w_out = c_w_out[0].astype(BF16)
    L = MLSTM_CHUNK

    z, zg = _norm_matmul_side(xp, c_norm_g[0], w_all, 4 * W_C, w_gate, tm_in, 1024)
    gates = zg[:, :gw].reshape(Bp, Tp // L, L, 2, H_C).transpose(3, 0, 4, 1, 2)
    h, C_p, n_p, m_p = _mlstm_prefill(z, gates[0], gates[1], c_gate_bias[0], c_out_norm_g[0], Bp, Tp)
    xp = _proj_residual([h], [w_out], xp, tm_p)
    outs["C_prompt"] = C_p[None]
    outs["n_prompt"] = n_p.reshape(1, Bp, H_C, DH_C)
    outs["m_prompt"] = m_p[:, :, 0, 0][None]

    zs, zgs = _norm_matmul_side(xs, c_norm_g[0], w_all, 4 * W_C, w_gate, tm_s, 1024)
    segc = lambda t: zs[:, t * W_C:(t + 1) * W_C].reshape(Bs, H_C, DH_C)
    ig_s = zgs[:, :H_C][:, :, None]
    fg_s = zgs[:, H_C:gw][:, :, None]
    h_s, C_s, n_s, m_s = _mlstm_decode(segc(0), segc(1), segc(2), segc(3), ig_s, fg_s, c_gate_bias[0], c_out_norm_g[0],
                                       state_mlstm_C[0], state_mlstm_n[0], state_mlstm_m[0][:, :, None])
    xs = _proj_residual([h_s.reshape(Bs, W_C).astype(BF16)], [w_out], xs, tm_s)
    outs["C_sample"] = C_s[None]
    outs["n_sample"] = n_s[None]
    outs["m_sample"] = m_s[:, :, 0][None]

    xp, xs = ffn_both(1, xp, xs, final_norm_g)

    return (xp.reshape(Bp, Tp, D), xs.reshape(Bs, 1, D),
            outs["k_prompt"], outs["v_prompt"], outs["k_sample"], outs["v_sample"],
            outs["ret_prompt"], outs["ret_sample"], outs["C_prompt"], outs["C_sample"],
            outs["n_prompt"], outs["n_sample"], outs["m_prompt"], outs["m_sample"],
            jnp.stack(conv_p), jnp.stack(conv_s))
```

```python
import functools

import jax
import jax.numpy as jnp
from jax import lax
from jax.experimental import pallas as pl
from jax.experimental.pallas import tpu as pltpu

F32 = jnp.float32
BF16 = jnp.bfloat16
HIGHEST = lax.Precision.HIGHEST

LANES = 128
D_MODEL = 1024
H_A = 8
DH_A = 64
MOBA_BLOCK = 256
MOBA_TOPK = 3
H_B = 8
DK_B = 64
DV_B = 64
RET_CHUNK = 128
H_C = 8
DH_C = D_MODEL // H_C
MLSTM_CHUNK = 128
D_FF = 11 * D_MODEL // 4
CONV_W = 3
ROPE_THETA = 10000.0
EPS = 1e-6
W_A = H_A * DH_A
W_B = H_B * DV_B
W_C = H_C * DH_C
IN_AB = 3 * W_A + 2 * H_B * DK_B + 2 * W_B
AB_TILE = 512
AB_ROPE_TILES = (0, 1, 3, 4)
AB_KB_TILE = 4
VMEM_LIMIT = 56 * 1024 * 1024

MOBA_Q_SPLIT = 2
NEG_INF = float("-inf")
LOG2_E = 1.4426950408889634


def _cparams(sem):
    return pltpu.CompilerParams(dimension_semantics=sem, vmem_limit_bytes=VMEM_LIMIT)


def _nt(a, b, **kw):
    return lax.dot_general(a, b, (((1,), (1,)), ((), ())), preferred_element_type=F32, **kw)


def _tn(a, b, **kw):
    return lax.dot_general(a, b, (((0,), (0,)), ((), ())), preferred_element_type=F32, **kw)


def _rms_rows(x, g):
    ms = jnp.mean(x * x, axis=-1, keepdims=True)
    return x * lax.rsqrt(ms + EPS) * g


def _eye(n):
    return lax.broadcasted_iota(jnp.int32, (n, n), 0) == lax.broadcasted_iota(jnp.int32, (n, n), 1)


def _row_to_col(row, eye):
    return jnp.sum(jnp.where(eye, row, 0.0), axis=1, keepdims=True)


def _col_to_row(col, eye):
    return jnp.sum(jnp.where(eye, col, 0.0), axis=0, keepdims=True)


def _rowsum_rep(x, ones=None):
    if ones is None:
        ones = jnp.ones((x.shape[1], LANES), BF16)
    hi = x.astype(BF16)
    lo = (x - hi.astype(F32)).astype(BF16)
    return (jnp.dot(hi, ones, preferred_element_type=F32) + jnp.dot(lo, ones, preferred_element_type=F32))


def _log_sigmoid(x):
    return jnp.minimum(x, 0.0) - jnp.log1p(jnp.exp(-jnp.abs(x)))


def _sigmoid(x):
    return 1.0 / (1.0 + jnp.exp(-x))


def _gelu_tanh(x):
    c = 0.7978845608028654
    return (0.5 * x) * (1.0 + jnp.tanh(x * (c + (0.044715 * c) * (x * x))))


def _norm_matmul_kernel(x_ref, g_ref, w_ref, cos_ref, sin_ref, o_ref, xn_ref, *, rope_tiles, scale_tile, scale):
    j = pl.program_id(1)

    @pl.when(j == 0)
    def _():
        xn_ref[...] = _rms_rows(x_ref[...], g_ref[...]).astype(BF16)

    z = jnp.dot(xn_ref[...], w_ref[...], preferred_element_type=F32)
    if not rope_tiles:
        o_ref[...] = z
        return

    is_rope = functools.reduce(jnp.logical_or, [j == t for t in rope_tiles])

    @pl.when(is_rope)
    def _():
        sc = jnp.where(j == scale_tile, scale, 1.0).astype(F32)
        cos = cos_ref[...]
        sin = sin_ref[...]
        for c in range(z.shape[1] // LANES):
            cols = slice(c * LANES, (c + 1) * LANES)
            o_ref[:, cols] = _rope_lanes(z[:, cols], cos, sin) * sc

    @pl.when(jnp.logical_not(is_rope))
    def _():
        o_ref[...] = z


def _norm_matmul_side_kernel(x_ref, g_ref, w_ref, ws_ref, o_ref, os_ref, xn_ref):
    @pl.when(pl.program_id(1) == 0)
    def _():
        xn_ref[...] = _rms_rows(x_ref[...], g_ref[...]).astype(BF16)
        os_ref[...] = jnp.dot(xn_ref[...], ws_ref[...], preferred_element_type=F32)

    o_ref[...] = jnp.dot(xn_ref[...], w_ref[...], preferred_element_type=F32)


def _norm_matmul_side(x, g, w, n_cols, w_side, tm, tn):
    M, D = x.shape
    N = n_cols
    Ns = w_side.shape[1]
    assert M % tm == 0 and N % tn == 0
    return pl.pallas_call(
        _norm_matmul_side_kernel,
        grid=(M // tm, N // tn),
        in_specs=[pl.BlockSpec((tm, D), lambda i, j: (i, 0)),
                  pl.BlockSpec((1, D), lambda i, j: (0, 0)),
                  pl.BlockSpec((D, tn), lambda i, j: (0, j)),
                  pl.BlockSpec((D, Ns), lambda i, j: (0, 0))],
        out_specs=[pl.BlockSpec((tm, tn), lambda i, j: (i, j)),
                   pl.BlockSpec((tm, Ns), lambda i, j: (i, 0))],
        out_shape=[jax.ShapeDtypeStruct((M, N), F32), jax.ShapeDtypeStruct((M, Ns), F32)],
        scratch_shapes=[pltpu.VMEM((tm, D), BF16)],
        compiler_params=_cparams(("parallel", "arbitrary")),
        name="norm_matmul_side",
    )(x, g.reshape(1, D), w, w_side)


def _norm_matmul(x, g, w, tm, tn, rope=None):
    M, D = x.shape
    N = w.shape[1]
    assert M % tm == 0 and N % tn == 0
    if rope is None:
        cos = sin = jnp.zeros((8, LANES), F32)
        tab_spec = pl.BlockSpec((8, LANES), lambda i, j: (0, 0))
        kern = functools.partial(_norm_matmul_kernel, rope_tiles=(), scale_tile=-1, scale=1.0)
    else:
        cos, sin = rope
        nt = cos.shape[0] // tm
        tab_spec = pl.BlockSpec((tm, LANES), lambda i, j: (i % nt, 0))
        kern = functools.partial(_norm_matmul_kernel, rope_tiles=AB_ROPE_TILES, scale_tile=AB_KB_TILE,
                                 scale=DK_B ** -0.5)
    return pl.pallas_call(
        kern,
        grid=(M // tm, N // tn),
        in_specs=[pl.BlockSpec((tm, D), lambda i, j: (i, 0)),
                  pl.BlockSpec((1, D), lambda i, j: (0, 0)),
                  pl.BlockSpec((D, tn), lambda i, j: (0, j)),
                  tab_spec, tab_spec],
        out_specs=pl.BlockSpec((tm, tn), lambda i, j: (i, j)),
        out_shape=jax.ShapeDtypeStruct((M, N), F32),
        scratch_shapes=[pltpu.VMEM((tm, D), BF16)],
        compiler_params=_cparams(("parallel", "arbitrary")),
        name="norm_matmul",
    )(x, g.reshape(1, D), w, cos, sin)


def _proj_residual_kernel(*refs, n_in):
    a_refs = refs[:n_in]
    w_refs = refs[n_in:2 * n_in]
    res_ref = refs[2 * n_in]
    o_ref = refs[2 * n_in + 1]
    y = res_ref[...]
    acc = None
    for a_ref, w_ref in zip(a_refs, w_refs):
        d = jnp.dot(a_ref[...], w_ref[...], preferred_element_type=F32)
        acc = d if acc is None else acc + d
    o_ref[...] = y + acc


def _proj_residual(acts, ws, res, tm):
    M, D = res.shape
    n_in = len(acts)
    in_specs = ([pl.BlockSpec((tm, a.shape[1]), lambda i: (i, 0)) for a in acts]
                + [pl.BlockSpec(w.shape, lambda i: (0, 0)) for w in ws]
                + [pl.BlockSpec((tm, D), lambda i: (i, 0))])
    return pl.pallas_call(
        functools.partial(_proj_residual_kernel, n_in=n_in),
        grid=(M // tm,),
        in_specs=in_specs,
        out_specs=pl.BlockSpec((tm, D), lambda i: (i, 0)),
        out_shape=jax.ShapeDtypeStruct((M, D), F32),
        compiler_params=_cparams(("parallel",)),
        name="proj_residual",
    )(*acts, *ws, res)


PREV_ROWS = 16
FFN_CHUNK = 256


def _ffn_kernel(*refs, seq_mode, tiles_per_seq, final_norm):
    if seq_mode:
        (x_ref, xp_ref, g_ref, w1_ref, w3_ref, cw_ref, cb_ref, w2_ref, fg_ref, o_ref, a_ref, y_ref) = refs
    else:
        (x_ref, s0_ref, s1_ref, g_ref, w1_ref, w3_ref, cw_ref, cb_ref, w2_ref, fg_ref, o_ref, a_ref, y_ref) = refs
    i = pl.program_id(0)
    x = x_ref[...]
    tm = x.shape[0]
    F = w1_ref.shape[1]
    xn = _rms_rows(x, g_ref[...]).astype(BF16)
    if seq_mode:
        xpn = _rms_rows(xp_ref[...], g_ref[...]).astype(BF16)
        has_prev = ((i % tiles_per_seq) != 0).astype(F32)
        row8 = lax.broadcasted_iota(jnp.int32, (8, 1), 0)
    for c0 in range(0, F, FFN_CHUNK):
        cols = slice(c0, min(c0 + FFN_CHUNK, F))
        w1c = w1_ref[:, cols]
        a = jnp.dot(xn, w1c, preferred_element_type=F32)
        gate = jnp.dot(xn, w3_ref[:, cols], preferred_element_type=F32)
        a_ref[:, cols] = a[tm - a_ref.shape[0]:, :]
        if seq_mode:
            ap = jnp.dot(xpn, w1c, preferred_element_type=F32)
            p1 = ap[PREV_ROWS - 1:PREV_ROWS, :] * has_prev
            p2 = ap[PREV_ROWS - 2:PREV_ROWS - 1, :] * has_prev
            a1 = pltpu.roll(a, 1, 0)
            a2 = pltpu.roll(a, 2, 0)
            top1 = jnp.where(row8 == 0, p1, a1[0:8, :])
            top2 = jnp.where(row8 == 0, p2, jnp.where(row8 == 1, p1, a2[0:8, :]))
            a1 = jnp.concatenate([top1, a1[8:, :]], axis=0)
            a2 = jnp.concatenate([top2, a2[8:, :]], axis=0)
        else:
            a1 = s1_ref[:, cols]
            a2 = s0_ref[:, cols]
        ac = cb_ref[:, cols] + a2 * cw_ref[0:1, cols]
        ac = ac + a1 * cw_ref[1:2, cols]
        ac = ac + a * cw_ref[2:3, cols]
        y_ref[:, cols] = (_gelu_tanh(ac) * gate).astype(BF16)
    acc = x + jnp.dot(y_ref[...], w2_ref[...], preferred_element_type=F32)
    if final_norm:
        acc = _rms_rows(acc, fg_ref[...])
    o_ref[...] = acc


def _ffn(x, norm_g, layer, w1, w3, conv_w, conv_b, w2, final_g, tm, seq_len=None, state=None):
    M, D = x.shape
    F = w1.shape[2]
    seq_mode = state is None
    final_norm = final_g is not None
    fg = (final_g if final_norm else jnp.ones((D,), F32)).reshape(1, D)
    whole = lambda shape: pl.BlockSpec(shape, lambda i: (0, 0), pipeline_mode=pl.Buffered(1))
    stacked = lambda shape: pl.BlockSpec((None,) + shape, lambda i: (layer, 0, 0), pipeline_mode=pl.Buffered(1))
    common = [whole((1, D)), stacked((D, F)), stacked((D, F)), whole((CONV_W, F)), whole((1, F)), stacked((F, D)),
              whole((1, D))]
    common_args = (norm_g.reshape(1, D), w1, w3, conv_w, conv_b.reshape(1, F), w2, fg)
    x_spec = pl.BlockSpec((tm, D), lambda i: (i, 0))
    if seq_mode:
        assert seq_len % tm == 0 and tm % PREV_ROWS == 0
        r = tm // PREV_ROWS
        in_specs = [x_spec, pl.BlockSpec((PREV_ROWS, D), lambda i: (jnp.maximum(i * r - 1, 0), 0))] + common
        args = (x, x) + common_args
        tiles_per_seq = seq_len // tm
    else:
        s_spec = pl.BlockSpec((tm, F), lambda i: (i, 0))
        in_specs = [x_spec, s_spec, s_spec] + common
        args = (x, state[0], state[1]) + common_args
        tiles_per_seq = 1
    if seq_mode:
        a_spec = pl.BlockSpec((None, 8, F), lambda i: (i // tiles_per_seq, 0, 0))
        a_shape = jax.ShapeDtypeStruct((M // seq_len, 8, F), F32)
    else:
        a_spec = pl.BlockSpec((tm, F), lambda i: (i, 0))
        a_shape = jax.ShapeDtypeStruct((M, F), F32)
    return pl.pallas_call(
        functools.partial(_ffn_kernel, seq_mode=seq_mode, tiles_per_seq=tiles_per_seq, final_norm=final_norm),
        grid=(M // tm,),
        in_specs=in_specs,
        out_specs=[pl.BlockSpec((tm, D), lambda i: (i, 0)), a_spec],
        out_shape=[jax.ShapeDtypeStruct((M, D), F32), a_shape],
        scratch_shapes=[pltpu.VMEM((tm, F), BF16)],
        compiler_params=_cparams(("arbitrary",)),
        name="conv_ffn",
    )(*args)


def _rope_lanes(x, cos, sin):
    lane = lax.broadcasted_iota(jnp.int32, (1, LANES), 1)
    first_half = (lane % DH_A) < (DH_A // 2)
    partner = jnp.where(first_half, pltpu.roll(x, LANES - DH_A // 2, 1), pltpu.roll(x, DH_A // 2, 1))
    return x * cos + partner * sin


def _moba_prefill_kernel(q_ref, k_ref, v_ref, cos_ref, sin_ref, o_ref, kt_ref, vt_ref, kb_ref, vb_ref, *, nb):
    blk = MOBA_BLOCK
    heads = LANES // DH_A
    T = q_ref.shape[0]
    cos = cos_ref[...]
    sin = sin_ref[...]
    k = _rope_lanes(k_ref[...], cos, sin)
    kt_ref[...] = k.T
    vt_ref[...] = v_ref[...].T
    kb_ref[...] = k.astype(BF16)
    vb_ref[:, 0:LANES] = v_ref[...].astype(BF16)
    vb_ref[:, LANES:2 * LANES] = jnp.ones((T, LANES), BF16)
    kmean = jnp.concatenate(
        [jnp.sum(k[n * blk:(n + 1) * blk, :], axis=0, keepdims=True) * (1.0 / blk) for n in range(nb)]
        + [jnp.zeros((8 - nb, LANES), F32)] * (nb < 8), axis=0)

    q2 = _rope_lanes(q_ref[...], cos, sin)
    lane = lax.broadcasted_iota(jnp.int32, (1, LANES), 1)
    blk_id = lax.broadcasted_iota(jnp.int32, (8, 1), 0)
    q_blk = lax.broadcasted_iota(jnp.int32, (1, T), 1) // blk
    past = blk_id < q_blk
    causal = (lax.broadcasted_iota(jnp.int32, (blk, blk), 1) <= lax.broadcasted_iota(jnp.int32, (blk, blk), 0))
    hms, qss, sels = [], [], []
    for h in range(heads):
        hm = (lane // DH_A) == h
        qh = jnp.where(hm, q2, 0.0)
        gt = _nt(kmean, qh, precision=HIGHEST)
        sel_t = jnp.zeros_like(gt)
        for n in range(nb - 1):
            gn = gt[n:n + 1, :]
            beats = jnp.logical_and(past, jnp.logical_or(gt > gn, jnp.logical_and(gt == gn, blk_id < n)))
            rank = jnp.sum(beats.astype(F32), axis=0, keepdims=True)
            sel_n = jnp.where(jnp.logical_and(rank < MOBA_TOPK, n < q_blk), 1.0, 0.0)
            sel_t = jnp.where(blk_id == n, sel_n, sel_t)
        hms.append(hm)
        qss.append((qh * (DH_A ** -0.5 * LOG2_E)).astype(BF16))
        sels.append(jnp.where(sel_t.T > 0.5, 0.0, NEG_INF))

    for qt in range(nb * MOBA_Q_SPLIT):
        qi, part = divmod(qt, MOBA_Q_SPLIT)
        qrows = blk // MOBA_Q_SPLIT
        rows = slice(qt * qrows, (qt + 1) * qrows)
        own_mask = causal[part * qrows:(part + 1) * qrows, :]
        outs = []
        for h in range(heads):
            qs = qss[h][rows, :]
            pieces = []
            for n in range(qi + 1):
                s = _nt(qs, kb_ref[n * blk:(n + 1) * blk, :])
                pieces.append(jnp.where(own_mask, s, NEG_INF) if n == qi else s + sels[h][rows, n:n + 1])
            m = functools.reduce(jnp.maximum, pieces)
            m = jnp.max(m, axis=-1, keepdims=True)
            p_all = jnp.concatenate([jnp.exp2((s - m).astype(BF16)) for s in pieces], axis=1)
            acc = jnp.dot(p_all, vb_ref[0:(qi + 1) * blk, :], preferred_element_type=F32)
            outs.append(acc[:, 0:LANES] / acc[:, LANES:2 * LANES])
        out = outs[0]
        for h in range(1, heads):
            out = jnp.where(hms[h], outs[h], out)
        o_ref[rows, :] = out.astype(o_ref.dtype)


def _moba_prefill(z, rope, B, T):
    blk = MOBA_BLOCK
    assert T % blk == 0
    nb = T // blk
    assert nb <= 8
    cpt = AB_TILE // LANES
    return pl.pallas_call(
        functools.partial(_moba_prefill_kernel, nb=nb),
        grid=(B, W_A // LANES),
        in_specs=[pl.BlockSpec((T, LANES), lambda b, p: (b, p)),
                  pl.BlockSpec((T, LANES), lambda b, p: (b, cpt + p)),
                  pl.BlockSpec((T, LANES), lambda b, p: (b, 2 * cpt + p)),
                  pl.BlockSpec((T, LANES), lambda b, p: (0, 0)),
                  pl.BlockSpec((T, LANES), lambda b, p: (0, 0))],
        out_specs=[pl.BlockSpec((T, LANES), lambda b, p: (b, p)),
                   pl.BlockSpec((None, LANES, T), lambda b, p: (b, p, 0)),
                   pl.BlockSpec((None, LANES, T), lambda b, p: (b, p, 0))],
        out_shape=[jax.ShapeDtypeStruct((B * T, W_A), BF16),
                   jax.ShapeDtypeStruct((B, W_A, T), F32),
                   jax.ShapeDtypeStruct((B, W_A, T), F32)],
        scratch_shapes=[pltpu.VMEM((T, LANES), BF16), pltpu.VMEM((T, 2 * LANES), BF16)],
        compiler_params=_cparams(("parallel", "parallel")),
        name="moba_prefill",
    )(z, z, z, rope[0], rope[1])


def _ret_prefill_kernel(q_ref, k_ref, v_ref, gb_ref, cos_ref, sin_ref, rg_ref, dmask_ref, din_ref, dout_ref,
                        gch_ref, o_ref, s_ref, *, n_chunks):
    C = RET_CHUNK
    heads = LANES // DK_B
    lane = lax.broadcasted_iota(jnp.int32, (1, LANES), 1)
    hms = [(lane // DK_B) == h for h in range(heads)]
    row_h = lax.broadcasted_iota(jnp.int32, (LANES, LANES), 0) // DK_B
    col_h = lax.broadcasted_iota(jnp.int32, (LANES, LANES), 1) // DV_B
    same_head = row_h == col_h
    seg_ones = jnp.where(same_head, 1.0, 0.0).astype(BF16)
    din = din_ref[...]
    dout = dout_ref[...]
    gch = gch_ref[...]
    rg = rg_ref[...]
    dmask = dmask_ref[...]

    S = jnp.zeros((LANES, LANES), F32)
    for j in range(n_chunks):
        rows = slice(j * C, (j + 1) * C)
        cos = cos_ref[rows, :]
        sin = sin_ref[rows, :]
        q = _rope_lanes(q_ref[rows, :], cos, sin)
        k = _rope_lanes(k_ref[rows, :], cos, sin) * (DK_B ** -0.5)
        kb = k.astype(BF16)
        vb = v_ref[rows, :].astype(BF16)
        o = jnp.dot((q * din).astype(BF16), S.astype(BF16), preferred_element_type=F32)
        q_st = jnp.concatenate([jnp.where(hm, q, 0.0) for hm in hms], axis=0).astype(BF16)
        att = _nt(q_st, kb) * dmask
        res = jnp.dot(att.astype(BF16), vb, preferred_element_type=F32)
        intra = res[0:C, :]
        for h in range(1, heads):
            intra = jnp.where(hms[h], res[h * C:(h + 1) * C, :], intra)
        o = o + intra
        S = S * gch + jnp.where(same_head, _tn((k * dout).astype(BF16), vb), 0.0)
        ms = _rowsum_rep(o * o, seg_ones) * (1.0 / DV_B)
        g = gb_ref[rows, :]
        y = o * lax.rsqrt(ms + EPS) * rg * (g * _sigmoid(g))
        o_ref[rows, :] = y.astype(o_ref.dtype)
    s_ref[...] = S


def _ret_tables(chunk):
    log_g = jnp.log1p(-jnp.exp2(-5.0 - jnp.arange(H_B, dtype=F32)))
    i = jnp.arange(chunk, dtype=F32)
    d_in = jnp.exp((i[:, None] + 1.0) * log_g)
    d_out = jnp.exp((chunk - 1.0 - i)[:, None] * log_g)
    diff = i[:, None] - i[None, :]
    d_mask = jnp.where(diff >= 0, jnp.exp(jnp.maximum(diff, 0.0)[None] * log_g[:, None, None]), 0.0)
    g_chunk = jnp.exp(chunk * log_g)
    return d_in, d_out, d_mask, g_chunk


def _ret_prefill(z, rope, ret_g, B, T):
    C = RET_CHUNK
    assert T % C == 0
    d_in, d_out, d_mask, g_chunk = _ret_tables(C)
    npair = W_B // LANES
    lanes = lambda t: jnp.repeat(t, DK_B, axis=-1)
    din_l = lanes(d_in).reshape(C, npair, LANES).transpose(1, 0, 2)
    dout_l = lanes(d_out).reshape(C, npair, LANES).transpose(1, 0, 2)
    gch_l = lanes(g_chunk).reshape(npair, 1, LANES)
    cpt = AB_TILE // LANES
    col = lambda t: (lambda b, p: (b, t * cpt + p))
    return pl.pallas_call(
        functools.partial(_ret_prefill_kernel, n_chunks=T // C),
        grid=(B, npair),
        in_specs=[pl.BlockSpec((T, LANES), col(3)), pl.BlockSpec((T, LANES), col(4)),
                  pl.BlockSpec((T, LANES), col(5)), pl.BlockSpec((T, LANES), col(6)),
                  pl.BlockSpec((T, LANES), lambda b, p: (0, 0)),
                  pl.BlockSpec((T, LANES), lambda b, p: (0, 0)),
                  pl.BlockSpec((1, LANES), lambda b, p: (0, p)),
                  pl.BlockSpec((None, (LANES // DK_B) * C, C), lambda b, p: (p, 0, 0)),
                  pl.BlockSpec((None, C, LANES), lambda b, p: (p, 0, 0)),
                  pl.BlockSpec((None, C, LANES), lambda b, p: (p, 0, 0)),
                  pl.BlockSpec((None, 1, LANES), lambda b, p: (p, 0, 0))],
        out_specs=[pl.BlockSpec((T, LANES), lambda b, p: (b, p)),
                   pl.BlockSpec((None, None, LANES, LANES), lambda b, p: (b, p, 0, 0))],
        out_shape=[jax.ShapeDtypeStruct((B * T, W_B), BF16),
                   jax.ShapeDtypeStruct((B, npair, LANES, LANES), F32)],
        compiler_params=_cparams(("parallel", "parallel")),
        name="retention_prefill",
    )(z, z, z, z, rope[0], rope[1], ret_g.reshape(1, W_B), d_mask.reshape(npair, -1, C), din_l, dout_l, gch_l)


def _mlstm_prefill_kernel(bias_ref, q_ref, k_ref, v_ref, og_ref, ig_ref, fg_ref, igc_ref, fgc_ref, ng_ref,
                          h_ref, c_ref, n_ref, m_ref, *, n_chunks):
    L = MLSTM_CHUNK
    hd = pl.program_id(1)
    r_id = lax.broadcasted_iota(jnp.int32, (L, L), 0)
    c_id = lax.broadcasted_iota(jnp.int32, (L, L), 1)
    causal = c_id <= r_id
    upper = jnp.where(r_id <= c_id, 1.0, 0.0).astype(F32)
    lower = jnp.where(c_id <= r_id, 1.0, 0.0).astype(F32)
    ones_b = jnp.ones((L, L), BF16)
    i_rows = ig_ref[...] + bias_ref[0, hd]
    b_rows = jnp.dot(_log_sigmoid(fg_ref[...] + bias_ref[1, hd]), upper, preferred_element_type=F32, precision=HIGHEST)
    i_cols = igc_ref[...] + bias_ref[0, hd]
    b_cols = jnp.dot(lower, _log_sigmoid(fgc_ref[...] + bias_ref[1, hd]), preferred_element_type=F32, precision=HIGHEST)
    ng = ng_ref[...]

    Cm = jnp.zeros((L, L), F32)
    n = jnp.zeros((1, L), F32)
    m = jnp.zeros((1, 1), F32)
    for j in range(n_chunks):
        rows = slice(j * L, (j + 1) * L)
        q = q_ref[rows, :]
        k = k_ref[rows, :] * (DH_C ** -0.5)
        v = v_ref[rows, :]
        b_row, i_row = b_rows[j:j + 1, :], i_rows[j:j + 1, :]
        b_col, i_col = b_cols[:, j:j + 1], i_cols[:, j:j + 1]
        b_last = b_row[:, L - 1:L]
        qb = q.astype(BF16)
        kb = k.astype(BF16)
        b_rep = jnp.broadcast_to(b_col, (L, L))
        log_d = jnp.where(causal, (b_rep - b_row) + i_row, NEG_INF)
        m_row = jnp.broadcast_to(jnp.max(log_d, axis=-1, keepdims=True), (L, L))
        s = _nt(qb, kb) * jnp.exp(log_d - m_row)
        sv_ext = jnp.dot(s.astype(BF16), jnp.concatenate([v.astype(BF16), ones_b], axis=1),
                         preferred_element_type=F32)
        sv, s_sum = sv_ext[:, 0:L], sv_ext[:, L:2 * L]
        log_w = (b_last - b_col) + i_col
        m_loc = jnp.max(log_w, axis=0, keepdims=True)
        e_k = jnp.broadcast_to(jnp.exp(log_w - m_loc), (L, L))
        U = _tn((v * e_k).astype(BF16), kb)
        nk = jnp.sum(k * e_k, axis=0, keepdims=True)
        log_inter = b_rep + m
        m_t = jnp.maximum(log_inter, m_row)
        f_intra = jnp.exp(m_row - m_t)
        w_inter = jnp.exp(log_inter - m_t)
        num = sv * f_intra + _nt(qb, Cm.astype(BF16)) * w_inter
        den = s_sum * f_intra + w_inter * _rowsum_rep(q * n)
        h = num / jnp.maximum(jnp.abs(den), jnp.exp(-m_t))
        m_new = jnp.maximum(b_last + m, m_loc)
        decay = jnp.exp(b_last + m - m_new)
        f_k = jnp.exp(m_loc - m_new)
        Cm = decay * Cm + f_k * U
        n = decay * n + f_k * nk
        m = m_new
        hn = h * lax.rsqrt(_rowsum_rep(h * h) * (1.0 / L) + EPS) * ng
        h_ref[rows, :] = (hn * _sigmoid(og_ref[rows, :])).astype(h_ref.dtype)
    c_ref[...] = Cm
    n_ref[...] = n
    m_ref[...] = jnp.broadcast_to(m, m_ref.shape)


def _mlstm_prefill(z, ig, fg, gate_bias, out_g, B, T):
    L = MLSTM_CHUNK
    assert T % L == 0 and DH_C == LANES
    nc = T // L
    col = lambda t: (lambda b, h, bias: (b, t * H_C + h))
    gspec = pl.BlockSpec((None, None, nc, L), lambda b, h, bias: (b, h, 0, 0))
    gcspec = pl.BlockSpec((None, None, L, nc), lambda b, h, bias: (b, h, 0, 0))
    grid_spec = pltpu.PrefetchScalarGridSpec(
        num_scalar_prefetch=1,
        grid=(B, H_C),
        in_specs=[pl.BlockSpec((T, LANES), col(0)), pl.BlockSpec((T, LANES), col(1)),
                  pl.BlockSpec((T, LANES), col(2)), pl.BlockSpec((T, LANES), col(3)),
                  gspec, gspec, gcspec, gcspec,
                  pl.BlockSpec((1, LANES), lambda b, h, bias: (0, h))],
        out_specs=[pl.BlockSpec((T, LANES), lambda b, h, bias: (b, h)),
                   pl.BlockSpec((None, None, L, L), lambda b, h, bias: (b, h, 0, 0)),
                   pl.BlockSpec((None, None, 1, L), lambda b, h, bias: (b, h, 0, 0)),
                   pl.BlockSpec((None, None, 1, LANES), lambda b, h, bias: (b, h, 0, 0))],
    )
    return pl.pallas_call(
        functools.partial(_mlstm_prefill_kernel, n_chunks=nc),
        grid_spec=grid_spec,
        out_shape=[jax.ShapeDtypeStruct((B * T, W_C), BF16),
                   jax.ShapeDtypeStruct((B, H_C, L, L), F32),
                   jax.ShapeDtypeStruct((B, H_C, 1, L), F32),
                   jax.ShapeDtypeStruct((B, H_C, 1, LANES), F32)],
        compiler_params=_cparams(("parallel", "parallel")),
        name="mlstm_prefill",
    )(gate_bias, z, z, z, z, ig, fg, ig.swapaxes(2, 3), fg.swapaxes(2, 3), out_g.reshape(1, W_C))


SELECT_PAGES = 16


def _moba_select_kernel(pt_ref, *refs, pages_per_block):
    k_refs = refs[:SELECT_PAGES]
    q_ref, sel_ref, gate_ref, qb_ref = refs[SELECT_PAGES:]
    s = pl.program_id(1)
    ppb = pages_per_block
    blocks_per_step = SELECT_PAGES // ppb
    H, dh, page = k_refs[0].shape

    @pl.when(s == 0)
    def _():
        eye = _eye(dh)
        q = q_ref[...]
        for h in range(H):
            qb_ref[h] = jnp.broadcast_to(_row_to_col(q[h:h + 1, :], eye), (dh, page))

    qb = qb_ref[...]
    for i in range(blocks_per_step):
        acc = k_refs[i * ppb][...]
        for g in range(1, ppb):
            acc = acc + k_refs[i * ppb + g][...]
        prod = acc * qb
        part = prod[:, 0:8, :]
        for r in range(1, prod.shape[1] // 8):
            part = part + prod[:, 8 * r:8 * (r + 1), :]
        gate_ref[:, s * blocks_per_step + i] = part

    @pl.when(s == pl.num_programs(1) - 1)
    def _():
        nb = gate_ref.shape[1]
        part_sum = jnp.sum(gate_ref[...], axis=2).reshape(H * nb, page)
        col = jnp.sum(part_sum, axis=-1, keepdims=True) * (1.0 / MOBA_BLOCK)
        eye = _eye(nb)
        head_row = lax.broadcasted_iota(jnp.int32, (H, 1), 0)
        gate = jnp.zeros((H, nb), F32)
        for h in range(H):
            gate = jnp.where(head_row == h, _col_to_row(col[h * nb:(h + 1) * nb, :], eye), gate)
        blk_id = lax.broadcasted_iota(jnp.int32, (1, nb), 1)
        rank = jnp.zeros(gate.shape, jnp.int32)
        for m in range(nb):
            gm = gate[:, m:m + 1]
            beats = jnp.logical_or(gm > gate, jnp.logical_and(gm == gate, m < blk_id))
            rank = rank + jnp.where(beats, 1, 0)
        out_col = lax.broadcasted_iota(jnp.int32, sel_ref.shape, 1)
        out = jnp.zeros(sel_ref.shape, jnp.int32)
        for r in range(MOBA_TOPK):
            idx = jnp.sum(jnp.where(rank == r, blk_id, 0), axis=1, keepdims=True)
            out = jnp.where(out_col == r, idx, out)
        sel_ref[...] = out


def _moba_select(cache_kt, page_table, q):
    Bs, n_pages = page_table.shape
    _, H, dh, page = cache_kt.shape
    ppb = MOBA_BLOCK // page
    nb = n_pages // ppb
    assert nb >= MOBA_TOPK and n_pages % SELECT_PAGES == 0 and SELECT_PAGES % ppb == 0

    def page_spec(i):
        return pl.BlockSpec((None, H, dh, page), lambda b, s, pt: (pt[b, s * SELECT_PAGES + i], 0, 0, 0))

    grid_spec = pltpu.PrefetchScalarGridSpec(
        num_scalar_prefetch=1,
        grid=(Bs, n_pages // SELECT_PAGES),
        in_specs=[page_spec(i) for i in range(SELECT_PAGES)]
        + [pl.BlockSpec((None, H, dh), lambda b, s, pt: (b, 0, 0))],
        out_specs=pl.BlockSpec((None, H, 8), lambda b, s, pt: (b, 0, 0)),
        scratch_shapes=[pltpu.VMEM((H, nb, 8, page), F32), pltpu.VMEM((H, dh, page), F32)],
    )
    return pl.pallas_call(
        functools.partial(_moba_select_kernel, pages_per_block=ppb),
        grid_spec=grid_spec,
        out_shape=jax.ShapeDtypeStruct((Bs, H, 8), jnp.int32),
        compiler_params=_cparams(("parallel", "arbitrary")),
        name="moba_select",
    )(page_table, *([cache_kt] * SELECT_PAGES), q)


def _moba_decode_kernel(pt_ref, sel_ref, q_ref, kn_ref, vn_ref, ck_ref, cv_ref, o_ref, kbuf, vbuf, sem,
                        *, pages_per_block, page):
    b = pl.program_id(0)
    nb_ = pl.num_programs(0)
    H = q_ref.shape[0]
    ppb = pages_per_block

    def copies(bb, slot):
        out = []
        for h in range(H):
            for r in range(MOBA_TOPK):
                blk = sel_ref[bb, h, r]
                for g in range(ppb):
                    pg = pt_ref[bb, blk * ppb + g]
                    dst = pl.ds((r * ppb + g) * page, page)
                    out.append(pltpu.make_async_copy(ck_ref.at[pg, h], kbuf.at[slot, h, :, dst], sem.at[slot, 0]))
                    out.append(pltpu.make_async_copy(cv_ref.at[pg, h], vbuf.at[slot, h, :, dst], sem.at[slot, 1]))
        return out

    slot = b % 2

    @pl.when(b == 0)
    def _():
        for c in copies(b, slot):
            c.start()

    @pl.when(b + 1 < nb_)
    def _():
        for c in copies(b + 1, 1 - slot):
            c.start()

    for c in copies(b, slot):
        c.wait()

    row = lax.broadcasted_iota(jnp.int32, (H, 1), 0)
    out = jnp.zeros(o_ref.shape, F32)
    q = q_ref[...]
    kn = kn_ref[...]
    vn = vn_ref[...]
    for h in range(H):
        qh = q[h:h + 1, :] * (DH_A ** -0.5)
        q8 = jnp.broadcast_to(qh, (8, qh.shape[1])).astype(BF16)
        kh = kbuf[slot, h].astype(BF16)
        vh = vbuf[slot, h].astype(BF16)
        s = jnp.dot(q8, kh, preferred_element_type=F32)[0:1, :]
        s_self = jnp.sum(qh * kn[h:h + 1, :], axis=-1, keepdims=True)
        m = jnp.maximum(jnp.max(s, axis=-1, keepdims=True), s_self)
        p = jnp.exp(s - m)
        p_self = jnp.exp(s_self - m)
        l = jnp.sum(p, axis=-1, keepdims=True) + p_self
        p8 = jnp.broadcast_to(p, (8, p.shape[1])).astype(BF16)
        pv = _nt(p8, vh)[0:1, :]
        oh = (pv + p_self * vn[h:h + 1, :]) / l
        out = jnp.where(row == h, oh, out)
    o_ref[...] = out


def _moba_decode(cache_kt, cache_vt, page_table, sel, q, k_new, v_new):
    Bs, n_pages = page_table.shape
    _, H, dh, page = cache_kt.shape
    ppb = MOBA_BLOCK // page
    rows = MOBA_TOPK * MOBA_BLOCK
    vec = pl.BlockSpec((None, H, dh), lambda b, pt, sl: (b, 0, 0))
    grid_spec = pltpu.PrefetchScalarGridSpec(
        num_scalar_prefetch=2,
        grid=(Bs,),
        in_specs=[vec, vec, vec, pl.BlockSpec(memory_space=pl.ANY), pl.BlockSpec(memory_space=pl.ANY)],
        out_specs=vec,
        scratch_shapes=[pltpu.VMEM((2, H, dh, rows), F32), pltpu.VMEM((2, H, dh, rows), F32),
                        pltpu.SemaphoreType.DMA((2, 2))],
    )
    return pl.pallas_call(
        functools.partial(_moba_decode_kernel, pages_per_block=ppb, page=page),
        grid_spec=grid_spec,
        out_shape=jax.ShapeDtypeStruct((Bs, H, dh), F32),
        compiler_params=_cparams(("arbitrary",)),
        name="moba_decode",
    )(page_table, sel, q, k_new, v_new, cache_kt, cache_vt)


DECODE_ROWS = 4


def _per_row(kernel, shared):
    def wrapped(*refs):
        rows = next(r.shape[0] for i, r in enumerate(refs) if i not in shared)
        for r in range(rows):
            kernel(*[ref if i in shared else ref.at[r] for i, ref in enumerate(refs)])
    return wrapped


def _ret_decode_kernel(q_ref, k_ref, v_ref, gb_ref, rg_ref, g_ref, s_ref, o_ref, so_ref):
    H, dk = q_ref.shape
    eye = _eye(dk)
    row = lax.broadcasted_iota(jnp.int32, (H, 1), 0)
    q = q_ref[...]
    k = k_ref[...]
    v = v_ref[...]
    g_all = g_ref[...]
    out = jnp.zeros(o_ref.shape, F32)
    for h in range(H):
        qh, kh, vh = q[h:h + 1, :], k[h:h + 1, :], v[h:h + 1, :]
        g = g_all[h:h + 1, 0:1]
        S = s_ref[h]
        att = jnp.sum(qh * kh, axis=-1, keepdims=True)
        cross = jnp.sum(_row_to_col(qh * g, eye) * S, axis=0, keepdims=True)
        out = jnp.where(row == h, att * vh + cross, out)
        so_ref[h] = S * g + _row_to_col(kh, eye) * vh
    gb = gb_ref[...]
    y = out * lax.rsqrt(jnp.mean(out * out, axis=-1, keepdims=True) + EPS) * rg_ref[...]
    o_ref[...] = y * (gb * _sigmoid(gb))


def _ret_decode(q, k, v, gb, ret_g, state):
    Bs, H, dk = q.shape
    log_g = jnp.log1p(-jnp.exp2(-5.0 - jnp.arange(H_B, dtype=F32)))
    g = jnp.broadcast_to(jnp.exp(1.0 * log_g)[:, None], (H, LANES))
    R = DECODE_ROWS if Bs % DECODE_ROWS == 0 else 1
    vec = pl.BlockSpec((R, H, dk), lambda b: (b, 0, 0))
    st = pl.BlockSpec((R, H, dk, DV_B), lambda b: (b, 0, 0, 0))
    return pl.pallas_call(
        _per_row(_ret_decode_kernel, shared=(4, 5)),
        grid=(Bs // R,),
        in_specs=[vec, vec, vec, vec, pl.BlockSpec((H, dk), lambda b: (0, 0)),
                  pl.BlockSpec((H, LANES), lambda b: (0, 0)), st],
        out_specs=[vec, st],
        out_shape=[jax.ShapeDtypeStruct((Bs, H, DV_B), F32), jax.ShapeDtypeStruct(state.shape, F32)],
        compiler_params=_cparams(("parallel",)),
        name="retention_decode",
    )(q, k, v, gb, ret_g.reshape(H, DV_B), g, state)


def _mlstm_decode_kernel(q_ref, k_ref, v_ref, og_ref, ig_ref, fg_ref, bi_ref, bf_ref, ng_ref, c_ref, n_ref,
                         m_ref, h_ref, co_ref, no_ref, mo_ref):
    H, d = q_ref.shape
    eye = _eye(d)
    row = lax.broadcasted_iota(jnp.int32, (H, 1), 0)
    q = q_ref[...]
    k = k_ref[...] * (DH_C ** -0.5)
    v = v_ref[...]
    n0 = n_ref[...]
    ig = ig_ref[...] + bi_ref[...]
    b = _log_sigmoid(fg_ref[...] + bf_ref[...])
    m0 = m_ref[...]
    log_inter = b + m0
    m_t = jnp.maximum(log_inter, ig)
    w_intra = jnp.exp(ig - m_t)
    w_inter = jnp.exp(log_inter - m_t)
    s = jnp.sum(q * k, axis=-1, keepdims=True) * w_intra
    den = s + w_inter * jnp.sum(q * n0, axis=-1, keepdims=True)
    scale = 1.0 / jnp.maximum(jnp.abs(den), jnp.exp(-m_t))
    w_k = w_intra
    decay = w_inter
    hs = jnp.zeros(h_ref.shape, F32)
    for h in range(H):
        qh, kh, vh = q[h:h + 1, :], k[h:h + 1, :], v[h:h + 1, :]
        Cm = c_ref[h]
        cq = _col_to_row(jnp.sum(Cm * qh, axis=1, keepdims=True), eye)
        num = s[h:h + 1, :] * vh + cq * w_inter[h:h + 1, :]
        hs = jnp.where(row == h, num * scale[h:h + 1, :], hs)
        co_ref[h] = decay[h:h + 1, :] * Cm + _row_to_col(vh * w_k[h:h + 1, :], eye) * kh
    no_ref[...] = decay * n0 + k * w_k
    mo_ref[...] = m_t
    hn = hs * lax.rsqrt(jnp.mean(hs * hs, axis=-1, keepdims=True) + EPS) * ng_ref[...]
    h_ref[...] = hn * _sigmoid(og_ref[...])


def _mlstm_decode(q, k, v, og, ig, fg, gate_bias, out_g, C0, n0, m0):
    Bs, H, d = q.shape
    R = DECODE_ROWS if Bs % DECODE_ROWS == 0 else 1
    vec = pl.BlockSpec((R, H, d), lambda b: (b, 0, 0))
    sc = pl.BlockSpec((R, H, 1), lambda b: (b, 0, 0))
    st = pl.BlockSpec((R, H, d, d), lambda b: (b, 0, 0, 0))
    bias = pl.BlockSpec((H, 1), lambda b: (0, 0))
    return pl.pallas_call(
        _per_row(_mlstm_decode_kernel, shared=(6, 7, 8)),
        grid=(Bs // R,),
        in_specs=[vec, vec, vec, vec, sc, sc, bias, bias, pl.BlockSpec((H, d), lambda b: (0, 0)), st, vec, sc],
        out_specs=[vec, st, vec, sc],
        out_shape=[jax.ShapeDtypeStruct((Bs, H, d), F32), jax.ShapeDtypeStruct(C0.shape, F32),
                   jax.ShapeDtypeStruct((Bs, H, d), F32), jax.ShapeDtypeStruct((Bs, H, 1), F32)],
        compiler_params=_cparams(("parallel",)),
        name="mlstm_decode",
    )(q, k, v, og, ig, fg, gate_bias[0].reshape(H, 1), gate_bias[1].reshape(H, 1), out_g.reshape(H, d), C0, n0, m0)


def _rope_tables(pos):
    half = DH_A // 2
    inv = ROPE_THETA ** (-jnp.arange(half, dtype=F32) / half)
    ang = pos[:, None] * inv[None, :]
    cos = jnp.cos(ang)
    sin = jnp.sin(ang)
    reps = LANES // DH_A
    cos_t = jnp.tile(jnp.concatenate([cos, cos], axis=-1), (1, reps))
    sin_t = jnp.tile(jnp.concatenate([-sin, sin], axis=-1), (1, reps))
    return cos_t, sin_t


def _row_tile(m, pref):
    return pref if m % pref == 0 else m


def kernel(x_prompt, x_sample, cache_k, cache_v, page_table, state_ret, state_mlstm_C, state_mlstm_n,
           state_mlstm_m, state_ffn_conv, ab_norm_g, ab_w_in, ab_ret_norm_g, ab_w_out, c_norm_g, c_w_in,
           c_gate_bias, c_out_norm_g, c_w_out, ffn_norm_g, ffn_w1, ffn_w3, ffn_conv_w, ffn_conv_b, ffn_w2,
           final_norm_g):
    Bp, Tp, D = x_prompt.shape
    Bs, Ts, _ = x_sample.shape
    assert Ts == 1
    past_len = page_table.shape[1] * cache_k.shape[2]
    Mp = Bp * Tp
    xp = x_prompt.reshape(Mp, D)
    xs = x_sample.reshape(Bs, D)
    tm_p = _row_tile(Tp, 1024)
    tm_in = _row_tile(Tp, 2048)
    tm_ffn = _row_tile(Tp, 1024)
    tm_s = Bs
    rope_p = _rope_tables(jnp.arange(Tp, dtype=jnp.int32).astype(F32))
    rope_s = _rope_tables(jnp.full((Bs,), past_len, jnp.int32).astype(F32))
    gw = 2 * H_C

    outs = {}

    w_in = ab_w_in[0].astype(BF16)
    w_out = ab_w_out[0].astype(BF16)
    w_out_a, w_out_b = w_out[:W_A], w_out[W_A:]

    z = _norm_matmul(xp, ab_norm_g[0], w_in, tm_p, IN_AB // 2)
    oa, kt, vt = _moba_prefill(z, rope_p, Bp, Tp)
    ob, s_pair = _ret_prefill(z, rope_p, ab_ret_norm_g[0], Bp, Tp)
    xp = _proj_residual([oa, ob], [w_out_a, w_out_b], xp, tm_p)
    outs["k_prompt"] = kt.reshape(Bp, H_A, DH_A, Tp).transpose(0, 3, 1, 2)[None]
    outs["v_prompt"] = vt.reshape(Bp, H_A, DH_A, Tp).transpose(0, 3, 1, 2)[None]
    hb = LANES // DK_B
    s_heads = jnp.stack([s_pair[:, :, i * DK_B:(i + 1) * DK_B, i * DV_B:(i + 1) * DV_B] for i in range(hb)], axis=2)
    outs["ret_prompt"] = s_heads.reshape(1, Bp, H_B, DK_B, DV_B)

    zs = _norm_matmul(xs, ab_norm_g[0], w_in, tm_s, AB_TILE, rope=rope_s)
    seg = lambda t: zs[:, t * AB_TILE:(t + 1) * AB_TILE].reshape(Bs, H_A, DH_A)
    qa_s, ka_s, va_s, qb_s, kb_s, vb_s, gb_s = (seg(t) for t in range(7))
    cache_kt = cache_k[0].transpose(0, 2, 3, 1)
    cache_vt = cache_v[0].transpose(0, 2, 3, 1)
    sel = _moba_select(cache_kt, page_table, qa_s)
    oa_s = _moba_decode(cache_kt, cache_vt, page_table, sel, qa_s, ka_s, va_s)
    ob_s, s_new = _ret_decode(qb_s, kb_s, vb_s, gb_s, ab_ret_norm_g[0], state_ret[0])
    xs = _proj_residual([oa_s.reshape(Bs, W_A).astype(BF16), ob_s.reshape(Bs, W_B).astype(BF16)],
                        [w_out_a, w_out_b], xs, tm_s)
    outs["k_sample"] = ka_s.reshape(1, Bs, 1, H_A, DH_A)
    outs["v_sample"] = va_s.reshape(1, Bs, 1, H_A, DH_A)
    outs["ret_sample"] = s_new[None]

    conv_p, conv_s = [], []

    w1_b, w3_b, w2_b = ffn_w1.astype(BF16), ffn_w3.astype(BF16), ffn_w2.astype(BF16)

    def ffn_both(l, xp, xs, final_g):
        st = state_ffn_conv[l]
        xp, a_tail = _ffn(xp, ffn_norm_g[l], l, w1_b, w3_b, ffn_conv_w[l], ffn_conv_b[l], w2_b, final_g, tm_ffn,
                          seq_len=Tp)
        xs, a_s = _ffn(xs, ffn_norm_g[l], l, w1_b, w3_b, ffn_conv_w[l], ffn_conv_b[l], w2_b, final_g, tm_s,
                       state=(st[:, 0, :], st[:, 1, :]))
        conv_p.append(a_tail[:, 8 - (CONV_W - 1):, :])
        conv_s.append(jnp.stack([st[:, 1, :], a_s], axis=1))
        return xp, xs

    xp, xs = ffn_both(0, xp, xs, None)

    w_in = c_w_in[0]
    w_all = w_in.astype(BF16)
    w_gate = jnp.pad(w_in[:, 4 * W_C:], ((0, 0), (0, LANES - gw))).astype(BF16)
    w_out = c_w_out[0].astype(BF16)
    L = MLSTM_CHUNK

    z, zg = _norm_matmul_side(xp, c_norm_g[0], w_all, 4 * W_C, w_gate, tm_in, 1024)
    gates = zg[:, :gw].reshape(Bp, Tp // L, L, 2, H_C).transpose(3, 0, 4, 1, 2)
    h, C_p, n_p, m_p = _mlstm_prefill(z, gates[0], gates[1], c_gate_bias[0], c_out_norm_g[0], Bp, Tp)
    xp = _proj_residual([h], [w_out], xp, tm_p)
    outs["C_prompt"] = C_p[None]
    outs["n_prompt"] = n_p.reshape(1, Bp, H_C, DH_C)
    outs["m_prompt"] = m_p[:, :, 0, 0][None]

    zs, zgs = _norm_matmul_side(xs, c_norm_g[0], w_all, 4 * W_C, w_gate, tm_s, 1024)
    segc = lambda t: zs[:, t * W_C:(t + 1) * W_C].reshape(Bs, H_C, DH_C)
    ig_s = zgs[:, :H_C][:, :, None]
    fg_s = zgs[:, H_C:gw][:, :, None]
    h_s, C_s, n_s, m_s = _mlstm_decode(segc(0), segc(1), segc(2), segc(3), ig_s, fg_s, c_gate_bias[0], c_out_norm_g[0],
                                       state_mlstm_C[0], state_mlstm_n[0], state_mlstm_m[0][:, :, None])
    xs = _proj_residual([h_s.reshape(Bs, W_C).astype(BF16)], [w_out], xs, tm_s)
    outs["C_sample"] = C_s[None]
    outs["n_sample"] = n_s[None]
    outs["m_sample"] = m_s[:, :, 0][None]

    xp, xs = ffn_both(1, xp, xs, final_norm_g)

    return (xp.reshape(Bp, Tp, D), xs.reshape(Bs, 1, D),
            outs["k_prompt"], outs["v_prompt"], outs["k_sample"], outs["v_sample"],
            outs["ret_prompt"], outs["ret_sample"], outs["C_prompt"], outs["C_sample"],
            outs["n_prompt"], outs["n_sample"], outs["m_prompt"], outs["m_sample"],
            jnp.stack(conv_p), jnp.stack(conv_s))
```
